```python
import math
import jax
import jax.numpy as jnp
from jax import lax
import numpy as np

D_MODEL = 1024
BATCH = 32
SEQ = 2048
DEPTH = 4
DEC_BATCH = 2
DEC_SEQ = 16384
PAST_LEN = 128

GRID_W = 64
PLE_DIM = 256
N_MIXERS = 4
N_LAYERS_A = (DEPTH + 3) // 4
N_LAYERS_B = (DEPTH + 2) // 4
N_LAYERS_C = (DEPTH + 1) // 4
N_LAYERS_D = DEPTH // 4
Q_BLOCK = 128

A_HEADS = 8
A_HEAD_DIM = 64
B_HEADS = 16
B_NOPE = 64
B_ROPE = 32
B_VDIM = 64
B_Q_LORA = 384
B_KV_LORA = 256
ROPE_THETA = 10000.0
C_HEADS = 16
C_HEAD_DIM = 64
C_WIN_ROWS = 8
C_WIN_COLS = 16
D_HEADS = 8
D_HEAD_DIM = 64
D_GROUPS = ((128, 1), (512, 4), (2048, 16))
D_NG = len(D_GROUPS)
N_EXPERTS = 16
N_EXPERT_GROUPS = 4
EXPERTS_PER_GROUP = N_EXPERTS // N_EXPERT_GROUPS
TOP_K = 2
GROUP_SCORE_K = 2
EXPERT_FF = 256
SHARED_FF = 256
ALPHA = (2.0 * DEPTH) ** 0.25
BETA = (8.0 * DEPTH) ** -0.25
LN_EPS = 1e-5
RMS_EPS = 1e-6

kernel_name = "hybrid_bidir_encoder_trunk"


def layer_norm(x, g, b):
    xf = x.astype(jnp.float32)
    mu = jnp.mean(xf, axis=-1, keepdims=True)
    var = jnp.mean(jnp.square(xf - mu), axis=-1, keepdims=True)
    y = (xf - mu) * lax.rsqrt(var + LN_EPS)
    return (y * g.astype(jnp.float32) + b.astype(jnp.float32)).astype(x.dtype)


def rms_norm(x, g):
    xf = x.astype(jnp.float32)
    y = xf * lax.rsqrt(jnp.mean(jnp.square(xf), axis=-1, keepdims=True) + RMS_EPS)
    return (y * g.astype(jnp.float32)).astype(x.dtype)


def alibi_slopes(n):
    return jnp.asarray(2.0 ** (-8.0 * np.arange(1, n + 1) / n), dtype=jnp.float32)


def rope_tables(seq_len, dim):
    inv = 1.0 / (ROPE_THETA ** (np.arange(0, dim, 2, dtype=np.float32) / dim))
    ang = jnp.arange(seq_len, dtype=jnp.float32)[:, None] * jnp.asarray(inv, jnp.float32)[None, :]
    return jnp.cos(ang), jnp.sin(ang)


def apply_rope(x, cos, sin):
    half = x.shape[-1] // 2
    x1, x2 = x[..., :half], x[..., half:]
    c = cos.astype(x.dtype)
    s = sin.astype(x.dtype)
    return jnp.concatenate([x1 * c - x2 * s, x2 * c + x1 * s], axis=-1)


def diff_attention(x, w_in, lam, subln_g, w_out, layer_idx):
    bsz, seq, _ = x.shape
    H, d = A_HEADS, A_HEAD_DIM
    q, k, v = jnp.split(x @ w_in, 3, axis=-1)
    q = q.reshape(bsz, seq, H, 2, d)
    k = k.reshape(bsz, seq, H, 2, d)
    v = v.reshape(bsz, seq, H, 2 * d)
    lam_init = 0.8 - 0.6 * math.exp(-0.3 * layer_idx)
    lf = lam.astype(jnp.float32)
    lam_full = jnp.exp(jnp.sum(lf[0] * lf[1])) - jnp.exp(jnp.sum(lf[2] * lf[3])) + lam_init
    slopes = alibi_slopes(H)[None, :, None, None, None]
    scale = d ** -0.5
    nb = seq // Q_BLOCK
    qb = q.reshape(bsz, nb, Q_BLOCK, H, 2, d).transpose(1, 0, 2, 3, 4, 5)
    kpos = jnp.arange(seq)

    def block(args):
        qblk, start = args
        s = jnp.einsum("bqhmd,bkhmd->bhmqk", qblk, k).astype(jnp.float32) * scale
        qpos = start + jnp.arange(Q_BLOCK)
        dist = jnp.abs(qpos[:, None] - kpos[None, :]).astype(jnp.float32)
        p = jax.nn.softmax(s - slopes * dist, axis=-1)
        attn = p[:, :, 0] - lam_full * p[:, :, 1]
        return jnp.einsum("bhqk,bkhe->bqhe", attn.astype(v.dtype), v)

    o = lax.map(block, (qb, jnp.arange(nb) * Q_BLOCK))
    o = o.transpose(1, 0, 2, 3, 4).reshape(bsz, seq, H, 2 * d)
    o = rms_norm(o, subln_g) * (1.0 - lam_init)
    return o.reshape(bsz, seq, H * 2 * d) @ w_out


def mla(x, w_in, q_norm_g, kv_norm_g, w_uq, w_ukv, w_out):
    bsz, seq, _ = x.shape
    H = B_HEADS
    h = x @ w_in
    c_q = rms_norm(h[..., :B_Q_LORA], q_norm_g)
    c_kv = rms_norm(h[..., B_Q_LORA:B_Q_LORA + B_KV_LORA], kv_norm_g)
    k_rope = h[..., B_Q_LORA + B_KV_LORA:]
    q = (c_q @ w_uq).reshape(bsz, seq, H, B_NOPE + B_ROPE)
    kv = (c_kv @ w_ukv).reshape(bsz, seq, H, B_NOPE + B_VDIM)
    cos, sin = rope_tables(seq, B_ROPE)
    q_nope = q[..., :B_NOPE]
    q_rope = apply_rope(q[..., B_NOPE:], cos[:, None, :], sin[:, None, :])
    k_rope = apply_rope(k_rope, cos, sin)
    k_nope, v = kv[..., :B_NOPE], kv[..., B_NOPE:]
    scale = (B_NOPE + B_ROPE) ** -0.5
    nb = seq // Q_BLOCK

    def to_blocks(a):
        return a.reshape(bsz, nb, Q_BLOCK, a.shape[2], a.shape[3]).transpose(1, 0, 2, 3, 4)

    def block(args):
        qn, qr = args
        s = jnp.einsum("bqhd,bkhd->bhqk", qn, k_nope) + jnp.einsum("bqhr,bkr->bhqk", qr, k_rope)
        p = jax.nn.softmax(s.astype(jnp.float32) * scale, axis=-1)
        return jnp.einsum("bhqk,bkhv->bqhv", p.astype(v.dtype), v)

    o = lax.map(block, (to_blocks(q_nope), to_blocks(q_rope)))
    o = o.transpose(1, 0, 2, 3, 4).reshape(bsz, seq, H * B_VDIM)
    return o @ w_out


def neighborhood_attention(x, w_qkv, rpb, w_out):
    bsz, seq, _ = x.shape
    rows = seq // GRID_W
    kr = min(C_WIN_ROWS, rows)
    H, d = C_HEADS, C_HEAD_DIM
    qkv = (x @ w_qkv).reshape(bsz, rows, GRID_W, 3, H, d)
    q, k, v = qkv[:, :, :, 0], qkv[:, :, :, 1], qkv[:, :, :, 2]
    col = np.arange(GRID_W)
    col_start = np.clip(col - C_WIN_COLS // 2, 0, GRID_W - C_WIN_COLS)
    col_mask = jnp.asarray((col[None, :] >= col_start[:, None]) & (col[None, :] < col_start[:, None] + C_WIN_COLS))
    dc_idx = jnp.asarray(np.clip(col[None, :] - col[:, None] + C_WIN_COLS - 1, 0, 2 * C_WIN_COLS - 2))
    scale = d ** -0.5

    def row_fn(r):
        rs = jnp.clip(r - kr // 2, 0, rows - kr)
        qr = lax.dynamic_index_in_dim(q, r, axis=1, keepdims=False)
        kb = lax.dynamic_slice_in_dim(k, rs, kr, axis=1)
        vb = lax.dynamic_slice_in_dim(v, rs, kr, axis=1)
        s = jnp.einsum("bqhd,brchd->bhqrc", qr, kb).astype(jnp.float32) * scale
        dr_idx = rs + jnp.arange(kr) - r + (C_WIN_ROWS - 1)
        bias = rpb[:, dr_idx[None, :, None], dc_idx[:, None, :]]
        s = s + bias[None].astype(jnp.float32)
        s = jnp.where(col_mask[None, None, :, None, :], s, -jnp.inf)
        p = jax.nn.softmax(s, axis=(-2, -1))
        return jnp.einsum("bhqrc,brchd->bqhd", p.astype(vb.dtype), vb)

    o = lax.map(row_fn, jnp.arange(rows))
    o = o.transpose(1, 0, 2, 3, 4).reshape(bsz, seq, H * d)
    return o @ w_out


def dilated_group(q, k, v, window, dil, slopes):
    bsz, seq, H, d = q.shape
    rad = window // (2 * dil)
    L = seq // dil
    nb = -(-L // rad)
    Lp = nb * rad

    def comp(a):
        return a.reshape(bsz, L, dil, H, d).transpose(0, 2, 1, 3, 4)

    qc, kc, vc = comp(q), comp(k), comp(v)
    kv_pad = ((0, 0), (0, 0), (rad, Lp - L + rad), (0, 0), (0, 0))
    kp = jnp.pad(kc, kv_pad).reshape(bsz, dil, nb + 2, rad, H, d)
    vp = jnp.pad(vc, kv_pad).reshape(bsz, dil, nb + 2, rad, H, d)
    kw = jnp.concatenate([kp[:, :, :-2], kp[:, :, 1:-1], kp[:, :, 2:]], axis=3)
    vw = jnp.concatenate([vp[:, :, :-2], vp[:, :, 1:-1], vp[:, :, 2:]], axis=3)
    qp = jnp.pad(qc, ((0, 0), (0, 0), (0, Lp - L), (0, 0), (0, 0))).reshape(bsz, dil, nb, rad, H, d)
    s = jnp.einsum("bgnqhd,bgnkhd->bghnqk", qp, kw).astype(jnp.float32) * (d ** -0.5)
    qi = np.arange(nb)[:, None] * rad + np.arange(rad)[None, :]
    ki = np.arange(nb)[:, None] * rad - rad + np.arange(3 * rad)[None, :]
    rel = ki[:, None, :] - qi[:, :, None]
    valid = jnp.asarray((np.abs(rel) <= rad) & (ki[:, None, :] >= 0) & (ki[:, None, :] < L))
    dist = jnp.asarray(np.abs(rel) * dil, dtype=jnp.float32)
    s = s - slopes[None, None, :, None, None, None] * dist[None, None, None]
    s = jnp.where(valid[None, None, None], s, -jnp.inf)
    lse = jax.nn.logsumexp(s, axis=-1)
    p = jnp.exp(s - lse[..., None])
    o = jnp.einsum("bghnqk,bgnkhd->bgnqhd", p.astype(vw.dtype), vw)
    o = o.reshape(bsz, dil, Lp, H, d)[:, :, :L].transpose(0, 2, 1, 3, 4).reshape(bsz, seq, H, d)
    lse = lse.transpose(0, 1, 3, 4, 2).reshape(bsz, dil, Lp, H)[:, :, :L]
    lse = lse.transpose(0, 2, 1, 3).reshape(bsz, seq, H)
    return o, lse


def dilated_attention(x, w_qkv, w_out):
    bsz, seq, _ = x.shape
    H, d = D_HEADS, D_HEAD_DIM
    qkv = (x @ w_qkv).reshape(bsz, seq, 3, D_NG, H, d)
    slopes = alibi_slopes(H)
    outs, lses = [], []
    for g, (window, dil) in enumerate(D_GROUPS):
        o, lse = dilated_group(qkv[:, :, 0, g], qkv[:, :, 1, g], qkv[:, :, 2, g], window, dil, slopes)
        outs.append(o)
        lses.append(lse)
    wts = jax.nn.softmax(jnp.stack(lses), axis=0)
    o = jnp.einsum("gbsh,gbshd->bshd", wts.astype(x.dtype), jnp.stack(outs))
    return o.reshape(bsz, seq, H * d) @ w_out


def grouped_moe(x2d, router_w, router_b, w_gate, w_up, w_down, ws_gate, ws_up, ws_down):
    T = x2d.shape[0]
    scores = jax.nn.sigmoid((x2d @ router_w).astype(jnp.float32))
    biased = scores + router_b.astype(jnp.float32)
    grp = biased.reshape(T, N_EXPERT_GROUPS, EXPERTS_PER_GROUP)
    grp_score = jnp.sum(lax.top_k(grp, GROUP_SCORE_K)[0], axis=-1)
    sel_group = jnp.argmax(grp_score, axis=-1)
    gmask = jnp.arange(N_EXPERT_GROUPS)[None, :] == sel_group[:, None]
    emask = jnp.repeat(gmask, EXPERTS_PER_GROUP, axis=1)
    _, top_idx = lax.top_k(jnp.where(emask, biased, -jnp.inf), TOP_K)
    top_w = jnp.take_along_axis(scores, top_idx, axis=-1)
    top_w = top_w / jnp.sum(top_w, axis=-1, keepdims=True)
    gates = jnp.sum(jax.nn.one_hot(top_idx, N_EXPERTS, dtype=jnp.float32) * top_w[..., None], axis=1)
    gates = gates.astype(x2d.dtype)
    y0 = (jax.nn.silu(x2d @ ws_gate) * (x2d @ ws_up)) @ ws_down

    def expert_step(y, args):
        wg, wu, wd, gcol = args
        h = jax.nn.silu(x2d @ wg) * (x2d @ wu)
        return y + (h @ wd) * gcol[:, None], None

    y, _ = lax.scan(expert_step, y0, (w_gate, w_up, w_down, gates.T))
    return y


def encoder_trunk(x, p, prm):
    for i in range(DEPTH):
        mixer, j = i % N_MIXERS, i // N_MIXERS
        if mixer == 0:
            h = diff_attention(x, prm["a_w_in"][j], prm["a_lambda"][j], prm["a_subln"][j], prm["a_w_out"][j], i)
        elif mixer == 1:
            h = mla(x, prm["b_w_in"][j], prm["b_q_norm"][j], prm["b_kv_norm"][j], prm["b_w_uq"][j], prm["b_w_ukv"][j], prm["b_w_out"][j])
        elif mixer == 2:
            h = neighborhood_attention(x, prm["c_w_qkv"][j], prm["c_rpb"][j], prm["c_w_out"][j])
        else:
            h = dilated_attention(x, prm["d_w_qkv"][j], prm["d_w_out"][j])
        x = layer_norm(ALPHA * x + h, prm["ln1_g"][i], prm["ln1_b"][i])
        bsz, seq, dm = x.shape
        h = grouped_moe(x.reshape(bsz * seq, dm), prm["router_w"], prm["router_b"],
                        prm["moe_w_gate"][i], prm["moe_w_up"][i], prm["moe_w_down"][i],
                        prm["moe_ws_gate"][i], prm["moe_ws_up"][i], prm["moe_ws_down"][i]).reshape(bsz, seq, dm)
        x = layer_norm(ALPHA * x + h, prm["ln2_g"][i], prm["ln2_b"][i])
        e = p[i] @ prm["ple_w_in"][i]
        x = x + jax.nn.sigmoid(x @ prm["ple_w_gate"][i]) * e
    return x


def setup_inputs(seed: int = 0) -> dict:
    key = jax.random.key(seed)
    ks = iter(jax.random.split(key, 48))

    def nrm(shape, scale):
        return jax.random.normal(next(ks), shape, jnp.float32) * scale

    def gain(shape):
        return 1.0 + nrm(shape, 0.02)

    D = D_MODEL
    inp = {}
    inp["x_prompt"] = nrm((BATCH, SEQ, D), 1.0)
    inp["x_sample"] = nrm((DEC_BATCH, DEC_SEQ, D), 1.0)
    inp["p_prompt"] = nrm((DEPTH, BATCH, SEQ, PLE_DIM), 1.0)
    inp["p_sample"] = nrm((DEPTH, DEC_BATCH, DEC_SEQ, PLE_DIM), 1.0)
    inp["a_w_in"] = nrm((N_LAYERS_A, D, 3 * A_HEADS * 2 * A_HEAD_DIM), D ** -0.5)
    inp["a_lambda"] = nrm((N_LAYERS_A, 4, A_HEAD_DIM), 0.1)
    inp["a_subln"] = gain((N_LAYERS_A, 2 * A_HEAD_DIM))
    inp["a_w_out"] = nrm((N_LAYERS_A, A_HEADS * 2 * A_HEAD_DIM, D), BETA * (A_HEADS * 2 * A_HEAD_DIM) ** -0.5)
    inp["b_w_in"] = nrm((N_LAYERS_B, D, B_Q_LORA + B_KV_LORA + B_ROPE), D ** -0.5)
    inp["b_q_norm"] = gain((N_LAYERS_B, B_Q_LORA))
    inp["b_kv_norm"] = gain((N_LAYERS_B, B_KV_LORA))
    inp["b_w_uq"] = nrm((N_LAYERS_B, B_Q_LORA, B_HEADS * (B_NOPE + B_ROPE)), B_Q_LORA ** -0.5)
    inp["b_w_ukv"] = nrm((N_LAYERS_B, B_KV_LORA, B_HEADS * (B_NOPE + B_VDIM)), B_KV_LORA ** -0.5)
    inp["b_w_out"] = nrm((N_LAYERS_B, B_HEADS * B_VDIM, D), BETA * (B_HEADS * B_VDIM) ** -0.5)
    inp["c_w_qkv"] = nrm((N_LAYERS_C, D, 3 * C_HEADS * C_HEAD_DIM), D ** -0.5)
    inp["c_rpb"] = nrm((N_LAYERS_C, C_HEADS, 2 * C_WIN_ROWS - 1, 2 * C_WIN_COLS - 1), 0.1)
    inp["c_w_out"] = nrm((N_LAYERS_C, C_HEADS * C_HEAD_DIM, D), BETA * (C_HEADS * C_HEAD_DIM) ** -0.5)
    inp["d_w_qkv"] = nrm((N_LAYERS_D, D, 3 * D_NG * D_HEADS * D_HEAD_DIM), D ** -0.5)
    inp["d_w_out"] = nrm((N_LAYERS_D, D_HEADS * D_HEAD_DIM, D), BETA * (D_HEADS * D_HEAD_DIM) ** -0.5)
    inp["router_w"] = nrm((D, N_EXPERTS), D ** -0.5)
    inp["router_b"] = nrm((N_EXPERTS,), 0.01)
    inp["moe_w_gate"] = nrm((DEPTH, N_EXPERTS, D, EXPERT_FF), D ** -0.5)
    inp["moe_w_up"] = nrm((DEPTH, N_EXPERTS, D, EXPERT_FF), D ** -0.5)
    inp["moe_w_down"] = nrm((DEPTH, N_EXPERTS, EXPERT_FF, D), BETA * EXPERT_FF ** -0.5)
    inp["moe_ws_gate"] = nrm((DEPTH, D, SHARED_FF), D ** -0.5)
    inp["moe_ws_up"] = nrm((DEPTH, D, SHARED_FF), D ** -0.5)
    inp["moe_ws_down"] = nrm((DEPTH, SHARED_FF, D), BETA * SHARED_FF ** -0.5)
    inp["ln1_g"] = gain((DEPTH, D))
    inp["ln1_b"] = nrm((DEPTH, D), 0.02)
    inp["ln2_g"] = gain((DEPTH, D))
    inp["ln2_b"] = nrm((DEPTH, D), 0.02)
    inp["ple_w_in"] = nrm((DEPTH, PLE_DIM, D), PLE_DIM ** -0.5)
    inp["ple_w_gate"] = nrm((DEPTH, D, D), D ** -0.5)
    return inp


def reference(x_prompt, x_sample, p_prompt, p_sample,
              a_w_in, a_lambda, a_subln, a_w_out,
              b_w_in, b_q_norm, b_kv_norm, b_w_uq, b_w_ukv, b_w_out,
              c_w_qkv, c_rpb, c_w_out,
              d_w_qkv, d_w_out,
              router_w, router_b, moe_w_gate, moe_w_up, moe_w_down,
              moe_ws_gate, moe_ws_up, moe_ws_down,
              ln1_g, ln1_b, ln2_g, ln2_b,
              ple_w_in, ple_w_gate):
    prm = dict(a_w_in=a_w_in, a_lambda=a_lambda, a_subln=a_subln, a_w_out=a_w_out,
               b_w_in=b_w_in, b_q_norm=b_q_norm, b_kv_norm=b_kv_norm, b_w_uq=b_w_uq,
               b_w_ukv=b_w_ukv, b_w_out=b_w_out,
               c_w_qkv=c_w_qkv, c_rpb=c_rpb, c_w_out=c_w_out,
               d_w_qkv=d_w_qkv, d_w_out=d_w_out,
               router_w=router_w, router_b=router_b, moe_w_gate=moe_w_gate, moe_w_up=moe_w_up,
               moe_w_down=moe_w_down, moe_ws_gate=moe_ws_gate, moe_ws_up=moe_ws_up,
               moe_ws_down=moe_ws_down,
               ln1_g=ln1_g, ln1_b=ln1_b, ln2_g=ln2_g, ln2_b=ln2_b,
               ple_w_in=ple_w_in, ple_w_gate=ple_w_gate)
    y_prompt = encoder_trunk(x_prompt, p_prompt, prm)
    y_sample = encoder_trunk(x_sample, p_sample, prm)
    return (y_prompt, y_sample)
```

```python
import functools
import math

import numpy as np
import jax
import jax.numpy as jnp
from jax import lax
from jax.experimental import pallas as pl
from jax.experimental.pallas import tpu as pltpu

BF = jnp.bfloat16
F32 = jnp.float32

VMEM_LIMIT_BYTES = 56 * 1024 * 1024
LANES = 128

GRID_W = 64
LN_EPS = 1e-5
RMS_EPS = 1e-6
ROPE_THETA = 10000.0
NEG = -1e30

A_HEADS, A_HEAD_DIM = 8, 64
B_HEADS, B_NOPE, B_ROPE, B_VDIM, B_Q_LORA, B_KV_LORA = 16, 64, 32, 64, 384, 256
C_HEADS, C_HEAD_DIM, C_WIN_ROWS, C_WIN_COLS = 16, 64, 8, 16
D_HEADS, D_HEAD_DIM = 8, 64
D_GROUPS = ((128, 1), (512, 4), (2048, 16))
N_EXPERTS, N_EXPERT_GROUPS, EXPERT_FF = 16, 4, 256

NT_DIMS = (((1,), (1,)), ((), ()))


def _cparams(*sem):
    return pltpu.CompilerParams(dimension_semantics=sem, vmem_limit_bytes=VMEM_LIMIT_BYTES)


def _layer_norm_rows(z, g, b):
    mu = jnp.mean(z, axis=-1, keepdims=True)
    zc = z - mu
    var = jnp.mean(zc * zc, axis=-1, keepdims=True)
    return zc * lax.rsqrt(var + LN_EPS) * g + b


def _rms_rows(z, g):
    return z * lax.rsqrt(jnp.mean(z * z, axis=-1, keepdims=True) + RMS_EPS) * g


def _linear_body(x_ref, w_ref, o_ref):
    o_ref[...] = jnp.dot(x_ref[...].astype(BF), w_ref[...], preferred_element_type=F32).astype(o_ref.dtype)


def _linear(x, w, out_dtype, tm=512, tn=None):
    m, k = x.shape
    n = w.shape[1]
    tn = n if tn is None else tn
    return pl.pallas_call(
        _linear_body,
        grid=(n // tn, m // tm),
        in_specs=[pl.BlockSpec((tm, k), lambda j, i: (i, 0)), pl.BlockSpec((k, tn), lambda j, i: (0, j))],
        out_specs=pl.BlockSpec((tm, tn), lambda j, i: (i, j)),
        out_shape=jax.ShapeDtypeStruct((m, n), out_dtype),
        compiler_params=_cparams("parallel", "parallel"),
        name="linear",
    )(x, w)


def _post_body(o_ref, x_ref, w_ref, g_ref, b_ref, y_ref, *, alpha):
    h = jnp.dot(o_ref[...], w_ref[...], preferred_element_type=F32)
    y_ref[...] = _layer_norm_rows(alpha * x_ref[...] + h, g_ref[...], b_ref[...])


def _post_attn(o, x, w_out, g, b, alpha, tm=512):
    t, ko = o.shape
    d = x.shape[1]
    return pl.pallas_call(
        functools.partial(_post_body, alpha=alpha),
        grid=(t // tm,),
        in_specs=[
            pl.BlockSpec((tm, ko), lambda i: (i, 0)),
            pl.BlockSpec((tm, d), lambda i: (i, 0)),
            pl.BlockSpec((ko, d), lambda i: (0, 0)),
            pl.BlockSpec((1, d), lambda i: (0, 0)),
            pl.BlockSpec((1, d), lambda i: (0, 0)),
        ],
        out_specs=pl.BlockSpec((tm, d), lambda i: (i, 0)),
        out_shape=jax.ShapeDtypeStruct((t, d), F32),
        compiler_params=_cparams("parallel"),
        name="post_attn",
    )(o, x, w_out, g.reshape(1, d), b.reshape(1, d))


def _flash_keys_major(q_m, k_view, vt_view, n_chunks, tk, bias_fn):
    tq = q_m.shape[0]
    dv = vt_view.shape[0]

    def body(c, carry):
        m, l, acc = carry
        off = pl.multiple_of(c * tk, tk)
        k_c = k_view[pl.ds(off, tk), :]
        s = lax.dot_general(k_c, q_m, NT_DIMS, preferred_element_type=F32)
        if bias_fn is not None:
            s = s + bias_fn(c)
        m_new = jnp.maximum(m, jnp.max(s, axis=0, keepdims=True))
        p = jnp.exp(s - m_new)
        a = jnp.exp(m - m_new)
        l = a * l + jnp.sum(p, axis=0, keepdims=True)
        v_c = vt_view[:, pl.ds(off, tk)]
        acc = a * acc + jnp.dot(v_c, p.astype(BF), preferred_element_type=F32)
        return m_new, l, acc

    init = (jnp.full((1, tq), NEG, F32), jnp.zeros((1, tq), F32), jnp.zeros((dv, tq), F32))
    _, l, acc = lax.fori_loop(0, n_chunks, body, init)
    return acc / l


def _attn_a_body(slopes_ref, lam_ref, g_ref, q_ref, k_ref, vt_ref, o_ref, *, seq, tq, tk, lam_init):
    h = pl.program_id(1)
    qi = pl.program_id(2)
    slope = slopes_ref[h]
    q = q_ref[0]
    lane = lax.broadcasted_iota(jnp.int32, q.shape, 1)
    rel = lax.broadcasted_iota(jnp.int32, (tk, tq), 0) - lax.broadcasted_iota(jnp.int32, (tk, tq), 1)

    def bias_fn(c):
        dist = jnp.abs(rel + (c * tk - qi * tq)).astype(F32)
        return -slope * dist

    k_view = k_ref.at[0]
    vt_view = vt_ref.at[0, 0]
    outs = []
    for half in range(2):
        q_m = jnp.where((lane >= 64 * half) & (lane < 64 * (half + 1)), q, jnp.zeros_like(q))
        outs.append(_flash_keys_major(q_m, k_view, vt_view, seq // tk, tk, bias_fn))
    lf = lam_ref[...]
    lam_full = (jnp.exp(jnp.sum(lf[0:1] * lf[1:2], axis=-1, keepdims=True))
                - jnp.exp(jnp.sum(lf[2:3] * lf[3:4], axis=-1, keepdims=True)) + lam_init)
    o = outs[0] - lam_full * outs[1]
    ms = jnp.mean(o * o, axis=0, keepdims=True)
    o = o * lax.rsqrt(ms + RMS_EPS) * g_ref[...] * (1.0 - lam_init)
    o_ref[0] = o.T.astype(o_ref.dtype)


def _attn_a(qkv, vt, lam, subln_g, lam_init, tq=256, tk=512):
    bsz, seq, _ = qkv.shape
    tq, tk = min(tq, seq), min(tk, seq)
    slopes = jnp.asarray(2.0 ** (-8.0 * np.arange(1, A_HEADS + 1) / A_HEADS), dtype=F32)
    body = functools.partial(_attn_a_body, seq=seq, tq=tq, tk=tk, lam_init=lam_init)
    return pl.pallas_call(
        body,
        grid=(bsz, A_HEADS, seq // tq),
        in_specs=[
            pl.BlockSpec(memory_space=pltpu.SMEM),
            pl.BlockSpec((4, A_HEAD_DIM), lambda b, h, i: (0, 0)),
            pl.BlockSpec((2 * A_HEAD_DIM, 1), lambda b, h, i: (0, 0)),
            pl.BlockSpec((1, tq, LANES), lambda b, h, i: (b, i, h)),
            pl.BlockSpec((1, seq, LANES), lambda b, h, i: (b, 0, A_HEADS + h)),
            pl.BlockSpec((1, 1, LANES, seq), lambda b, h, i: (b, h, 0, 0)),
        ],
        out_specs=pl.BlockSpec((1, tq, LANES), lambda b, h, i: (b, i, h)),
        out_shape=jax.ShapeDtypeStruct((bsz, seq, A_HEADS * LANES), BF),
        compiler_params=_cparams("parallel", "parallel", "arbitrary"),
        name="attn_a",
    )(slopes, lam.astype(F32), subln_g.astype(F32).reshape(2 * A_HEAD_DIM, 1), qkv, qkv, vt)


def _attn_b_body(q_ref, k_ref, vt_ref, o_ref, *, seq, tk):
    outs = []
    for j in range(2):
        q_m = q_ref[0, :, j * LANES:(j + 1) * LANES]
        k_view = k_ref.at[0, :, pl.ds(j * LANES, LANES)]
        vt_view = vt_ref.at[0, 0, pl.ds(j * B_VDIM, B_VDIM), :]
        outs.append(_flash_keys_major(q_m, k_view, vt_view, seq // tk, tk, None))
    o = jnp.concatenate(outs, axis=0)
    o_ref[0] = o.T.astype(o_ref.dtype)


def _attn_b(q, k, vt, tq=256, tk=512):
    bsz, seq, _ = q.shape
    tq, tk = min(tq, seq), min(tk, seq)
    npair = B_HEADS // 2
    return pl.pallas_call(
        functools.partial(_attn_b_body, seq=seq, tk=tk),
        grid=(bsz, npair, seq // tq),
        in_specs=[
            pl.BlockSpec((1, tq, 2 * LANES), lambda b, h, i: (b, i, h)),
            pl.BlockSpec((1, seq, 2 * LANES), lambda b, h, i: (b, 0, h)),
            pl.BlockSpec((1, 1, LANES, seq), lambda b, h, i: (b, h, 0, 0)),
        ],
        out_specs=pl.BlockSpec((1, tq, LANES), lambda b, h, i: (b, i, h)),
        out_shape=jax.ShapeDtypeStruct((bsz, seq, B_HEADS * B_VDIM), BF),
        compiler_params=_cparams("parallel", "parallel", "arbitrary"),
        name="attn_b",
    )(q, k, vt)


def _mla_in_body(x_ref, w_ref, gq_ref, gkv_ref, rc_ref, rs_ref, cq_ref, ckv_ref, kr_ref):
    h = jnp.dot(x_ref[...].astype(BF), w_ref[...], preferred_element_type=F32)
    cq_ref[...] = _rms_rows(h[:, :B_Q_LORA], gq_ref[...]).astype(BF)
    ckv_ref[...] = _rms_rows(h[:, B_Q_LORA:B_Q_LORA + B_KV_LORA], gkv_ref[...]).astype(BF)
    t = h[:, B_Q_LORA + B_KV_LORA:]
    kr = t * rc_ref[...] + pltpu.roll(t, LANES - B_ROPE, 1) * rs_ref[...]
    kr_ref[...] = kr.astype(BF)


def _mla_q_body(cq_ref, w_ref, ta_ref, tb_ref, q_ref):
    t = jnp.dot(cq_ref[...], w_ref[...], preferred_element_type=F32)
    ta = ta_ref[...]
    tb = tb_ref[...]
    for h in range(B_HEADS):
        th = t[:, h * LANES:(h + 1) * LANES]
        q_ref[:, h * LANES:(h + 1) * LANES] = (th * ta + pltpu.roll(th, LANES - B_ROPE, 1) * tb).astype(BF)


def _mla_kv_body(ckv_ref, kr_ref, wk_ref, e_ref, wv_ref, k_ref, v_ref):
    ckv = ckv_ref[...]
    k = jnp.dot(ckv, wk_ref[...], preferred_element_type=F32)
    k = k + jnp.dot(kr_ref[...], e_ref[...], preferred_element_type=F32)
    k_ref[...] = k.astype(BF)
    v_ref[...] = jnp.dot(ckv, wv_ref[...], preferred_element_type=F32).astype(BF)


def _rope_partner(w):
    half = B_ROPE // 2
    return jnp.concatenate([-w[..., half:], w[..., :half]], axis=-1)


def _mla_tables(seq):
    inv = 1.0 / (ROPE_THETA ** (np.arange(0, B_ROPE, 2, dtype=np.float32) / B_ROPE))
    ang = jnp.arange(seq, dtype=F32)[:, None] * jnp.asarray(inv, F32)[None, :]
    cos = jnp.concatenate([jnp.cos(ang), jnp.cos(ang)], axis=-1)
    sin = jnp.concatenate([jnp.sin(ang), jnp.sin(ang)], axis=-1)
    z = lambda n: jnp.zeros((seq, n), F32)
    rc = jnp.concatenate([cos, z(LANES - B_ROPE)], axis=-1)
    rs = jnp.concatenate([sin, z(LANES - B_ROPE)], axis=-1)
    scale = (B_NOPE + B_ROPE) ** -0.5
    ta = jnp.concatenate([jnp.full((seq, B_NOPE), scale, F32), cos * scale, z(B_ROPE)], axis=-1)
    tb = jnp.concatenate([z(B_NOPE), sin * scale, z(B_ROPE)], axis=-1)
    return rc, rs, ta, tb


def _mla_weights(w_in, w_uq, w_ukv):
    d = w_in.shape[0]
    kr0 = B_Q_LORA + B_KV_LORA
    w_in_ext = jnp.concatenate(
        [w_in, _rope_partner(w_in[:, kr0:kr0 + B_ROPE]), jnp.zeros((d, LANES - 2 * B_ROPE), w_in.dtype)], axis=-1)
    wq = w_uq.reshape(B_Q_LORA, B_HEADS, B_NOPE + B_ROPE)
    wq_ext = jnp.concatenate([wq, _rope_partner(wq[..., B_NOPE:])], axis=-1).reshape(B_Q_LORA, B_HEADS * LANES)
    wkv = w_ukv.reshape(B_KV_LORA, B_HEADS, B_NOPE + B_VDIM)
    wk = jnp.concatenate([wkv[..., :B_NOPE], jnp.zeros((B_KV_LORA, B_HEADS, LANES - B_NOPE), w_ukv.dtype)], axis=-1)
    wk = wk.reshape(B_KV_LORA, B_HEADS * LANES)
    wv = wkv[..., B_NOPE:].reshape(B_KV_LORA, B_HEADS * B_VDIM)
    place = np.zeros((LANES, B_HEADS, LANES), np.float32)
    for j in range(B_ROPE):
        place[j, :, B_NOPE + j] = 1.0
    place = jnp.asarray(place.reshape(LANES, B_HEADS * LANES), BF)
    return w_in_ext.astype(BF), wq_ext.astype(BF), wk.astype(BF), place, wv.astype(BF)


def _mixer_b(x, w_in, q_norm_g, kv_norm_g, w_uq, w_ukv, tm=512):
    bsz, seq, d = x.shape
    t = bsz * seq
    tm = min(tm, seq)
    nblk = seq // tm
    w_in_ext, wq_ext, wk, place, wv = _mla_weights(w_in, w_uq, w_ukv)
    rc, rs, ta, tb = _mla_tables(seq)
    x2 = x.reshape(t, d)
    n_in = w_in_ext.shape[1]
    row = lambda i: (i, 0)
    fixed = lambda i: (0, 0)
    pos = lambda i: (i % nblk, 0)
    cq, ckv, kr = pl.pallas_call(
        _mla_in_body,
        grid=(t // tm,),
        in_specs=[
            pl.BlockSpec((tm, d), row), pl.BlockSpec((d, n_in), fixed),
            pl.BlockSpec((1, B_Q_LORA), fixed), pl.BlockSpec((1, B_KV_LORA), fixed),
            pl.BlockSpec((tm, LANES), pos), pl.BlockSpec((tm, LANES), pos),
        ],
        out_specs=[pl.BlockSpec((tm, B_Q_LORA), row), pl.BlockSpec((tm, B_KV_LORA), row),
                   pl.BlockSpec((tm, LANES), row)],
        out_shape=[jax.ShapeDtypeStruct((t, B_Q_LORA), BF), jax.ShapeDtypeStruct((t, B_KV_LORA), BF),
                   jax.ShapeDtypeStruct((t, LANES), BF)],
        compiler_params=_cparams("parallel"),
        name="mla_in",
    )(x2, w_in_ext, q_norm_g.astype(F32).reshape(1, -1), kv_norm_g.astype(F32).reshape(1, -1), rc, rs)
    nq = B_HEADS * LANES
    q = pl.pallas_call(
        _mla_q_body,
        grid=(t // tm,),
        in_specs=[pl.BlockSpec((tm, B_Q_LORA), row), pl.BlockSpec((B_Q_LORA, nq), fixed),
                  pl.BlockSpec((tm, LANES), pos), pl.BlockSpec((tm, LANES), pos)],
        out_specs=pl.BlockSpec((tm, nq), row),
        out_shape=jax.ShapeDtypeStruct((t, nq), BF),
        compiler_params=_cparams("parallel"),
        name="mla_q",
    )(cq, wq_ext, ta, tb)
    nv = B_HEADS * B_VDIM
    k, v = pl.pallas_call(
        _mla_kv_body,
        grid=(t // tm,),
        in_specs=[pl.BlockSpec((tm, B_KV_LORA), row), pl.BlockSpec((tm, LANES), row),
                  pl.BlockSpec((B_KV_LORA, nq), fixed), pl.BlockSpec((LANES, nq), fixed),
                  pl.BlockSpec((B_KV_LORA, nv), fixed)],
        out_specs=[pl.BlockSpec((tm, nq), row), pl.BlockSpec((tm, nv), row)],
        out_shape=[jax.ShapeDtypeStruct((t, nq), BF), jax.ShapeDtypeStruct((t, nv), BF)],
        compiler_params=_cparams("parallel"),
        name="mla_kv",
    )(ckv, kr, wk, place, wv)
    vt = v.reshape(bsz, seq, B_HEADS // 2, LANES).transpose(0, 2, 3, 1)
    o = _attn_b(q.reshape(bsz, seq, nq), k.reshape(bsz, seq, nq), vt)
    return o.reshape(t, nv)


def _mixer_a(x, w_in, lam, subln_g, layer_idx):
    bsz, seq, d = x.shape
    t = bsz * seq
    hd2 = 2 * A_HEAD_DIM
    nq = A_HEADS * hd2
    scale = A_HEAD_DIM ** -0.5
    w = jnp.concatenate([w_in[:, :nq] * scale, w_in[:, nq:]], axis=-1).astype(BF)
    qkv = _linear(x.reshape(t, d), w, BF, tn=nq).reshape(bsz, seq, 3 * nq)
    vt = qkv[:, :, 2 * nq:].reshape(bsz, seq, A_HEADS, hd2).transpose(0, 2, 3, 1)
    lam_init = 0.8 - 0.6 * math.exp(-0.3 * layer_idx)
    o = _attn_a(qkv, vt, lam, subln_g, lam_init)
    return o.reshape(t, nq)


def _attn_c_body(q_ref, k0, k1, k2, k3, v0, v1, v2, v3, bt_ref, o_ref, kcat, vcat, *, rows):
    g = pl.program_id(2)
    qb = C_WIN_ROWS * GRID_W // 2
    for j, (kr, vr) in enumerate(((k0, v0), (k1, v1), (k2, v2), (k3, v3))):
        kcat[j * qb:(j + 1) * qb, :] = kr[0]
        vcat[j * qb:(j + 1) * qb, :] = vr[0]
    lane = lax.broadcasted_iota(jnp.int32, (GRID_W, LANES), 1)
    nkeys = C_WIN_ROWS * GRID_W

    def row(t, carry):
        r = g * C_WIN_ROWS + t
        rs = jnp.clip(r - C_WIN_ROWS // 2, 0, rows - C_WIN_ROWS)
        u0 = rs - (g * C_WIN_ROWS - C_WIN_ROWS // 2)
        base = u0 - t + (C_WIN_ROWS // 2 - 1)
        koff = pl.multiple_of(u0 * GRID_W, GRID_W)
        kw = kcat[pl.ds(koff, nkeys), :]
        vw = vcat[pl.ds(koff, nkeys), :]
        qoff = pl.multiple_of(t * GRID_W, GRID_W)
        qt = q_ref[0, pl.ds(qoff, GRID_W), :]
        outs = []
        for hh in range(2):
            qm = jnp.where((lane >= 64 * hh) & (lane < 64 * (hh + 1)), qt, jnp.zeros_like(qt))
            s = lax.dot_general(qm, kw, NT_DIMS, preferred_element_type=F32) + bt_ref[hh, base]
            m = jnp.max(s, axis=1, keepdims=True)
            p = jnp.exp(s - m)
            l = jnp.sum(p, axis=1, keepdims=True)
            outs.append(jnp.dot(p.astype(BF), vw, preferred_element_type=F32) / l)
        o = jnp.where(lane < 64, outs[0], outs[1])
        o_ref[0, pl.ds(qoff, GRID_W), :] = o.astype(o_ref.dtype)
        return carry

    lax.fori_loop(0, C_WIN_ROWS, row, 0)


def _nbr_bias_table(rpb):
    col = np.arange(GRID_W)
    col_start = np.clip(col - C_WIN_COLS // 2, 0, GRID_W - C_WIN_COLS)
    col_mask = (col[None, :] >= col_start[:, None]) & (col[None, :] < col_start[:, None] + C_WIN_COLS)
    dc_idx = np.clip(col[None, :] - col[:, None] + C_WIN_COLS - 1, 0, 2 * C_WIN_COLS - 2)
    dr = np.arange(C_WIN_ROWS)[:, None] + np.arange(C_WIN_ROWS)[None, :]
    bt = rpb.astype(F32)[:, dr[:, None, :, None], dc_idx[None, :, None, :]]
    bt = jnp.where(jnp.asarray(col_mask)[None, None, :, None, :], bt, NEG)
    return bt.reshape(rpb.shape[0], C_WIN_ROWS, GRID_W, C_WIN_ROWS * GRID_W)


def _mixer_c(x, w_qkv, rpb):
    bsz, seq, d = x.shape
    t = bsz * seq
    rows = seq // GRID_W
    nq = C_HEADS * C_HEAD_DIM
    scale = C_HEAD_DIM ** -0.5
    w = jnp.concatenate([w_qkv[:, :nq] * scale, w_qkv[:, nq:]], axis=-1).astype(BF)
    qkv = _linear(x.reshape(t, d), w, BF, tn=nq).reshape(bsz, seq, 3 * nq)
    bt = _nbr_bias_table(rpb)
    npair = C_HEADS // 2
    qtok = C_WIN_ROWS * GRID_W
    kb = qtok // 2
    nkb = seq // kb
    kspecs = []
    for off in (npair, 2 * npair):
        for j in range(4):
            kspecs.append(pl.BlockSpec(
                (1, kb, LANES),
                lambda h, b, g, j=j, off=off: (b, jnp.clip(2 * g - 1 + j, 0, nkb - 1), off + h)))
    o = pl.pallas_call(
        functools.partial(_attn_c_body, rows=rows),
        grid=(npair, bsz, seq // qtok),
        in_specs=[pl.BlockSpec((1, qtok, LANES), lambda h, b, g: (b, g, h))] + kspecs + [
            pl.BlockSpec((2, C_WIN_ROWS, GRID_W, qtok), lambda h, b, g: (h, 0, 0, 0))],
        out_specs=pl.BlockSpec((1, qtok, LANES), lambda h, b, g: (b, g, h)),
        out_shape=jax.ShapeDtypeStruct((bsz, seq, nq), BF),
        scratch_shapes=[pltpu.VMEM((2 * qtok, LANES), BF), pltpu.VMEM((2 * qtok, LANES), BF)],
        compiler_params=_cparams("parallel", "parallel", "arbitrary"),
        name="attn_c",
    )(qkv, *([qkv] * 8), bt)
    return o.reshape(t, nq)


def _attn_d_body(slopes_ref, q_ref, kp, kc, kn, vp, vc, vn, o_ref, lse_ref, *, tq, length, dil, rad):
    hp = pl.program_id(2)
    i = pl.program_id(3)
    q = q_ref[0]
    kw = jnp.concatenate([kp[0, tq - rad:tq, :], kc[0], kn[0, 0:rad, :]], axis=0)
    vw = jnp.concatenate([vp[0, tq - rad:tq, :], vc[0], vn[0, 0:rad, :]], axis=0)
    nk = tq + 2 * rad
    qi = lax.broadcasted_iota(jnp.int32, (tq, nk), 0)
    kk = lax.broadcasted_iota(jnp.int32, (tq, nk), 1)
    dist = jnp.abs(kk - rad - qi)
    ki = i * tq - rad + kk
    valid = (dist <= rad) & (ki >= 0) & (ki < length)
    distf = dist.astype(F32) * float(dil)
    lane = lax.broadcasted_iota(jnp.int32, (tq, LANES), 1)
    outs, lses = [], []
    for hh in range(2):
        slope = slopes_ref[hp * 2 + hh]
        qm = jnp.where((lane >= 64 * hh) & (lane < 64 * (hh + 1)), q, jnp.zeros_like(q))
        s = lax.dot_general(qm, kw, NT_DIMS, preferred_element_type=F32)
        s = jnp.where(valid, s - slope * distf, NEG)
        m = jnp.max(s, axis=1, keepdims=True)
        p = jnp.exp(s - m)
        l = jnp.sum(p, axis=1, keepdims=True)
        outs.append(jnp.dot(p.astype(BF), vw, preferred_element_type=F32) / l)
        lses.append(m + jnp.log(l))
    o_ref[0] = jnp.where(lane < 64, outs[0], outs[1]).astype(o_ref.dtype)
    lse_ref[0] = jnp.where(lane < 64, lses[0], lses[1])


def _attn_d_group(qkv, g, window, dil, tq=256):
    bsz, seq, ncol = qkv.shape
    rad = window // (2 * dil)
    length = seq // dil
    tq = min(tq, length)
    nq = length // tq
    nslot = ncol // LANES
    npair = D_HEADS // 2
    ng = len(D_GROUPS)
    view = qkv.reshape(bsz, length, dil * ncol)
    slopes = jnp.asarray(2.0 ** (-8.0 * np.arange(1, D_HEADS + 1) / D_HEADS), dtype=F32)

    def spec(which, shift):
        def imap(b, c, h, i):
            blk = jnp.clip(i + shift, 0, nq - 1)
            return (b, blk, c * nslot + which * ng * npair + g * npair + h)
        return pl.BlockSpec((1, tq, LANES), imap)

    ospec = pl.BlockSpec((1, tq, LANES), lambda b, c, h, i: (b, i, c * npair + h))
    o, lse = pl.pallas_call(
        functools.partial(_attn_d_body, tq=tq, length=length, dil=dil, rad=rad),
        grid=(bsz, dil, npair, nq),
        in_specs=[pl.BlockSpec(memory_space=pltpu.SMEM), spec(0, 0),
                  spec(1, -1), spec(1, 0), spec(1, 1), spec(2, -1), spec(2, 0), spec(2, 1)],
        out_specs=[ospec, ospec],
        out_shape=[jax.ShapeDtypeStruct((bsz, length, dil * npair * LANES), BF),
                   jax.ShapeDtypeStruct((bsz, length, dil * npair * LANES), F32)],
        compiler_params=_cparams("parallel", "parallel", "parallel", "arbitrary"),
        name=f"attn_d{g}",
    )(slopes, view, view, view, view, view, view, view)
    nh = npair * LANES
    return o.reshape(bsz * seq, nh), lse.reshape(bsz * seq, nh)


def _post_d_body(o0, o1, o2, l0, l1, l2, x_ref, w_ref, g_ref, b_ref, y_ref, *, alpha):
    a0, a1, a2 = l0[...], l1[...], l2[...]
    m = jnp.maximum(jnp.maximum(a0, a1), a2)
    e0, e1, e2 = jnp.exp(a0 - m), jnp.exp(a1 - m), jnp.exp(a2 - m)
    den = e0 + e1 + e2
    o = ((e0 / den) * o0[...].astype(F32) + (e1 / den) * o1[...].astype(F32)
         + (e2 / den) * o2[...].astype(F32))
    h = jnp.dot(o.astype(BF), w_ref[...], preferred_element_type=F32)
    y_ref[...] = _layer_norm_rows(alpha * x_ref[...] + h, g_ref[...], b_ref[...])


def _mixer_d_and_post(x2, bsz, seq, w_qkv, w_out, g, b, alpha, tm=512):
    t, d = x2.shape
    nh = D_HEADS * D_HEAD_DIM
    ng = len(D_GROUPS)
    scale = D_HEAD_DIM ** -0.5
    w = jnp.concatenate([w_qkv[:, :ng * nh] * scale, w_qkv[:, ng * nh:]], axis=-1).astype(BF)
    qkv = _linear(x2, w, BF, tn=ng * nh).reshape(bsz, seq, 3 * ng * nh)
    os_, ls_ = [], []
    for gi, (window, dil) in enumerate(D_GROUPS):
        o, lse = _attn_d_group(qkv, gi, window, dil)
        os_.append(o)
        ls_.append(lse)
    row = lambda i: (i, 0)
    fixed = lambda i: (0, 0)
    return pl.pallas_call(
        functools.partial(_post_d_body, alpha=alpha),
        grid=(t // tm,),
        in_specs=[pl.BlockSpec((tm, nh), row)] * 6 + [
            pl.BlockSpec((tm, d), row), pl.BlockSpec((nh, d), fixed),
            pl.BlockSpec((1, d), fixed), pl.BlockSpec((1, d), fixed)],
        out_specs=pl.BlockSpec((tm, d), row),
        out_shape=jax.ShapeDtypeStruct((t, d), F32),
        compiler_params=_cparams("parallel"),
        name="post_d",
    )(*os_, *ls_, x2, w_out.astype(BF), g.reshape(1, d), b.reshape(1, d))


def _router_gates(x, wh_ref, wl_ref, rb_ref):
    tm = x.shape[0]
    xh = x.astype(BF)
    xl = (x - xh.astype(F32)).astype(BF)
    wh = wh_ref[...]
    logits = (lax.dot_general(wh, xh, NT_DIMS, preferred_element_type=F32)
              + lax.dot_general(wh, xl, NT_DIMS, preferred_element_type=F32)
              + lax.dot_general(wl_ref[...], xh, NT_DIMS, preferred_element_type=F32))
    scores = jax.nn.sigmoid(logits)
    biased = scores + rb_ref[...]
    epg = N_EXPERTS // N_EXPERT_GROUPS
    sc = [scores[e:e + 1, :] for e in range(N_EXPERTS)]
    bi = [biased[e:e + 1, :] for e in range(N_EXPERTS)]
    gs = []
    for g in range(N_EXPERT_GROUPS):
        v = bi[g * epg:(g + 1) * epg]
        best = None
        for a in range(epg):
            for c in range(a + 1, epg):
                pair = v[a] + v[c]
                best = pair if best is None else jnp.maximum(best, pair)
        gs.append(best)
    gmax = functools.reduce(jnp.maximum, gs)
    taken = jnp.zeros((1, tm), jnp.bool_)
    cand = []
    for g in range(N_EXPERT_GROUPS):
        sel = (gs[g] == gmax) & jnp.logical_not(taken)
        taken = taken | sel
        for a in range(epg):
            cand.append(jnp.where(sel, bi[g * epg + a], -jnp.inf))
    m1 = functools.reduce(jnp.maximum, cand)
    taken = jnp.zeros((1, tm), jnp.bool_)
    is1 = []
    for e in range(N_EXPERTS):
        hit = (cand[e] == m1) & jnp.logical_not(taken)
        taken = taken | hit
        is1.append(hit)
    cand2 = [jnp.where(is1[e], -jnp.inf, cand[e]) for e in range(N_EXPERTS)]
    m2 = functools.reduce(jnp.maximum, cand2)
    taken = jnp.zeros((1, tm), jnp.bool_)
    is2 = []
    for e in range(N_EXPERTS):
        hit = (cand2[e] == m2) & jnp.logical_not(taken)
        taken = taken | hit
        is2.append(hit)
    zero = jnp.zeros((1, tm), F32)
    w1 = functools.reduce(jnp.add, [jnp.where(is1[e], sc[e], zero) for e in range(N_EXPERTS)])
    w2 = functools.reduce(jnp.add, [jnp.where(is2[e], sc[e], zero) for e in range(N_EXPERTS)])
    den = w1 + w2
    rows = [jnp.where(is1[e], w1 / den, zero) + jnp.where(is2[e], w2 / den, zero) for e in range(N_EXPERTS)]
    rows.append(jnp.ones((1, tm), F32))
    rows.append(jnp.zeros((LANES - N_EXPERTS - 1, tm), F32))
    return jnp.concatenate(rows, axis=0).T


def _moe_body(x_ref, p_ref, wh_ref, wl_ref, rb_ref, wgu_ref, wd_ref, g_ref, b_ref, wpi_ref, wpg_ref,
              y_ref, gates, xb, acc, *, alpha, n_steps):
    e = pl.program_id(1)

    @pl.when(e == 0)
    def _():
        x = x_ref[...]
        gates[...] = _router_gates(x, wh_ref, wl_ref, rb_ref)
        xb[...] = x.astype(BF)
        acc[...] = jnp.zeros_like(acc)

    gu = jnp.dot(xb[...], wgu_ref[0], preferred_element_type=F32)
    hmid = jax.nn.silu(gu[:, :EXPERT_FF]) * gu[:, EXPERT_FF:]
    lane = lax.broadcasted_iota(jnp.int32, gates.shape, 1)
    gcol = jnp.sum(jnp.where(lane == e, gates[...], 0.0), axis=1, keepdims=True)
    acc[...] += jnp.dot((hmid * gcol).astype(BF), wd_ref[0], preferred_element_type=F32)

    @pl.when(e == n_steps - 1)
    def _():
        x2 = _layer_norm_rows(alpha * x_ref[...] + acc[...], g_ref[...], b_ref[...])
        emb = jnp.dot(p_ref[...].astype(BF), wpi_ref[...], preferred_element_type=F32)
        gate = jax.nn.sigmoid(jnp.dot(x2.astype(BF), wpg_ref[...], preferred_element_type=F32))
        y_ref[...] = x2 + gate * emb


def _moe_layer(x2, p2, router, wgu, wd, g, b, wpi, wpg, alpha, tm=1024):
    t, d = x2.shape
    n_steps = wgu.shape[0]
    wh_t, wl_t, rb = router
    tok = lambda i, e: (i, 0)
    fixed = lambda i, e: (0, 0)
    return pl.pallas_call(
        functools.partial(_moe_body, alpha=alpha, n_steps=n_steps),
        grid=(t // tm, n_steps),
        in_specs=[
            pl.BlockSpec((tm, d), tok), pl.BlockSpec((tm, p2.shape[1]), tok),
            pl.BlockSpec((N_EXPERTS, d), fixed), pl.BlockSpec((N_EXPERTS, d), fixed),
            pl.BlockSpec((N_EXPERTS, 1), fixed),
            pl.BlockSpec((1, d, 2 * EXPERT_FF), lambda i, e: (e, 0, 0)),
            pl.BlockSpec((1, EXPERT_FF, d), lambda i, e: (e, 0, 0)),
            pl.BlockSpec((1, d), fixed), pl.BlockSpec((1, d), fixed),
            pl.BlockSpec(wpi.shape, fixed), pl.BlockSpec(wpg.shape, fixed),
        ],
        out_specs=pl.BlockSpec((tm, d), tok),
        out_shape=jax.ShapeDtypeStruct((t, d), F32),
        scratch_shapes=[pltpu.VMEM((tm, LANES), F32), pltpu.VMEM((tm, d), BF), pltpu.VMEM((tm, d), F32)],
        compiler_params=_cparams("parallel", "arbitrary"),
        name="moe",
    )(x2, p2, wh_t, wl_t, rb, wgu, wd, g.reshape(1, d), b.reshape(1, d), wpi, wpg)


def _prep_shared(prm):
    depth = prm["ln1_g"].shape[0]
    rw = prm["router_w"].astype(F32)
    wh = rw.astype(BF)
    wl = (rw - wh.astype(F32)).astype(BF)
    router = (wh.T, wl.T, prm["router_b"].astype(F32).reshape(N_EXPERTS, 1))
    layers = []
    for i in range(depth):
        wgu = jnp.concatenate([prm["moe_w_gate"][i], prm["moe_w_up"][i]], axis=-1)
        wgu_s = jnp.concatenate([prm["moe_ws_gate"][i], prm["moe_ws_up"][i]], axis=-1)[None]
        wgu = jnp.concatenate([wgu, wgu_s], axis=0).astype(BF)
        wd = jnp.concatenate([prm["moe_w_down"][i], prm["moe_ws_down"][i][None]], axis=0).astype(BF)
        layers.append(dict(wgu=wgu, wd=wd, wpi=prm["ple_w_in"][i].astype(BF), wpg=prm["ple_w_gate"][i].astype(BF)))
    return router, layers


def _trunk(x, p, prm, router, layers):
    depth = prm["ln1_g"].shape[0]
    alpha = (2.0 * depth) ** 0.25
    bsz, seq, d = x.shape
    t = bsz * seq
    x2 = x.reshape(t, d)
    for i in range(depth):
        mixer, j = i % 4, i // 4
        xb = x2.reshape(bsz, seq, d)
        if mixer == 3:
            x2 = _mixer_d_and_post(x2, bsz, seq, prm["d_w_qkv"][j], prm["d_w_out"][j],
                                   prm["ln1_g"][i], prm["ln1_b"][i], alpha)
        else:
            if mixer == 0:
                o = _mixer_a(xb, prm["a_w_in"][j], prm["a_lambda"][j], prm["a_subln"][j], i)
                w_out = prm["a_w_out"][j]
            elif mixer == 1:
                o = _mixer_b(xb, prm["b_w_in"][j], prm["b_q_norm"][j], prm["b_kv_norm"][j],
                             prm["b_w_uq"][j], prm["b_w_ukv"][j])
                w_out = prm["b_w_out"][j]
            else:
                o = _mixer_c(xb, prm["c_w_qkv"][j], prm["c_rpb"][j])
                w_out = prm["c_w_out"][j]
            x2 = _post_attn(o, x2, w_out.astype(BF), prm["ln1_g"][i], prm["ln1_b"][i], alpha)
        lw = layers[i]
        x2 = _moe_layer(x2, p[i].reshape(t, -1), router, lw["wgu"], lw["wd"],
                        prm["ln2_g"][i], prm["ln2_b"][i], lw["wpi"], lw["wpg"], alpha)
    return x2.reshape(bsz, seq, d)


def kernel(x_prompt, x_sample, p_prompt, p_sample, a_w_in, a_lambda, a_subln, a_w_out, b_w_in, b_q_norm, b_kv_norm, b_w_uq, b_w_ukv, b_w_out, c_w_qkv, c_rpb, c_w_out, d_w_qkv, d_w_out, router_w, router_b, moe_w_gate, moe_w_up, moe_w_down, moe_ws_gate, moe_ws_up, moe_ws_down, ln1_g, ln1_b, ln2_g, ln2_b, ple_w_in, ple_w_gate):
    prm = dict(a_w_in=a_w_in, a_lambda=a_lambda, a_subln=a_subln, a_w_out=a_w_out,
               b_w_in=b_w_in, b_q_norm=b_q_norm, b_kv_norm=b_kv_norm, b_w_uq=b_w_uq,
               b_w_ukv=b_w_ukv, b_w_out=b_w_out,
               c_w_qkv=c_w_qkv, c_rpb=c_rpb, c_w_out=c_w_out,
               d_w_qkv=d_w_qkv, d_w_out=d_w_out,
               router_w=router_w, router_b=router_b, moe_w_gate=moe_w_gate, moe_w_up=moe_w_up,
               moe_w_down=moe_w_down, moe_ws_gate=moe_ws_gate, moe_ws_up=moe_ws_up,
               moe_ws_down=moe_ws_down,
               ln1_g=ln1_g, ln1_b=ln1_b, ln2_g=ln2_g, ln2_b=ln2_b,
               ple_w_in=ple_w_in, ple_w_gate=ple_w_gate)
    router, layers = _prep_shared(prm)
    y_prompt = _trunk(x_prompt, p_prompt, prm, router, layers)
    y_sample = _trunk(x_sample, p_sample, prm, router, layers)
    return (y_prompt, y_sample)
```

```python
import functools
import math

import numpy as np
import jax
import jax.numpy as jnp
from jax import lax
from jax.experimental import pallas as pl
from jax.experimental.pallas import tpu as pltpu

BF = jnp.bfloat16
F32 = jnp.float32

VMEM_LIMIT_BYTES = 56 * 1024 * 1024
LANES = 128

GRID_W = 64
LN_EPS = 1e-5
RMS_EPS = 1e-6
ROPE_THETA = 10000.0
NEG = -1e30

A_HEADS, A_HEAD_DIM = 8, 64
B_HEADS, B_NOPE, B_ROPE, B_VDIM, B_Q_LORA, B_KV_LORA = 16, 64, 32, 64, 384, 256
C_HEADS, C_HEAD_DIM, C_WIN_ROWS, C_WIN_COLS = 16, 64, 8, 16
D_HEADS, D_HEAD_DIM = 8, 64
D_GROUPS = ((128, 1), (512, 4), (2048, 16))
N_EXPERTS, N_EXPERT_GROUPS, EXPERT_FF = 16, 4, 256

NT_DIMS = (((1,), (1,)), ((), ()))


def _cparams(*sem):
    return pltpu.CompilerParams(dimension_semantics=sem, vmem_limit_bytes=VMEM_LIMIT_BYTES)


def _layer_norm_rows(z, g, b):
    mu = jnp.mean(z, axis=-1, keepdims=True)
    zc = z - mu
    var = jnp.mean(zc * zc, axis=-1, keepdims=True)
    return zc * lax.rsqrt(var + LN_EPS) * g + b


def _rms_rows(z, g):
    return z * lax.rsqrt(jnp.mean(z * z, axis=-1, keepdims=True) + RMS_EPS) * g


def _linear_body(x_ref, w_ref, o_ref):
    o_ref[...] = jnp.dot(x_ref[...].astype(BF), w_ref[...], preferred_element_type=F32).astype(o_ref.dtype)


def _linear(x, w, out_dtype, tm=512, tn=None):
    m, k = x.shape
    n = w.shape[1]
    tn = n if tn is None else tn
    return pl.pallas_call(
        _linear_body,
        grid=(n // tn, m // tm),
        in_specs=[pl.BlockSpec((tm, k), lambda j, i: (i, 0)), pl.BlockSpec((k, tn), lambda j, i: (0, j))],
        out_specs=pl.BlockSpec((tm, tn), lambda j, i: (i, j)),
        out_shape=jax.ShapeDtypeStruct((m, n), out_dtype),
        compiler_params=_cparams("parallel", "parallel"),
        name="linear",
    )(x, w)


def _post_body(o_ref, x_ref, w_ref, g_ref, b_ref, y_ref, *, alpha):
    h = jnp.dot(o_ref[...], w_ref[...], preferred_element_type=F32)
    y_ref[...] = _layer_norm_rows(alpha * x_ref[...] + h, g_ref[...], b_ref[...])


def _post_attn(o, x, w_out, g, b, alpha, tm=512):
    t, ko = o.shape
    d = x.shape[1]
    return pl.pallas_call(
        functools.partial(_post_body, alpha=alpha),
        grid=(t // tm,),
        in_specs=[
            pl.BlockSpec((tm, ko), lambda i: (i, 0)),
            pl.BlockSpec((tm, d), lambda i: (i, 0)),
            pl.BlockSpec((ko, d), lambda i: (0, 0)),
            pl.BlockSpec((1, d), lambda i: (0, 0)),
            pl.BlockSpec((1, d), lambda i: (0, 0)),
        ],
        out_specs=pl.BlockSpec((tm, d), lambda i: (i, 0)),
        out_shape=jax.ShapeDtypeStruct((t, d), F32),
        compiler_params=_cparams("parallel"),
        name="post_attn",
    )(o, x, w_out, g.reshape(1, d), b.reshape(1, d))


LOG2E = 1.4426950408889634
SUM_ROWS = 16


def _softmax_chunk(s, m, shift):
    mx = jnp.max(s, axis=0, keepdims=True)
    if shift is not None:
        mx = mx - shift
    m_new = jnp.maximum(m, mx)
    ref = m_new if shift is None else m_new + shift
    return m_new, jnp.exp2(s - ref).astype(BF)


def _pipelined_flash(n, nh, tq, qk_fn, sm_fn, pv_fn, s_buf, p_buf, acc_ref):
    assert n >= 4 and n % 2 == 0
    acc_ref[...] = jnp.zeros_like(acc_ref)

    def stage(j, slot, ms, alphas, do_pv=True, do_qk=True, first=False):
        if do_pv:
            for h in range(nh):
                acc_ref[h] = alphas[h] * acc_ref[h] + pv_fn(j - 1, h, p_buf[1 - slot, h])
        if do_qk:
            for h in range(nh):
                s_buf[1 - slot, h] = qk_fn(j + 1, h)
        new_ms, new_alphas = [], []
        for h in range(nh):
            m_new, p = sm_fn(j, h, s_buf[slot, h], ms[h], first)
            p_buf[slot, h] = p
            new_alphas.append(jnp.exp2(ms[h] - m_new))
            new_ms.append(m_new)
        return tuple(new_ms), tuple(new_alphas)

    for h in range(nh):
        s_buf[0, h] = qk_fn(0, h)
    ms = tuple(jnp.full((1, tq), NEG, F32) for _ in range(nh))
    alphas = tuple(jnp.zeros((1, tq), F32) for _ in range(nh))
    ms, alphas = stage(0, 0, ms, alphas, do_pv=False, first=True)

    def body(t, carry):
        ms, alphas = carry
        ms, alphas = stage(2 * t + 1, 1, ms, alphas)
        return stage(2 * t + 2, 0, ms, alphas)

    ms, alphas = lax.fori_loop(0, (n - 2) // 2, body, (ms, alphas))
    ms, alphas = stage(n - 1, 1, ms, alphas, do_qk=False)
    for h in range(nh):
        acc_ref[h] = alphas[h] * acc_ref[h] + pv_fn(n - 1, h, p_buf[1, h])


def _chunk_off(j, tk):
    return j * tk if isinstance(j, int) else pl.multiple_of(j * tk, tk)


def _attn_a_body(cs_ref, ctab_ref, lam_ref, g_ref, q_ref, k_ref, kpos_ref, vt_ref, o_ref,
                 s_buf, p_buf, acc_ref, *, seq, tq, tk, lam_init):
    h = pl.program_id(1)
    qi = pl.program_id(2)
    n = seq // tk
    cd = (qi * tq) // tk
    c_slope = cs_ref[h]
    q = q_ref[0]
    lane = lax.broadcasted_iota(jnp.int32, q.shape, 1)
    qaug = jnp.broadcast_to(ctab_ref[pl.ds(h, 1), :], (tq, LANES)).astype(BF)
    qpos = (qi * tq + lax.broadcasted_iota(jnp.int32, (1, tq), 1)).astype(F32)
    f0 = c_slope * qpos
    dv = vt_ref.shape[2] - SUM_ROWS
    q_left, q_right = [], []
    for half in range(2):
        q_m = jnp.where((lane >= 64 * half) & (lane < 64 * (half + 1)), q, jnp.zeros_like(q))
        q_left.append(jnp.concatenate([q_m, qaug], axis=1))
        q_right.append(jnp.concatenate([q_m, -qaug], axis=1))

    def chunk_of(j):
        jm = j - 1
        c = jnp.where(j == 0, cd, jm + (jm >= cd).astype(jnp.int32))
        return c, c <= cd

    def qk_fn(j, half):
        c, left = chunk_of(j)
        off = pl.multiple_of(c * tk, tk)
        k_c = jnp.concatenate([k_ref[0, pl.ds(off, tk), :], kpos_ref[pl.ds(off, tk), :]], axis=1)
        q_full = jnp.where(left, q_left[half], q_right[half])
        return lax.dot_general(k_c, q_full, NT_DIMS, preferred_element_type=F32)

    def sm_fn(j, half, s, m, first):
        if first:
            d = (lax.broadcasted_iota(jnp.int32, (tk, tq), 0) - lax.broadcasted_iota(jnp.int32, (tk, tq), 1)
                 + (cd * tk - qi * tq)).astype(F32)
            return _softmax_chunk(s - (2.0 * c_slope) * jnp.maximum(d, 0.0), m, f0)
        _, left = chunk_of(j)
        return _softmax_chunk(s, m, jnp.where(left, f0, -f0))

    def pv_fn(j, half, p):
        c, _ = chunk_of(j)
        off = pl.multiple_of(c * tk, tk)
        return jnp.dot(vt_ref[0, 0, :, pl.ds(off, tk)], p, preferred_element_type=F32)

    _pipelined_flash(n, 2, tq, qk_fn, sm_fn, pv_fn, s_buf, p_buf, acc_ref)
    outs = [acc_ref[half, :dv, :] / acc_ref[half, dv:dv + 1, :] for half in range(2)]
    lf = lam_ref[...]
    lam_full = (jnp.exp(jnp.sum(lf[0:1] * lf[1:2], axis=-1, keepdims=True))
                - jnp.exp(jnp.sum(lf[2:3] * lf[3:4], axis=-1, keepdims=True)) + lam_init)
    o = outs[0] - lam_full * outs[1]
    ms = jnp.mean(o * o, axis=0, keepdims=True)
    o = o * lax.rsqrt(ms + RMS_EPS) * g_ref[...] * (1.0 - lam_init)
    o_ref[0] = o.T.astype(o_ref.dtype)


def _bf16_pieces(c, n=3):
    pieces, rest = [], c.astype(F32)
    for _ in range(n):
        p = rest.astype(BF)
        pieces.append(p.astype(F32))
        rest = rest - p.astype(F32)
    return pieces


def _with_sum_rows(vt):
    ones = jnp.ones(vt.shape[:-2] + (1, vt.shape[-1]), vt.dtype)
    zeros = jnp.zeros(vt.shape[:-2] + (SUM_ROWS - 1, vt.shape[-1]), vt.dtype)
    return jnp.concatenate([vt, ones, zeros], axis=-2)


def _flash_chunk_len(seq, tk_max):
    tk = tk_max
    while seq % tk or (seq // tk) < 4 or (seq // tk) % 2:
        tk //= 2
    return tk


def _flash_scratch(nh, rows, tq, tk):
    return [pltpu.VMEM((2, nh, tk, tq), F32), pltpu.VMEM((2, nh, tk, tq), BF), pltpu.VMEM((nh, rows, tq), F32)]


def _attn_a(qkv, vt, lam, subln_g, lam_init, tq=256, tk=1024):
    bsz, seq, _ = qkv.shape
    tq, tk = min(tq, seq), _flash_chunk_len(seq, tk)
    c_slope = jnp.asarray(2.0 ** (-8.0 * np.arange(1, A_HEADS + 1) / A_HEADS) * LOG2E, dtype=F32)
    c1, c2, c3 = _bf16_pieces(c_slope)
    ctab = jnp.stack([c1, c1, c2, c2, c3, c3], axis=-1)
    ctab = jnp.concatenate([ctab, jnp.zeros((A_HEADS, LANES - 6), F32)], axis=-1)
    pos = np.arange(seq)
    kpos = np.zeros((seq, LANES), np.float32)
    for j in range(3):
        kpos[:, 2 * j] = (pos // LANES) * LANES
        kpos[:, 2 * j + 1] = pos % LANES
    kpos = jnp.asarray(kpos, BF)
    vrows = vt.shape[2]
    body = functools.partial(_attn_a_body, seq=seq, tq=tq, tk=tk, lam_init=lam_init)
    return pl.pallas_call(
        body,
        grid=(bsz, A_HEADS, seq // tq),
        in_specs=[
            pl.BlockSpec(memory_space=pltpu.SMEM),
            pl.BlockSpec((A_HEADS, LANES), lambda b, h, i: (0, 0)),
            pl.BlockSpec((4, A_HEAD_DIM), lambda b, h, i: (0, 0)),
            pl.BlockSpec((2 * A_HEAD_DIM, 1), lambda b, h, i: (0, 0)),
            pl.BlockSpec((1, tq, LANES), lambda b, h, i: (b, i, h)),
            pl.BlockSpec((1, seq, LANES), lambda b, h, i: (b, 0, A_HEADS + h)),
            pl.BlockSpec((seq, LANES), lambda b, h, i: (0, 0)),
            pl.BlockSpec((1, 1, vrows, seq), lambda b, h, i: (b, h, 0, 0)),
        ],
        out_specs=pl.BlockSpec((1, tq, LANES), lambda b, h, i: (b, i, h)),
        out_shape=jax.ShapeDtypeStruct((bsz, seq, A_HEADS * LANES), BF),
        scratch_shapes=_flash_scratch(2, vrows, tq, tk),
        compiler_params=_cparams("parallel", "parallel", "arbitrary"),
        name="attn_a",
    )(c_slope, ctab, lam.astype(F32), subln_g.astype(F32).reshape(2 * A_HEAD_DIM, 1), qkv, qkv, kpos, vt)


def _attn_b_body(q_ref, k_ref, vt_ref, o_ref, s_buf, p_buf, acc_ref, *, seq, tk):
    tq = q_ref.shape[1]
    rows = B_VDIM + SUM_ROWS
    qs = [q_ref[0, :, h * LANES:(h + 1) * LANES] for h in range(2)]

    def qk_fn(j, h):
        k_c = k_ref[0, pl.ds(_chunk_off(j, tk), tk), h * LANES:(h + 1) * LANES]
        return lax.dot_general(k_c, qs[h], NT_DIMS, preferred_element_type=F32)

    def sm_fn(j, h, s, m, first):
        return _softmax_chunk(s, m, None)

    def pv_fn(j, h, p):
        vt_c = vt_ref[0, 0, h * rows:(h + 1) * rows, pl.ds(_chunk_off(j, tk), tk)]
        return jnp.dot(vt_c, p, preferred_element_type=F32)

    _pipelined_flash(seq // tk, 2, tq, qk_fn, sm_fn, pv_fn, s_buf, p_buf, acc_ref)
    outs = [acc_ref[h, :B_VDIM, :] / acc_ref[h, B_VDIM:B_VDIM + 1, :] for h in range(2)]
    o = jnp.concatenate(outs, axis=0)
    o_ref[0] = o.T.astype(o_ref.dtype)


def _attn_b(q, k, vt, tq=256, tk=1024):
    bsz, seq, _ = q.shape
    tq, tk = min(tq, seq), _flash_chunk_len(seq, tk)
    npair = B_HEADS // 2
    return pl.pallas_call(
        functools.partial(_attn_b_body, seq=seq, tk=tk),
        grid=(bsz, npair, seq // tq),
        in_specs=[
            pl.BlockSpec((1, tq, 2 * LANES), lambda b, h, i: (b, i, h)),
            pl.BlockSpec((1, seq, 2 * LANES), lambda b, h, i: (b, 0, h)),
            pl.BlockSpec((1, 1, 2 * (B_VDIM + SUM_ROWS), seq), lambda b, h, i: (b, h, 0, 0)),
        ],
        out_specs=pl.BlockSpec((1, tq, LANES), lambda b, h, i: (b, i, h)),
        out_shape=jax.ShapeDtypeStruct((bsz, seq, B_HEADS * B_VDIM), BF),
        scratch_shapes=_flash_scratch(2, B_VDIM + SUM_ROWS, tq, tk),
        compiler_params=_cparams("parallel", "parallel", "arbitrary"),
        name="attn_b",
    )(q, k, vt)


def _mla_in_body(x_ref, w_ref, gq_ref, gkv_ref, rc_ref, rs_ref, cq_ref, ckv_ref, kr_ref):
    h = jnp.dot(x_ref[...].astype(BF), w_ref[...], preferred_element_type=F32)
    cq_ref[...] = _rms_rows(h[:, :B_Q_LORA], gq_ref[...]).astype(BF)
    ckv_ref[...] = _rms_rows(h[:, B_Q_LORA:B_Q_LORA + B_KV_LORA], gkv_ref[...]).astype(BF)
    t = h[:, B_Q_LORA + B_KV_LORA:]
    kr = t * rc_ref[...] + pltpu.roll(t, LANES - B_ROPE, 1) * rs_ref[...]
    kr_ref[...] = kr.astype(BF)


def _mla_q_body(cq_ref, w_ref, ta_ref, tb_ref, q_ref):
    t = jnp.dot(cq_ref[...], w_ref[...], preferred_element_type=F32)
    ta = ta_ref[...]
    tb = tb_ref[...]
    for h in range(B_HEADS):
        th = t[:, h * LANES:(h + 1) * LANES]
        q_ref[:, h * LANES:(h + 1) * LANES] = (th * ta + pltpu.roll(th, LANES - B_ROPE, 1) * tb).astype(BF)


def _mla_kv_body(ckv_ref, kr_ref, wk_ref, e_ref, wv_ref, k_ref, v_ref):
    ckv = ckv_ref[...]
    k = jnp.dot(ckv, wk_ref[...], preferred_element_type=F32)
    k = k + jnp.dot(kr_ref[...], e_ref[...], preferred_element_type=F32)
    k_ref[...] = k.astype(BF)
    v_ref[...] = jnp.dot(ckv, wv_ref[...], preferred_element_type=F32).astype(BF)


def _rope_partner(w):
    half = B_ROPE // 2
    return jnp.concatenate([-w[..., half:], w[..., :half]], axis=-1)


def _mla_tables(seq):
    inv = 1.0 / (ROPE_THETA ** (np.arange(0, B_ROPE, 2, dtype=np.float32) / B_ROPE))
    ang = jnp.arange(seq, dtype=F32)[:, None] * jnp.asarray(inv, F32)[None, :]
    cos = jnp.concatenate([jnp.cos(ang), jnp.cos(ang)], axis=-1)
    sin = jnp.concatenate([jnp.sin(ang), jnp.sin(ang)], axis=-1)
    z = lambda n: jnp.zeros((seq, n), F32)
    rc = jnp.concatenate([cos, z(LANES - B_ROPE)], axis=-1)
    rs = jnp.concatenate([sin, z(LANES - B_ROPE)], axis=-1)
    scale = (B_NOPE + B_ROPE) ** -0.5 * LOG2E
    ta =jnp.concatenate([jnp.full((seq, B_NOPE), scale, F32), cos * scale, z(B_ROPE)], axis=-1)
    tb = jnp.concatenate([z(B_NOPE), sin * scale, z(B_ROPE)], axis=-1)
    return rc, rs, ta, tb


def _mla_weights(w_in, w_uq, w_ukv):
    d = w_in.shape[0]
    kr0 = B_Q_LORA + B_KV_LORA
    w_in_ext = jnp.concatenate(
        [w_in, _rope_partner(w_in[:, kr0:kr0 + B_ROPE]), jnp.zeros((d, LANES - 2 * B_ROPE), w_in.dtype)], axis=-1)
    wq = w_uq.reshape(B_Q_LORA, B_HEADS, B_NOPE + B_ROPE)
    wq_ext = jnp.concatenate([wq, _rope_partner(wq[..., B_NOPE:])], axis=-1).reshape(B_Q_LORA, B_HEADS * LANES)
    wkv = w_ukv.reshape(B_KV_LORA, B_HEADS, B_NOPE + B_VDIM)
    wk = jnp.concatenate([wkv[..., :B_NOPE], jnp.zeros((B_KV_LORA, B_HEADS, LANES - B_NOPE), w_ukv.dtype)], axis=-1)
    wk = wk.reshape(B_KV_LORA, B_HEADS * LANES)
    wv = wkv[..., B_NOPE:].reshape(B_KV_LORA, B_HEADS * B_VDIM)
    place = np.zeros((LANES, B_HEADS, LANES), np.float32)
    for j in range(B_ROPE):
        place[j, :, B_NOPE + j] = 1.0
    place = jnp.asarray(place.reshape(LANES, B_HEADS * LANES), BF)
    return w_in_ext.astype(BF), wq_ext.astype(BF), wk.astype(BF), place, wv.astype(BF)


def _mixer_b(x, w_in, q_norm_g, kv_norm_g, w_uq, w_ukv, tm=512):
    bsz, seq, d = x.shape
    t = bsz * seq
    tm = min(tm, seq)
    nblk = seq // tm
    w_in_ext, wq_ext, wk, place, wv = _mla_weights(w_in, w_uq, w_ukv)
    rc, rs, ta, tb = _mla_tables(seq)
    x2 = x.reshape(t, d)
    n_in = w_in_ext.shape[1]
    row = lambda i: (i, 0)
    fixed = lambda i: (0, 0)
    pos = lambda i: (i % nblk, 0)
    cq, ckv, kr = pl.pallas_call(
        _mla_in_body,
        grid=(t // tm,),
        in_specs=[
            pl.BlockSpec((tm, d), row), pl.BlockSpec((d, n_in), fixed),
            pl.BlockSpec((1, B_Q_LORA), fixed), pl.BlockSpec((1, B_KV_LORA), fixed),
            pl.BlockSpec((tm, LANES), pos), pl.BlockSpec((tm, LANES), pos),
        ],
        out_specs=[pl.BlockSpec((tm, B_Q_LORA), row), pl.BlockSpec((tm, B_KV_LORA), row),
                   pl.BlockSpec((tm, LANES), row)],
        out_shape=[jax.ShapeDtypeStruct((t, B_Q_LORA), BF), jax.ShapeDtypeStruct((t, B_KV_LORA), BF),
                   jax.ShapeDtypeStruct((t, LANES), BF)],
        compiler_params=_cparams("parallel"),
        name="mla_in",
    )(x2, w_in_ext, q_norm_g.astype(F32).reshape(1, -1), kv_norm_g.astype(F32).reshape(1, -1), rc, rs)
    nq = B_HEADS * LANES
    q = pl.pallas_call(
        _mla_q_body,
        grid=(t // tm,),
        in_specs=[pl.BlockSpec((tm, B_Q_LORA), row), pl.BlockSpec((B_Q_LORA, nq), fixed),
                  pl.BlockSpec((tm, LANES), pos), pl.BlockSpec((tm, LANES), pos)],
        out_specs=pl.BlockSpec((tm, nq), row),
        out_shape=jax.ShapeDtypeStruct((t, nq), BF),
        compiler_params=_cparams("parallel"),
        name="mla_q",
    )(cq, wq_ext, ta, tb)
    nv = B_HEADS * B_VDIM
    k, v = pl.pallas_call(
        _mla_kv_body,
        grid=(t // tm,),
        in_specs=[pl.BlockSpec((tm, B_KV_LORA), row), pl.BlockSpec((tm, LANES), row),
                  pl.BlockSpec((B_KV_LORA, nq), fixed), pl.BlockSpec((LANES, nq), fixed),
                  pl.BlockSpec((B_KV_LORA, nv), fixed)],
        out_specs=[pl.BlockSpec((tm, nq), row), pl.BlockSpec((tm, nv), row)],
        out_shape=[jax.ShapeDtypeStruct((t, nq), BF), jax.ShapeDtypeStruct((t, nv), BF)],
        compiler_params=_cparams("parallel"),
        name="mla_kv",
    )(ckv, kr, wk, place, wv)
    vt = _with_sum_rows(v.reshape(bsz, seq, B_HEADS, B_VDIM).transpose(0, 2, 3, 1))
    vt = vt.reshape(bsz, B_HEADS // 2, 2 * (B_VDIM + SUM_ROWS), seq)
    o = _attn_b(q.reshape(bsz, seq, nq), k.reshape(bsz, seq, nq), vt)
    return o.reshape(t, nv)


def _mixer_a(x, w_in, lam, subln_g, layer_idx):
    bsz, seq, d = x.shape
    t = bsz * seq
    hd2 = 2 * A_HEAD_DIM
    nq = A_HEADS * hd2
    scale = A_HEAD_DIM ** -0.5 * LOG2E
    w = jnp.concatenate([w_in[:, :nq] * scale, w_in[:, nq:]], axis=-1).astype(BF)
    qkv = _linear(x.reshape(t, d), w, BF, tn=nq).reshape(bsz, seq, 3 * nq)
    vt = _with_sum_rows(qkv[:, :, 2 * nq:].reshape(bsz, seq, A_HEADS, hd2).transpose(0, 2, 3, 1))
    lam_init = 0.8 - 0.6 * math.exp(-0.3 * layer_idx)
    o = _attn_a(qkv, vt, lam, subln_g, lam_init)
    return o.reshape(t, nq)


def _attn_c_body(q_ref, k0, k1, k2, k3, v0, v1, v2, v3, bt_ref, o_ref):
    q = q_ref[0]
    kw = jnp.concatenate([k0[0], k1[0], k2[0], k3[0]], axis=0)
    vw = jnp.concatenate([v0[0], v1[0], v2[0], v3[0]], axis=0)
    lane = lax.broadcasted_iota(jnp.int32, q.shape, 1)
    outs = []
    for hh in range(2):
        qm = jnp.where((lane >= 64 * hh) & (lane < 64 * (hh + 1)), q, jnp.zeros_like(q))
        s = lax.dot_general(qm, kw, NT_DIMS, preferred_element_type=F32) + bt_ref[hh, 0]
        m = jnp.max(s, axis=1, keepdims=True)
        p = jnp.exp2(s - m)
        l = jnp.sum(p, axis=1, keepdims=True)
        outs.append(jnp.dot(p.astype(BF), vw, preferred_element_type=F32) / l)
    o_ref[0] = jnp.where(lane < 64, outs[0], outs[1]).astype(o_ref.dtype)


def _nbr_bias_table(rpb):
    col = np.arange(GRID_W)
    col_start = np.clip(col - C_WIN_COLS // 2, 0, GRID_W - C_WIN_COLS)
    col_mask = (col[None, :] >= col_start[:, None]) & (col[None, :] < col_start[:, None] + C_WIN_COLS)
    pad = GRID_W - C_WIN_COLS
    ext = jnp.pad(rpb.astype(F32) * LOG2E, ((0, 0), (0, 0), (pad, pad)), mode="edge")
    toep = jnp.stack([ext[:, :, GRID_W - 1 - qc:2 * GRID_W - 1 - qc] for qc in range(GRID_W)], axis=2)
    toep = jnp.where(jnp.asarray(col_mask)[None, None], toep, NEG)
    neg = jnp.full((rpb.shape[0], GRID_W, GRID_W), NEG, F32)
    half = C_WIN_ROWS // 2
    kinds = []
    for kind in range(3):
        qrows = []
        for t in range(C_WIN_ROWS):
            u0 = (max(t, half), t, min(t, half))[kind]
            blocks = [toep[:, u - t + half - 1] if u0 <= u < u0 + C_WIN_ROWS else neg
                      for u in range(2 * C_WIN_ROWS)]
            qrows.append(jnp.concatenate(blocks, axis=-1))
        kinds.append(jnp.concatenate(qrows, axis=1))
    return jnp.stack(kinds, axis=1)


def _mixer_c(x, w_qkv, rpb):
    bsz, seq, d = x.shape
    t = bsz * seq
    nq = C_HEADS * C_HEAD_DIM
    scale = C_HEAD_DIM ** -0.5 * LOG2E
    w = jnp.concatenate([w_qkv[:, :nq] * scale, w_qkv[:, nq:]], axis=-1).astype(BF)
    qkv = _linear(x.reshape(t, d), w, BF, tn=nq).reshape(bsz, seq, 3 * nq)
    bt = _nbr_bias_table(rpb)
    npair = C_HEADS // 2
    qtok = C_WIN_ROWS * GRID_W
    ngrp = seq // qtok
    assert ngrp >= 2
    kb = qtok // 2
    nkb = seq // kb
    kspecs = []
    for off in (npair, 2 * npair):
        for j in range(4):
            kspecs.append(pl.BlockSpec(
                (1, kb, LANES),
                lambda h, g, b, j=j, off=off: (b, jnp.clip(2 * g - 1 + j, 0, nkb - 1), off + h)))
    kind = lambda g: jnp.where(g == 0, 0, jnp.where(g == ngrp - 1, 2, 1))
    o = pl.pallas_call(
        _attn_c_body,
        grid=(npair, ngrp, bsz),
        in_specs=[pl.BlockSpec((1, qtok, LANES), lambda h, g, b: (b, g, h))] + kspecs + [
            pl.BlockSpec((2, 1, qtok, 2 * qtok), lambda h, g, b: (h, kind(g), 0, 0))],
        out_specs=pl.BlockSpec((1, qtok, LANES), lambda h, g, b: (b, g, h)),
        out_shape=jax.ShapeDtypeStruct((bsz, seq, nq), BF),
        compiler_params=_cparams("parallel", "parallel", "arbitrary"),
        name="attn_c",
    )(qkv, *([qkv] * 8), bt)
    return o.reshape(t, nq)


def _attn_d_body(slopes_ref, q_ref, kp, kc, kn, vp, vc, vn, o_ref, lse_ref, *, tq, length, dil, rad):
    hp = pl.program_id(2)
    i = pl.program_id(3)
    q = q_ref[0]
    kw = jnp.concatenate([kp[0, tq - rad:tq, :], kc[0], kn[0, 0:rad, :]], axis=0)
    vw = jnp.concatenate([vp[0, tq - rad:tq, :], vc[0], vn[0, 0:rad, :]], axis=0)
    nk = tq + 2 * rad
    qi = lax.broadcasted_iota(jnp.int32, (tq, nk), 0)
    kk = lax.broadcasted_iota(jnp.int32, (tq, nk), 1)
    dist = jnp.abs(kk - rad - qi)
    ki = i * tq - rad + kk
    valid = (dist <= rad) & (ki >= 0) & (ki < length)
    distf = dist.astype(F32) * float(dil)
    lane = lax.broadcasted_iota(jnp.int32, (tq, LANES), 1)
    outs, lses = [], []
    for hh in range(2):
        slope = slopes_ref[hp * 2 + hh]
        qm = jnp.where((lane >= 64 * hh) & (lane < 64 * (hh + 1)), q, jnp.zeros_like(q))
        s = lax.dot_general(qm, kw, NT_DIMS, preferred_element_type=F32)
        s = jnp.where(valid, s - slope * distf, NEG)
        m = jnp.max(s, axis=1, keepdims=True)
        p = jnp.exp(s - m)
        l = jnp.sum(p, axis=1, keepdims=True)
        outs.append(jnp.dot(p.astype(BF), vw, preferred_element_type=F32) / l)
        lses.append(m + jnp.log(l))
    o_ref[0] = jnp.where(lane < 64, outs[0], outs[1]).astype(o_ref.dtype)
    lse_ref[0] = jnp.where(lane < 64, lses[0], lses[1])


def _attn_d_group(qkv, g, window, dil, tq=256):
    bsz, seq, ncol = qkv.shape
    rad = window // (2 * dil)
    length = seq // dil
    tq = min(tq, length)
    nq = length // tq
    nslot = ncol // LANES
    npair = D_HEADS // 2
    ng = len(D_GROUPS)
    view = qkv.reshape(bsz, length, dil * ncol)
    slopes = jnp.asarray(2.0 ** (-8.0 * np.arange(1, D_HEADS + 1) / D_HEADS), dtype=F32)

    def spec(which, shift):
        def imap(b, c, h, i):
            blk = jnp.clip(i + shift, 0, nq - 1)
            return (b, blk, c * nslot + which * ng * npair + g * npair + h)
        return pl.BlockSpec((1, tq, LANES), imap)

    ospec = pl.BlockSpec((1, tq, LANES), lambda b, c, h, i: (b, i, c * npair + h))
    o, lse = pl.pallas_call(
        functools.partial(_attn_d_body, tq=tq, length=length, dil=dil, rad=rad),
        grid=(bsz, dil, npair, nq),
        in_specs=[pl.BlockSpec(memory_space=pltpu.SMEM), spec(0, 0),
                  spec(1, -1), spec(1, 0), spec(1, 1), spec(2, -1), spec(2, 0), spec(2, 1)],
        out_specs=[ospec, ospec],
        out_shape=[jax.ShapeDtypeStruct((bsz, length, dil * npair * LANES), BF),
                   jax.ShapeDtypeStruct((bsz, length, dil * npair * LANES), F32)],
        compiler_params=_cparams("parallel", "parallel", "parallel", "arbitrary"),
        name=f"attn_d{g}",
    )(slopes, view, view, view, view, view, view, view)
    nh = npair * LANES
    return o.reshape(bsz * seq, nh), lse.reshape(bsz * seq, nh)


def _post_d_body(o0, o1, o2, l0, l1, l2, x_ref, w_ref, g_ref, b_ref, y_ref, *, alpha):
    a0, a1, a2 = l0[...], l1[...], l2[...]
    m = jnp.maximum(jnp.maximum(a0, a1), a2)
    e0, e1, e2 = jnp.exp(a0 - m), jnp.exp(a1 - m), jnp.exp(a2 - m)
    den = e0 + e1 + e2
    o = ((e0 / den) * o0[...].astype(F32) + (e1 / den) * o1[...].astype(F32)
         + (e2 / den) * o2[...].astype(F32))
    h = jnp.dot(o.astype(BF), w_ref[...], preferred_element_type=F32)
    y_ref[...] = _layer_norm_rows(alpha * x_ref[...] + h, g_ref[...], b_ref[...])


def _mixer_d_and_post(x2, bsz, seq, w_qkv, w_out, g, b, alpha, tm=512):
    t, d = x2.shape
    nh = D_HEADS * D_HEAD_DIM
    ng = len(D_GROUPS)
    scale = D_HEAD_DIM ** -0.5
    w = jnp.concatenate([w_qkv[:, :ng * nh] * scale, w_qkv[:, ng * nh:]], axis=-1).astype(BF)
    qkv = _linear(x2, w, BF, tn=ng * nh).reshape(bsz, seq, 3 * ng * nh)
    os_, ls_ = [], []
    for gi, (window, dil) in enumerate(D_GROUPS):
        o, lse = _attn_d_group(qkv, gi, window, dil)
        os_.append(o)
        ls_.append(lse)
    row = lambda i: (i, 0)
    fixed = lambda i: (0, 0)
    return pl.pallas_call(
        functools.partial(_post_d_body, alpha=alpha),
        grid=(t // tm,),
        in_specs=[pl.BlockSpec((tm, nh), row)] * 6 + [
            pl.BlockSpec((tm, d), row), pl.BlockSpec((nh, d), fixed),
            pl.BlockSpec((1, d), fixed), pl.BlockSpec((1, d), fixed)],
        out_specs=pl.BlockSpec((tm, d), row),
        out_shape=jax.ShapeDtypeStruct((t, d), F32),
        compiler_params=_cparams("parallel"),
        name="post_d",
    )(*os_, *ls_, x2, w_out.astype(BF), g.reshape(1, d), b.reshape(1, d))


def _router_gates(x, wh_ref, wl_ref, rb_ref):
    tm = x.shape[0]
    xh = x.astype(BF)
    xl = (x - xh.astype(F32)).astype(BF)
    wh = wh_ref[...]
    logits = (lax.dot_general(wh, xh, NT_DIMS, preferred_element_type=F32)
              + lax.dot_general(wh, xl, NT_DIMS, preferred_element_type=F32)
              + lax.dot_general(wl_ref[...], xh, NT_DIMS, preferred_element_type=F32))
    scores = jax.nn.sigmoid(logits)
    biased = scores + rb_ref[...]
    epg = N_EXPERTS // N_EXPERT_GROUPS
    sc = [scores[e:e + 1, :] for e in range(N_EXPERTS)]
    bi = [biased[e:e + 1, :] for e in range(N_EXPERTS)]
    gs = []
    for g in range(N_EXPERT_GROUPS):
        v = bi[g * epg:(g + 1) * epg]
        best = None
        for a in range(epg):
            for c in range(a + 1, epg):
                pair = v[a] + v[c]
                best = pair if best is None else jnp.maximum(best, pair)
        gs.append(best)
    gmax = functools.reduce(jnp.maximum, gs)
    taken = jnp.zeros((1, tm), jnp.bool_)
    cand = []
    for g in range(N_EXPERT_GROUPS):
        sel = (gs[g] == gmax) & jnp.logical_not(taken)
        taken = taken | sel
        for a in range(epg):
            cand.append(jnp.where(sel, bi[g * epg + a], -jnp.inf))
    m1 = functools.reduce(jnp.maximum, cand)
    taken = jnp.zeros((1, tm), jnp.bool_)
    is1 = []
    for e in range(N_EXPERTS):
        hit = (cand[e] == m1) & jnp.logical_not(taken)
        taken = taken | hit
        is1.append(hit)
    cand2 = [jnp.where(is1[e], -jnp.inf, cand[e]) for e in range(N_EXPERTS)]
    m2 = functools.reduce(jnp.maximum, cand2)
    taken = jnp.zeros((1, tm), jnp.bool_)
    is2 = []
    for e in range(N_EXPERTS):
        hit = (cand2[e] == m2) & jnp.logical_not(taken)
        taken = taken | hit
        is2.append(hit)
    zero = jnp.zeros((1, tm), F32)
    w1 = functools.reduce(jnp.add, [jnp.where(is1[e], sc[e], zero) for e in range(N_EXPERTS)])
    w2 = functools.reduce(jnp.add, [jnp.where(is2[e], sc[e], zero) for e in range(N_EXPERTS)])
    den = w1 + w2
    rows = [jnp.where(is1[e], w1 / den, zero) + jnp.where(is2[e], w2 / den, zero) for e in range(N_EXPERTS)]
    rows.append(jnp.ones((1, tm), F32))
    rows.append(jnp.zeros((LANES - N_EXPERTS - 1, tm), F32))
    return jnp.concatenate(rows, axis=0).T


def _moe_body(x_ref, p_ref, wh_ref, wl_ref, rb_ref, wgu_ref, wd_ref, g_ref, b_ref, wpi_ref, wpg_ref,
              y_ref, gates, xb, acc, *, alpha, n_steps):
    e = pl.program_id(1)

    @pl.when(e == 0)
    def _():
        x = x_ref[...]
        gates[...] = _router_gates(x, wh_ref, wl_ref, rb_ref)
        xb[...] = x.astype(BF)
        acc[...] = jnp.zeros_like(acc)

    gu = jnp.dot(xb[...], wgu_ref[0], preferred_element_type=F32)
    hmid = jax.nn.silu(gu[:, :EXPERT_FF]) * gu[:, EXPERT_FF:]
    lane = lax.broadcasted_iota(jnp.int32, gates.shape, 1)
    gcol = jnp.sum(jnp.where(lane == e, gates[...], 0.0), axis=1, keepdims=True)
    acc[...] += jnp.dot((hmid * gcol).astype(BF), wd_ref[0], preferred_element_type=F32)

    @pl.when(e == n_steps - 1)
    def _():
        x2 = _layer_norm_rows(alpha * x_ref[...] + acc[...], g_ref[...], b_ref[...])
        emb = jnp.dot(p_ref[...].astype(BF), wpi_ref[...], preferred_element_type=F32)
        gate = jax.nn.sigmoid(jnp.dot(x2.astype(BF), wpg_ref[...], preferred_element_type=F32))
        y_ref[...] = x2 + gate * emb


def _moe_layer(x2, p2, router, wgu, wd, g, b, wpi, wpg, alpha, tm=1024):
    t, d = x2.shape
    n_steps = wgu.shape[0]
    wh_t, wl_t, rb = router
    tok = lambda i, e: (i, 0)
    fixed = lambda i, e: (0, 0)
    return pl.pallas_call(
        functools.partial(_moe_body, alpha=alpha, n_steps=n_steps),
        grid=(t // tm, n_steps),
        in_specs=[
            pl.BlockSpec((tm, d), tok), pl.BlockSpec((tm, p2.shape[1]), tok),
            pl.BlockSpec((N_EXPERTS, d), fixed), pl.BlockSpec((N_EXPERTS, d), fixed),
            pl.BlockSpec((N_EXPERTS, 1), fixed),
            pl.BlockSpec((1, d, 2 * EXPERT_FF), lambda i, e: (e, 0, 0)),
            pl.BlockSpec((1, EXPERT_FF, d), lambda i, e: (e, 0, 0)),
            pl.BlockSpec((1, d), fixed), pl.BlockSpec((1, d), fixed),
            pl.BlockSpec(wpi.shape, fixed), pl.BlockSpec(wpg.shape, fixed),
        ],
        out_specs=pl.BlockSpec((tm, d), tok),
        out_shape=jax.ShapeDtypeStruct((t, d), F32),
        scratch_shapes=[pltpu.VMEM((tm, LANES), F32), pltpu.VMEM((tm, d), BF), pltpu.VMEM((tm, d), F32)],
        compiler_params=_cparams("parallel", "arbitrary"),
        name="moe",
    )(x2, p2, wh_t, wl_t, rb, wgu, wd, g.reshape(1, d), b.reshape(1, d), wpi, wpg)


def _prep_shared(prm):
    depth = prm["ln1_g"].shape[0]
    rw = prm["router_w"].astype(F32)
    wh = rw.astype(BF)
    wl = (rw - wh.astype(F32)).astype(BF)
    router = (wh.T, wl.T, prm["router_b"].astype(F32).reshape(N_EXPERTS, 1))
    layers = []
    for i in range(depth):
        wgu = jnp.concatenate([prm["moe_w_gate"][i], prm["moe_w_up"][i]], axis=-1)
        wgu_s = jnp.concatenate([prm["moe_ws_gate"][i], prm["moe_ws_up"][i]], axis=-1)[None]
        wgu = jnp.concatenate([wgu, wgu_s], axis=0).astype(BF)
        wd = jnp.concatenate([prm["moe_w_down"][i], prm["moe_ws_down"][i][None]], axis=0).astype(BF)
        layers.append(dict(wgu=wgu, wd=wd, wpi=prm["ple_w_in"][i].astype(BF), wpg=prm["ple_w_gate"][i].astype(BF)))
    return router, layers


def _trunk(x, p, prm, router, layers):
    depth = prm["ln1_g"].shape[0]
    alpha = (2.0 * depth) ** 0.25
    bsz, seq, d = x.shape
    t = bsz * seq
    x2 = x.reshape(t, d)
    for i in range(depth):
        mixer, j = i % 4, i // 4
        xb = x2.reshape(bsz, seq, d)
        if mixer == 3:
            x2 = _mixer_d_and_post(x2, bsz, seq, prm["d_w_qkv"][j], prm["d_w_out"][j],
                                   prm["ln1_g"][i], prm["ln1_b"][i], alpha)
        else:
            if mixer == 0:
                o = _mixer_a(xb, prm["a_w_in"][j], prm["a_lambda"][j], prm["a_subln"][j], i)
                w_out = prm["a_w_out"][j]
            elif mixer == 1:
                o = _mixer_b(xb, prm["b_w_in"][j], prm["b_q_norm"][j], prm["b_kv_norm"][j],
                             prm["b_w_uq"][j], prm["b_w_ukv"][j])
                w_out = prm["b_w_out"][j]
            else:
                o = _mixer_c(xb, prm["c_w_qkv"][j], prm["c_rpb"][j])
                w_out = prm["c_w_out"][j]
            x2 = _post_attn(o, x2, w_out.astype(BF), prm["ln1_g"][i], prm["ln1_b"][i], alpha)
        lw = layers[i]
        x2 = _moe_layer(x2, p[i].reshape(t, -1), router, lw["wgu"], lw["wd"],
                        prm["ln2_g"][i], prm["ln2_b"][i], lw["wpi"], lw["wpg"], alpha)
    return x2.reshape(bsz, seq, d)


def kernel(x_prompt, x_sample, p_prompt, p_sample, a_w_in, a_lambda, a_subln, a_w_out, b_w_in, b_q_norm, b_kv_norm, b_w_uq, b_w_ukv, b_w_out, c_w_qkv, c_rpb, c_w_out, d_w_qkv, d_w_out, router_w, router_b, moe_w_gate, moe_w_up, moe_w_down, moe_ws_gate, moe_ws_up, moe_ws_down, ln1_g, ln1_b, ln2_g, ln2_b, ple_w_in, ple_w_gate):
    prm = dict(a_w_in=a_w_in, a_lambda=a_lambda, a_subln=a_subln, a_w_out=a_w_out,
               b_w_in=b_w_in, b_q_norm=b_q_norm, b_kv_norm=b_kv_norm, b_w_uq=b_w_uq,
               b_w_ukv=b_w_ukv, b_w_out=b_w_out,
               c_w_qkv=c_w_qkv, c_rpb=c_rpb, c_w_out=c_w_out,
               d_w_qkv=d_w_qkv, d_w_out=d_w_out,
               router_w=router_w, router_b=router_b, moe_w_gate=moe_w_gate, moe_w_up=moe_w_up,
               moe_w_down=moe_w_down, moe_ws_gate=moe_ws_gate, moe_ws_up=moe_ws_up,
               moe_ws_down=moe_ws_down,
               ln1_g=ln1_g, ln1_b=ln1_b, ln2_g=ln2_g, ln2_b=ln2_b,
               ple_w_in=ple_w_in, ple_w_gate=ple_w_gate)
    router, layers = _prep_shared(prm)
    y_prompt = _trunk(x_prompt, p_prompt, prm, router, layers)
    y_sample = _trunk(x_sample, p_sample, prm, router, layers)
    return (y_prompt, y_sample)
```

```python
import functools
import math

import numpy as np
import jax
import jax.numpy as jnp
from jax import lax
from jax.experimental import pallas as pl
from jax.experimental.pallas import tpu as pltpu

BF = jnp.bfloat16
F32 = jnp.float32

VMEM_LIMIT_BYTES = 56 * 1024 * 1024
LANES = 128

GRID_W = 64
LN_EPS = 1e-5
RMS_EPS = 1e-6
ROPE_THETA = 10000.0
NEG = -1e30

A_HEADS, A_HEAD_DIM = 8, 64
B_HEADS, B_NOPE, B_ROPE, B_VDIM, B_Q_LORA, B_KV_LORA = 16, 64, 32, 64, 384, 256
C_HEADS, C_HEAD_DIM, C_WIN_ROWS, C_WIN_COLS = 16, 64, 8, 16
D_HEADS, D_HEAD_DIM = 8, 64
D_GROUPS = ((128, 1), (512, 4), (2048, 16))
N_EXPERTS, N_EXPERT_GROUPS, EXPERT_FF = 16, 4, 256

NT_DIMS = (((1,), (1,)), ((), ()))


def _cparams(*sem):
    return pltpu.CompilerParams(dimension_semantics=sem, vmem_limit_bytes=VMEM_LIMIT_BYTES)


def _layer_norm_rows(z, g, b):
    mu = jnp.mean(z, axis=-1, keepdims=True)
    zc = z - mu
    var = jnp.mean(zc * zc, axis=-1, keepdims=True)
    return zc * lax.rsqrt(var + LN_EPS) * g + b


def _rms_rows(z, g):
    return z * lax.rsqrt(jnp.mean(z * z, axis=-1, keepdims=True) + RMS_EPS) * g


def _linear_body(x_ref, w_ref, o_ref):
    o_ref[...] = jnp.dot(x_ref[...].astype(BF), w_ref[...], preferred_element_type=F32).astype(o_ref.dtype)


def _linear(x, w, out_dtype, tm=512, tn=None):
    m, k = x.shape
    n = w.shape[1]
    tn = n if tn is None else tn
    return pl.pallas_call(
        _linear_body,
        grid=(n // tn, m // tm),
        in_specs=[pl.BlockSpec((tm, k), lambda j, i: (i, 0)), pl.BlockSpec((k, tn), lambda j, i: (0, j))],
        out_specs=pl.BlockSpec((tm, tn), lambda j, i: (i, j)),
        out_shape=jax.ShapeDtypeStruct((m, n), out_dtype),
        compiler_params=_cparams("parallel", "parallel"),
        name="linear",
    )(x, w)


def _post_body(o_ref, x_ref, w_ref, g_ref, b_ref, y_ref, *, alpha):
    h = jnp.dot(o_ref[...], w_ref[...], preferred_element_type=F32)
    y_ref[...] = _layer_norm_rows(alpha * x_ref[...] + h, g_ref[...], b_ref[...])


def _post_attn(o, x, w_out, g, b, alpha, tm=512):
    t, ko = o.shape
    d = x.shape[1]
    return pl.pallas_call(
        functools.partial(_post_body, alpha=alpha),
        grid=(t // tm,),
        in_specs=[
            pl.BlockSpec((tm, ko), lambda i: (i, 0)),
            pl.BlockSpec((tm, d), lambda i: (i, 0)),
            pl.BlockSpec((ko, d), lambda i: (0, 0)),
            pl.BlockSpec((1, d), lambda i: (0, 0)),
            pl.BlockSpec((1, d), lambda i: (0, 0)),
        ],
        out_specs=pl.BlockSpec((tm, d), lambda i: (i, 0)),
        out_shape=jax.ShapeDtypeStruct((t, d), F32),
        compiler_params=_cparams("parallel"),
        name="post_attn",
    )(o, x, w_out, g.reshape(1, d), b.reshape(1, d))


LOG2E = 1.4426950408889634
SUM_ROWS = 16


def _softmax_chunk(s, m, shift):
    mx = jnp.max(s, axis=0, keepdims=True)
    if shift is not None:
        mx = mx - shift
    m_new = jnp.maximum(m, mx)
    ref = m_new if shift is None else m_new + shift
    return m_new, jnp.exp2(s - ref).astype(BF)


def _pipelined_flash(n, nh, tq, qk_fn, sm_fn, pv_fn, s_buf, p_buf, acc_ref):
    assert n >= 4 and n % 2 == 0
    acc_ref[...] = jnp.zeros_like(acc_ref)

    def stage(j, slot, ms, alphas, do_pv=True, do_qk=True, first=False):
        if do_pv:
            for h in range(nh):
                acc_ref[h] = alphas[h] * acc_ref[h] + pv_fn(j - 1, h, p_buf[1 - slot, h])
        if do_qk:
            for h in range(nh):
                s_buf[1 - slot, h] = qk_fn(j + 1, h)
        new_ms, new_alphas = [], []
        for h in range(nh):
            m_new, p = sm_fn(j, h, s_buf[slot, h], ms[h], first)
            p_buf[slot, h] = p
            new_alphas.append(jnp.exp2(ms[h] - m_new))
            new_ms.append(m_new)
        return tuple(new_ms), tuple(new_alphas)

    for h in range(nh):
        s_buf[0, h] = qk_fn(0, h)
    ms = tuple(jnp.full((1, tq), NEG, F32) for _ in range(nh))
    alphas = tuple(jnp.zeros((1, tq), F32) for _ in range(nh))
    ms, alphas = stage(0, 0, ms, alphas, do_pv=False, first=True)

    def body(t, carry):
        ms, alphas = carry
        ms, alphas = stage(2 * t + 1, 1, ms, alphas)
        return stage(2 * t + 2, 0, ms, alphas)

    ms, alphas = lax.fori_loop(0, (n - 2) // 2, body, (ms, alphas))
    ms, alphas = stage(n - 1, 1, ms, alphas, do_qk=False)
    for h in range(nh):
        acc_ref[h] = alphas[h] * acc_ref[h] + pv_fn(n - 1, h, p_buf[1, h])


def _chunk_off(j, tk):
    return j * tk if isinstance(j, int) else pl.multiple_of(j * tk, tk)


def _attn_a_body(cs_ref, ctab_ref, lam_ref, g_ref, q_ref, k_ref, kpos_ref, vt_ref, o_ref,
                 s_buf, p_buf, acc_ref, *, seq, tq, tk, lam_init):
    h = pl.program_id(1)
    qi = pl.program_id(2)
    n = seq // tk
    cd = (qi * tq) // tk
    c_slope = cs_ref[h]
    q = q_ref[0]
    lane = lax.broadcasted_iota(jnp.int32, q.shape, 1)
    qaug = jnp.broadcast_to(ctab_ref[pl.ds(h, 1), :], (tq, LANES)).astype(BF)
    qpos = (qi * tq + lax.broadcasted_iota(jnp.int32, (1, tq), 1)).astype(F32)
    f0 = c_slope * qpos
    dv = vt_ref.shape[2] - SUM_ROWS
    q_left, q_right = [], []
    for half in range(2):
        q_m = jnp.where((lane >= 64 * half) & (lane < 64 * (half + 1)), q, jnp.zeros_like(q))
        q_left.append(jnp.concatenate([q_m, qaug], axis=1))
        q_right.append(jnp.concatenate([q_m, -qaug], axis=1))

    def chunk_of(j):
        jm = j - 1
        c = jnp.where(j == 0, cd, jm + (jm >= cd).astype(jnp.int32))
        return c, c <= cd

    def qk_fn(j, half):
        c, left = chunk_of(j)
        off = pl.multiple_of(c * tk, tk)
        k_c = jnp.concatenate([k_ref[0, pl.ds(off, tk), :], kpos_ref[pl.ds(off, tk), :]], axis=1)
        q_full = jnp.where(left, q_left[half], q_right[half])
        return lax.dot_general(k_c, q_full, NT_DIMS, preferred_element_type=F32)

    def sm_fn(j, half, s, m, first):
        if first:
            d = (lax.broadcasted_iota(jnp.int32, (tk, tq), 0) - lax.broadcasted_iota(jnp.int32, (tk, tq), 1)
                 + (cd * tk - qi * tq)).astype(F32)
            return _softmax_chunk(s - (2.0 * c_slope) * jnp.maximum(d, 0.0), m, f0)
        _, left = chunk_of(j)
        return _softmax_chunk(s, m, jnp.where(left, f0, -f0))

    def pv_fn(j, half, p):
        c, _ = chunk_of(j)
        off = pl.multiple_of(c * tk, tk)
        return jnp.dot(vt_ref[0, 0, :, pl.ds(off, tk)], p, preferred_element_type=F32)

    _pipelined_flash(n, 2, tq, qk_fn, sm_fn, pv_fn, s_buf, p_buf, acc_ref)
    outs = [acc_ref[half, :dv, :] / acc_ref[half, dv:dv + 1, :] for half in range(2)]
    lf = lam_ref[...]
    lam_full = (jnp.exp(jnp.sum(lf[0:1] * lf[1:2], axis=-1, keepdims=True))
                - jnp.exp(jnp.sum(lf[2:3] * lf[3:4], axis=-1, keepdims=True)) + lam_init)
    o = outs[0] - lam_full * outs[1]
    ms = jnp.mean(o * o, axis=0, keepdims=True)
    o = o * lax.rsqrt(ms + RMS_EPS) * g_ref[...] * (1.0 - lam_init)
    o_ref[0] = o.T.astype(o_ref.dtype)


def _bf16_pieces(c, n=3):
    pieces, rest = [], c.astype(F32)
    for _ in range(n):
        p = rest.astype(BF)
        pieces.append(p.astype(F32))
        rest = rest - p.astype(F32)
    return pieces


def _store_vt_heads(vt_ref, vt, dv, heads_per_slot):
    tm = vt.shape[1]
    tail = jnp.where(lax.broadcasted_iota(jnp.int32, (SUM_ROWS, tm), 0) == 0, 1.0, 0.0).astype(vt_ref.dtype)
    rows = dv + SUM_ROWS
    for h in range(vt.shape[0] // dv):
        slot, r0 = h // heads_per_slot, (h % heads_per_slot) * rows
        vt_ref[0, slot, r0:r0 + dv, :] = vt[h * dv:(h + 1) * dv, :].astype(vt_ref.dtype)
        vt_ref[0, slot, r0 + dv:r0 + rows, :] = tail


def _proj_a_body(x_ref, wqk_ref, wvt_ref, qk_ref, vt_ref):
    xb = x_ref[...].astype(BF)
    qk_ref[...] = jnp.dot(xb, wqk_ref[...], preferred_element_type=F32).astype(BF)
    vt = lax.dot_general(wvt_ref[...], xb, NT_DIMS, preferred_element_type=F32)
    _store_vt_heads(vt_ref, vt, 2 * A_HEAD_DIM, 1)


def _flash_chunk_len(seq, tk_max):
    tk = tk_max
    while seq % tk or (seq // tk) < 4 or (seq // tk) % 2:
        tk //= 2
    return tk


def _flash_scratch(nh, rows, tq, tk):
    return [pltpu.VMEM((2, nh, tk, tq), F32), pltpu.VMEM((2, nh, tk, tq), BF), pltpu.VMEM((nh, rows, tq), F32)]


def _attn_a(qkv, vt, lam, subln_g, lam_init, tq=256, tk=1024):
    bsz, seq, _ = qkv.shape
    tq, tk = min(tq, seq), _flash_chunk_len(seq, tk)
    c_slope = jnp.asarray(2.0 ** (-8.0 * np.arange(1, A_HEADS + 1) / A_HEADS) * LOG2E, dtype=F32)
    c1, c2, c3 = _bf16_pieces(c_slope)
    ctab = jnp.stack([c1, c1, c2, c2, c3, c3], axis=-1)
    ctab = jnp.concatenate([ctab, jnp.zeros((A_HEADS, LANES - 6), F32)], axis=-1)
    pos = np.arange(seq)
    kpos = np.zeros((seq, LANES), np.float32)
    for j in range(3):
        kpos[:, 2 * j] = (pos // LANES) * LANES
        kpos[:, 2 * j + 1] = pos % LANES
    kpos = jnp.asarray(kpos, BF)
    vrows = vt.shape[2]
    body = functools.partial(_attn_a_body, seq=seq, tq=tq, tk=tk, lam_init=lam_init)
    return pl.pallas_call(
        body,
        grid=(bsz, A_HEADS, seq // tq),
        in_specs=[
            pl.BlockSpec(memory_space=pltpu.SMEM),
            pl.BlockSpec((A_HEADS, LANES), lambda b, h, i: (0, 0)),
            pl.BlockSpec((4, A_HEAD_DIM), lambda b, h, i: (0, 0)),
            pl.BlockSpec((2 * A_HEAD_DIM, 1), lambda b, h, i: (0, 0)),
            pl.BlockSpec((1, tq, LANES), lambda b, h, i: (b, i, h)),
            pl.BlockSpec((1, seq, LANES), lambda b, h, i: (b, 0, A_HEADS + h)),
            pl.BlockSpec((seq, LANES), lambda b, h, i: (0, 0)),
            pl.BlockSpec((1, 1, vrows, seq), lambda b, h, i: (b, h, 0, 0)),
        ],
        out_specs=pl.BlockSpec((1, tq, LANES), lambda b, h, i: (b, i, h)),
        out_shape=jax.ShapeDtypeStruct((bsz, seq, A_HEADS * LANES), BF),
        scratch_shapes=_flash_scratch(2, vrows, tq, tk),
        compiler_params=_cparams("parallel", "parallel", "arbitrary"),
        name="attn_a",
    )(c_slope, ctab, lam.astype(F32), subln_g.astype(F32).reshape(2 * A_HEAD_DIM, 1), qkv, qkv, kpos, vt)


def _attn_b_body(q_ref, k_ref, vt_ref, o_ref, s_buf, p_buf, acc_ref, *, seq, tk):
    tq = q_ref.shape[1]
    rows = B_VDIM + SUM_ROWS
    qs = [q_ref[0, :, h * LANES:(h + 1) * LANES] for h in range(2)]

    def qk_fn(j, h):
        k_c = k_ref[0, pl.ds(_chunk_off(j, tk), tk), h * LANES:(h + 1) * LANES]
        return lax.dot_general(k_c, qs[h], NT_DIMS, preferred_element_type=F32)

    def sm_fn(j, h, s, m, first):
        return _softmax_chunk(s, m, None)

    def pv_fn(j, h, p):
        vt_c = vt_ref[0, 0, h * rows:(h + 1) * rows, pl.ds(_chunk_off(j, tk), tk)]
        return jnp.dot(vt_c, p, preferred_element_type=F32)

    _pipelined_flash(seq // tk, 2, tq, qk_fn, sm_fn, pv_fn, s_buf, p_buf, acc_ref)
    outs = [acc_ref[h, :B_VDIM, :] / acc_ref[h, B_VDIM:B_VDIM + 1, :] for h in range(2)]
    o = jnp.concatenate(outs, axis=0)
    o_ref[0] = o.T.astype(o_ref.dtype)


def _attn_b(q, k, vt, tq=256, tk=1024):
    bsz, seq, _ = q.shape
    tq, tk = min(tq, seq), _flash_chunk_len(seq, tk)
    npair = B_HEADS // 2
    return pl.pallas_call(
        functools.partial(_attn_b_body, seq=seq, tk=tk),
        grid=(bsz, npair, seq // tq),
        in_specs=[
            pl.BlockSpec((1, tq, 2 * LANES), lambda b, h, i: (b, i, h)),
            pl.BlockSpec((1, seq, 2 * LANES), lambda b, h, i: (b, 0, h)),
            pl.BlockSpec((1, 1, 2 * (B_VDIM + SUM_ROWS), seq), lambda b, h, i: (b, h, 0, 0)),
        ],
        out_specs=pl.BlockSpec((1, tq, LANES), lambda b, h, i: (b, i, h)),
        out_shape=jax.ShapeDtypeStruct((bsz, seq, B_HEADS * B_VDIM), BF),
        scratch_shapes=_flash_scratch(2, B_VDIM + SUM_ROWS, tq, tk),
        compiler_params=_cparams("parallel", "parallel", "arbitrary"),
        name="attn_b",
    )(q, k, vt)


def _mla_in_body(x_ref, w_ref, gq_ref, gkv_ref, rc_ref, rs_ref, cq_ref, ckv_ref, kr_ref):
    h = jnp.dot(x_ref[...].astype(BF), w_ref[...], preferred_element_type=F32)
    cq_ref[...] = _rms_rows(h[:, :B_Q_LORA], gq_ref[...]).astype(BF)
    ckv_ref[...] = _rms_rows(h[:, B_Q_LORA:B_Q_LORA + B_KV_LORA], gkv_ref[...]).astype(BF)
    t = h[:, B_Q_LORA + B_KV_LORA:]
    kr = t * rc_ref[...] + pltpu.roll(t, LANES - B_ROPE, 1) * rs_ref[...]
    kr_ref[...] = kr.astype(BF)


def _mla_q_body(cq_ref, w_ref, ta_ref, tb_ref, q_ref):
    t = jnp.dot(cq_ref[...], w_ref[...], preferred_element_type=F32)
    ta = ta_ref[...]
    tb = tb_ref[...]
    for h in range(B_HEADS):
        th = t[:, h * LANES:(h + 1) * LANES]
        q_ref[:, h * LANES:(h + 1) * LANES] = (th * ta + pltpu.roll(th, LANES - B_ROPE, 1) * tb).astype(BF)


def _mla_kv_body(ckv_ref, kr_ref, wk_ref, e_ref, wvt_ref, k_ref, vt_ref):
    ckv = ckv_ref[...]
    k = jnp.dot(ckv, wk_ref[...], preferred_element_type=F32)
    k = k + jnp.dot(kr_ref[...], e_ref[...], preferred_element_type=F32)
    k_ref[...] = k.astype(BF)
    vt = lax.dot_general(wvt_ref[...], ckv, NT_DIMS, preferred_element_type=F32)
    _store_vt_heads(vt_ref, vt, B_VDIM, 2)


def _rope_partner(w):
    half = B_ROPE // 2
    return jnp.concatenate([-w[..., half:], w[..., :half]], axis=-1)


def _mla_tables(seq):
    inv = 1.0 / (ROPE_THETA ** (np.arange(0, B_ROPE, 2, dtype=np.float32) / B_ROPE))
    ang = jnp.arange(seq, dtype=F32)[:, None] * jnp.asarray(inv, F32)[None, :]
    cos = jnp.concatenate([jnp.cos(ang), jnp.cos(ang)], axis=-1)
    sin = jnp.concatenate([jnp.sin(ang), jnp.sin(ang)], axis=-1)
    z = lambda n: jnp.zeros((seq, n), F32)
    rc = jnp.concatenate([cos, z(LANES - B_ROPE)], axis=-1)
    rs = jnp.concatenate([sin, z(LANES - B_ROPE)], axis=-1)
    scale = (B_NOPE + B_ROPE) ** -0.5 * LOG2E
    ta =jnp.concatenate([jnp.full((seq, B_NOPE), scale, F32), cos * scale, z(B_ROPE)], axis=-1)
    tb = jnp.concatenate([z(B_NOPE), sin * scale, z(B_ROPE)], axis=-1)
    return rc, rs, ta, tb


def _mla_weights(w_in, w_uq, w_ukv):
    d = w_in.shape[0]
    kr0 = B_Q_LORA + B_KV_LORA
    w_in_ext = jnp.concatenate(
        [w_in, _rope_partner(w_in[:, kr0:kr0 + B_ROPE]), jnp.zeros((d, LANES - 2 * B_ROPE), w_in.dtype)], axis=-1)
    wq = w_uq.reshape(B_Q_LORA, B_HEADS, B_NOPE + B_ROPE)
    wq_ext = jnp.concatenate([wq, _rope_partner(wq[..., B_NOPE:])], axis=-1).reshape(B_Q_LORA, B_HEADS * LANES)
    wkv = w_ukv.reshape(B_KV_LORA, B_HEADS, B_NOPE + B_VDIM)
    wk = jnp.concatenate([wkv[..., :B_NOPE], jnp.zeros((B_KV_LORA, B_HEADS, LANES - B_NOPE), w_ukv.dtype)], axis=-1)
    wk = wk.reshape(B_KV_LORA, B_HEADS * LANES)
    wv = wkv[..., B_NOPE:].reshape(B_KV_LORA, B_HEADS * B_VDIM)
    place = np.zeros((LANES, B_HEADS, LANES), np.float32)
    for j in range(B_ROPE):
        place[j, :, B_NOPE + j] = 1.0
    place = jnp.asarray(place.reshape(LANES, B_HEADS * LANES), BF)
    return w_in_ext.astype(BF), wq_ext.astype(BF), wk.astype(BF), place, wv.astype(BF)


def _mixer_b(x, w_in, q_norm_g, kv_norm_g, w_uq, w_ukv, tm=512):
    bsz, seq, d = x.shape
    t = bsz * seq
    tm = min(tm, seq)
    nblk = seq // tm
    w_in_ext, wq_ext, wk, place, wv = _mla_weights(w_in, w_uq, w_ukv)
    rc, rs, ta, tb = _mla_tables(seq)
    x2 = x.reshape(t, d)
    n_in = w_in_ext.shape[1]
    row = lambda i: (i, 0)
    fixed = lambda i: (0, 0)
    pos = lambda i: (i % nblk, 0)
    cq, ckv, kr = pl.pallas_call(
        _mla_in_body,
        grid=(t // tm,),
        in_specs=[
            pl.BlockSpec((tm, d), row), pl.BlockSpec((d, n_in), fixed),
            pl.BlockSpec((1, B_Q_LORA), fixed), pl.BlockSpec((1, B_KV_LORA), fixed),
            pl.BlockSpec((tm, LANES), pos), pl.BlockSpec((tm, LANES), pos),
        ],
        out_specs=[pl.BlockSpec((tm, B_Q_LORA), row), pl.BlockSpec((tm, B_KV_LORA), row),
                   pl.BlockSpec((tm, LANES), row)],
        out_shape=[jax.ShapeDtypeStruct((t, B_Q_LORA), BF), jax.ShapeDtypeStruct((t, B_KV_LORA), BF),
                   jax.ShapeDtypeStruct((t, LANES), BF)],
        compiler_params=_cparams("parallel"),
        name="mla_in",
    )(x2, w_in_ext, q_norm_g.astype(F32).reshape(1, -1), kv_norm_g.astype(F32).reshape(1, -1), rc, rs)
    nq = B_HEADS * LANES
    q = pl.pallas_call(
        _mla_q_body,
        grid=(t // tm,),
        in_specs=[pl.BlockSpec((tm, B_Q_LORA), row), pl.BlockSpec((B_Q_LORA, nq), fixed),
                  pl.BlockSpec((tm, LANES), pos), pl.BlockSpec((tm, LANES), pos)],
        out_specs=pl.BlockSpec((tm, nq), row),
        out_shape=jax.ShapeDtypeStruct((t, nq), BF),
        compiler_params=_cparams("parallel"),
        name="mla_q",
    )(cq, wq_ext, ta, tb)
    nv = B_HEADS * B_VDIM
    npair = B_HEADS // 2
    vrows = 2 * (B_VDIM + SUM_ROWS)
    k, vt = pl.pallas_call(
        _mla_kv_body,
        grid=(t // tm,),
        in_specs=[pl.BlockSpec((tm, B_KV_LORA), row), pl.BlockSpec((tm, LANES), row),
                  pl.BlockSpec((B_KV_LORA, nq), fixed), pl.BlockSpec((LANES, nq), fixed),
                  pl.BlockSpec((nv, B_KV_LORA), fixed)],
        out_specs=[pl.BlockSpec((tm, nq), row),
                   pl.BlockSpec((1, npair, vrows, tm), lambda i: (i // nblk, 0, 0, i % nblk))],
        out_shape=[jax.ShapeDtypeStruct((t, nq), BF), jax.ShapeDtypeStruct((bsz, npair, vrows, seq), BF)],
        compiler_params=_cparams("parallel"),
        name="mla_kv",
    )(ckv, kr, wk, place, wv.T)
    o = _attn_b(q.reshape(bsz, seq, nq), k.reshape(bsz, seq, nq), vt)
    return o.reshape(t, nv)


def _mixer_a(x, w_in, lam, subln_g, layer_idx):
    bsz, seq, d = x.shape
    t = bsz * seq
    hd2 = 2 * A_HEAD_DIM
    nq = A_HEADS * hd2
    scale = A_HEAD_DIM ** -0.5 * LOG2E
    wqk = jnp.concatenate([w_in[:, :nq] * scale, w_in[:, nq:2 * nq]], axis=-1).astype(BF)
    wvt = w_in[:, 2 * nq:].T.astype(BF)
    tm = min(512, seq)
    nblk = seq // tm
    vrows = hd2 + SUM_ROWS
    qk, vt = pl.pallas_call(
        _proj_a_body,
        grid=(t // tm,),
        in_specs=[pl.BlockSpec((tm, d), lambda i: (i, 0)), pl.BlockSpec((d, 2 * nq), lambda i: (0, 0)),
                  pl.BlockSpec((nq, d), lambda i: (0, 0))],
        out_specs=[pl.BlockSpec((tm, 2 * nq), lambda i: (i, 0)),
                   pl.BlockSpec((1, A_HEADS, vrows, tm), lambda i: (i // nblk, 0, 0, i % nblk))],
        out_shape=[jax.ShapeDtypeStruct((t, 2 * nq), BF), jax.ShapeDtypeStruct((bsz, A_HEADS, vrows, seq), BF)],
        compiler_params=_cparams("parallel"),
        name="proj_a",
    )(x.reshape(t, d), wqk, wvt)
    lam_init = 0.8 - 0.6 * math.exp(-0.3 * layer_idx)
    o = _attn_a(qk.reshape(bsz, seq, 2 * nq), vt, lam, subln_g, lam_init)
    return o.reshape(t, nq)


def _attn_c_body(q_ref, k0, k1, k2, k3, v0, v1, v2, v3, bt_ref, o_ref):
    q = q_ref[0]
    kw = jnp.concatenate([k0[0], k1[0], k2[0], k3[0]], axis=0)
    vw = jnp.concatenate([v0[0], v1[0], v2[0], v3[0]], axis=0)
    lane = lax.broadcasted_iota(jnp.int32, q.shape, 1)
    outs = []
    for hh in range(2):
        qm = jnp.where((lane >= 64 * hh) & (lane < 64 * (hh + 1)), q, jnp.zeros_like(q))
        s = lax.dot_general(qm, kw, NT_DIMS, preferred_element_type=F32) + bt_ref[hh, 0]
        m = jnp.max(s, axis=1, keepdims=True)
        p = jnp.exp2(s - m)
        l = jnp.sum(p, axis=1, keepdims=True)
        outs.append(jnp.dot(p.astype(BF), vw, preferred_element_type=F32) / l)
    o_ref[0] = jnp.where(lane < 64, outs[0], outs[1]).astype(o_ref.dtype)


def _nbr_bias_table(rpb):
    col = np.arange(GRID_W)
    col_start = np.clip(col - C_WIN_COLS // 2, 0, GRID_W - C_WIN_COLS)
    col_mask = (col[None, :] >= col_start[:, None]) & (col[None, :] < col_start[:, None] + C_WIN_COLS)
    pad = GRID_W - C_WIN_COLS
    ext = jnp.pad(rpb.astype(F32) * LOG2E, ((0, 0), (0, 0), (pad, pad)), mode="edge")
    toep = jnp.stack([ext[:, :, GRID_W - 1 - qc:2 * GRID_W - 1 - qc] for qc in range(GRID_W)], axis=2)
    toep = jnp.where(jnp.asarray(col_mask)[None, None], toep, NEG)
    neg = jnp.full((rpb.shape[0], GRID_W, GRID_W), NEG, F32)
    half = C_WIN_ROWS // 2
    kinds = []
    for kind in range(3):
        qrows = []
        for t in range(C_WIN_ROWS):
            u0 = (max(t, half), t, min(t, half))[kind]
            blocks = [toep[:, u - t + half - 1] if u0 <= u < u0 + C_WIN_ROWS else neg
                      for u in range(2 * C_WIN_ROWS)]
            qrows.append(jnp.concatenate(blocks, axis=-1))
        kinds.append(jnp.concatenate(qrows, axis=1))
    return jnp.stack(kinds, axis=1)


def _mixer_c(x, w_qkv, rpb):
    bsz, seq, d = x.shape
    t = bsz * seq
    nq = C_HEADS * C_HEAD_DIM
    scale = C_HEAD_DIM ** -0.5 * LOG2E
    w = jnp.concatenate([w_qkv[:, :nq] * scale, w_qkv[:, nq:]], axis=-1).astype(BF)
    qkv = _linear(x.reshape(t, d), w, BF, tn=nq).reshape(bsz, seq, 3 * nq)
    bt = _nbr_bias_table(rpb)
    npair = C_HEADS // 2
    qtok = C_WIN_ROWS * GRID_W
    ngrp = seq // qtok
    assert ngrp >= 2
    kb = qtok // 2
    nkb = seq // kb
    kspecs = []
    for off in (npair, 2 * npair):
        for j in range(4):
            kspecs.append(pl.BlockSpec(
                (1, kb, LANES),
                lambda h, g, b, j=j, off=off: (b, jnp.clip(2 * g - 1 + j, 0, nkb - 1), off + h)))
    kind = lambda g: jnp.where(g == 0, 0, jnp.where(g == ngrp - 1, 2, 1))
    o = pl.pallas_call(
        _attn_c_body,
        grid=(npair, ngrp, bsz),
        in_specs=[pl.BlockSpec((1, qtok, LANES), lambda h, g, b: (b, g, h))] + kspecs + [
            pl.BlockSpec((2, 1, qtok, 2 * qtok), lambda h, g, b: (h, kind(g), 0, 0))],
        out_specs=pl.BlockSpec((1, qtok, LANES), lambda h, g, b: (b, g, h)),
        out_shape=jax.ShapeDtypeStruct((bsz, seq, nq), BF),
        compiler_params=_cparams("parallel", "parallel", "arbitrary"),
        name="attn_c",
    )(qkv, *([qkv] * 8), bt)
    return o.reshape(t, nq)


def _attn_d_body(slopes_ref, q_ref, kp, kc, kn, vp, vc, vn, o_ref, lse_ref, *, tq, length, dil, rad):
    i = pl.program_id(2)
    nk = tq + 2 * rad
    qi = lax.broadcasted_iota(jnp.int32, (tq, nk), 0)
    kk = lax.broadcasted_iota(jnp.int32, (tq, nk), 1)
    dist = jnp.abs(kk - rad - qi)
    ki = i * tq - rad + kk
    valid = (dist <= rad) & (ki >= 0) & (ki < length)
    distf = dist.astype(F32) * float(dil)
    lane = lax.broadcasted_iota(jnp.int32, (tq, LANES), 1)
    for hp in range(D_HEADS // 2):
        sl = slice(hp * LANES, (hp + 1) * LANES)
        q = q_ref[0, :, sl]
        kw = jnp.concatenate([kp[0, tq - rad:tq, sl], kc[0, :, sl], kn[0, 0:rad, sl]], axis=0)
        vw = jnp.concatenate([vp[0, tq - rad:tq, sl], vc[0, :, sl], vn[0, 0:rad, sl]], axis=0)
        outs, lses = [], []
        for hh in range(2):
            slope = slopes_ref[hp * 2 + hh]
            qm = jnp.where((lane >= 64 * hh) & (lane < 64 * (hh + 1)), q, jnp.zeros_like(q))
            s = lax.dot_general(qm, kw, NT_DIMS, preferred_element_type=F32)
            s = jnp.where(valid, s - slope * distf, NEG)
            m = jnp.max(s, axis=1, keepdims=True)
            p = jnp.exp2(s - m)
            l = jnp.sum(p, axis=1, keepdims=True)
            outs.append(jnp.dot(p.astype(BF), vw, preferred_element_type=F32) / l)
            lses.append(m + jnp.log(l) * LOG2E)
        o_ref[0, :, sl] = jnp.where(lane < 64, outs[0], outs[1]).astype(o_ref.dtype)
        lse_ref[0, :, sl] = jnp.where(lane < 64, lses[0], lses[1])


def _attn_d_group(qkv, g, window, dil, tq=256):
    bsz, seq, ncol = qkv.shape
    rad = window // (2 * dil)
    length = seq // dil
    tq = min(tq, length)
    nq = length // tq
    nh = D_HEADS * D_HEAD_DIM
    ng = len(D_GROUPS)
    nslot = ncol // nh
    view = qkv.reshape(bsz, length, dil * ncol)
    slopes = jnp.asarray(2.0 ** (-8.0 * np.arange(1, D_HEADS + 1) / D_HEADS) * LOG2E, dtype=F32)

    def spec(which, shift):
        def imap(b, c, i):
            return (b, jnp.clip(i + shift, 0, nq - 1), c * nslot + which * ng + g)
        return pl.BlockSpec((1, tq, nh), imap)

    ospec = pl.BlockSpec((1, tq, nh), lambda b, c, i: (b, i, c))
    o, lse = pl.pallas_call(
        functools.partial(_attn_d_body, tq=tq, length=length, dil=dil, rad=rad),
        grid=(bsz, dil, nq),
        in_specs=[pl.BlockSpec(memory_space=pltpu.SMEM), spec(0, 0),
                  spec(1, -1), spec(1, 0), spec(1, 1), spec(2, -1), spec(2, 0), spec(2, 1)],
        out_specs=[ospec, ospec],
        out_shape=[jax.ShapeDtypeStruct((bsz, length, dil * nh), BF),
                   jax.ShapeDtypeStruct((bsz, length, dil * nh), F32)],
        compiler_params=_cparams("parallel", "parallel", "arbitrary"),
        name=f"attn_d{g}",
    )(slopes, view, view, view, view, view, view, view)
    return o.reshape(bsz * seq, nh), lse.reshape(bsz * seq, nh)


def _post_d_body(o0, o1, o2, l0, l1, l2, x_ref, w_ref, g_ref, b_ref, y_ref, *, alpha):
    a0, a1, a2 = l0[...], l1[...], l2[...]
    m = jnp.maximum(jnp.maximum(a0, a1), a2)
    e0, e1, e2 = jnp.exp2(a0 - m), jnp.exp2(a1 - m), jnp.exp2(a2 - m)
    den = e0 + e1 + e2
    o = ((e0 / den) * o0[...].astype(F32) + (e1 / den) * o1[...].astype(F32)
         + (e2 / den) * o2[...].astype(F32))
    h = jnp.dot(o.astype(BF), w_ref[...], preferred_element_type=F32)
    y_ref[...] = _layer_norm_rows(alpha * x_ref[...] + h, g_ref[...], b_ref[...])


def _mixer_d_and_post(x2, bsz, seq, w_qkv, w_out, g, b, alpha, tm=512):
    t, d = x2.shape
    nh = D_HEADS * D_HEAD_DIM
    ng = len(D_GROUPS)
    scale = D_HEAD_DIM ** -0.5 * LOG2E
    w = jnp.concatenate([w_qkv[:, :ng * nh] * scale, w_qkv[:, ng * nh:]], axis=-1).astype(BF)
    qkv = _linear(x2, w, BF, tn=ng * nh).reshape(bsz, seq, 3 * ng * nh)
    os_, ls_ = [], []
    for gi, (window, dil) in enumerate(D_GROUPS):
        o, lse = _attn_d_group(qkv, gi, window, dil)
        os_.append(o)
        ls_.append(lse)
    row = lambda i: (i, 0)
    fixed = lambda i: (0, 0)
    return pl.pallas_call(
        functools.partial(_post_d_body, alpha=alpha),
        grid=(t // tm,),
        in_specs=[pl.BlockSpec((tm, nh), row)] * 6 + [
            pl.BlockSpec((tm, d), row), pl.BlockSpec((nh, d), fixed),
            pl.BlockSpec((1, d), fixed), pl.BlockSpec((1, d), fixed)],
        out_specs=pl.BlockSpec((tm, d), row),
        out_shape=jax.ShapeDtypeStruct((t, d), F32),
        compiler_params=_cparams("parallel"),
        name="post_d",
    )(*os_, *ls_, x2, w_out.astype(BF), g.reshape(1, d), b.reshape(1, d))


def _router_gates(x, wh_ref, wl_ref, rb_ref):
    tm = x.shape[0]
    xh = x.astype(BF)
    xl = (x - xh.astype(F32)).astype(BF)
    wh = wh_ref[...]
    logits = (lax.dot_general(wh, xh, NT_DIMS, preferred_element_type=F32)
              + lax.dot_general(wh, xl, NT_DIMS, preferred_element_type=F32)
              + lax.dot_general(wl_ref[...], xh, NT_DIMS, preferred_element_type=F32))
    scores = jax.nn.sigmoid(logits)
    biased = scores + rb_ref[...]
    epg = N_EXPERTS // N_EXPERT_GROUPS
    sc = [scores[e:e + 1, :] for e in range(N_EXPERTS)]
    bi = [biased[e:e + 1, :] for e in range(N_EXPERTS)]
    gs = []
    for g in range(N_EXPERT_GROUPS):
        v = bi[g * epg:(g + 1) * epg]
        best = None
        for a in range(epg):
            for c in range(a + 1, epg):
                pair = v[a] + v[c]
                best = pair if best is None else jnp.maximum(best, pair)
        gs.append(best)
    gmax = functools.reduce(jnp.maximum, gs)
    taken = jnp.zeros((1, tm), jnp.bool_)
    cand = []
    for g in range(N_EXPERT_GROUPS):
        sel = (gs[g] == gmax) & jnp.logical_not(taken)
        taken = taken | sel
        for a in range(epg):
            cand.append(jnp.where(sel, bi[g * epg + a], -jnp.inf))
    m1 = functools.reduce(jnp.maximum, cand)
    taken = jnp.zeros((1, tm), jnp.bool_)
    is1 = []
    for e in range(N_EXPERTS):
        hit = (cand[e] == m1) & jnp.logical_not(taken)
        taken = taken | hit
        is1.append(hit)
    cand2 = [jnp.where(is1[e], -jnp.inf, cand[e]) for e in range(N_EXPERTS)]
    m2 = functools.reduce(jnp.maximum, cand2)
    taken = jnp.zeros((1, tm), jnp.bool_)
    is2 = []
    for e in range(N_EXPERTS):
        hit = (cand2[e] == m2) & jnp.logical_not(taken)
        taken = taken | hit
        is2.append(hit)
    zero = jnp.zeros((1, tm), F32)
    w1 = functools.reduce(jnp.add, [jnp.where(is1[e], sc[e], zero) for e in range(N_EXPERTS)])
    w2 = functools.reduce(jnp.add, [jnp.where(is2[e], sc[e], zero) for e in range(N_EXPERTS)])
    den = w1 + w2
    rows = [jnp.where(is1[e], w1 / den, zero) + jnp.where(is2[e], w2 / den, zero) for e in range(N_EXPERTS)]
    rows.append(jnp.ones((1, tm), F32))
    rows.append(jnp.zeros((LANES - N_EXPERTS - 1, tm), F32))
    return jnp.concatenate(rows, axis=0).T


def _moe_body(x_ref, p_ref, wh_ref, wl_ref, rb_ref, wgu_ref, wd_ref, g_ref, b_ref, wpi_ref, wpg_ref,
              y_ref, gates, xb, acc, *, alpha, n_steps):
    e = pl.program_id(1)

    @pl.when(e == 0)
    def _():
        x = x_ref[...]
        gates[...] = _router_gates(x, wh_ref, wl_ref, rb_ref)
        xb[...] = x.astype(BF)
        acc[...] = jnp.zeros_like(acc)

    gu = jnp.dot(xb[...], wgu_ref[0], preferred_element_type=F32)
    hmid = jax.nn.silu(gu[:, :EXPERT_FF]) * gu[:, EXPERT_FF:]
    lane = lax.broadcasted_iota(jnp.int32, gates.shape, 1)
    gcol = jnp.sum(jnp.where(lane == e, gates[...], 0.0), axis=1, keepdims=True)
    acc[...] += jnp.dot((hmid * gcol).astype(BF), wd_ref[0], preferred_element_type=F32)

    @pl.when(e == n_steps - 1)
    def _():
        x2 = _layer_norm_rows(alpha * x_ref[...] + acc[...], g_ref[...], b_ref[...])
        emb = jnp.dot(p_ref[...].astype(BF), wpi_ref[...], preferred_element_type=F32)
        gate = jax.nn.sigmoid(jnp.dot(x2.astype(BF), wpg_ref[...], preferred_element_type=F32))
        y_ref[...] = x2 + gate * emb


def _moe_layer(x2, p2, router, wgu, wd, g, b, wpi, wpg, alpha, tm=1024):
    t, d = x2.shape
    n_steps = wgu.shape[0]
    wh_t, wl_t, rb = router
    tok = lambda i, e: (i, 0)
    fixed = lambda i, e: (0, 0)
    return pl.pallas_call(
        functools.partial(_moe_body, alpha=alpha, n_steps=n_steps),
        grid=(t // tm, n_steps),
        in_specs=[
            pl.BlockSpec((tm, d), tok), pl.BlockSpec((tm, p2.shape[1]), tok),
            pl.BlockSpec((N_EXPERTS, d), fixed), pl.BlockSpec((N_EXPERTS, d), fixed),
            pl.BlockSpec((N_EXPERTS, 1), fixed),
            pl.BlockSpec((1, d, 2 * EXPERT_FF), lambda i, e: (e, 0, 0)),
            pl.BlockSpec((1, EXPERT_FF, d), lambda i, e: (e, 0, 0)),
            pl.BlockSpec((1, d), fixed), pl.BlockSpec((1, d), fixed),
            pl.BlockSpec(wpi.shape, fixed), pl.BlockSpec(wpg.shape, fixed),
        ],
        out_specs=pl.BlockSpec((tm, d), tok),
        out_shape=jax.ShapeDtypeStruct((t, d), F32),
        scratch_shapes=[pltpu.VMEM((tm, LANES), F32), pltpu.VMEM((tm, d), BF), pltpu.VMEM((tm, d), F32)],
        compiler_params=_cparams("parallel", "arbitrary"),
        name="moe",
    )(x2, p2, wh_t, wl_t, rb, wgu, wd, g.reshape(1, d), b.reshape(1, d), wpi, wpg)


def _prep_shared(prm):
    depth = prm["ln1_g"].shape[0]
    rw = prm["router_w"].astype(F32)
    wh = rw.astype(BF)
    wl = (rw - wh.astype(F32)).astype(BF)
    router = (wh.T, wl.T, prm["router_b"].astype(F32).reshape(N_EXPERTS, 1))
    layers = []
    for i in range(depth):
        wgu = jnp.concatenate([prm["moe_w_gate"][i], prm["moe_w_up"][i]], axis=-1)
        wgu_s = jnp.concatenate([prm["moe_ws_gate"][i], prm["moe_ws_up"][i]], axis=-1)[None]
        wgu = jnp.concatenate([wgu, wgu_s], axis=0).astype(BF)
        wd = jnp.concatenate([prm["moe_w_down"][i], prm["moe_ws_down"][i][None]], axis=0).astype(BF)
        layers.append(dict(wgu=wgu, wd=wd, wpi=prm["ple_w_in"][i].astype(BF), wpg=prm["ple_w_gate"][i].astype(BF)))
    return router, layers


def _trunk(x, p, prm, router, layers):
    depth = prm["ln1_g"].shape[0]
    alpha = (2.0 * depth) ** 0.25
    bsz, seq, d = x.shape
    t = bsz * seq
    x2 = x.reshape(t, d)
    for i in range(depth):
        mixer, j = i % 4, i // 4
        xb = x2.reshape(bsz, seq, d)
        if mixer == 3:
            x2 = _mixer_d_and_post(x2, bsz, seq, prm["d_w_qkv"][j], prm["d_w_out"][j],
                                   prm["ln1_g"][i], prm["ln1_b"][i], alpha)
        else:
            if mixer == 0:
                o = _mixer_a(xb, prm["a_w_in"][j], prm["a_lambda"][j], prm["a_subln"][j], i)
                w_out = prm["a_w_out"][j]
            elif mixer == 1:
                o = _mixer_b(xb, prm["b_w_in"][j], prm["b_q_norm"][j], prm["b_kv_norm"][j],
                             prm["b_w_uq"][j], prm["b_w_ukv"][j])
                w_out = prm["b_w_out"][j]
            else:
                o = _mixer_c(xb, prm["c_w_qkv"][j], prm["c_rpb"][j])
                w_out = prm["c_w_out"][j]
            x2 = _post_attn(o, x2, w_out.astype(BF), prm["ln1_g"][i], prm["ln1_b"][i], alpha)
        lw = layers[i]
        x2 = _moe_layer(x2, p[i].reshape(t, -1), router, lw["wgu"], lw["wd"],
                        prm["ln2_g"][i], prm["ln2_b"][i], lw["wpi"], lw["wpg"], alpha)
    return x2.reshape(bsz, seq, d)


def kernel(x_prompt, x_sample, p_prompt, p_sample, a_w_in, a_lambda, a_subln, a_w_out, b_w_in, b_q_norm, b_kv_norm, b_w_uq, b_w_ukv, b_w_out, c_w_qkv, c_rpb, c_w_out, d_w_qkv, d_w_out, router_w, router_b, moe_w_gate, moe_w_up, moe_w_down, moe_ws_gate, moe_ws_up, moe_ws_down, ln1_g, ln1_b, ln2_g, ln2_b, ple_w_in, ple_w_gate):
    prm = dict(a_w_in=a_w_in, a_lambda=a_lambda, a_subln=a_subln, a_w_out=a_w_out,
               b_w_in=b_w_in, b_q_norm=b_q_norm, b_kv_norm=b_kv_norm, b_w_uq=b_w_uq,
               b_w_ukv=b_w_ukv, b_w_out=b_w_out,
               c_w_qkv=c_w_qkv, c_rpb=c_rpb, c_w_out=c_w_out,
               d_w_qkv=d_w_qkv, d_w_out=d_w_out,
               router_w=router_w, router_b=router_b, moe_w_gate=moe_w_gate, moe_w_up=moe_w_up,
               moe_w_down=moe_w_down, moe_ws_gate=moe_ws_gate, moe_ws_up=moe_ws_up,
               moe_ws_down=moe_ws_down,
               ln1_g=ln1_g, ln1_b=ln1_b, ln2_g=ln2_g, ln2_b=ln2_b,
               ple_w_in=ple_w_in, ple_w_gate=ple_w_gate)
    router, layers = _prep_shared(prm)
    y_prompt = _trunk(x_prompt, p_prompt, prm, router, layers)
    y_sample = _trunk(x_sample, p_sample, prm, router, layers)
    return (y_prompt, y_sample)
```

```python
import functools
import math

import numpy as np
import jax
import jax.numpy as jnp
from jax import lax
from jax.experimental import pallas as pl
from jax.experimental.pallas import tpu as pltpu

BF = jnp.bfloat16
F32 = jnp.float32

VMEM_LIMIT_BYTES = 56 * 1024 * 1024
LANES = 128

GRID_W = 64
LN_EPS = 1e-5
RMS_EPS = 1e-6
ROPE_THETA = 10000.0
NEG = -1e30

A_HEADS, A_HEAD_DIM = 8, 64
B_HEADS, B_NOPE, B_ROPE, B_VDIM, B_Q_LORA, B_KV_LORA = 16, 64, 32, 64, 384, 256
C_HEADS, C_HEAD_DIM, C_WIN_ROWS, C_WIN_COLS = 16, 64, 8, 16
D_HEADS, D_HEAD_DIM = 8, 64
D_GROUPS = ((128, 1), (512, 4), (2048, 16))
N_EXPERTS, N_EXPERT_GROUPS, EXPERT_FF = 16, 4, 256

NT_DIMS = (((1,), (1,)), ((), ()))


def _cparams(*sem):
    return pltpu.CompilerParams(dimension_semantics=sem, vmem_limit_bytes=VMEM_LIMIT_BYTES)


def _layer_norm_rows(z, g, b):
    mu = jnp.mean(z, axis=-1, keepdims=True)
    zc = z - mu
    var = jnp.mean(zc * zc, axis=-1, keepdims=True)
    return zc * lax.rsqrt(var + LN_EPS) * g + b


def _rms_rows(z, g):
    return z * lax.rsqrt(jnp.mean(z * z, axis=-1, keepdims=True) + RMS_EPS) * g


def _linear_body(x_ref, w_ref, o_ref):
    o_ref[...] = jnp.dot(x_ref[...].astype(BF), w_ref[...], preferred_element_type=F32).astype(o_ref.dtype)


def _linear(x, w, out_dtype, tm=512, tn=None):
    m, k = x.shape
    n = w.shape[1]
    tn = n if tn is None else tn
    return pl.pallas_call(
        _linear_body,
        grid=(n // tn, m // tm),
        in_specs=[pl.BlockSpec((tm, k), lambda j, i: (i, 0)), pl.BlockSpec((k, tn), lambda j, i: (0, j))],
        out_specs=pl.BlockSpec((tm, tn), lambda j, i: (i, j)),
        out_shape=jax.ShapeDtypeStruct((m, n), out_dtype),
        compiler_params=_cparams("parallel", "parallel"),
        name="linear",
    )(x, w)


def _post_body(o_ref, x_ref, w_ref, g_ref, b_ref, y_ref, *, alpha):
    h = jnp.dot(o_ref[...], w_ref[...], preferred_element_type=F32)
    y_ref[...] = _layer_norm_rows(alpha * x_ref[...] + h, g_ref[...], b_ref[...])


def _post_attn(o, x, w_out, g, b, alpha, tm=512):
    t, ko = o.shape
    d = x.shape[1]
    return pl.pallas_call(
        functools.partial(_post_body, alpha=alpha),
        grid=(t // tm,),
        in_specs=[
            pl.BlockSpec((tm, ko), lambda i: (i, 0)),
            pl.BlockSpec((tm, d), lambda i: (i, 0)),
            pl.BlockSpec((ko, d), lambda i: (0, 0)),
            pl.BlockSpec((1, d), lambda i: (0, 0)),
            pl.BlockSpec((1, d), lambda i: (0, 0)),
        ],
        out_specs=pl.BlockSpec((tm, d), lambda i: (i, 0)),
        out_shape=jax.ShapeDtypeStruct((t, d), F32),
        compiler_params=_cparams("parallel"),
        name="post_attn",
    )(o, x, w_out, g.reshape(1, d), b.reshape(1, d))


LOG2E = 1.4426950408889634
SUM_ROWS = 16


def _softmax_chunk(s, m, shift):
    mx = jnp.max(s, axis=0, keepdims=True)
    if shift is not None:
        mx = mx - shift
    m_new = jnp.maximum(m, mx)
    ref = m_new if shift is None else m_new + shift
    return m_new, jnp.exp2(s - ref).astype(BF)


def _pipelined_flash(n, nh, tq, qk_fn, sm_fn, pv_fn, s_buf, p_buf, acc_ref):
    assert n >= 4 and n % 2 == 0
    acc_ref[...] = jnp.zeros_like(acc_ref)

    def stage(j, slot, ms, alphas, do_pv=True, do_qk=True, first=False):
        if do_pv:
            for h in range(nh):
                acc_ref[h] = alphas[h] * acc_ref[h] + pv_fn(j - 1, h, p_buf[1 - slot, h])
        if do_qk:
            for h in range(nh):
                s_buf[1 - slot, h] = qk_fn(j + 1, h)
        new_ms, new_alphas = [], []
        for h in range(nh):
            m_new, p = sm_fn(j, h, s_buf[slot, h], ms[h], first)
            p_buf[slot, h] = p
            new_alphas.append(jnp.exp2(ms[h] - m_new))
            new_ms.append(m_new)
        return tuple(new_ms), tuple(new_alphas)

    for h in range(nh):
        s_buf[0, h] = qk_fn(0, h)
    ms = tuple(jnp.full((1, tq), NEG, F32) for _ in range(nh))
    alphas = tuple(jnp.zeros((1, tq), F32) for _ in range(nh))
    ms, alphas = stage(0, 0, ms, alphas, do_pv=False, first=True)

    def body(t, carry):
        ms, alphas = carry
        ms, alphas = stage(2 * t + 1, 1, ms, alphas)
        return stage(2 * t + 2, 0, ms, alphas)

    ms, alphas = lax.fori_loop(0, (n - 2) // 2, body, (ms, alphas))
    ms, alphas = stage(n - 1, 1, ms, alphas, do_qk=False)
    for h in range(nh):
        acc_ref[h] = alphas[h] * acc_ref[h] + pv_fn(n - 1, h, p_buf[1, h])


def _chunk_off(j, tk):
    return j * tk if isinstance(j, int) else pl.multiple_of(j * tk, tk)


def _attn_a_body(cs_ref, ctab_ref, lam_ref, g_ref, q_ref, k_ref, kpos_ref, vt_ref, o_ref,
                 s_buf, p_buf, acc_ref, *, seq, tq, tk, lam_init):
    h = pl.program_id(1)
    qi = pl.program_id(2)
    n = seq // tk
    cd = (qi * tq) // tk
    c_slope = cs_ref[h]
    q = q_ref[0]
    lane = lax.broadcasted_iota(jnp.int32, q.shape, 1)
    qaug = jnp.broadcast_to(ctab_ref[pl.ds(h, 1), :], (tq, LANES)).astype(BF)
    qpos = (qi * tq + lax.broadcasted_iota(jnp.int32, (1, tq), 1)).astype(F32)
    f0 = c_slope * qpos
    dv = vt_ref.shape[2] - SUM_ROWS
    q_left, q_right = [], []
    for half in range(2):
        q_m = jnp.where((lane >= 64 * half) & (lane < 64 * (half + 1)), q, jnp.zeros_like(q))
        q_left.append(jnp.concatenate([q_m, qaug], axis=1))
        q_right.append(jnp.concatenate([q_m, -qaug], axis=1))

    def chunk_of(j):
        jm = j - 1
        c = jnp.where(j == 0, cd, jm + (jm >= cd).astype(jnp.int32))
        return c, c <= cd

    def qk_fn(j, half):
        c, left = chunk_of(j)
        off = pl.multiple_of(c * tk, tk)
        k_c = jnp.concatenate([k_ref[0, pl.ds(off, tk), :], kpos_ref[pl.ds(off, tk), :]], axis=1)
        q_full = jnp.where(left, q_left[half], q_right[half])
        return lax.dot_general(k_c, q_full, NT_DIMS, preferred_element_type=F32)

    def sm_fn(j, half, s, m, first):
        if first:
            d = (lax.broadcasted_iota(jnp.int32, (tk, tq), 0) - lax.broadcasted_iota(jnp.int32, (tk, tq), 1)
                 + (cd * tk - qi * tq)).astype(F32)
            return _softmax_chunk(s - (2.0 * c_slope) * jnp.maximum(d, 0.0), m, f0)
        _, left = chunk_of(j)
        return _softmax_chunk(s, m, jnp.where(left, f0, -f0))

    def pv_fn(j, half, p):
        c, _ = chunk_of(j)
        off = pl.multiple_of(c * tk, tk)
        return jnp.dot(vt_ref[0, 0, :, pl.ds(off, tk)], p, preferred_element_type=F32)

    _pipelined_flash(n, 2, tq, qk_fn, sm_fn, pv_fn, s_buf, p_buf, acc_ref)
    outs = [acc_ref[half, :dv, :] / acc_ref[half, dv:dv + 1, :] for half in range(2)]
    lf = lam_ref[...]
    lam_full = (jnp.exp(jnp.sum(lf[0:1] * lf[1:2], axis=-1, keepdims=True))
                - jnp.exp(jnp.sum(lf[2:3] * lf[3:4], axis=-1, keepdims=True)) + lam_init)
    o = outs[0] - lam_full * outs[1]
    ms = jnp.mean(o * o, axis=0, keepdims=True)
    o = o * lax.rsqrt(ms + RMS_EPS) * g_ref[...] * (1.0 - lam_init)
    o_ref[0] = o.T.astype(o_ref.dtype)


def _bf16_pieces(c, n=3):
    pieces, rest = [], c.astype(F32)
    for _ in range(n):
        p = rest.astype(BF)
        pieces.append(p.astype(F32))
        rest = rest - p.astype(F32)
    return pieces


def _store_vt_heads(vt_ref, vt, dv, heads_per_slot):
    tm = vt.shape[1]
    tail = jnp.where(lax.broadcasted_iota(jnp.int32, (SUM_ROWS, tm), 0) == 0, 1.0, 0.0).astype(vt_ref.dtype)
    rows = dv + SUM_ROWS
    for h in range(vt.shape[0] // dv):
        slot, r0 = h // heads_per_slot, (h % heads_per_slot) * rows
        vt_ref[0, slot, r0:r0 + dv, :] = vt[h * dv:(h + 1) * dv, :].astype(vt_ref.dtype)
        vt_ref[0, slot, r0 + dv:r0 + rows, :] = tail


def _proj_a_body(x_ref, wqk_ref, wvt_ref, qk_ref, vt_ref):
    xb = x_ref[...].astype(BF)
    qk_ref[...] = jnp.dot(xb, wqk_ref[...], preferred_element_type=F32).astype(BF)
    vt = lax.dot_general(wvt_ref[...], xb, NT_DIMS, preferred_element_type=F32)
    _store_vt_heads(vt_ref, vt, 2 * A_HEAD_DIM, 1)


def _flash_chunk_len(seq, tk_max):
    tk = tk_max
    while seq % tk or (seq // tk) < 4 or (seq // tk) % 2:
        tk //= 2
    return tk


def _flash_scratch(nh, rows, tq, tk):
    return [pltpu.VMEM((2, nh, tk, tq), F32), pltpu.VMEM((2, nh, tk, tq), BF), pltpu.VMEM((nh, rows, tq), F32)]


def _attn_a(qkv, vt, lam, subln_g, lam_init, tq=256, tk=1024):
    bsz, seq, _ = qkv.shape
    tq, tk = min(tq, seq), _flash_chunk_len(seq, tk)
    c_slope = jnp.asarray(2.0 ** (-8.0 * np.arange(1, A_HEADS + 1) / A_HEADS) * LOG2E, dtype=F32)
    c1, c2, c3 = _bf16_pieces(c_slope)
    ctab = jnp.stack([c1, c1, c2, c2, c3, c3], axis=-1)
    ctab = jnp.concatenate([ctab, jnp.zeros((A_HEADS, LANES - 6), F32)], axis=-1)
    pos = np.arange(seq)
    kpos = np.zeros((seq, LANES), np.float32)
    for j in range(3):
        kpos[:, 2 * j] = (pos // LANES) * LANES
        kpos[:, 2 * j + 1] = pos % LANES
    kpos = jnp.asarray(kpos, BF)
    vrows = vt.shape[2]
    body = functools.partial(_attn_a_body, seq=seq, tq=tq, tk=tk, lam_init=lam_init)
    return pl.pallas_call(
        body,
        grid=(bsz, A_HEADS, seq // tq),
        in_specs=[
            pl.BlockSpec(memory_space=pltpu.SMEM),
            pl.BlockSpec((A_HEADS, LANES), lambda b, h, i: (0, 0)),
            pl.BlockSpec((4, A_HEAD_DIM), lambda b, h, i: (0, 0)),
            pl.BlockSpec((2 * A_HEAD_DIM, 1), lambda b, h, i: (0, 0)),
            pl.BlockSpec((1, tq, LANES), lambda b, h, i: (b, i, h)),
            pl.BlockSpec((1, seq, LANES), lambda b, h, i: (b, 0, A_HEADS + h)),
            pl.BlockSpec((seq, LANES), lambda b, h, i: (0, 0)),
            pl.BlockSpec((1, 1, vrows, seq), lambda b, h, i: (b, h, 0, 0)),
        ],
        out_specs=pl.BlockSpec((1, tq, LANES), lambda b, h, i: (b, i, h)),
        out_shape=jax.ShapeDtypeStruct((bsz, seq, A_HEADS * LANES), BF),
        scratch_shapes=_flash_scratch(2, vrows, tq, tk),
        compiler_params=_cparams("parallel", "parallel", "arbitrary"),
        name="attn_a",
    )(c_slope, ctab, lam.astype(F32), subln_g.astype(F32).reshape(2 * A_HEAD_DIM, 1), qkv, qkv, kpos, vt)


def _attn_b_body(q_ref, k_ref, vt_ref, o_ref, s_buf, p_buf, acc_ref, *, seq, tk):
    tq = q_ref.shape[1]
    rows = B_VDIM + SUM_ROWS
    qs = [q_ref[0, :, h * LANES:(h + 1) * LANES] for h in range(2)]

    def qk_fn(j, h):
        k_c = k_ref[0, pl.ds(_chunk_off(j, tk), tk), h * LANES:(h + 1) * LANES]
        return lax.dot_general(k_c, qs[h], NT_DIMS, preferred_element_type=F32)

    def sm_fn(j, h, s, m, first):
        return _softmax_chunk(s, m, None)

    def pv_fn(j, h, p):
        vt_c = vt_ref[0, 0, h * rows:(h + 1) * rows, pl.ds(_chunk_off(j, tk), tk)]
        return jnp.dot(vt_c, p, preferred_element_type=F32)

    _pipelined_flash(seq // tk, 2, tq, qk_fn, sm_fn, pv_fn, s_buf, p_buf, acc_ref)
    outs = [acc_ref[h, :B_VDIM, :] / acc_ref[h, B_VDIM:B_VDIM + 1, :] for h in range(2)]
    o = jnp.concatenate(outs, axis=0)
    o_ref[0] = o.T.astype(o_ref.dtype)


def _attn_b(q, k, vt, tq=256, tk=1024):
    bsz, seq, _ = q.shape
    tq, tk = min(tq, seq), _flash_chunk_len(seq, tk)
    npair = B_HEADS // 2
    return pl.pallas_call(
        functools.partial(_attn_b_body, seq=seq, tk=tk),
        grid=(bsz, npair, seq // tq),
        in_specs=[
            pl.BlockSpec((1, tq, 2 * LANES), lambda b, h, i: (b, i, h)),
            pl.BlockSpec((1, seq, 2 * LANES), lambda b, h, i: (b, 0, h)),
            pl.BlockSpec((1, 1, 2 * (B_VDIM + SUM_ROWS), seq), lambda b, h, i: (b, h, 0, 0)),
        ],
        out_specs=pl.BlockSpec((1, tq, LANES), lambda b, h, i: (b, i, h)),
        out_shape=jax.ShapeDtypeStruct((bsz, seq, B_HEADS * B_VDIM), BF),
        scratch_shapes=_flash_scratch(2, B_VDIM + SUM_ROWS, tq, tk),
        compiler_params=_cparams("parallel", "parallel", "arbitrary"),
        name="attn_b",
    )(q, k, vt)


def _mla_in_body(x_ref, w_ref, gq_ref, gkv_ref, rc_ref, rs_ref, cq_ref, ckv_ref, kr_ref):
    h = jnp.dot(x_ref[...].astype(BF), w_ref[...], preferred_element_type=F32)
    cq_ref[...] = _rms_rows(h[:, :B_Q_LORA], gq_ref[...]).astype(BF)
    ckv_ref[...] = _rms_rows(h[:, B_Q_LORA:B_Q_LORA + B_KV_LORA], gkv_ref[...]).astype(BF)
    t = h[:, B_Q_LORA + B_KV_LORA:]
    kr = t * rc_ref[...] + pltpu.roll(t, LANES - B_ROPE, 1) * rs_ref[...]
    kr_ref[...] = kr.astype(BF)


def _mla_q_body(cq_ref, w_ref, ta_ref, tb_ref, q_ref):
    t = jnp.dot(cq_ref[...], w_ref[...], preferred_element_type=F32)
    ta = ta_ref[...]
    tb = tb_ref[...]
    for h in range(B_HEADS):
        th = t[:, h * LANES:(h + 1) * LANES]
        q_ref[:, h * LANES:(h + 1) * LANES] = (th * ta + pltpu.roll(th, LANES - B_ROPE, 1) * tb).astype(BF)


def _mla_kv_body(ckv_ref, kr_ref, wk_ref, e_ref, wvt_ref, k_ref, vt_ref):
    ckv = ckv_ref[...]
    k = jnp.dot(ckv, wk_ref[...], preferred_element_type=F32)
    k = k + jnp.dot(kr_ref[...], e_ref[...], preferred_element_type=F32)
    k_ref[...] = k.astype(BF)
    vt = lax.dot_general(wvt_ref[...], ckv, NT_DIMS, preferred_element_type=F32)
    _store_vt_heads(vt_ref, vt, B_VDIM, 2)


def _rope_partner(w):
    half = B_ROPE // 2
    return jnp.concatenate([-w[..., half:], w[..., :half]], axis=-1)


def _mla_tables(seq):
    inv = 1.0 / (ROPE_THETA ** (np.arange(0, B_ROPE, 2, dtype=np.float32) / B_ROPE))
    ang = jnp.arange(seq, dtype=F32)[:, None] * jnp.asarray(inv, F32)[None, :]
    cos = jnp.concatenate([jnp.cos(ang), jnp.cos(ang)], axis=-1)
    sin = jnp.concatenate([jnp.sin(ang), jnp.sin(ang)], axis=-1)
    z = lambda n: jnp.zeros((seq, n), F32)
    rc = jnp.concatenate([cos, z(LANES - B_ROPE)], axis=-1)
    rs = jnp.concatenate([sin, z(LANES - B_ROPE)], axis=-1)
    scale = (B_NOPE + B_ROPE) ** -0.5 * LOG2E
    ta =jnp.concatenate([jnp.full((seq, B_NOPE), scale, F32), cos * scale, z(B_ROPE)], axis=-1)
    tb = jnp.concatenate([z(B_NOPE), sin * scale, z(B_ROPE)], axis=-1)
    return rc, rs, ta, tb


def _mla_weights(w_in, w_uq, w_ukv):
    d = w_in.shape[0]
    kr0 = B_Q_LORA + B_KV_LORA
    w_in_ext = jnp.concatenate(
        [w_in, _rope_partner(w_in[:, kr0:kr0 + B_ROPE]), jnp.zeros((d, LANES - 2 * B_ROPE), w_in.dtype)], axis=-1)
    wq = w_uq.reshape(B_Q_LORA, B_HEADS, B_NOPE + B_ROPE)
    wq_ext = jnp.concatenate([wq, _rope_partner(wq[..., B_NOPE:])], axis=-1).reshape(B_Q_LORA, B_HEADS * LANES)
    wkv = w_ukv.reshape(B_KV_LORA, B_HEADS, B_NOPE + B_VDIM)
    wk = jnp.concatenate([wkv[..., :B_NOPE], jnp.zeros((B_KV_LORA, B_HEADS, LANES - B_NOPE), w_ukv.dtype)], axis=-1)
    wk = wk.reshape(B_KV_LORA, B_HEADS * LANES)
    wv = wkv[..., B_NOPE:].reshape(B_KV_LORA, B_HEADS * B_VDIM)
    place = np.zeros((LANES, B_HEADS, LANES), np.float32)
    for j in range(B_ROPE):
        place[j, :, B_NOPE + j] = 1.0
    place = jnp.asarray(place.reshape(LANES, B_HEADS * LANES), BF)
    return w_in_ext.astype(BF), wq_ext.astype(BF), wk.astype(BF), place, wv.astype(BF)


def _mixer_b(x, w_in, q_norm_g, kv_norm_g, w_uq, w_ukv, tm=512):
    bsz, seq, d = x.shape
    t = bsz * seq
    tm = min(tm, seq)
    nblk = seq // tm
    w_in_ext, wq_ext, wk, place, wv = _mla_weights(w_in, w_uq, w_ukv)
    rc, rs, ta, tb = _mla_tables(seq)
    x2 = x.reshape(t, d)
    n_in = w_in_ext.shape[1]
    row = lambda i: (i, 0)
    fixed = lambda i: (0, 0)
    pos = lambda i: (i % nblk, 0)
    cq, ckv, kr = pl.pallas_call(
        _mla_in_body,
        grid=(t // tm,),
        in_specs=[
            pl.BlockSpec((tm, d), row), pl.BlockSpec((d, n_in), fixed),
            pl.BlockSpec((1, B_Q_LORA), fixed), pl.BlockSpec((1, B_KV_LORA), fixed),
            pl.BlockSpec((tm, LANES), pos), pl.BlockSpec((tm, LANES), pos),
        ],
        out_specs=[pl.BlockSpec((tm, B_Q_LORA), row), pl.BlockSpec((tm, B_KV_LORA), row),
                   pl.BlockSpec((tm, LANES), row)],
        out_shape=[jax.ShapeDtypeStruct((t, B_Q_LORA), BF), jax.ShapeDtypeStruct((t, B_KV_LORA), BF),
                   jax.ShapeDtypeStruct((t, LANES), BF)],
        compiler_params=_cparams("parallel"),
        name="mla_in",
    )(x2, w_in_ext, q_norm_g.astype(F32).reshape(1, -1), kv_norm_g.astype(F32).reshape(1, -1), rc, rs)
    nq = B_HEADS * LANES
    q = pl.pallas_call(
        _mla_q_body,
        grid=(t // tm,),
        in_specs=[pl.BlockSpec((tm, B_Q_LORA), row), pl.BlockSpec((B_Q_LORA, nq), fixed),
                  pl.BlockSpec((tm, LANES), pos), pl.BlockSpec((tm, LANES), pos)],
        out_specs=pl.BlockSpec((tm, nq), row),
        out_shape=jax.ShapeDtypeStruct((t, nq), BF),
        compiler_params=_cparams("parallel"),
        name="mla_q",
    )(cq, wq_ext, ta, tb)
    nv = B_HEADS * B_VDIM
    npair = B_HEADS // 2
    vrows = 2 * (B_VDIM + SUM_ROWS)
    k, vt = pl.pallas_call(
        _mla_kv_body,
        grid=(t // tm,),
        in_specs=[pl.BlockSpec((tm, B_KV_LORA), row), pl.BlockSpec((tm, LANES), row),
                  pl.BlockSpec((B_KV_LORA, nq), fixed), pl.BlockSpec((LANES, nq), fixed),
                  pl.BlockSpec((nv, B_KV_LORA), fixed)],
        out_specs=[pl.BlockSpec((tm, nq), row),
                   pl.BlockSpec((1, npair, vrows, tm), lambda i: (i // nblk, 0, 0, i % nblk))],
        out_shape=[jax.ShapeDtypeStruct((t, nq), BF), jax.ShapeDtypeStruct((bsz, npair, vrows, seq), BF)],
        compiler_params=_cparams("parallel"),
        name="mla_kv",
    )(ckv, kr, wk, place, wv.T)
    o = _attn_b(q.reshape(bsz, seq, nq), k.reshape(bsz, seq, nq), vt)
    return o.reshape(t, nv)


def _mixer_a(x, w_in, lam, subln_g, layer_idx):
    bsz, seq, d = x.shape
    t = bsz * seq
    hd2 = 2 * A_HEAD_DIM
    nq = A_HEADS * hd2
    scale = A_HEAD_DIM ** -0.5 * LOG2E
    wqk = jnp.concatenate([w_in[:, :nq] * scale, w_in[:, nq:2 * nq]], axis=-1).astype(BF)
    wvt = w_in[:, 2 * nq:].T.astype(BF)
    tm = min(512, seq)
    nblk = seq // tm
    vrows = hd2 + SUM_ROWS
    qk, vt = pl.pallas_call(
        _proj_a_body,
        grid=(t // tm,),
        in_specs=[pl.BlockSpec((tm, d), lambda i: (i, 0)), pl.BlockSpec((d, 2 * nq), lambda i: (0, 0)),
                  pl.BlockSpec((nq, d), lambda i: (0, 0))],
        out_specs=[pl.BlockSpec((tm, 2 * nq), lambda i: (i, 0)),
                   pl.BlockSpec((1, A_HEADS, vrows, tm), lambda i: (i // nblk, 0, 0, i % nblk))],
        out_shape=[jax.ShapeDtypeStruct((t, 2 * nq), BF), jax.ShapeDtypeStruct((bsz, A_HEADS, vrows, seq), BF)],
        compiler_params=_cparams("parallel"),
        name="proj_a",
    )(x.reshape(t, d), wqk, wvt)
    lam_init = 0.8 - 0.6 * math.exp(-0.3 * layer_idx)
    o = _attn_a(qk.reshape(bsz, seq, 2 * nq), vt, lam, subln_g, lam_init)
    return o.reshape(t, nq)


def _attn_c_body(q_ref, k0, k1, k2, k3, v0, v1, v2, v3, bt_ref, o_ref):
    q = q_ref[0]
    kw = jnp.concatenate([k0[0], k1[0], k2[0], k3[0]], axis=0)
    vw = jnp.concatenate([v0[0], v1[0], v2[0], v3[0]], axis=0)
    lane = lax.broadcasted_iota(jnp.int32, q.shape, 1)
    outs = []
    for hh in range(2):
        qm = jnp.where((lane >= 64 * hh) & (lane < 64 * (hh + 1)), q, jnp.zeros_like(q))
        s = lax.dot_general(qm, kw, NT_DIMS, preferred_element_type=F32) + bt_ref[hh, 0]
        m = jnp.max(s, axis=1, keepdims=True)
        p = jnp.exp2(s - m)
        l = jnp.sum(p, axis=1, keepdims=True)
        outs.append(jnp.dot(p.astype(BF), vw, preferred_element_type=F32) / l)
    o_ref[0] = jnp.where(lane < 64, outs[0], outs[1]).astype(o_ref.dtype)


def _nbr_bias_table(rpb):
    col = np.arange(GRID_W)
    col_start = np.clip(col - C_WIN_COLS // 2, 0, GRID_W - C_WIN_COLS)
    col_mask = (col[None, :] >= col_start[:, None]) & (col[None, :] < col_start[:, None] + C_WIN_COLS)
    pad = GRID_W - C_WIN_COLS
    ext = jnp.pad(rpb.astype(F32) * LOG2E, ((0, 0), (0, 0), (pad, pad)), mode="edge")
    toep = jnp.stack([ext[:, :, GRID_W - 1 - qc:2 * GRID_W - 1 - qc] for qc in range(GRID_W)], axis=2)
    toep = jnp.where(jnp.asarray(col_mask)[None, None], toep, NEG)
    neg = jnp.full((rpb.shape[0], GRID_W, GRID_W), NEG, F32)
    half = C_WIN_ROWS // 2
    kinds = []
    for kind in range(3):
        qrows = []
        for t in range(C_WIN_ROWS):
            u0 = (max(t, half), t, min(t, half))[kind]
            blocks = [toep[:, u - t + half - 1] if u0 <= u < u0 + C_WIN_ROWS else neg
                      for u in range(2 * C_WIN_ROWS)]
            qrows.append(jnp.concatenate(blocks, axis=-1))
        kinds.append(jnp.concatenate(qrows, axis=1))
    return jnp.stack(kinds, axis=1)


def _mixer_c(x, w_qkv, rpb):
    bsz, seq, d = x.shape
    t = bsz * seq
    nq = C_HEADS * C_HEAD_DIM
    scale = C_HEAD_DIM ** -0.5 * LOG2E
    w = jnp.concatenate([w_qkv[:, :nq] * scale, w_qkv[:, nq:]], axis=-1).astype(BF)
    qkv = _linear(x.reshape(t, d), w, BF, tn=nq).reshape(bsz, seq, 3 * nq)
    bt = _nbr_bias_table(rpb)
    npair = C_HEADS // 2
    qtok = C_WIN_ROWS * GRID_W
    ngrp = seq // qtok
    assert ngrp >= 2
    kb = qtok // 2
    nkb = seq // kb
    kspecs = []
    for off in (npair, 2 * npair):
        for j in range(4):
            kspecs.append(pl.BlockSpec(
                (1, kb, LANES),
                lambda h, g, b, j=j, off=off: (b, jnp.clip(2 * g - 1 + j, 0, nkb - 1), off + h)))
    kind = lambda g: jnp.where(g == 0, 0, jnp.where(g == ngrp - 1, 2, 1))
    o = pl.pallas_call(
        _attn_c_body,
        grid=(npair, ngrp, bsz),
        in_specs=[pl.BlockSpec((1, qtok, LANES), lambda h, g, b: (b, g, h))] + kspecs + [
            pl.BlockSpec((2, 1, qtok, 2 * qtok), lambda h, g, b: (h, kind(g), 0, 0))],
        out_specs=pl.BlockSpec((1, qtok, LANES), lambda h, g, b: (b, g, h)),
        out_shape=jax.ShapeDtypeStruct((bsz, seq, nq), BF),
        compiler_params=_cparams("parallel", "parallel", "arbitrary"),
        name="attn_c",
    )(qkv, *([qkv] * 8), bt)
    return o.reshape(t, nq)


def _attn_d_body(slopes_ref, q_ref, kp, kc, kn, vp, vc, vn, o_ref, lse_ref, *, tq, length, dil, rad):
    i = pl.program_id(2)
    nk = tq + 2 * rad
    qi = lax.broadcasted_iota(jnp.int32, (tq, nk), 0)
    kk = lax.broadcasted_iota(jnp.int32, (tq, nk), 1)
    dist = jnp.abs(kk - rad - qi)
    ki = i * tq - rad + kk
    valid = (dist <= rad) & (ki >= 0) & (ki < length)
    distf = dist.astype(F32) * float(dil)
    lane = lax.broadcasted_iota(jnp.int32, (tq, LANES), 1)
    for hp in range(D_HEADS // 2):
        sl = slice(hp * LANES, (hp + 1) * LANES)
        q = q_ref[0, 0, :, sl]
        kw = jnp.concatenate([kp[0, 0, tq - rad:tq, sl], kc[0, 0, :, sl], kn[0, 0, 0:rad, sl]], axis=0)
        vw = jnp.concatenate([vp[0, 0, tq - rad:tq, sl], vc[0, 0, :, sl], vn[0, 0, 0:rad, sl]], axis=0)
        outs, lses = [], []
        for hh in range(2):
            slope = slopes_ref[hp * 2 + hh]
            qm = jnp.where((lane >= 64 * hh) & (lane < 64 * (hh + 1)), q, jnp.zeros_like(q))
            s = lax.dot_general(qm, kw, NT_DIMS, preferred_element_type=F32)
            s = jnp.where(valid, s - slope * distf, NEG)
            m = jnp.max(s, axis=1, keepdims=True)
            p = jnp.exp2(s - m)
            l = jnp.sum(p, axis=1, keepdims=True)
            outs.append(jnp.dot(p.astype(BF), vw, preferred_element_type=F32) / l)
            lses.append(m + jnp.log(l) * LOG2E)
        o_ref[0, 0, :, sl] = jnp.where(lane < 64, outs[0], outs[1]).astype(o_ref.dtype)
        lse_ref[0, 0, :, sl] = jnp.where(lane < 64, lses[0], lses[1])


def _attn_d_group(qkv, window, dil, tq=256):
    bsz, _, length, ncol = qkv.shape
    rad = window // (2 * dil)
    tq = min(tq, length)
    nq = length // tq
    nh = ncol // 3
    slopes = jnp.asarray(2.0 ** (-8.0 * np.arange(1, D_HEADS + 1) / D_HEADS) * LOG2E, dtype=F32)

    def spec(which, shift):
        return pl.BlockSpec((1, 1, tq, nh), lambda b, c, i: (b, c, jnp.clip(i + shift, 0, nq - 1), which))

    ospec = pl.BlockSpec((1, 1, tq, nh), lambda b, c, i: (b, c, i, 0))
    return pl.pallas_call(
        functools.partial(_attn_d_body, tq=tq, length=length, dil=dil, rad=rad),
        grid=(bsz, dil, nq),
        in_specs=[pl.BlockSpec(memory_space=pltpu.SMEM), spec(0, 0),
                  spec(1, -1), spec(1, 0), spec(1, 1), spec(2, -1), spec(2, 0), spec(2, 1)],
        out_specs=[ospec, ospec],
        out_shape=[jax.ShapeDtypeStruct((bsz, dil, length, nh), BF),
                   jax.ShapeDtypeStruct((bsz, dil, length, nh), F32)],
        compiler_params=_cparams("parallel", "parallel", "arbitrary"),
        name=f"attn_d{dil}",
    )(slopes, qkv, qkv, qkv, qkv, qkv, qkv, qkv)


def _proj_d_body(x_ref, w_ref, o0, o1, o2, scr, *, dils):
    xb = x_ref[...].astype(BF)
    tm = xb.shape[0]
    nblk = scr.shape[0]
    ncol = nblk * LANES
    for gi, (o_ref, dil) in enumerate(zip((o0, o1, o2), dils)):
        res = jnp.dot(xb, w_ref[:, gi * ncol:(gi + 1) * ncol], preferred_element_type=F32)
        if dil == 1:
            o_ref[0, 0] = res.astype(BF)
            continue
        for j in range(nblk):
            scr[j] = res[:, j * LANES:(j + 1) * LANES]
        for c in range(dil):
            for j in range(nblk):
                o_ref[0, c, :, j * LANES:(j + 1) * LANES] = scr[j, pl.ds(c, tm // dil, stride=dil), :].astype(BF)


def _post_d_body(o0, o1, o2, l0, l1, l2, x_ref, w_ref, g_ref, b_ref, y_ref, scr_o, scr_l, *, alpha, dils):
    tm = x_ref.shape[0]

    def interleaved(ref, scr, dil):
        if dil == 1:
            return ref[0, 0].astype(F32)
        nblk = scr.shape[0]
        for c in range(dil):
            for j in range(nblk):
                scr[j, pl.ds(c, tm // dil, stride=dil), :] = ref[0, c, :, j * LANES:(j + 1) * LANES].astype(F32)
        return jnp.concatenate([scr[j] for j in range(nblk)], axis=1)

    ov = [interleaved(r, scr_o.at[j], dil) for j, (r, dil) in enumerate(zip((o0, o1, o2), dils))]
    a0, a1, a2 = [interleaved(r, scr_l.at[j], dil) for j, (r, dil) in enumerate(zip((l0, l1, l2), dils))]
    m = jnp.maximum(jnp.maximum(a0, a1), a2)
    e0, e1, e2 = jnp.exp2(a0 - m), jnp.exp2(a1 - m), jnp.exp2(a2 - m)
    den = e0 + e1 + e2
    o = (e0 / den) * ov[0] + (e1 / den) * ov[1] + (e2 / den) * ov[2]
    h = jnp.dot(o.astype(BF), w_ref[...], preferred_element_type=F32)
    y_ref[...] = _layer_norm_rows(alpha * x_ref[...] + h, g_ref[...], b_ref[...])


def _mixer_d_and_post(x2, bsz, seq, w_qkv, w_out, g, b, alpha, tm=512):
    t, d = x2.shape
    nh = D_HEADS * D_HEAD_DIM
    ng = len(D_GROUPS)
    dils = tuple(dil for _, dil in D_GROUPS)
    tm = min(tm, seq)
    nblk = seq // tm
    scale = D_HEAD_DIM ** -0.5 * LOG2E
    wq = w_qkv.reshape(d, 3, ng, nh)
    wq = jnp.stack([wq[:, 0] * scale, wq[:, 1], wq[:, 2]], axis=1)
    w = wq.transpose(0, 2, 1, 3).reshape(d, ng * 3 * nh).astype(BF)

    def deint_spec(dil, width):
        return pl.BlockSpec((1, dil, tm // dil, width), lambda i: (i // nblk, 0, i % nblk, 0))

    qkvs = pl.pallas_call(
        functools.partial(_proj_d_body, dils=dils),
        grid=(t // tm,),
        in_specs=[pl.BlockSpec((tm, d), lambda i: (i, 0)), pl.BlockSpec((d, ng * 3 * nh), lambda i: (0, 0))],
        out_specs=[deint_spec(dil, 3 * nh) for dil in dils],
        out_shape=[jax.ShapeDtypeStruct((bsz, dil, seq // dil, 3 * nh), BF) for dil in dils],
        scratch_shapes=[pltpu.VMEM((3 * nh // LANES, tm, LANES), F32)],
        compiler_params=_cparams("parallel"),
        name="proj_d",
    )(x2, w)
    os_, ls_ = [], []
    for qkv_g, (window, dil) in zip(qkvs, D_GROUPS):
        o, lse = _attn_d_group(qkv_g, window, dil)
        os_.append(o)
        ls_.append(lse)
    row = lambda i: (i, 0)
    fixed = lambda i: (0, 0)
    return pl.pallas_call(
        functools.partial(_post_d_body, alpha=alpha, dils=dils),
        grid=(t // tm,),
        in_specs=[deint_spec(dil, nh) for dil in dils] * 2 + [
            pl.BlockSpec((tm, d), row), pl.BlockSpec((nh, d), fixed),
            pl.BlockSpec((1, d), fixed), pl.BlockSpec((1, d), fixed)],
        out_specs=pl.BlockSpec((tm, d), row),
        out_shape=jax.ShapeDtypeStruct((t, d), F32),
        scratch_shapes=[pltpu.VMEM((ng, nh // LANES, tm, LANES), F32), pltpu.VMEM((ng, nh // LANES, tm, LANES), F32)],
        compiler_params=_cparams("parallel"),
        name="post_d",
    )(*os_, *ls_, x2, w_out.astype(BF), g.reshape(1, d), b.reshape(1, d))


def _router_gates(x, wh_ref, wl_ref, rb_ref):
    tm = x.shape[0]
    xh = x.astype(BF)
    xl = (x - xh.astype(F32)).astype(BF)
    wh = wh_ref[...]
    logits = (lax.dot_general(wh, xh, NT_DIMS, preferred_element_type=F32)
              + lax.dot_general(wh, xl, NT_DIMS, preferred_element_type=F32)
              + lax.dot_general(wl_ref[...], xh, NT_DIMS, preferred_element_type=F32))
    scores = jax.nn.sigmoid(logits)
    biased = scores + rb_ref[...]
    epg = N_EXPERTS // N_EXPERT_GROUPS
    sc = [scores[e:e + 1, :] for e in range(N_EXPERTS)]
    bi = [biased[e:e + 1, :] for e in range(N_EXPERTS)]
    gs = []
    for g in range(N_EXPERT_GROUPS):
        v = bi[g * epg:(g + 1) * epg]
        best = None
        for a in range(epg):
            for c in range(a + 1, epg):
                pair = v[a] + v[c]
                best = pair if best is None else jnp.maximum(best, pair)
        gs.append(best)
    gmax = functools.reduce(jnp.maximum, gs)
    taken = jnp.zeros((1, tm), jnp.bool_)
    cand = []
    for g in range(N_EXPERT_GROUPS):
        sel = (gs[g] == gmax) & jnp.logical_not(taken)
        taken = taken | sel
        for a in range(epg):
            cand.append(jnp.where(sel, bi[g * epg + a], -jnp.inf))
    m1 = functools.reduce(jnp.maximum, cand)
    taken = jnp.zeros((1, tm), jnp.bool_)
    is1 = []
    for e in range(N_EXPERTS):
        hit = (cand[e] == m1) & jnp.logical_not(taken)
        taken = taken | hit
        is1.append(hit)
    cand2 = [jnp.where(is1[e], -jnp.inf, cand[e]) for e in range(N_EXPERTS)]
    m2 = functools.reduce(jnp.maximum, cand2)
    taken = jnp.zeros((1, tm), jnp.bool_)
    is2 = []
    for e in range(N_EXPERTS):
        hit = (cand2[e] == m2) & jnp.logical_not(taken)
        taken = taken | hit
        is2.append(hit)
    zero = jnp.zeros((1, tm), F32)
    w1 = functools.reduce(jnp.add, [jnp.where(is1[e], sc[e], zero) for e in range(N_EXPERTS)])
    w2 = functools.reduce(jnp.add, [jnp.where(is2[e], sc[e], zero) for e in range(N_EXPERTS)])
    den = w1 + w2
    rows = [jnp.where(is1[e], w1 / den, zero) + jnp.where(is2[e], w2 / den, zero) for e in range(N_EXPERTS)]
    rows.append(jnp.ones((1, tm), F32))
    rows.append(jnp.zeros((LANES - N_EXPERTS - 1, tm), F32))
    return jnp.concatenate(rows, axis=0).T


MOE_EXPERTS_PER_STEP = 2


def _moe_body(x_ref, p_ref, wh_ref, wl_ref, rb_ref, wgu_ref, wd_ref, wgus_ref, wds_ref, g_ref, b_ref,
              wpi_ref, wpg_ref, y_ref, gates, xb, acc, *, alpha, n_routed_steps):
    e = pl.program_id(1)
    ff = EXPERT_FF

    def hidden(gu, j):
        return jax.nn.silu(gu[:, 2 * j * ff:(2 * j + 1) * ff]) * gu[:, (2 * j + 1) * ff:(2 * j + 2) * ff]

    @pl.when(e == 0)
    def _():
        x = x_ref[...]
        gates[...] = _router_gates(x, wh_ref, wl_ref, rb_ref)
        xb[...] = x.astype(BF)
        acc[...] = jnp.zeros_like(acc)

    @pl.when(e < n_routed_steps)
    def _():
        gu = jnp.dot(xb[...], wgu_ref[0], preferred_element_type=F32)
        lane = lax.broadcasted_iota(jnp.int32, gates.shape, 1)
        gt = gates[...]
        hs = []
        for j in range(MOE_EXPERTS_PER_STEP):
            gcol = jnp.sum(jnp.where(lane == e * MOE_EXPERTS_PER_STEP + j, gt, 0.0), axis=1, keepdims=True)
            hs.append((hidden(gu, j) * gcol).astype(BF))
        acc[...] += jnp.dot(jnp.concatenate(hs, axis=1), wd_ref[0], preferred_element_type=F32)

    @pl.when(e == n_routed_steps)
    def _():
        gu = jnp.dot(xb[...], wgus_ref[...], preferred_element_type=F32)
        y = acc[...] + jnp.dot(hidden(gu, 0).astype(BF), wds_ref[...], preferred_element_type=F32)
        x2 = _layer_norm_rows(alpha * x_ref[...] + y, g_ref[...], b_ref[...])
        emb = jnp.dot(p_ref[...].astype(BF), wpi_ref[...], preferred_element_type=F32)
        gate = jax.nn.sigmoid(jnp.dot(x2.astype(BF), wpg_ref[...], preferred_element_type=F32))
        y_ref[...] = x2 + gate * emb


def _moe_layer(x2, p2, router, lw, g, b, alpha, tm=1024):
    t, d = x2.shape
    wgu, wd, wgus, wds, wpi, wpg = lw["wgu"], lw["wd"], lw["wgus"], lw["wds"], lw["wpi"], lw["wpg"]
    n_routed_steps = wgu.shape[0]
    wh_t, wl_t, rb = router
    tok = lambda i, e: (i, 0)
    fixed = lambda i, e: (0, 0)
    step = lambda i, e: (jnp.minimum(e, n_routed_steps - 1), 0, 0)
    return pl.pallas_call(
        functools.partial(_moe_body, alpha=alpha, n_routed_steps=n_routed_steps),
        grid=(t // tm, n_routed_steps + 1),
        in_specs=[
            pl.BlockSpec((tm, d), tok), pl.BlockSpec((tm, p2.shape[1]), tok),
            pl.BlockSpec((N_EXPERTS, d), fixed), pl.BlockSpec((N_EXPERTS, d), fixed),
            pl.BlockSpec((N_EXPERTS, 1), fixed),
            pl.BlockSpec((1,) + wgu.shape[1:], step), pl.BlockSpec((1,) + wd.shape[1:], step),
            pl.BlockSpec(wgus.shape, fixed), pl.BlockSpec(wds.shape, fixed),
            pl.BlockSpec((1, d), fixed), pl.BlockSpec((1, d), fixed),
            pl.BlockSpec(wpi.shape, fixed), pl.BlockSpec(wpg.shape, fixed),
        ],
        out_specs=pl.BlockSpec((tm, d), tok),
        out_shape=jax.ShapeDtypeStruct((t, d), F32),
        scratch_shapes=[pltpu.VMEM((tm, LANES), F32), pltpu.VMEM((tm, d), BF), pltpu.VMEM((tm, d), F32)],
        compiler_params=_cparams("parallel", "arbitrary"),
        name="moe",
    )(x2, p2, wh_t, wl_t, rb, wgu, wd, wgus, wds, g.reshape(1, d), b.reshape(1, d), wpi, wpg)


def _prep_shared(prm):
    depth = prm["ln1_g"].shape[0]
    rw = prm["router_w"].astype(F32)
    wh = rw.astype(BF)
    wl = (rw - wh.astype(F32)).astype(BF)
    router = (wh.T, wl.T, prm["router_b"].astype(F32).reshape(N_EXPERTS, 1))
    layers = []
    for i in range(depth):
        eps = MOE_EXPERTS_PER_STEP
        d = prm["moe_w_gate"].shape[2]
        wgu = jnp.concatenate([prm["moe_w_gate"][i], prm["moe_w_up"][i]], axis=-1)
        wgu = wgu.reshape(N_EXPERTS // eps, eps, d, 2 * EXPERT_FF).transpose(0, 2, 1, 3)
        wgu = wgu.reshape(N_EXPERTS // eps, d, eps * 2 * EXPERT_FF).astype(BF)
        wd = prm["moe_w_down"][i].reshape(N_EXPERTS // eps, eps * EXPERT_FF, d).astype(BF)
        wgus = jnp.concatenate([prm["moe_ws_gate"][i], prm["moe_ws_up"][i]], axis=-1).astype(BF)
        layers.append(dict(wgu=wgu, wd=wd, wgus=wgus, wds=prm["moe_ws_down"][i].astype(BF),
                           wpi=prm["ple_w_in"][i].astype(BF), wpg=prm["ple_w_gate"][i].astype(BF)))
    return router, layers


def _trunk(x, p, prm, router, layers):
    depth = prm["ln1_g"].shape[0]
    alpha = (2.0 * depth) ** 0.25
    bsz, seq, d = x.shape
    t = bsz * seq
    x2 = x.reshape(t, d)
    for i in range(depth):
        mixer, j = i % 4, i // 4
        xb = x2.reshape(bsz, seq, d)
        if mixer == 3:
            x2 = _mixer_d_and_post(x2, bsz, seq, prm["d_w_qkv"][j], prm["d_w_out"][j],
                                   prm["ln1_g"][i], prm["ln1_b"][i], alpha)
        else:
            if mixer == 0:
                o = _mixer_a(xb, prm["a_w_in"][j], prm["a_lambda"][j], prm["a_subln"][j], i)
                w_out = prm["a_w_out"][j]
            elif mixer == 1:
                o = _mixer_b(xb, prm["b_w_in"][j], prm["b_q_norm"][j], prm["b_kv_norm"][j],
                             prm["b_w_uq"][j], prm["b_w_ukv"][j])
                w_out = prm["b_w_out"][j]
            else:
                o = _mixer_c(xb, prm["c_w_qkv"][j], prm["c_rpb"][j])
                w_out = prm["c_w_out"][j]
            x2 = _post_attn(o, x2, w_out.astype(BF), prm["ln1_g"][i], prm["ln1_b"][i], alpha)
        x2 = _moe_layer(x2, p[i].reshape(t, -1), router, layers[i], prm["ln2_g"][i], prm["ln2_b"][i], alpha)
    return x2.reshape(bsz, seq, d)


def kernel(x_prompt, x_sample, p_prompt, p_sample, a_w_in, a_lambda, a_subln, a_w_out, b_w_in, b_q_norm, b_kv_norm, b_w_uq, b_w_ukv, b_w_out, c_w_qkv, c_rpb, c_w_out, d_w_qkv, d_w_out, router_w, router_b, moe_w_gate, moe_w_up, moe_w_down, moe_ws_gate, moe_ws_up, moe_ws_down, ln1_g, ln1_b, ln2_g, ln2_b, ple_w_in, ple_w_gate):
    prm = dict(a_w_in=a_w_in, a_lambda=a_lambda, a_subln=a_subln, a_w_out=a_w_out,
               b_w_in=b_w_in, b_q_norm=b_q_norm, b_kv_norm=b_kv_norm, b_w_uq=b_w_uq,
               b_w_ukv=b_w_ukv, b_w_out=b_w_out,
               c_w_qkv=c_w_qkv, c_rpb=c_rpb, c_w_out=c_w_out,
               d_w_qkv=d_w_qkv, d_w_out=d_w_out,
               router_w=router_w, router_b=router_b, moe_w_gate=moe_w_gate, moe_w_up=moe_w_up,
               moe_w_down=moe_w_down, moe_ws_gate=moe_ws_gate, moe_ws_up=moe_ws_up,
               moe_ws_down=moe_ws_down,
               ln1_g=ln1_g, ln1_b=ln1_b, ln2_g=ln2_g, ln2_b=ln2_b,
               ple_w_in=ple_w_in, ple_w_gate=ple_w_gate)
    router, layers = _prep_shared(prm)
    y_prompt = _trunk(x_prompt, p_prompt, prm, router, layers)
    y_sample = _trunk(x_sample, p_sample, prm, router, layers)
    return (y_prompt, y_sample)
```

```python
import functools
import math

import numpy as np
import jax
import jax.numpy as jnp
from jax import lax
from jax.experimental import pallas as pl
from jax.experimental.pallas import tpu as pltpu

BF = jnp.bfloat16
F32 = jnp.float32

VMEM_LIMIT_BYTES = 56 * 1024 * 1024
LANES = 128

GRID_W = 64
LN_EPS = 1e-5
RMS_EPS = 1e-6
ROPE_THETA = 10000.0
NEG = -1e30

A_HEADS, A_HEAD_DIM = 8, 64
B_HEADS, B_NOPE, B_ROPE, B_VDIM, B_Q_LORA, B_KV_LORA = 16, 64, 32, 64, 384, 256
C_HEADS, C_HEAD_DIM, C_WIN_ROWS, C_WIN_COLS = 16, 64, 8, 16
D_HEADS, D_HEAD_DIM = 8, 64
D_GROUPS = ((128, 1), (512, 4), (2048, 16))
N_EXPERTS, N_EXPERT_GROUPS, EXPERT_FF = 16, 4, 256

NT_DIMS = (((1,), (1,)), ((), ()))


def _cparams(*sem):
    return pltpu.CompilerParams(dimension_semantics=sem, vmem_limit_bytes=VMEM_LIMIT_BYTES)


def _layer_norm_rows(z, g, b):
    mu = jnp.mean(z, axis=-1, keepdims=True)
    zc = z - mu
    var = jnp.mean(zc * zc, axis=-1, keepdims=True)
    return zc * lax.rsqrt(var + LN_EPS) * g + b


def _rms_rows(z, g):
    return z * lax.rsqrt(jnp.mean(z * z, axis=-1, keepdims=True) + RMS_EPS) * g


def _linear_body(x_ref, w_ref, o_ref):
    o_ref[...] = jnp.dot(x_ref[...].astype(BF), w_ref[...], preferred_element_type=F32).astype(o_ref.dtype)


def _linear(x, w, out_dtype, tm=512, tn=None):
    m, k = x.shape
    n = w.shape[1]
    tn = n if tn is None else tn
    return pl.pallas_call(
        _linear_body,
        grid=(n // tn, m // tm),
        in_specs=[pl.BlockSpec((tm, k), lambda j, i: (i, 0)), pl.BlockSpec((k, tn), lambda j, i: (0, j))],
        out_specs=pl.BlockSpec((tm, tn), lambda j, i: (i, j)),
        out_shape=jax.ShapeDtypeStruct((m, n), out_dtype),
        compiler_params=_cparams("parallel", "parallel"),
        name="linear",
    )(x, w)


def _post_body(o_ref, x_ref, w_ref, g_ref, b_ref, y_ref, *, alpha):
    h = jnp.dot(o_ref[...], w_ref[...], preferred_element_type=F32)
    y_ref[...] = _layer_norm_rows(alpha * x_ref[...] + h, g_ref[...], b_ref[...])


def _post_attn(o, x, w_out, g, b, alpha, tm=512):
    t, ko = o.shape
    d = x.shape[1]
    return pl.pallas_call(
        functools.partial(_post_body, alpha=alpha),
        grid=(t // tm,),
        in_specs=[
            pl.BlockSpec((tm, ko), lambda i: (i, 0)),
            pl.BlockSpec((tm, d), lambda i: (i, 0)),
            pl.BlockSpec((ko, d), lambda i: (0, 0)),
            pl.BlockSpec((1, d), lambda i: (0, 0)),
            pl.BlockSpec((1, d), lambda i: (0, 0)),
        ],
        out_specs=pl.BlockSpec((tm, d), lambda i: (i, 0)),
        out_shape=jax.ShapeDtypeStruct((t, d), F32),
        compiler_params=_cparams("parallel"),
        name="post_attn",
    )(o, x, w_out, g.reshape(1, d), b.reshape(1, d))


LOG2E = 1.4426950408889634
SUM_ROWS = 16


def _softmax_chunk(s, m, shift, mx=None):
    if mx is None:
        mx = jnp.max(s, axis=0, keepdims=True)
    if shift is not None:
        mx = mx - shift
    m_new = jnp.maximum(m, mx)
    ref = m_new if shift is None else m_new + shift
    return m_new, jnp.exp2(s - ref).astype(BF)


def _pipelined_flash(n, nh, tq, qk_fn, sm_fn, pv_fn, s_buf, p_buf, acc_ref):
    assert n >= 4 and n % 2 == 0
    acc_ref[...] = jnp.zeros_like(acc_ref)

    def scores(j, slot):
        mxs = []
        for h in range(nh):
            s = qk_fn(j, h)
            s_buf[slot, h] = s
            mxs.append(jnp.max(s, axis=0, keepdims=True))
        return tuple(mxs)

    def stage(j, slot, carry, do_pv=True, do_qk=True, first=False):
        ms, alphas, mxs = carry
        if do_pv:
            for h in range(nh):
                acc_ref[h] = alphas[h] * acc_ref[h] + pv_fn(j - 1, h, p_buf[1 - slot, h])
        mxs_next = scores(j + 1, 1 - slot) if do_qk else mxs
        new_ms, new_alphas = [], []
        for h in range(nh):
            m_new, p = sm_fn(j, h, s_buf[slot, h], ms[h], mxs[h], first)
            p_buf[slot, h] = p
            new_alphas.append(jnp.exp2(ms[h] - m_new))
            new_ms.append(m_new)
        return tuple(new_ms), tuple(new_alphas), mxs_next

    mxs = scores(0, 0)
    ms = tuple(jnp.full((1, tq), NEG, F32) for _ in range(nh))
    alphas = tuple(jnp.zeros((1, tq), F32) for _ in range(nh))
    carry = stage(0, 0, (ms, alphas, mxs), do_pv=False, first=True)

    def body(t, carry):
        return stage(2 * t + 2, 0, stage(2 * t + 1, 1, carry))

    carry = lax.fori_loop(0, (n - 2) // 2, body, carry)
    _, alphas, _ = stage(n - 1, 1, carry, do_qk=False)
    for h in range(nh):
        acc_ref[h] = alphas[h] * acc_ref[h] + pv_fn(n - 1, h, p_buf[1, h])


def _chunk_off(j, tk):
    return j * tk if isinstance(j, int) else pl.multiple_of(j * tk, tk)


def _attn_a_body(cs_ref, ctab_ref, lam_ref, g_ref, q_ref, k_ref, kpos_ref, vt_ref, o_ref,
                 s_buf, p_buf, acc_ref, *, seq, tq, tk, hps, lam_init):
    hg = pl.program_id(1)
    qi = pl.program_id(2)
    n = seq // tk
    cd = (qi * tq) // tk
    lane = lax.broadcasted_iota(jnp.int32, (tq, LANES), 1)
    qpos = (qi * tq + lax.broadcasted_iota(jnp.int32, (1, tq), 1)).astype(F32)
    dv = vt_ref.shape[2] - SUM_ROWS
    c_slope, f0, q_left, q_right = [], [], [], []
    for hh in range(hps):
        head = hg * hps + hh
        c_slope.append(cs_ref[head])
        f0.append(c_slope[hh] * qpos)
        q = q_ref[0, :, hh * LANES:(hh + 1) * LANES]
        qaug = jnp.broadcast_to(ctab_ref[pl.ds(head, 1), :], (tq, LANES)).astype(BF)
        for half in range(2):
            q_m = jnp.where((lane >= 64 * half) & (lane < 64 * (half + 1)), q, jnp.zeros_like(q))
            q_left.append(jnp.concatenate([q_m, qaug], axis=1))
            q_right.append(jnp.concatenate([q_m, -qaug], axis=1))

    def chunk_of(j):
        jm = j - 1
        c = jnp.where(j == 0, cd, jm + (jm >= cd).astype(jnp.int32))
        return c, c <= cd

    def qk_fn(j, mm):
        hh = mm // 2
        c, left = chunk_of(j)
        off = pl.multiple_of(c * tk, tk)
        k_c = jnp.concatenate([k_ref[0, pl.ds(off, tk), hh * LANES:(hh + 1) * LANES],
                               kpos_ref[pl.ds(off, tk), :]], axis=1)
        q_full = jnp.where(left, q_left[mm], q_right[mm])
        return lax.dot_general(k_c, q_full, NT_DIMS, preferred_element_type=F32)

    def sm_fn(j, mm, s, m, mx, first):
        hh = mm // 2
        if first:
            d = (lax.broadcasted_iota(jnp.int32, (tk, tq), 0) - lax.broadcasted_iota(jnp.int32, (tk, tq), 1)
                 + (cd * tk - qi * tq)).astype(F32)
            return _softmax_chunk(s - (2.0 * c_slope[hh]) * jnp.maximum(d, 0.0), m, f0[hh])
        _, left = chunk_of(j)
        return _softmax_chunk(s, m, jnp.where(left, f0[hh], -f0[hh]), mx)

    def pv_fn(j, mm, p):
        c, _ = chunk_of(j)
        off = pl.multiple_of(c * tk, tk)
        return jnp.dot(vt_ref[0, mm // 2, :, pl.ds(off, tk)], p, preferred_element_type=F32)

    _pipelined_flash(n, 2 * hps, tq, qk_fn, sm_fn, pv_fn, s_buf, p_buf, acc_ref)
    lf = lam_ref[...]
    lam_full = (jnp.exp(jnp.sum(lf[0:1] * lf[1:2], axis=-1, keepdims=True))
                - jnp.exp(jnp.sum(lf[2:3] * lf[3:4], axis=-1, keepdims=True)) + lam_init)
    outs = []
    for hh in range(hps):
        o0, o1 = [acc_ref[2 * hh + half, :dv, :] / acc_ref[2 * hh + half, dv:dv + 1, :] for half in range(2)]
        o = o0 - lam_full * o1
        ms = jnp.mean(o * o, axis=0, keepdims=True)
        outs.append(o * lax.rsqrt(ms + RMS_EPS) * g_ref[...] * (1.0 - lam_init))
    o_ref[0] = jnp.concatenate(outs, axis=0).T.astype(o_ref.dtype)


def _bf16_pieces(c, n=3):
    pieces, rest = [], c.astype(F32)
    for _ in range(n):
        p = rest.astype(BF)
        pieces.append(p.astype(F32))
        rest = rest - p.astype(F32)
    return pieces


def _store_vt_heads(vt_ref, vt, dv, heads_per_slot):
    tm = vt.shape[1]
    tail = jnp.where(lax.broadcasted_iota(jnp.int32, (SUM_ROWS, tm), 0) == 0, 1.0, 0.0).astype(vt_ref.dtype)
    rows = dv + SUM_ROWS
    for h in range(vt.shape[0] // dv):
        slot, r0 = h // heads_per_slot, (h % heads_per_slot) * rows
        vt_ref[0, slot, r0:r0 + dv, :] = vt[h * dv:(h + 1) * dv, :].astype(vt_ref.dtype)
        vt_ref[0, slot, r0 + dv:r0 + rows, :] = tail


def _proj_a_body(x_ref, wqk_ref, wvt_ref, qk_ref, vt_ref):
    xb = x_ref[...].astype(BF)
    qk_ref[...] = jnp.dot(xb, wqk_ref[...], preferred_element_type=F32).astype(BF)
    vt = lax.dot_general(wvt_ref[...], xb, NT_DIMS, preferred_element_type=F32)
    _store_vt_heads(vt_ref, vt, 2 * A_HEAD_DIM, 1)


def _flash_chunk_len(seq, tk_max):
    for min_chunks in (8, 4):
        tk = tk_max
        while tk >= 2 * LANES and (seq % tk or (seq // tk) < min_chunks or (seq // tk) % 2):
            tk //= 2
        if tk >= 2 * LANES:
            return tk
    raise ValueError(f"sequence length {seq} too short for the pipelined attention kernel")


def _flash_scratch(nh, rows, tq, tk):
    return [pltpu.VMEM((2, nh, tk, tq), F32), pltpu.VMEM((2, nh, tk, tq), BF), pltpu.VMEM((nh, rows, tq), F32)]


def _attn_a(qkv, vt, lam, subln_g, lam_init, tq=256, tk=1024):
    bsz, seq, _ = qkv.shape
    tq, tk = min(tq, seq), _flash_chunk_len(seq, tk)
    c_slope = jnp.asarray(2.0 ** (-8.0 * np.arange(1, A_HEADS + 1) / A_HEADS) * LOG2E, dtype=F32)
    c1, c2, c3 = _bf16_pieces(c_slope)
    ctab = jnp.stack([c1, c1, c2, c2, c3, c3], axis=-1)
    ctab = jnp.concatenate([ctab, jnp.zeros((A_HEADS, LANES - 6), F32)], axis=-1)
    pos = np.arange(seq)
    kpos = np.zeros((seq, LANES), np.float32)
    for j in range(3):
        kpos[:, 2 * j] = (pos // LANES) * LANES
        kpos[:, 2 * j + 1] = pos % LANES
    kpos = jnp.asarray(kpos, BF)
    vrows = vt.shape[2]
    hps = 2 if seq <= 4096 else 1
    ngrp = A_HEADS // hps
    body = functools.partial(_attn_a_body, seq=seq, tq=tq, tk=tk, hps=hps, lam_init=lam_init)
    return pl.pallas_call(
        body,
        grid=(bsz, ngrp, seq // tq),
        in_specs=[
            pl.BlockSpec(memory_space=pltpu.SMEM),
            pl.BlockSpec((A_HEADS, LANES), lambda b, h, i: (0, 0)),
            pl.BlockSpec((4, A_HEAD_DIM), lambda b, h, i: (0, 0)),
            pl.BlockSpec((2 * A_HEAD_DIM, 1), lambda b, h, i: (0, 0)),
            pl.BlockSpec((1, tq, hps * LANES), lambda b, h, i: (b, i, h)),
            pl.BlockSpec((1, seq, hps * LANES), lambda b, h, i: (b, 0, ngrp + h)),
            pl.BlockSpec((seq, LANES), lambda b, h, i: (0, 0)),
            pl.BlockSpec((1, hps, vrows, seq), lambda b, h, i: (b, h, 0, 0)),
        ],
        out_specs=pl.BlockSpec((1, tq, hps * LANES), lambda b, h, i: (b, i, h)),
        out_shape=jax.ShapeDtypeStruct((bsz, seq, A_HEADS * LANES), BF),
        scratch_shapes=_flash_scratch(2 * hps, vrows, tq, tk),
        compiler_params=_cparams("parallel", "parallel", "arbitrary"),
        name="attn_a",
    )(c_slope, ctab, lam.astype(F32), subln_g.astype(F32).reshape(2 * A_HEAD_DIM, 1), qkv, qkv, kpos, vt)


def _attn_b_body(q_ref, k_ref, vt_ref, o_ref, s_buf, p_buf, acc_ref, *, seq, tk, nh):
    tq = q_ref.shape[1]
    rows = B_VDIM + SUM_ROWS
    qs = [q_ref[0, :, h * LANES:(h + 1) * LANES] for h in range(nh)]

    def qk_fn(j, h):
        k_c = k_ref[0, pl.ds(_chunk_off(j, tk), tk), h * LANES:(h + 1) * LANES]
        return lax.dot_general(k_c, qs[h], NT_DIMS, preferred_element_type=F32)

    def sm_fn(j, h, s, m, mx, first):
        return _softmax_chunk(s, m, None, mx)

    def pv_fn(j, h, p):
        vt_c = vt_ref[0, h // 2, (h % 2) * rows:(h % 2 + 1) * rows, pl.ds(_chunk_off(j, tk), tk)]
        return jnp.dot(vt_c, p, preferred_element_type=F32)

    _pipelined_flash(seq // tk, nh, tq, qk_fn, sm_fn, pv_fn, s_buf, p_buf, acc_ref)
    outs = [acc_ref[h, :B_VDIM, :] / acc_ref[h, B_VDIM:B_VDIM + 1, :] for h in range(nh)]
    o = jnp.concatenate(outs, axis=0)
    o_ref[0] = o.T.astype(o_ref.dtype)


def _attn_b(q, k, vt, tq=256, tk=1024):
    bsz, seq, _ = q.shape
    tq, tk = min(tq, seq), _flash_chunk_len(seq, tk)
    nh = 4 if seq <= 4096 else 2
    return pl.pallas_call(
        functools.partial(_attn_b_body, seq=seq, tk=tk, nh=nh),
        grid=(bsz, B_HEADS // nh, seq // tq),
        in_specs=[
            pl.BlockSpec((1, tq, nh * LANES), lambda b, h, i: (b, i, h)),
            pl.BlockSpec((1, seq, nh * LANES), lambda b, h, i: (b, 0, h)),
            pl.BlockSpec((1, nh // 2, 2 * (B_VDIM + SUM_ROWS), seq), lambda b, h, i: (b, h, 0, 0)),
        ],
        out_specs=pl.BlockSpec((1, tq, nh * B_VDIM), lambda b, h, i: (b, i, h)),
        out_shape=jax.ShapeDtypeStruct((bsz, seq, B_HEADS * B_VDIM), BF),
        scratch_shapes=_flash_scratch(nh, B_VDIM + SUM_ROWS, tq, tk),
        compiler_params=_cparams("parallel", "parallel", "arbitrary"),
        name="attn_b",
    )(q, k, vt)


def _mla_in_body(x_ref, w_ref, gq_ref, gkv_ref, rc_ref, rs_ref, cq_ref, ckv_ref, kr_ref):
    h = jnp.dot(x_ref[...].astype(BF), w_ref[...], preferred_element_type=F32)
    cq_ref[...] = _rms_rows(h[:, :B_Q_LORA], gq_ref[...]).astype(BF)
    ckv_ref[...] = _rms_rows(h[:, B_Q_LORA:B_Q_LORA + B_KV_LORA], gkv_ref[...]).astype(BF)
    t = h[:, B_Q_LORA + B_KV_LORA:]
    kr = t * rc_ref[...] + pltpu.roll(t, LANES - B_ROPE, 1) * rs_ref[...]
    kr_ref[...] = kr.astype(BF)


def _mla_q_body(cq_ref, w_ref, ta_ref, tb_ref, q_ref):
    t = jnp.dot(cq_ref[...], w_ref[...], preferred_element_type=F32)
    ta = ta_ref[...]
    tb = tb_ref[...]
    for h in range(B_HEADS):
        th = t[:, h * LANES:(h + 1) * LANES]
        q_ref[:, h * LANES:(h + 1) * LANES] = (th * ta + pltpu.roll(th, LANES - B_ROPE, 1) * tb).astype(BF)


def _mla_kv_body(ckv_ref, kr_ref, wk_ref, e_ref, wvt_ref, k_ref, vt_ref):
    ckv = ckv_ref[...]
    k = jnp.dot(ckv, wk_ref[...], preferred_element_type=F32)
    k = k + jnp.dot(kr_ref[...], e_ref[...], preferred_element_type=F32)
    k_ref[...] = k.astype(BF)
    vt = lax.dot_general(wvt_ref[...], ckv, NT_DIMS, preferred_element_type=F32)
    _store_vt_heads(vt_ref, vt, B_VDIM, 2)


def _rope_partner(w):
    half = B_ROPE // 2
    return jnp.concatenate([-w[..., half:], w[..., :half]], axis=-1)


def _mla_tables(seq):
    inv = 1.0 / (ROPE_THETA ** (np.arange(0, B_ROPE, 2, dtype=np.float32) / B_ROPE))
    ang = jnp.arange(seq, dtype=F32)[:, None] * jnp.asarray(inv, F32)[None, :]
    cos = jnp.concatenate([jnp.cos(ang), jnp.cos(ang)], axis=-1)
    sin = jnp.concatenate([jnp.sin(ang), jnp.sin(ang)], axis=-1)
    z = lambda n: jnp.zeros((seq, n), F32)
    rc = jnp.concatenate([cos, z(LANES - B_ROPE)], axis=-1)
    rs = jnp.concatenate([sin, z(LANES - B_ROPE)], axis=-1)
    scale = (B_NOPE + B_ROPE) ** -0.5 * LOG2E
    ta =jnp.concatenate([jnp.full((seq, B_NOPE), scale, F32), cos * scale, z(B_ROPE)], axis=-1)
    tb = jnp.concatenate([z(B_NOPE), sin * scale, z(B_ROPE)], axis=-1)
    return rc, rs, ta, tb


def _mla_weights(w_in, w_uq, w_ukv):
    d = w_in.shape[0]
    kr0 = B_Q_LORA + B_KV_LORA
    w_in_ext = jnp.concatenate(
        [w_in, _rope_partner(w_in[:, kr0:kr0 + B_ROPE]), jnp.zeros((d, LANES - 2 * B_ROPE), w_in.dtype)], axis=-1)
    wq = w_uq.reshape(B_Q_LORA, B_HEADS, B_NOPE + B_ROPE)
    wq_ext = jnp.concatenate([wq, _rope_partner(wq[..., B_NOPE:])], axis=-1).reshape(B_Q_LORA, B_HEADS * LANES)
    wkv = w_ukv.reshape(B_KV_LORA, B_HEADS, B_NOPE + B_VDIM)
    wk = jnp.concatenate([wkv[..., :B_NOPE], jnp.zeros((B_KV_LORA, B_HEADS, LANES - B_NOPE), w_ukv.dtype)], axis=-1)
    wk = wk.reshape(B_KV_LORA, B_HEADS * LANES)
    wv = wkv[..., B_NOPE:].reshape(B_KV_LORA, B_HEADS * B_VDIM)
    place = np.zeros((LANES, B_HEADS, LANES), np.float32)
    for j in range(B_ROPE):
        place[j, :, B_NOPE + j] = 1.0
    place = jnp.asarray(place.reshape(LANES, B_HEADS * LANES), BF)
    return w_in_ext.astype(BF), wq_ext.astype(BF), wk.astype(BF), place, wv.astype(BF)


def _mixer_b(x, w_in, q_norm_g, kv_norm_g, w_uq, w_ukv, tm=512):
    bsz, seq, d = x.shape
    t = bsz * seq
    tm = min(tm, seq)
    nblk = seq // tm
    w_in_ext, wq_ext, wk, place, wv = _mla_weights(w_in, w_uq, w_ukv)
    rc, rs, ta, tb = _mla_tables(seq)
    x2 = x.reshape(t, d)
    n_in = w_in_ext.shape[1]
    row = lambda i: (i, 0)
    fixed = lambda i: (0, 0)
    pos = lambda i: (i % nblk, 0)
    cq, ckv, kr = pl.pallas_call(
        _mla_in_body,
        grid=(t // tm,),
        in_specs=[
            pl.BlockSpec((tm, d), row), pl.BlockSpec((d, n_in), fixed),
            pl.BlockSpec((1, B_Q_LORA), fixed), pl.BlockSpec((1, B_KV_LORA), fixed),
            pl.BlockSpec((tm, LANES), pos), pl.BlockSpec((tm, LANES), pos),
        ],
        out_specs=[pl.BlockSpec((tm, B_Q_LORA), row), pl.BlockSpec((tm, B_KV_LORA), row),
                   pl.BlockSpec((tm, LANES), row)],
        out_shape=[jax.ShapeDtypeStruct((t, B_Q_LORA), BF), jax.ShapeDtypeStruct((t, B_KV_LORA), BF),
                   jax.ShapeDtypeStruct((t, LANES), BF)],
        compiler_params=_cparams("parallel"),
        name="mla_in",
    )(x2, w_in_ext, q_norm_g.astype(F32).reshape(1, -1), kv_norm_g.astype(F32).reshape(1, -1), rc, rs)
    nq = B_HEADS * LANES
    q = pl.pallas_call(
        _mla_q_body,
        grid=(t // tm,),
        in_specs=[pl.BlockSpec((tm, B_Q_LORA), row), pl.BlockSpec((B_Q_LORA, nq), fixed),
                  pl.BlockSpec((tm, LANES), pos), pl.BlockSpec((tm, LANES), pos)],
        out_specs=pl.BlockSpec((tm, nq), row),
        out_shape=jax.ShapeDtypeStruct((t, nq), BF),
        compiler_params=_cparams("parallel"),
        name="mla_q",
    )(cq, wq_ext, ta, tb)
    nv = B_HEADS * B_VDIM
    npair = B_HEADS // 2
    vrows = 2 * (B_VDIM + SUM_ROWS)
    k, vt = pl.pallas_call(
        _mla_kv_body,
        grid=(t // tm,),
        in_specs=[pl.BlockSpec((tm, B_KV_LORA), row), pl.BlockSpec((tm, LANES), row),
                  pl.BlockSpec((B_KV_LORA, nq), fixed), pl.BlockSpec((LANES, nq), fixed),
                  pl.BlockSpec((nv, B_KV_LORA), fixed)],
        out_specs=[pl.BlockSpec((tm, nq), row),
                   pl.BlockSpec((1, npair, vrows, tm), lambda i: (i // nblk, 0, 0, i % nblk))],
        out_shape=[jax.ShapeDtypeStruct((t, nq), BF), jax.ShapeDtypeStruct((bsz, npair, vrows, seq), BF)],
        compiler_params=_cparams("parallel"),
        name="mla_kv",
    )(ckv, kr, wk, place, wv.T)
    o = _attn_b(q.reshape(bsz, seq, nq), k.reshape(bsz, seq, nq), vt)
    return o.reshape(t, nv)


def _mixer_a(x, w_in, lam, subln_g, layer_idx):
    bsz, seq, d = x.shape
    t = bsz * seq
    hd2 = 2 * A_HEAD_DIM
    nq = A_HEADS * hd2
    scale = A_HEAD_DIM ** -0.5 * LOG2E
    wqk = jnp.concatenate([w_in[:, :nq] * scale, w_in[:, nq:2 * nq]], axis=-1).astype(BF)
    wvt = w_in[:, 2 * nq:].T.astype(BF)
    tm = min(512, seq)
    nblk = seq // tm
    vrows = hd2 + SUM_ROWS
    qk, vt = pl.pallas_call(
        _proj_a_body,
        grid=(t // tm,),
        in_specs=[pl.BlockSpec((tm, d), lambda i: (i, 0)), pl.BlockSpec((d, 2 * nq), lambda i: (0, 0)),
                  pl.BlockSpec((nq, d), lambda i: (0, 0))],
        out_specs=[pl.BlockSpec((tm, 2 * nq), lambda i: (i, 0)),
                   pl.BlockSpec((1, A_HEADS, vrows, tm), lambda i: (i // nblk, 0, 0, i % nblk))],
        out_shape=[jax.ShapeDtypeStruct((t, 2 * nq), BF), jax.ShapeDtypeStruct((bsz, A_HEADS, vrows, seq), BF)],
        compiler_params=_cparams("parallel"),
        name="proj_a",
    )(x.reshape(t, d), wqk, wvt)
    lam_init = 0.8 - 0.6 * math.exp(-0.3 * layer_idx)
    o = _attn_a(qk.reshape(bsz, seq, 2 * nq), vt, lam, subln_g, lam_init)
    return o.reshape(t, nq)


def _attn_c_body(q_ref, k0, k1, k2, k3, v0, v1, v2, v3, bt_ref, o_ref):
    q = q_ref[0]
    kw = jnp.concatenate([k0[0], k1[0], k2[0], k3[0]], axis=0)
    vw = jnp.concatenate([v0[0], v1[0], v2[0], v3[0]], axis=0)
    lane = lax.broadcasted_iota(jnp.int32, q.shape, 1)
    outs = []
    for hh in range(2):
        qm = jnp.where((lane >= 64 * hh) & (lane < 64 * (hh + 1)), q, jnp.zeros_like(q))
        s = lax.dot_general(qm, kw, NT_DIMS, preferred_element_type=F32) + bt_ref[hh, 0]
        m = jnp.max(s, axis=1, keepdims=True)
        p = jnp.exp2(s - m)
        l = jnp.sum(p, axis=1, keepdims=True)
        outs.append(jnp.dot(p.astype(BF), vw, preferred_element_type=F32) / l)
    o_ref[0] = jnp.where(lane < 64, outs[0], outs[1]).astype(o_ref.dtype)


def _nbr_bias_table(rpb):
    col = np.arange(GRID_W)
    col_start = np.clip(col - C_WIN_COLS // 2, 0, GRID_W - C_WIN_COLS)
    col_mask = (col[None, :] >= col_start[:, None]) & (col[None, :] < col_start[:, None] + C_WIN_COLS)
    pad = GRID_W - C_WIN_COLS
    ext = jnp.pad(rpb.astype(F32) * LOG2E, ((0, 0), (0, 0), (pad, pad)), mode="edge")
    toep = jnp.stack([ext[:, :, GRID_W - 1 - qc:2 * GRID_W - 1 - qc] for qc in range(GRID_W)], axis=2)
    toep = jnp.where(jnp.asarray(col_mask)[None, None], toep, NEG)
    neg = jnp.full((rpb.shape[0], GRID_W, GRID_W), NEG, F32)
    half = C_WIN_ROWS // 2
    kinds = []
    for kind in range(3):
        qrows = []
        for t in range(C_WIN_ROWS):
            u0 = (max(t, half), t, min(t, half))[kind]
            blocks = [toep[:, u - t + half - 1] if u0 <= u < u0 + C_WIN_ROWS else neg
                      for u in range(2 * C_WIN_ROWS)]
            qrows.append(jnp.concatenate(blocks, axis=-1))
        kinds.append(jnp.concatenate(qrows, axis=1))
    return jnp.stack(kinds, axis=1)


def _mixer_c(x, w_qkv, rpb):
    bsz, seq, d = x.shape
    t = bsz * seq
    nq = C_HEADS * C_HEAD_DIM
    scale = C_HEAD_DIM ** -0.5 * LOG2E
    w = jnp.concatenate([w_qkv[:, :nq] * scale, w_qkv[:, nq:]], axis=-1).astype(BF)
    qkv = _linear(x.reshape(t, d), w, BF, tn=nq).reshape(bsz, seq, 3 * nq)
    bt = _nbr_bias_table(rpb)
    npair = C_HEADS // 2
    qtok = C_WIN_ROWS * GRID_W
    ngrp = seq // qtok
    assert ngrp >= 2
    kb = qtok // 2
    nkb = seq // kb
    kspecs = []
    for off in (npair, 2 * npair):
        for j in range(4):
            kspecs.append(pl.BlockSpec(
                (1, kb, LANES),
                lambda h, g, b, j=j, off=off: (b, jnp.clip(2 * g - 1 + j, 0, nkb - 1), off + h)))
    kind = lambda g: jnp.where(g == 0, 0, jnp.where(g == ngrp - 1, 2, 1))
    o = pl.pallas_call(
        _attn_c_body,
        grid=(npair, ngrp, bsz),
        in_specs=[pl.BlockSpec((1, qtok, LANES), lambda h, g, b: (b, g, h))] + kspecs + [
            pl.BlockSpec((2, 1, qtok, 2 * qtok), lambda h, g, b: (h, kind(g), 0, 0))],
        out_specs=pl.BlockSpec((1, qtok, LANES), lambda h, g, b: (b, g, h)),
        out_shape=jax.ShapeDtypeStruct((bsz, seq, nq), BF),
        compiler_params=_cparams("parallel", "parallel", "arbitrary"),
        name="attn_c",
    )(qkv, *([qkv] * 8), bt)
    return o.reshape(t, nq)


def _attn_d_body(slopes_ref, q_ref, kp, kc, kn, vp, vc, vn, o_ref, lse_ref, *, tq, length, dil, rad):
    i = pl.program_id(2)
    nk = tq + 2 * rad
    qi = lax.broadcasted_iota(jnp.int32, (tq, nk), 0)
    kk = lax.broadcasted_iota(jnp.int32, (tq, nk), 1)
    dist = jnp.abs(kk - rad - qi)
    ki = i * tq - rad + kk
    valid = (dist <= rad) & (ki >= 0) & (ki < length)
    distf = dist.astype(F32) * float(dil)
    lane = lax.broadcasted_iota(jnp.int32, (tq, LANES), 1)
    for hp in range(D_HEADS // 2):
        sl = slice(hp * LANES, (hp + 1) * LANES)
        q = q_ref[0, 0, :, sl]
        kw = jnp.concatenate([kp[0, 0, tq - rad:tq, sl], kc[0, 0, :, sl], kn[0, 0, 0:rad, sl]], axis=0)
        vw = jnp.concatenate([vp[0, 0, tq - rad:tq, sl], vc[0, 0, :, sl], vn[0, 0, 0:rad, sl]], axis=0)
        outs, lses = [], []
        for hh in range(2):
            slope = slopes_ref[hp * 2 + hh]
            qm = jnp.where((lane >= 64 * hh) & (lane < 64 * (hh + 1)), q, jnp.zeros_like(q))
            s = lax.dot_general(qm, kw, NT_DIMS, preferred_element_type=F32)
            s = jnp.where(valid, s - slope * distf, NEG)
            m = jnp.max(s, axis=1, keepdims=True)
            p = jnp.exp2(s - m)
            l = jnp.sum(p, axis=1, keepdims=True)
            outs.append(jnp.dot(p.astype(BF), vw, preferred_element_type=F32) / l)
            lses.append(m + jnp.log(l) * LOG2E)
        o_ref[0, 0, :, sl] = jnp.where(lane < 64, outs[0], outs[1]).astype(o_ref.dtype)
        lse_ref[0, 0, :, sl] = jnp.where(lane < 64, lses[0], lses[1])


def _attn_d_group(qkv, window, dil, tq=256):
    bsz, _, length, ncol = qkv.shape
    rad = window // (2 * dil)
    tq = min(tq, length)
    nq = length // tq
    nh = ncol // 3
    slopes = jnp.asarray(2.0 ** (-8.0 * np.arange(1, D_HEADS + 1) / D_HEADS) * LOG2E, dtype=F32)

    def spec(which, shift):
        return pl.BlockSpec((1, 1, tq, nh), lambda b, c, i: (b, c, jnp.clip(i + shift, 0, nq - 1), which))

    ospec = pl.BlockSpec((1, 1, tq, nh), lambda b, c, i: (b, c, i, 0))
    return pl.pallas_call(
        functools.partial(_attn_d_body, tq=tq, length=length, dil=dil, rad=rad),
        grid=(bsz, dil, nq),
        in_specs=[pl.BlockSpec(memory_space=pltpu.SMEM), spec(0, 0),
                  spec(1, -1), spec(1, 0), spec(1, 1), spec(2, -1), spec(2, 0), spec(2, 1)],
        out_specs=[ospec, ospec],
        out_shape=[jax.ShapeDtypeStruct((bsz, dil, length, nh), BF),
                   jax.ShapeDtypeStruct((bsz, dil, length, nh), F32)],
        compiler_params=_cparams("parallel", "parallel", "arbitrary"),
        name=f"attn_d{dil}",
    )(slopes, qkv, qkv, qkv, qkv, qkv, qkv, qkv)


def _proj_d_body(x_ref, w_ref, o0, o1, o2, scr, *, dils):
    xb = x_ref[...].astype(BF)
    tm = xb.shape[0]
    nblk = scr.shape[0]
    ncol = nblk * LANES
    for gi, (o_ref, dil) in enumerate(zip((o0, o1, o2), dils)):
        res = jnp.dot(xb, w_ref[:, gi * ncol:(gi + 1) * ncol], preferred_element_type=F32)
        if dil == 1:
            o_ref[0, 0] = res.astype(BF)
            continue
        for j in range(nblk):
            scr[j] = res[:, j * LANES:(j + 1) * LANES]
        for c in range(dil):
            for j in range(nblk):
                o_ref[0, c, :, j * LANES:(j + 1) * LANES] = scr[j, pl.ds(c, tm // dil, stride=dil), :].astype(BF)


def _post_d_body(o0, o1, o2, l0, l1, l2, x_ref, w_ref, g_ref, b_ref, y_ref, scr_o, scr_l, *, alpha, dils):
    tm = x_ref.shape[0]

    def interleaved(ref, scr, dil):
        if dil == 1:
            return ref[0, 0].astype(F32)
        nblk = scr.shape[0]
        for c in range(dil):
            for j in range(nblk):
                scr[j, pl.ds(c, tm // dil, stride=dil), :] = ref[0, c, :, j * LANES:(j + 1) * LANES].astype(F32)
        return jnp.concatenate([scr[j] for j in range(nblk)], axis=1)

    ov = [interleaved(r, scr_o.at[j], dil) for j, (r, dil) in enumerate(zip((o0, o1, o2), dils))]
    a0, a1, a2 = [interleaved(r, scr_l.at[j], dil) for j, (r, dil) in enumerate(zip((l0, l1, l2), dils))]
    m = jnp.maximum(jnp.maximum(a0, a1), a2)
    e0, e1, e2 = jnp.exp2(a0 - m), jnp.exp2(a1 - m), jnp.exp2(a2 - m)
    den = e0 + e1 + e2
    o = (e0 / den) * ov[0] + (e1 / den) * ov[1] + (e2 / den) * ov[2]
    h = jnp.dot(o.astype(BF), w_ref[...], preferred_element_type=F32)
    y_ref[...] = _layer_norm_rows(alpha * x_ref[...] + h, g_ref[...], b_ref[...])


def _mixer_d_and_post(x2, bsz, seq, w_qkv, w_out, g, b, alpha, tm=512):
    t, d = x2.shape
    nh = D_HEADS * D_HEAD_DIM
    ng = len(D_GROUPS)
    dils = tuple(dil for _, dil in D_GROUPS)
    tm = min(tm, seq)
    nblk = seq // tm
    scale = D_HEAD_DIM ** -0.5 * LOG2E
    wq = w_qkv.reshape(d, 3, ng, nh)
    wq = jnp.stack([wq[:, 0] * scale, wq[:, 1], wq[:, 2]], axis=1)
    w = wq.transpose(0, 2, 1, 3).reshape(d, ng * 3 * nh).astype(BF)

    def deint_spec(dil, width):
        return pl.BlockSpec((1, dil, tm // dil, width), lambda i: (i // nblk, 0, i % nblk, 0))

    qkvs = pl.pallas_call(
        functools.partial(_proj_d_body, dils=dils),
        grid=(t // tm,),
        in_specs=[pl.BlockSpec((tm, d), lambda i: (i, 0)), pl.BlockSpec((d, ng * 3 * nh), lambda i: (0, 0))],
        out_specs=[deint_spec(dil, 3 * nh) for dil in dils],
        out_shape=[jax.ShapeDtypeStruct((bsz, dil, seq // dil, 3 * nh), BF) for dil in dils],
        scratch_shapes=[pltpu.VMEM((3 * nh // LANES, tm, LANES), F32)],
        compiler_params=_cparams("parallel"),
        name="proj_d",
    )(x2, w)
    os_, ls_ = [], []
    for qkv_g, (window, dil) in zip(qkvs, D_GROUPS):
        o, lse = _attn_d_group(qkv_g, window, dil)
        os_.append(o)
        ls_.append(lse)
    row = lambda i: (i, 0)
    fixed = lambda i: (0, 0)
    return pl.pallas_call(
        functools.partial(_post_d_body, alpha=alpha, dils=dils),
        grid=(t // tm,),
        in_specs=[deint_spec(dil, nh) for dil in dils] * 2 + [
            pl.BlockSpec((tm, d), row), pl.BlockSpec((nh, d), fixed),
            pl.BlockSpec((1, d), fixed), pl.BlockSpec((1, d), fixed)],
        out_specs=pl.BlockSpec((tm, d), row),
        out_shape=jax.ShapeDtypeStruct((t, d), F32),
        scratch_shapes=[pltpu.VMEM((ng, nh // LANES, tm, LANES), F32), pltpu.VMEM((ng, nh // LANES, tm, LANES), F32)],
        compiler_params=_cparams("parallel"),
        name="post_d",
    )(*os_, *ls_, x2, w_out.astype(BF), g.reshape(1, d), b.reshape(1, d))


def _router_gates(x, wh_ref, wl_ref, rb_ref):
    tm = x.shape[0]
    xh = x.astype(BF)
    xl = (x - xh.astype(F32)).astype(BF)
    wh = wh_ref[...]
    logits = (lax.dot_general(wh, xh, NT_DIMS, preferred_element_type=F32)
              + lax.dot_general(wh, xl, NT_DIMS, preferred_element_type=F32)
              + lax.dot_general(wl_ref[...], xh, NT_DIMS, preferred_element_type=F32))
    scores = jax.nn.sigmoid(logits)
    biased = scores + rb_ref[...]
    epg = N_EXPERTS // N_EXPERT_GROUPS
    sc = [scores[e:e + 1, :] for e in range(N_EXPERTS)]
    bi = [biased[e:e + 1, :] for e in range(N_EXPERTS)]
    gs = []
    for g in range(N_EXPERT_GROUPS):
        v = bi[g * epg:(g + 1) * epg]
        best = None
        for a in range(epg):
            for c in range(a + 1, epg):
                pair = v[a] + v[c]
                best = pair if best is None else jnp.maximum(best, pair)
        gs.append(best)
    gmax = functools.reduce(jnp.maximum, gs)
    taken = jnp.zeros((1, tm), jnp.bool_)
    cand = []
    for g in range(N_EXPERT_GROUPS):
        sel = (gs[g] == gmax) & jnp.logical_not(taken)
        taken = taken | sel
        for a in range(epg):
            cand.append(jnp.where(sel, bi[g * epg + a], -jnp.inf))
    m1 = functools.reduce(jnp.maximum, cand)
    taken = jnp.zeros((1, tm), jnp.bool_)
    is1 = []
    for e in range(N_EXPERTS):
        hit = (cand[e] == m1) & jnp.logical_not(taken)
        taken = taken | hit
        is1.append(hit)
    cand2 = [jnp.where(is1[e], -jnp.inf, cand[e]) for e in range(N_EXPERTS)]
    m2 = functools.reduce(jnp.maximum, cand2)
    taken = jnp.zeros((1, tm), jnp.bool_)
    is2 = []
    for e in range(N_EXPERTS):
        hit = (cand2[e] == m2) & jnp.logical_not(taken)
        taken = taken | hit
        is2.append(hit)
    zero = jnp.zeros((1, tm), F32)
    w1 = functools.reduce(jnp.add, [jnp.where(is1[e], sc[e], zero) for e in range(N_EXPERTS)])
    w2 = functools.reduce(jnp.add, [jnp.where(is2[e], sc[e], zero) for e in range(N_EXPERTS)])
    den = w1 + w2
    rows = [jnp.where(is1[e], w1 / den, zero) + jnp.where(is2[e], w2 / den, zero) for e in range(N_EXPERTS)]
    rows.append(jnp.ones((1, tm), F32))
    rows.append(jnp.zeros((LANES - N_EXPERTS - 1, tm), F32))
    return jnp.concatenate(rows, axis=0).T


MOE_EXPERTS_PER_STEP = 2


def _moe_body(x_ref, p_ref, wh_ref, wl_ref, rb_ref, wgu_ref, wd_ref, wgus_ref, wds_ref, g_ref, b_ref,
              wpi_ref, wpg_ref, y_ref, gates, xb, acc, *, alpha, n_routed_steps):
    e = pl.program_id(1)
    ff = EXPERT_FF

    def hidden(gu, j):
        return jax.nn.silu(gu[:, 2 * j * ff:(2 * j + 1) * ff]) * gu[:, (2 * j + 1) * ff:(2 * j + 2) * ff]

    @pl.when(e == 0)
    def _():
        x = x_ref[...]
        gates[...] = _router_gates(x, wh_ref, wl_ref, rb_ref)
        xb[...] = x.astype(BF)
        acc[...] = jnp.zeros_like(acc)

    @pl.when(e < n_routed_steps)
    def _():
        gu = jnp.dot(xb[...], wgu_ref[0], preferred_element_type=F32)
        lane = lax.broadcasted_iota(jnp.int32, gates.shape, 1)
        gt = gates[...]
        hs = []
        for j in range(MOE_EXPERTS_PER_STEP):
            gcol = jnp.sum(jnp.where(lane == e * MOE_EXPERTS_PER_STEP + j, gt, 0.0), axis=1, keepdims=True)
            hs.append((hidden(gu, j) * gcol).astype(BF))
        acc[...] += jnp.dot(jnp.concatenate(hs, axis=1), wd_ref[0], preferred_element_type=F32)

    @pl.when(e == n_routed_steps)
    def _():
        gu = jnp.dot(xb[...], wgus_ref[...], preferred_element_type=F32)
        y = acc[...] + jnp.dot(hidden(gu, 0).astype(BF), wds_ref[...], preferred_element_type=F32)
        x2 = _layer_norm_rows(alpha * x_ref[...] + y, g_ref[...], b_ref[...])
        emb = jnp.dot(p_ref[...].astype(BF), wpi_ref[...], preferred_element_type=F32)
        gate = jax.nn.sigmoid(jnp.dot(x2.astype(BF), wpg_ref[...], preferred_element_type=F32))
        y_ref[...] = x2 + gate * emb


def _moe_layer(x2, p2, router, lw, g, b, alpha, tm=1024):
    t, d = x2.shape
    wgu, wd, wgus, wds, wpi, wpg = lw["wgu"], lw["wd"], lw["wgus"], lw["wds"], lw["wpi"], lw["wpg"]
    n_routed_steps = wgu.shape[0]
    wh_t, wl_t, rb = router
    tok = lambda i, e: (i, 0)
    fixed = lambda i, e: (0, 0)
    step = lambda i, e: (jnp.minimum(e, n_routed_steps - 1), 0, 0)
    return pl.pallas_call(
        functools.partial(_moe_body, alpha=alpha, n_routed_steps=n_routed_steps),
        grid=(t // tm, n_routed_steps + 1),
        in_specs=[
            pl.BlockSpec((tm, d), tok), pl.BlockSpec((tm, p2.shape[1]), tok),
            pl.BlockSpec((N_EXPERTS, d), fixed), pl.BlockSpec((N_EXPERTS, d), fixed),
            pl.BlockSpec((N_EXPERTS, 1), fixed),
            pl.BlockSpec((1,) + wgu.shape[1:], step), pl.BlockSpec((1,) + wd.shape[1:], step),
            pl.BlockSpec(wgus.shape, fixed), pl.BlockSpec(wds.shape, fixed),
            pl.BlockSpec((1, d), fixed), pl.BlockSpec((1, d), fixed),
            pl.BlockSpec(wpi.shape, fixed), pl.BlockSpec(wpg.shape, fixed),
        ],
        out_specs=pl.BlockSpec((tm, d), tok),
        out_shape=jax.ShapeDtypeStruct((t, d), F32),
        scratch_shapes=[pltpu.VMEM((tm, LANES), F32), pltpu.VMEM((tm, d), BF), pltpu.VMEM((tm, d), F32)],
        compiler_params=_cparams("parallel", "arbitrary"),
        name="moe",
    )(x2, p2, wh_t, wl_t, rb, wgu, wd, wgus, wds, g.reshape(1, d), b.reshape(1, d), wpi, wpg)


def _prep_shared(prm):
    depth = prm["ln1_g"].shape[0]
    rw = prm["router_w"].astype(F32)
    wh = rw.astype(BF)
    wl = (rw - wh.astype(F32)).astype(BF)
    router = (wh.T, wl.T, prm["router_b"].astype(F32).reshape(N_EXPERTS, 1))
    layers = []
    for i in range(depth):
        eps = MOE_EXPERTS_PER_STEP
        d = prm["moe_w_gate"].shape[2]
        wgu = jnp.concatenate([prm["moe_w_gate"][i], prm["moe_w_up"][i]], axis=-1)
        wgu = wgu.reshape(N_EXPERTS // eps, eps, d, 2 * EXPERT_FF).transpose(0, 2, 1, 3)
        wgu = wgu.reshape(N_EXPERTS // eps, d, eps * 2 * EXPERT_FF).astype(BF)
        wd = prm["moe_w_down"][i].reshape(N_EXPERTS // eps, eps * EXPERT_FF, d).astype(BF)
        wgus = jnp.concatenate([prm["moe_ws_gate"][i], prm["moe_ws_up"][i]], axis=-1).astype(BF)
        layers.append(dict(wgu=wgu, wd=wd, wgus=wgus, wds=prm["moe_ws_down"][i].astype(BF),
                           wpi=prm["ple_w_in"][i].astype(BF), wpg=prm["ple_w_gate"][i].astype(BF)))
    return router, layers


def _trunk(x, p, prm, router, layers):
    depth = prm["ln1_g"].shape[0]
    alpha = (2.0 * depth) ** 0.25
    bsz, seq, d = x.shape
    t = bsz * seq
    x2 = x.reshape(t, d)
    for i in range(depth):
        mixer, j = i % 4, i // 4
        xb = x2.reshape(bsz, seq, d)
        if mixer == 3:
            x2 = _mixer_d_and_post(x2, bsz, seq, prm["d_w_qkv"][j], prm["d_w_out"][j],
                                   prm["ln1_g"][i], prm["ln1_b"][i], alpha)
        else:
            if mixer == 0:
                o = _mixer_a(xb, prm["a_w_in"][j], prm["a_lambda"][j], prm["a_subln"][j], i)
                w_out = prm["a_w_out"][j]
            elif mixer == 1:
                o = _mixer_b(xb, prm["b_w_in"][j], prm["b_q_norm"][j], prm["b_kv_norm"][j],
                             prm["b_w_uq"][j], prm["b_w_ukv"][j])
                w_out = prm["b_w_out"][j]
            else:
                o = _mixer_c(xb, prm["c_w_qkv"][j], prm["c_rpb"][j])
                w_out = prm["c_w_out"][j]
            x2 = _post_attn(o, x2, w_out.astype(BF), prm["ln1_g"][i], prm["ln1_b"][i], alpha)
        x2 = _moe_layer(x2, p[i].reshape(t, -1), router, layers[i], prm["ln2_g"][i], prm["ln2_b"][i], alpha)
    return x2.reshape(bsz, seq, d)


def kernel(x_prompt, x_sample, p_prompt, p_sample, a_w_in, a_lambda, a_subln, a_w_out, b_w_in, b_q_norm, b_kv_norm, b_w_uq, b_w_ukv, b_w_out, c_w_qkv, c_rpb, c_w_out, d_w_qkv, d_w_out, router_w, router_b, moe_w_gate, moe_w_up, moe_w_down, moe_ws_gate, moe_ws_up, moe_ws_down, ln1_g, ln1_b, ln2_g, ln2_b, ple_w_in, ple_w_gate):
    prm = dict(a_w_in=a_w_in, a_lambda=a_lambda, a_subln=a_subln, a_w_out=a_w_out,
               b_w_in=b_w_in, b_q_norm=b_q_norm, b_kv_norm=b_kv_norm, b_w_uq=b_w_uq,
               b_w_ukv=b_w_ukv, b_w_out=b_w_out,
               c_w_qkv=c_w_qkv, c_rpb=c_rpb, c_w_out=c_w_out,
               d_w_qkv=d_w_qkv, d_w_out=d_w_out,
               router_w=router_w, router_b=router_b, moe_w_gate=moe_w_gate, moe_w_up=moe_w_up,
               moe_w_down=moe_w_down, moe_ws_gate=moe_ws_gate, moe_ws_up=moe_ws_up,
               moe_ws_down=moe_ws_down,
               ln1_g=ln1_g, ln1_b=ln1_b, ln2_g=ln2_g, ln2_b=ln2_b,
               ple_w_in=ple_w_in, ple_w_gate=ple_w_gate)
    router, layers = _prep_shared(prm)
    y_prompt = _trunk(x_prompt, p_prompt, prm, router, layers)
    y_sample = _trunk(x_sample, p_sample, prm, router, layers)
    return (y_prompt, y_sample)
```

```python
import functools
import math

import numpy as np
import jax
import jax.numpy as jnp
from jax import lax
from jax.experimental import pallas as pl
from jax.experimental.pallas import tpu as pltpu

BF = jnp.bfloat16
F32 = jnp.float32

VMEM_LIMIT_BYTES = 56 * 1024 * 1024
LANES = 128

GRID_W = 64
LN_EPS = 1e-5
RMS_EPS = 1e-6
ROPE_THETA = 10000.0
NEG = -1e30

A_HEADS, A_HEAD_DIM = 8, 64
B_HEADS, B_NOPE, B_ROPE, B_VDIM, B_Q_LORA, B_KV_LORA = 16, 64, 32, 64, 384, 256
C_HEADS, C_HEAD_DIM, C_WIN_ROWS, C_WIN_COLS = 16, 64, 8, 16
D_HEADS, D_HEAD_DIM = 8, 64
D_GROUPS = ((128, 1), (512, 4), (2048, 16))
N_EXPERTS, N_EXPERT_GROUPS, EXPERT_FF = 16, 4, 256

NT_DIMS = (((1,), (1,)), ((), ()))


def _cparams(*sem):
    return pltpu.CompilerParams(dimension_semantics=sem, vmem_limit_bytes=VMEM_LIMIT_BYTES)


def _layer_norm_rows(z, g, b):
    mu = jnp.mean(z, axis=-1, keepdims=True)
    zc = z - mu
    var = jnp.mean(zc * zc, axis=-1, keepdims=True)
    return zc * lax.rsqrt(var + LN_EPS) * g + b


def _rms_rows(z, g):
    return z * lax.rsqrt(jnp.mean(z * z, axis=-1, keepdims=True) + RMS_EPS) * g


def _linear_body(x_ref, w_ref, o_ref):
    o_ref[...] = jnp.dot(x_ref[...].astype(BF), w_ref[...], preferred_element_type=F32).astype(o_ref.dtype)


def _linear(x, w, out_dtype, tm=512, tn=None):
    m, k = x.shape
    n = w.shape[1]
    tn = n if tn is None else tn
    return pl.pallas_call(
        _linear_body,
        grid=(n // tn, m // tm),
        in_specs=[pl.BlockSpec((tm, k), lambda j, i: (i, 0)), pl.BlockSpec((k, tn), lambda j, i: (0, j))],
        out_specs=pl.BlockSpec((tm, tn), lambda j, i: (i, j)),
        out_shape=jax.ShapeDtypeStruct((m, n), out_dtype),
        compiler_params=_cparams("parallel", "parallel"),
        name="linear",
    )(x, w)


def _post_body(o_ref, x_ref, w_ref, g_ref, b_ref, y_ref, *, alpha):
    h = jnp.dot(o_ref[...], w_ref[...], preferred_element_type=F32)
    y_ref[...] = _layer_norm_rows(alpha * x_ref[...] + h, g_ref[...], b_ref[...])


def _post_attn(o, x, w_out, g, b, alpha, tm=512):
    t, ko = o.shape
    d = x.shape[1]
    return pl.pallas_call(
        functools.partial(_post_body, alpha=alpha),
        grid=(t // tm,),
        in_specs=[
            pl.BlockSpec((tm, ko), lambda i: (i, 0)),
            pl.BlockSpec((tm, d), lambda i: (i, 0)),
            pl.BlockSpec((ko, d), lambda i: (0, 0)),
            pl.BlockSpec((1, d), lambda i: (0, 0)),
            pl.BlockSpec((1, d), lambda i: (0, 0)),
        ],
        out_specs=pl.BlockSpec((tm, d), lambda i: (i, 0)),
        out_shape=jax.ShapeDtypeStruct((t, d), F32),
        compiler_params=_cparams("parallel"),
        name="post_attn",
    )(o, x, w_out, g.reshape(1, d), b.reshape(1, d))


LOG2E = 1.4426950408889634
SUM_ROWS = 16


def _softmax_chunk(s, m, shift, mx=None):
    if mx is None:
        mx = jnp.max(s, axis=0, keepdims=True)
    if shift is not None:
        mx = mx - shift
    m_new = jnp.maximum(m, mx)
    ref = m_new if shift is None else m_new + shift
    return m_new, jnp.exp2((s - ref).astype(BF))


def _pipelined_flash(n, nh, tq, qk_fn, sm_fn, pv_fn, s_buf, p_buf, acc_ref):
    assert n >= 4 and n % 2 == 0
    acc_ref[...] = jnp.zeros_like(acc_ref)

    def scores(j, slot):
        mxs = []
        for h in range(nh):
            s = qk_fn(j, h)
            s_buf[slot, h] = s
            mxs.append(jnp.max(s, axis=0, keepdims=True))
        return tuple(mxs)

    def stage(j, slot, carry, do_pv=True, do_qk=True, first=False):
        ms, alphas, mxs = carry
        if do_pv:
            for h in range(nh):
                acc_ref[h] = alphas[h] * acc_ref[h] + pv_fn(j - 1, h, p_buf[1 - slot, h])
        mxs_next = scores(j + 1, 1 - slot) if do_qk else mxs
        new_ms, new_alphas = [], []
        for h in range(nh):
            m_new, p = sm_fn(j, h, s_buf[slot, h], ms[h], mxs[h], first)
            p_buf[slot, h] = p
            new_alphas.append(jnp.exp2(ms[h] - m_new))
            new_ms.append(m_new)
        return tuple(new_ms), tuple(new_alphas), mxs_next

    mxs = scores(0, 0)
    ms = tuple(jnp.full((1, tq), NEG, F32) for _ in range(nh))
    alphas = tuple(jnp.zeros((1, tq), F32) for _ in range(nh))
    carry = stage(0, 0, (ms, alphas, mxs), do_pv=False, first=True)

    def body(t, carry):
        return stage(2 * t + 2, 0, stage(2 * t + 1, 1, carry))

    carry = lax.fori_loop(0, (n - 2) // 2, body, carry)
    _, alphas, _ = stage(n - 1, 1, carry, do_qk=False)
    for h in range(nh):
        acc_ref[h] = alphas[h] * acc_ref[h] + pv_fn(n - 1, h, p_buf[1, h])


def _chunk_off(j, tk):
    return j * tk if isinstance(j, int) else pl.multiple_of(j * tk, tk)


def _attn_a_body(cs_ref, ctab_ref, lam_ref, g_ref, q_ref, k_ref, kpos_ref, vt_ref, o_ref,
                 s_buf, p_buf, acc_ref, *, seq, tq, tk, hps, lam_init):
    hg = pl.program_id(1)
    qi = pl.program_id(2)
    n = seq // tk
    cd = (qi * tq) // tk
    lane = lax.broadcasted_iota(jnp.int32, (tq, LANES), 1)
    qpos = (qi * tq + lax.broadcasted_iota(jnp.int32, (1, tq), 1)).astype(F32)
    dv = vt_ref.shape[2] - SUM_ROWS
    c_slope, f0, q_left, q_right = [], [], [], []
    for hh in range(hps):
        head = hg * hps + hh
        c_slope.append(cs_ref[head])
        f0.append(c_slope[hh] * qpos)
        q = q_ref[0, :, hh * LANES:(hh + 1) * LANES]
        qaug = jnp.broadcast_to(ctab_ref[pl.ds(head, 1), :], (tq, LANES)).astype(BF)
        for half in range(2):
            q_m = jnp.where((lane >= 64 * half) & (lane < 64 * (half + 1)), q, jnp.zeros_like(q))
            q_left.append(jnp.concatenate([q_m, qaug], axis=1))
            q_right.append(jnp.concatenate([q_m, -qaug], axis=1))

    def chunk_of(j):
        jm = j - 1
        c = jnp.where(j == 0, cd, jm + (jm >= cd).astype(jnp.int32))
        return c, c <= cd

    def qk_fn(j, mm):
        hh = mm // 2
        c, left = chunk_of(j)
        off = pl.multiple_of(c * tk, tk)
        k_c = jnp.concatenate([k_ref[0, pl.ds(off, tk), hh * LANES:(hh + 1) * LANES],
                               kpos_ref[pl.ds(off, tk), :]], axis=1)
        q_full = jnp.where(left, q_left[mm], q_right[mm])
        return lax.dot_general(k_c, q_full, NT_DIMS, preferred_element_type=F32)

    def sm_fn(j, mm, s, m, mx, first):
        hh = mm // 2
        if first:
            d = (lax.broadcasted_iota(jnp.int32, (tk, tq), 0) - lax.broadcasted_iota(jnp.int32, (tk, tq), 1)
                 + (cd * tk - qi * tq)).astype(F32)
            return _softmax_chunk(s - (2.0 * c_slope[hh]) * jnp.maximum(d, 0.0), m, f0[hh])
        _, left = chunk_of(j)
        return _softmax_chunk(s, m, jnp.where(left, f0[hh], -f0[hh]), mx)

    def pv_fn(j, mm, p):
        c, _ = chunk_of(j)
        off = pl.multiple_of(c * tk, tk)
        return jnp.dot(vt_ref[0, mm // 2, :, pl.ds(off, tk)], p, preferred_element_type=F32)

    _pipelined_flash(n, 2 * hps, tq, qk_fn, sm_fn, pv_fn, s_buf, p_buf, acc_ref)
    lf = lam_ref[...]
    lam_full = (jnp.exp(jnp.sum(lf[0:1] * lf[1:2], axis=-1, keepdims=True))
                - jnp.exp(jnp.sum(lf[2:3] * lf[3:4], axis=-1, keepdims=True)) + lam_init)
    outs = []
    for hh in range(hps):
        o0, o1 = [acc_ref[2 * hh + half, :dv, :] / acc_ref[2 * hh + half, dv:dv + 1, :] for half in range(2)]
        o = o0 - lam_full * o1
        ms = jnp.mean(o * o, axis=0, keepdims=True)
        outs.append(o * lax.rsqrt(ms + RMS_EPS) * g_ref[...] * (1.0 - lam_init))
    o_ref[0] = jnp.concatenate(outs, axis=0).T.astype(o_ref.dtype)


def _bf16_pieces(c, n=3):
    pieces, rest = [], c.astype(F32)
    for _ in range(n):
        p = rest.astype(BF)
        pieces.append(p.astype(F32))
        rest = rest - p.astype(F32)
    return pieces


def _store_vt_heads(vt_ref, vt, dv, heads_per_slot):
    tm = vt.shape[1]
    tail = jnp.where(lax.broadcasted_iota(jnp.int32, (SUM_ROWS, tm), 0) == 0, 1.0, 0.0).astype(vt_ref.dtype)
    rows = dv + SUM_ROWS
    for h in range(vt.shape[0] // dv):
        slot, r0 = h // heads_per_slot, (h % heads_per_slot) * rows
        vt_ref[0, slot, r0:r0 + dv, :] = vt[h * dv:(h + 1) * dv, :].astype(vt_ref.dtype)
        vt_ref[0, slot, r0 + dv:r0 + rows, :] = tail


def _proj_a_body(x_ref, wqk_ref, wvt_ref, qk_ref, vt_ref):
    xb = x_ref[...].astype(BF)
    qk_ref[...] = jnp.dot(xb, wqk_ref[...], preferred_element_type=F32).astype(BF)
    vt = lax.dot_general(wvt_ref[...], xb, NT_DIMS, preferred_element_type=F32)
    _store_vt_heads(vt_ref, vt, 2 * A_HEAD_DIM, 1)


def _flash_chunk_len(seq, tk_max):
    for min_chunks in (8, 4):
        tk = tk_max
        while tk >= 2 * LANES and (seq % tk or (seq // tk) < min_chunks or (seq // tk) % 2):
            tk //= 2
        if tk >= 2 * LANES:
            return tk
    raise ValueError(f"sequence length {seq} too short for the pipelined attention kernel")


def _flash_scratch(nh, rows, tq, tk):
    return [pltpu.VMEM((2, nh, tk, tq), F32), pltpu.VMEM((2, nh, tk, tq), BF), pltpu.VMEM((nh, rows, tq), F32)]


def _attn_a(qkv, vt, lam, subln_g, lam_init, tq=256, tk=1024):
    bsz, seq, _ = qkv.shape
    tq, tk = min(tq, seq), _flash_chunk_len(seq, tk)
    c_slope = jnp.asarray(2.0 ** (-8.0 * np.arange(1, A_HEADS + 1) / A_HEADS) * LOG2E, dtype=F32)
    c1, c2, c3 = _bf16_pieces(c_slope)
    ctab = jnp.stack([c1, c1, c2, c2, c3, c3], axis=-1)
    ctab = jnp.concatenate([ctab, jnp.zeros((A_HEADS, LANES - 6), F32)], axis=-1)
    pos = np.arange(seq)
    kpos = np.zeros((seq, LANES), np.float32)
    for j in range(3):
        kpos[:, 2 * j] = (pos // LANES) * LANES
        kpos[:, 2 * j + 1] = pos % LANES
    kpos = jnp.asarray(kpos, BF)
    vrows = vt.shape[2]
    hps = 2 if seq <= 4096 else 1
    ngrp = A_HEADS // hps
    body = functools.partial(_attn_a_body, seq=seq, tq=tq, tk=tk, hps=hps, lam_init=lam_init)
    return pl.pallas_call(
        body,
        grid=(bsz, ngrp, seq // tq),
        in_specs=[
            pl.BlockSpec(memory_space=pltpu.SMEM),
            pl.BlockSpec((A_HEADS, LANES), lambda b, h, i: (0, 0)),
            pl.BlockSpec((4, A_HEAD_DIM), lambda b, h, i: (0, 0)),
            pl.BlockSpec((2 * A_HEAD_DIM, 1), lambda b, h, i: (0, 0)),
            pl.BlockSpec((1, tq, hps * LANES), lambda b, h, i: (b, i, h)),
            pl.BlockSpec((1, seq, hps * LANES), lambda b, h, i: (b, 0, ngrp + h)),
            pl.BlockSpec((seq, LANES), lambda b, h, i: (0, 0)),
            pl.BlockSpec((1, hps, vrows, seq), lambda b, h, i: (b, h, 0, 0)),
        ],
        out_specs=pl.BlockSpec((1, tq, hps * LANES), lambda b, h, i: (b, i, h)),
        out_shape=jax.ShapeDtypeStruct((bsz, seq, A_HEADS * LANES), BF),
        scratch_shapes=_flash_scratch(2 * hps, vrows, tq, tk),
        compiler_params=_cparams("parallel", "parallel", "arbitrary"),
        name="attn_a",
    )(c_slope, ctab, lam.astype(F32), subln_g.astype(F32).reshape(2 * A_HEAD_DIM, 1), qkv, qkv, kpos, vt)


def _attn_b_body(q_ref, k_ref, vt_ref, o_ref, s_buf, p_buf, acc_ref, *, seq, tk, nh):
    tq = q_ref.shape[1]
    rows = B_VDIM + SUM_ROWS
    qs = [q_ref[0, :, h * LANES:(h + 1) * LANES] for h in range(nh)]

    def qk_fn(j, h):
        k_c = k_ref[0, pl.ds(_chunk_off(j, tk), tk), h * LANES:(h + 1) * LANES]
        return lax.dot_general(k_c, qs[h], NT_DIMS, preferred_element_type=F32)

    def sm_fn(j, h, s, m, mx, first):
        return _softmax_chunk(s, m, None, mx)

    def pv_fn(j, h, p):
        vt_c = vt_ref[0, h // 2, (h % 2) * rows:(h % 2 + 1) * rows, pl.ds(_chunk_off(j, tk), tk)]
        return jnp.dot(vt_c, p, preferred_element_type=F32)

    _pipelined_flash(seq // tk, nh, tq, qk_fn, sm_fn, pv_fn, s_buf, p_buf, acc_ref)
    outs = [acc_ref[h, :B_VDIM, :] / acc_ref[h, B_VDIM:B_VDIM + 1, :] for h in range(nh)]
    o = jnp.concatenate(outs, axis=0)
    o_ref[0] = o.T.astype(o_ref.dtype)


def _attn_b(q, k, vt, tq=256, tk=1024):
    bsz, seq, _ = q.shape
    tq, tk = min(tq, seq), _flash_chunk_len(seq, tk)
    nh = 4 if seq <= 4096 else 2
    return pl.pallas_call(
        functools.partial(_attn_b_body, seq=seq, tk=tk, nh=nh),
        grid=(bsz, B_HEADS // nh, seq // tq),
        in_specs=[
            pl.BlockSpec((1, tq, nh * LANES), lambda b, h, i: (b, i, h)),
            pl.BlockSpec((1, seq, nh * LANES), lambda b, h, i: (b, 0, h)),
            pl.BlockSpec((1, nh // 2, 2 * (B_VDIM + SUM_ROWS), seq), lambda b, h, i: (b, h, 0, 0)),
        ],
        out_specs=pl.BlockSpec((1, tq, nh * B_VDIM), lambda b, h, i: (b, i, h)),
        out_shape=jax.ShapeDtypeStruct((bsz, seq, B_HEADS * B_VDIM), BF),
        scratch_shapes=_flash_scratch(nh, B_VDIM + SUM_ROWS, tq, tk),
        compiler_params=_cparams("parallel", "parallel", "arbitrary"),
        name="attn_b",
    )(q, k, vt)


def _mla_in_body(x_ref, w_ref, gq_ref, gkv_ref, rc_ref, rs_ref, cq_ref, ckv_ref, kr_ref):
    h = jnp.dot(x_ref[...].astype(BF), w_ref[...], preferred_element_type=F32)
    cq_ref[...] = _rms_rows(h[:, :B_Q_LORA], gq_ref[...]).astype(BF)
    ckv_ref[...] = _rms_rows(h[:, B_Q_LORA:B_Q_LORA + B_KV_LORA], gkv_ref[...]).astype(BF)
    t = h[:, B_Q_LORA + B_KV_LORA:]
    kr = t * rc_ref[...] + pltpu.roll(t, LANES - B_ROPE, 1) * rs_ref[...]
    kr_ref[...] = kr.astype(BF)


def _mla_q_body(cq_ref, w_ref, ta_ref, tb_ref, q_ref):
    t = jnp.dot(cq_ref[...], w_ref[...], preferred_element_type=F32)
    ta = ta_ref[...]
    tb = tb_ref[...]
    for h in range(B_HEADS):
        th = t[:, h * LANES:(h + 1) * LANES]
        q_ref[:, h * LANES:(h + 1) * LANES] = (th * ta + pltpu.roll(th, LANES - B_ROPE, 1) * tb).astype(BF)


def _mla_kv_body(ckv_ref, kr_ref, wk_ref, e_ref, wvt_ref, k_ref, vt_ref):
    ckv = ckv_ref[...]
    k = jnp.dot(ckv, wk_ref[...], preferred_element_type=F32)
    k = k + jnp.dot(kr_ref[...], e_ref[...], preferred_element_type=F32)
    k_ref[...] = k.astype(BF)
    vt = lax.dot_general(wvt_ref[...], ckv, NT_DIMS, preferred_element_type=F32)
    _store_vt_heads(vt_ref, vt, B_VDIM, 2)


def _rope_partner(w):
    half = B_ROPE // 2
    return jnp.concatenate([-w[..., half:], w[..., :half]], axis=-1)


def _mla_tables(seq):
    inv = 1.0 / (ROPE_THETA ** (np.arange(0, B_ROPE, 2, dtype=np.float32) / B_ROPE))
    ang = jnp.arange(seq, dtype=F32)[:, None] * jnp.asarray(inv, F32)[None, :]
    cos = jnp.concatenate([jnp.cos(ang), jnp.cos(ang)], axis=-1)
    sin = jnp.concatenate([jnp.sin(ang), jnp.sin(ang)], axis=-1)
    z = lambda n: jnp.zeros((seq, n), F32)
    rc = jnp.concatenate([cos, z(LANES - B_ROPE)], axis=-1)
    rs = jnp.concatenate([sin, z(LANES - B_ROPE)], axis=-1)
    scale = (B_NOPE + B_ROPE) ** -0.5 * LOG2E
    ta =jnp.concatenate([jnp.full((seq, B_NOPE), scale, F32), cos * scale, z(B_ROPE)], axis=-1)
    tb = jnp.concatenate([z(B_NOPE), sin * scale, z(B_ROPE)], axis=-1)
    return rc, rs, ta, tb


def _mla_weights(w_in, w_uq, w_ukv):
    d = w_in.shape[0]
    kr0 = B_Q_LORA + B_KV_LORA
    w_in_ext = jnp.concatenate(
        [w_in, _rope_partner(w_in[:, kr0:kr0 + B_ROPE]), jnp.zeros((d, LANES - 2 * B_ROPE), w_in.dtype)], axis=-1)
    wq = w_uq.reshape(B_Q_LORA, B_HEADS, B_NOPE + B_ROPE)
    wq_ext = jnp.concatenate([wq, _rope_partner(wq[..., B_NOPE:])], axis=-1).reshape(B_Q_LORA, B_HEADS * LANES)
    wkv = w_ukv.reshape(B_KV_LORA, B_HEADS, B_NOPE + B_VDIM)
    wk = jnp.concatenate([wkv[..., :B_NOPE], jnp.zeros((B_KV_LORA, B_HEADS, LANES - B_NOPE), w_ukv.dtype)], axis=-1)
    wk = wk.reshape(B_KV_LORA, B_HEADS * LANES)
    wv = wkv[..., B_NOPE:].reshape(B_KV_LORA, B_HEADS * B_VDIM)
    place = np.zeros((LANES, B_HEADS, LANES), np.float32)
    for j in range(B_ROPE):
        place[j, :, B_NOPE + j] = 1.0
    place = jnp.asarray(place.reshape(LANES, B_HEADS * LANES), BF)
    return w_in_ext.astype(BF), wq_ext.astype(BF), wk.astype(BF), place, wv.astype(BF)


def _mixer_b(x, w_in, q_norm_g, kv_norm_g, w_uq, w_ukv, tm=512):
    bsz, seq, d = x.shape
    t = bsz * seq
    tm = min(tm, seq)
    nblk = seq // tm
    w_in_ext, wq_ext, wk, place, wv = _mla_weights(w_in, w_uq, w_ukv)
    rc, rs, ta, tb = _mla_tables(seq)
    x2 = x.reshape(t, d)
    n_in = w_in_ext.shape[1]
    row = lambda i: (i, 0)
    fixed = lambda i: (0, 0)
    pos = lambda i: (i % nblk, 0)
    cq, ckv, kr = pl.pallas_call(
        _mla_in_body,
        grid=(t // tm,),
        in_specs=[
            pl.BlockSpec((tm, d), row), pl.BlockSpec((d, n_in), fixed),
            pl.BlockSpec((1, B_Q_LORA), fixed), pl.BlockSpec((1, B_KV_LORA), fixed),
            pl.BlockSpec((tm, LANES), pos), pl.BlockSpec((tm, LANES), pos),
        ],
        out_specs=[pl.BlockSpec((tm, B_Q_LORA), row), pl.BlockSpec((tm, B_KV_LORA), row),
                   pl.BlockSpec((tm, LANES), row)],
        out_shape=[jax.ShapeDtypeStruct((t, B_Q_LORA), BF), jax.ShapeDtypeStruct((t, B_KV_LORA), BF),
                   jax.ShapeDtypeStruct((t, LANES), BF)],
        compiler_params=_cparams("parallel"),
        name="mla_in",
    )(x2, w_in_ext, q_norm_g.astype(F32).reshape(1, -1), kv_norm_g.astype(F32).reshape(1, -1), rc, rs)
    nq = B_HEADS * LANES
    q = pl.pallas_call(
        _mla_q_body,
        grid=(t // tm,),
        in_specs=[pl.BlockSpec((tm, B_Q_LORA), row), pl.BlockSpec((B_Q_LORA, nq), fixed),
                  pl.BlockSpec((tm, LANES), pos), pl.BlockSpec((tm, LANES), pos)],
        out_specs=pl.BlockSpec((tm, nq), row),
        out_shape=jax.ShapeDtypeStruct((t, nq), BF),
        compiler_params=_cparams("parallel"),
        name="mla_q",
    )(cq, wq_ext, ta, tb)
    nv = B_HEADS * B_VDIM
    npair = B_HEADS // 2
    vrows = 2 * (B_VDIM + SUM_ROWS)
    k, vt = pl.pallas_call(
        _mla_kv_body,
        grid=(t // tm,),
        in_specs=[pl.BlockSpec((tm, B_KV_LORA), row), pl.BlockSpec((tm, LANES), row),
                  pl.BlockSpec((B_KV_LORA, nq), fixed), pl.BlockSpec((LANES, nq), fixed),
                  pl.BlockSpec((nv, B_KV_LORA), fixed)],
        out_specs=[pl.BlockSpec((tm, nq), row),
                   pl.BlockSpec((1, npair, vrows, tm), lambda i: (i // nblk, 0, 0, i % nblk))],
        out_shape=[jax.ShapeDtypeStruct((t, nq), BF), jax.ShapeDtypeStruct((bsz, npair, vrows, seq), BF)],
        compiler_params=_cparams("parallel"),
        name="mla_kv",
    )(ckv, kr, wk, place, wv.T)
    o = _attn_b(q.reshape(bsz, seq, nq), k.reshape(bsz, seq, nq), vt)
    return o.reshape(t, nv)


def _mixer_a(x, w_in, lam, subln_g, layer_idx):
    bsz, seq, d = x.shape
    t = bsz * seq
    hd2 = 2 * A_HEAD_DIM
    nq = A_HEADS * hd2
    scale = A_HEAD_DIM ** -0.5 * LOG2E
    wqk = jnp.concatenate([w_in[:, :nq] * scale, w_in[:, nq:2 * nq]], axis=-1).astype(BF)
    wvt = w_in[:, 2 * nq:].T.astype(BF)
    tm = min(512, seq)
    nblk = seq // tm
    vrows = hd2 + SUM_ROWS
    qk, vt = pl.pallas_call(
        _proj_a_body,
        grid=(t // tm,),
        in_specs=[pl.BlockSpec((tm, d), lambda i: (i, 0)), pl.BlockSpec((d, 2 * nq), lambda i: (0, 0)),
                  pl.BlockSpec((nq, d), lambda i: (0, 0))],
        out_specs=[pl.BlockSpec((tm, 2 * nq), lambda i: (i, 0)),
                   pl.BlockSpec((1, A_HEADS, vrows, tm), lambda i: (i // nblk, 0, 0, i % nblk))],
        out_shape=[jax.ShapeDtypeStruct((t, 2 * nq), BF), jax.ShapeDtypeStruct((bsz, A_HEADS, vrows, seq), BF)],
        compiler_params=_cparams("parallel"),
        name="proj_a",
    )(x.reshape(t, d), wqk, wvt)
    lam_init = 0.8 - 0.6 * math.exp(-0.3 * layer_idx)
    o = _attn_a(qk.reshape(bsz, seq, 2 * nq), vt, lam, subln_g, lam_init)
    return o.reshape(t, nq)


def _attn_c_body(q_ref, k0, k1, k2, k3, v0, v1, v2, v3, bt_ref, o_ref):
    q = q_ref[0]
    kw = jnp.concatenate([k0[0], k1[0], k2[0], k3[0]], axis=0)
    vw = jnp.concatenate([v0[0], v1[0], v2[0], v3[0]], axis=0)
    lane = lax.broadcasted_iota(jnp.int32, q.shape, 1)
    outs = []
    for hh in range(2):
        qm = jnp.where((lane >= 64 * hh) & (lane < 64 * (hh + 1)), q, jnp.zeros_like(q))
        s = lax.dot_general(qm, kw, NT_DIMS, preferred_element_type=F32) + bt_ref[hh, 0]
        m = jnp.max(s, axis=1, keepdims=True)
        p = jnp.exp2(s - m)
        l = jnp.sum(p, axis=1, keepdims=True)
        outs.append(jnp.dot(p.astype(BF), vw, preferred_element_type=F32) / l)
    o_ref[0] = jnp.where(lane < 64, outs[0], outs[1]).astype(o_ref.dtype)


def _nbr_bias_table(rpb):
    col = np.arange(GRID_W)
    col_start = np.clip(col - C_WIN_COLS // 2, 0, GRID_W - C_WIN_COLS)
    col_mask = (col[None, :] >= col_start[:, None]) & (col[None, :] < col_start[:, None] + C_WIN_COLS)
    pad = GRID_W - C_WIN_COLS
    ext = jnp.pad(rpb.astype(F32) * LOG2E, ((0, 0), (0, 0), (pad, pad)), mode="edge")
    toep = jnp.stack([ext[:, :, GRID_W - 1 - qc:2 * GRID_W - 1 - qc] for qc in range(GRID_W)], axis=2)
    toep = jnp.where(jnp.asarray(col_mask)[None, None], toep, NEG)
    neg = jnp.full((rpb.shape[0], GRID_W, GRID_W), NEG, F32)
    half = C_WIN_ROWS // 2
    kinds = []
    for kind in range(3):
        qrows = []
        for t in range(C_WIN_ROWS):
            u0 = (max(t, half), t, min(t, half))[kind]
            blocks = [toep[:, u - t + half - 1] if u0 <= u < u0 + C_WIN_ROWS else neg
                      for u in range(2 * C_WIN_ROWS)]
            qrows.append(jnp.concatenate(blocks, axis=-1))
        kinds.append(jnp.concatenate(qrows, axis=1))
    return jnp.stack(kinds, axis=1)


def _mixer_c(x, w_qkv, rpb):
    bsz, seq, d = x.shape
    t = bsz * seq
    nq = C_HEADS * C_HEAD_DIM
    scale = C_HEAD_DIM ** -0.5 * LOG2E
    w = jnp.concatenate([w_qkv[:, :nq] * scale, w_qkv[:, nq:]], axis=-1).astype(BF)
    qkv = _linear(x.reshape(t, d), w, BF, tn=nq).reshape(bsz, seq, 3 * nq)
    bt = _nbr_bias_table(rpb)
    npair = C_HEADS // 2
    qtok = C_WIN_ROWS * GRID_W
    ngrp = seq // qtok
    assert ngrp >= 2
    kb = qtok // 2
    nkb = seq // kb
    kspecs = []
    for off in (npair, 2 * npair):
        for j in range(4):
            kspecs.append(pl.BlockSpec(
                (1, kb, LANES),
                lambda h, g, b, j=j, off=off: (b, jnp.clip(2 * g - 1 + j, 0, nkb - 1), off + h)))
    kind = lambda g: jnp.where(g == 0, 0, jnp.where(g == ngrp - 1, 2, 1))
    o = pl.pallas_call(
        _attn_c_body,
        grid=(npair, ngrp, bsz),
        in_specs=[pl.BlockSpec((1, qtok, LANES), lambda h, g, b: (b, g, h))] + kspecs + [
            pl.BlockSpec((2, 1, qtok, 2 * qtok), lambda h, g, b: (h, kind(g), 0, 0))],
        out_specs=pl.BlockSpec((1, qtok, LANES), lambda h, g, b: (b, g, h)),
        out_shape=jax.ShapeDtypeStruct((bsz, seq, nq), BF),
        compiler_params=_cparams("parallel", "parallel", "arbitrary"),
        name="attn_c",
    )(qkv, *([qkv] * 8), bt)
    return o.reshape(t, nq)


def _attn_d_body(slopes_ref, q_ref, kp, kc, kn, vp, vc, vn, o_ref, lse_ref, *, tq, length, dil, rad):
    i = pl.program_id(2)
    nk = tq + 2 * rad
    qi = lax.broadcasted_iota(jnp.int32, (tq, nk), 0)
    kk = lax.broadcasted_iota(jnp.int32, (tq, nk), 1)
    dist = jnp.abs(kk - rad - qi)
    ki = i * tq - rad + kk
    valid = (dist <= rad) & (ki >= 0) & (ki < length)
    distf = dist.astype(F32) * float(dil)
    lane = lax.broadcasted_iota(jnp.int32, (tq, LANES), 1)
    for hp in range(D_HEADS // 2):
        sl = slice(hp * LANES, (hp + 1) * LANES)
        q = q_ref[0, 0, :, sl]
        kw = jnp.concatenate([kp[0, 0, tq - rad:tq, sl], kc[0, 0, :, sl], kn[0, 0, 0:rad, sl]], axis=0)
        vw = jnp.concatenate([vp[0, 0, tq - rad:tq, sl], vc[0, 0, :, sl], vn[0, 0, 0:rad, sl]], axis=0)
        outs, lses = [], []
        for hh in range(2):
            slope = slopes_ref[hp * 2 + hh]
            qm = jnp.where((lane >= 64 * hh) & (lane < 64 * (hh + 1)), q, jnp.zeros_like(q))
            s = lax.dot_general(qm, kw, NT_DIMS, preferred_element_type=F32)
            s = jnp.where(valid, s - slope * distf, NEG)
            m = jnp.max(s, axis=1, keepdims=True)
            p = jnp.exp2(s - m)
            l = jnp.sum(p, axis=1, keepdims=True)
            outs.append(jnp.dot(p.astype(BF), vw, preferred_element_type=F32) / l)
            lses.append(m + jnp.log(l) * LOG2E)
        o_ref[0, 0, :, sl] = jnp.where(lane < 64, outs[0], outs[1]).astype(o_ref.dtype)
        lse_ref[0, 0, :, sl] = jnp.where(lane < 64, lses[0], lses[1])


def _attn_d_group(qkv, window, dil, tq=256):
    bsz, _, length, ncol = qkv.shape
    rad = window // (2 * dil)
    tq = min(tq, length)
    nq = length // tq
    nh = ncol // 3
    slopes = jnp.asarray(2.0 ** (-8.0 * np.arange(1, D_HEADS + 1) / D_HEADS) * LOG2E, dtype=F32)

    def spec(which, shift):
        return pl.BlockSpec((1, 1, tq, nh), lambda b, c, i: (b, c, jnp.clip(i + shift, 0, nq - 1), which))

    ospec = pl.BlockSpec((1, 1, tq, nh), lambda b, c, i: (b, c, i, 0))
    return pl.pallas_call(
        functools.partial(_attn_d_body, tq=tq, length=length, dil=dil, rad=rad),
        grid=(bsz, dil, nq),
        in_specs=[pl.BlockSpec(memory_space=pltpu.SMEM), spec(0, 0),
                  spec(1, -1), spec(1, 0), spec(1, 1), spec(2, -1), spec(2, 0), spec(2, 1)],
        out_specs=[ospec, ospec],
        out_shape=[jax.ShapeDtypeStruct((bsz, dil, length, nh), BF),
                   jax.ShapeDtypeStruct((bsz, dil, length, nh), F32)],
        compiler_params=_cparams("parallel", "parallel", "arbitrary"),
        name=f"attn_d{dil}",
    )(slopes, qkv, qkv, qkv, qkv, qkv, qkv, qkv)


def _proj_d_body(x_ref, w_ref, o0, o1, o2, scr, *, dils):
    xb = x_ref[...].astype(BF)
    tm = xb.shape[0]
    nblk = scr.shape[0]
    ncol = nblk * LANES
    for gi, (o_ref, dil) in enumerate(zip((o0, o1, o2), dils)):
        res = jnp.dot(xb, w_ref[:, gi * ncol:(gi + 1) * ncol], preferred_element_type=F32)
        if dil == 1:
            o_ref[0, 0] = res.astype(BF)
            continue
        for j in range(nblk):
            scr[j] = res[:, j * LANES:(j + 1) * LANES]
        for c in range(dil):
            for j in range(nblk):
                o_ref[0, c, :, j * LANES:(j + 1) * LANES] = scr[j, pl.ds(c, tm // dil, stride=dil), :].astype(BF)


def _post_d_body(o0, o1, o2, l0, l1, l2, x_ref, w_ref, g_ref, b_ref, y_ref, scr_o, scr_l, *, alpha, dils):
    tm = x_ref.shape[0]

    def interleaved(ref, scr, dil):
        if dil == 1:
            return ref[0, 0].astype(F32)
        nblk = scr.shape[0]
        for c in range(dil):
            for j in range(nblk):
                scr[j, pl.ds(c, tm // dil, stride=dil), :] = ref[0, c, :, j * LANES:(j + 1) * LANES].astype(F32)
        return jnp.concatenate([scr[j] for j in range(nblk)], axis=1)

    ov = [interleaved(r, scr_o.at[j], dil) for j, (r, dil) in enumerate(zip((o0, o1, o2), dils))]
    a0, a1, a2 = [interleaved(r, scr_l.at[j], dil) for j, (r, dil) in enumerate(zip((l0, l1, l2), dils))]
    m = jnp.maximum(jnp.maximum(a0, a1), a2)
    e0, e1, e2 = jnp.exp2(a0 - m), jnp.exp2(a1 - m), jnp.exp2(a2 - m)
    den = e0 + e1 + e2
    o = (e0 / den) * ov[0] + (e1 / den) * ov[1] + (e2 / den) * ov[2]
    h = jnp.dot(o.astype(BF), w_ref[...], preferred_element_type=F32)
    y_ref[...] = _layer_norm_rows(alpha * x_ref[...] + h, g_ref[...], b_ref[...])


def _mixer_d_and_post(x2, bsz, seq, w_qkv, w_out, g, b, alpha, tm=512):
    t, d = x2.shape
    nh = D_HEADS * D_HEAD_DIM
    ng = len(D_GROUPS)
    dils = tuple(dil for _, dil in D_GROUPS)
    tm = min(tm, seq)
    nblk = seq // tm
    scale = D_HEAD_DIM ** -0.5 * LOG2E
    wq = w_qkv.reshape(d, 3, ng, nh)
    wq = jnp.stack([wq[:, 0] * scale, wq[:, 1], wq[:, 2]], axis=1)
    w = wq.transpose(0, 2, 1, 3).reshape(d, ng * 3 * nh).astype(BF)

    def deint_spec(dil, width):
        return pl.BlockSpec((1, dil, tm // dil, width), lambda i: (i // nblk, 0, i % nblk, 0))

    qkvs = pl.pallas_call(
        functools.partial(_proj_d_body, dils=dils),
        grid=(t // tm,),
        in_specs=[pl.BlockSpec((tm, d), lambda i: (i, 0)), pl.BlockSpec((d, ng * 3 * nh), lambda i: (0, 0))],
        out_specs=[deint_spec(dil, 3 * nh) for dil in dils],
        out_shape=[jax.ShapeDtypeStruct((bsz, dil, seq // dil, 3 * nh), BF) for dil in dils],
        scratch_shapes=[pltpu.VMEM((3 * nh // LANES, tm, LANES), F32)],
        compiler_params=_cparams("parallel"),
        name="proj_d",
    )(x2, w)
    os_, ls_ = [], []
    for qkv_g, (window, dil) in zip(qkvs, D_GROUPS):
        o, lse = _attn_d_group(qkv_g, window, dil)
        os_.append(o)
        ls_.append(lse)
    row = lambda i: (i, 0)
    fixed = lambda i: (0, 0)
    return pl.pallas_call(
        functools.partial(_post_d_body, alpha=alpha, dils=dils),
        grid=(t // tm,),
        in_specs=[deint_spec(dil, nh) for dil in dils] * 2 + [
            pl.BlockSpec((tm, d), row), pl.BlockSpec((nh, d), fixed),
            pl.BlockSpec((1, d), fixed), pl.BlockSpec((1, d), fixed)],
        out_specs=pl.BlockSpec((tm, d), row),
        out_shape=jax.ShapeDtypeStruct((t, d), F32),
        scratch_shapes=[pltpu.VMEM((ng, nh // LANES, tm, LANES), F32), pltpu.VMEM((ng, nh // LANES, tm, LANES), F32)],
        compiler_params=_cparams("parallel"),
        name="post_d",
    )(*os_, *ls_, x2, w_out.astype(BF), g.reshape(1, d), b.reshape(1, d))


def _router_gates(x, wh_ref, wl_ref, rb_ref):
    tm = x.shape[0]
    xh = x.astype(BF)
    xl = (x - xh.astype(F32)).astype(BF)
    wh = wh_ref[...]
    logits = (lax.dot_general(wh, xh, NT_DIMS, preferred_element_type=F32)
              + lax.dot_general(wh, xl, NT_DIMS, preferred_element_type=F32)
              + lax.dot_general(wl_ref[...], xh, NT_DIMS, preferred_element_type=F32))
    scores = jax.nn.sigmoid(logits)
    biased = scores + rb_ref[...]
    epg = N_EXPERTS // N_EXPERT_GROUPS
    sc = [scores[e:e + 1, :] for e in range(N_EXPERTS)]
    bi = [biased[e:e + 1, :] for e in range(N_EXPERTS)]
    gs = []
    for g in range(N_EXPERT_GROUPS):
        v = bi[g * epg:(g + 1) * epg]
        best = None
        for a in range(epg):
            for c in range(a + 1, epg):
                pair = v[a] + v[c]
                best = pair if best is None else jnp.maximum(best, pair)
        gs.append(best)
    gmax = functools.reduce(jnp.maximum, gs)
    taken = jnp.zeros((1, tm), jnp.bool_)
    cand = []
    for g in range(N_EXPERT_GROUPS):
        sel = (gs[g] == gmax) & jnp.logical_not(taken)
        taken = taken | sel
        for a in range(epg):
            cand.append(jnp.where(sel, bi[g * epg + a], -jnp.inf))
    m1 = functools.reduce(jnp.maximum, cand)
    taken = jnp.zeros((1, tm), jnp.bool_)
    is1 = []
    for e in range(N_EXPERTS):
        hit = (cand[e] == m1) & jnp.logical_not(taken)
        taken = taken | hit
        is1.append(hit)
    cand2 = [jnp.where(is1[e], -jnp.inf, cand[e]) for e in range(N_EXPERTS)]
    m2 = functools.reduce(jnp.maximum, cand2)
    taken = jnp.zeros((1, tm), jnp.bool_)
    is2 = []
    for e in range(N_EXPERTS):
        hit = (cand2[e] == m2) & jnp.logical_not(taken)
        taken = taken | hit
        is2.append(hit)
    zero = jnp.zeros((1, tm), F32)
    w1 = functools.reduce(jnp.add, [jnp.where(is1[e], sc[e], zero) for e in range(N_EXPERTS)])
    w2 = functools.reduce(jnp.add, [jnp.where(is2[e], sc[e], zero) for e in range(N_EXPERTS)])
    den = w1 + w2
    rows = [jnp.where(is1[e], w1 / den, zero) + jnp.where(is2[e], w2 / den, zero) for e in range(N_EXPERTS)]
    rows.append(jnp.ones((1, tm), F32))
    rows.append(jnp.zeros((LANES - N_EXPERTS - 1, tm), F32))
    return jnp.concatenate(rows, axis=0).T


MOE_EXPERTS_PER_STEP = 2


def _moe_body(x_ref, p_ref, wh_ref, wl_ref, rb_ref, wgu_ref, wd_ref, wgus_ref, wds_ref, g_ref, b_ref,
              wpi_ref, wpg_ref, y_ref, gates, xb, acc, *, alpha, n_routed_steps):
    e = pl.program_id(1)
    ff = EXPERT_FF

    def hidden(gu, j):
        return jax.nn.silu(gu[:, 2 * j * ff:(2 * j + 1) * ff]) * gu[:, (2 * j + 1) * ff:(2 * j + 2) * ff]

    @pl.when(e == 0)
    def _():
        x = x_ref[...]
        gates[...] = _router_gates(x, wh_ref, wl_ref, rb_ref)
        xb[...] = x.astype(BF)
        acc[...] = jnp.zeros_like(acc)

    @pl.when(e < n_routed_steps)
    def _():
        gu = jnp.dot(xb[...], wgu_ref[0], preferred_element_type=F32)
        lane = lax.broadcasted_iota(jnp.int32, gates.shape, 1)
        gt = gates[...]
        hs = []
        for j in range(MOE_EXPERTS_PER_STEP):
            gcol = jnp.sum(jnp.where(lane == e * MOE_EXPERTS_PER_STEP + j, gt, 0.0), axis=1, keepdims=True)
            hs.append((hidden(gu, j) * gcol).astype(BF))
        acc[...] += jnp.dot(jnp.concatenate(hs, axis=1), wd_ref[0], preferred_element_type=F32)

    @pl.when(e == n_routed_steps)
    def _():
        gu = jnp.dot(xb[...], wgus_ref[...], preferred_element_type=F32)
        y = acc[...] + jnp.dot(hidden(gu, 0).astype(BF), wds_ref[...], preferred_element_type=F32)
        x2 = _layer_norm_rows(alpha * x_ref[...] + y, g_ref[...], b_ref[...])
        emb = jnp.dot(p_ref[...].astype(BF), wpi_ref[...], preferred_element_type=F32)
        gate = jax.nn.sigmoid(jnp.dot(x2.astype(BF), wpg_ref[...], preferred_element_type=F32))
        y_ref[...] = x2 + gate * emb


def _moe_layer(x2, p2, router, lw, g, b, alpha, tm=1024):
    t, d = x2.shape
    wgu, wd, wgus, wds, wpi, wpg = lw["wgu"], lw["wd"], lw["wgus"], lw["wds"], lw["wpi"], lw["wpg"]
    n_routed_steps = wgu.shape[0]
    wh_t, wl_t, rb = router
    tok = lambda i, e: (i, 0)
    fixed = lambda i, e: (0, 0)
    step = lambda i, e: (jnp.minimum(e, n_routed_steps - 1), 0, 0)
    return pl.pallas_call(
        functools.partial(_moe_body, alpha=alpha, n_routed_steps=n_routed_steps),
        grid=(t // tm, n_routed_steps + 1),
        in_specs=[
            pl.BlockSpec((tm, d), tok), pl.BlockSpec((tm, p2.shape[1]), tok),
            pl.BlockSpec((N_EXPERTS, d), fixed), pl.BlockSpec((N_EXPERTS, d), fixed),
            pl.BlockSpec((N_EXPERTS, 1), fixed),
            pl.BlockSpec((1,) + wgu.shape[1:], step), pl.BlockSpec((1,) + wd.shape[1:], step),
            pl.BlockSpec(wgus.shape, fixed), pl.BlockSpec(wds.shape, fixed),
            pl.BlockSpec((1, d), fixed), pl.BlockSpec((1, d), fixed),
            pl.BlockSpec(wpi.shape, fixed), pl.BlockSpec(wpg.shape, fixed),
        ],
        out_specs=pl.BlockSpec((tm, d), tok),
        out_shape=jax.ShapeDtypeStruct((t, d), F32),
        scratch_shapes=[pltpu.VMEM((tm, LANES), F32), pltpu.VMEM((tm, d), BF), pltpu.VMEM((tm, d), F32)],
        compiler_params=_cparams("parallel", "arbitrary"),
        name="moe",
    )(x2, p2, wh_t, wl_t, rb, wgu, wd, wgus, wds, g.reshape(1, d), b.reshape(1, d), wpi, wpg)


def _prep_shared(prm):
    depth = prm["ln1_g"].shape[0]
    rw = prm["router_w"].astype(F32)
    wh = rw.astype(BF)
    wl = (rw - wh.astype(F32)).astype(BF)
    router = (wh.T, wl.T, prm["router_b"].astype(F32).reshape(N_EXPERTS, 1))
    layers = []
    for i in range(depth):
        eps = MOE_EXPERTS_PER_STEP
        d = prm["moe_w_gate"].shape[2]
        wgu = jnp.concatenate([prm["moe_w_gate"][i], prm["moe_w_up"][i]], axis=-1)
        wgu = wgu.reshape(N_EXPERTS // eps, eps, d, 2 * EXPERT_FF).transpose(0, 2, 1, 3)
        wgu = wgu.reshape(N_EXPERTS // eps, d, eps * 2 * EXPERT_FF).astype(BF)
        wd = prm["moe_w_down"][i].reshape(N_EXPERTS // eps, eps * EXPERT_FF, d).astype(BF)
        wgus = jnp.concatenate([prm["moe_ws_gate"][i], prm["moe_ws_up"][i]], axis=-1).astype(BF)
        layers.append(dict(wgu=wgu, wd=wd, wgus=wgus, wds=prm["moe_ws_down"][i].astype(BF),
                           wpi=prm["ple_w_in"][i].astype(BF), wpg=prm["ple_w_gate"][i].astype(BF)))
    return router, layers


def _trunk(x, p, prm, router, layers):
    depth = prm["ln1_g"].shape[0]
    alpha = (2.0 * depth) ** 0.25
    bsz, seq, d = x.shape
    t = bsz * seq
    x2 = x.reshape(t, d)
    for i in range(depth):
        mixer, j = i % 4, i // 4
        xb = x2.reshape(bsz, seq, d)
        if mixer == 3:
            x2 = _mixer_d_and_post(x2, bsz, seq, prm["d_w_qkv"][j], prm["d_w_out"][j],
                                   prm["ln1_g"][i], prm["ln1_b"][i], alpha)
        else:
            if mixer == 0:
                o = _mixer_a(xb, prm["a_w_in"][j], prm["a_lambda"][j], prm["a_subln"][j], i)
                w_out = prm["a_w_out"][j]
            elif mixer == 1:
                o = _mixer_b(xb, prm["b_w_in"][j], prm["b_q_norm"][j], prm["b_kv_norm"][j],
                             prm["b_w_uq"][j], prm["b_w_ukv"][j])
                w_out = prm["b_w_out"][j]
            else:
                o = _mixer_c(xb, prm["c_w_qkv"][j], prm["c_rpb"][j])
                w_out = prm["c_w_out"][j]
            x2 = _post_attn(o, x2, w_out.astype(BF), prm["ln1_g"][i], prm["ln1_b"][i], alpha)
        x2 = _moe_layer(x2, p[i].reshape(t, -1), router, layers[i], prm["ln2_g"][i], prm["ln2_b"][i], alpha)
    return x2.reshape(bsz, seq, d)


def kernel(x_prompt, x_sample, p_prompt, p_sample, a_w_in, a_lambda, a_subln, a_w_out, b_w_in, b_q_norm, b_kv_norm, b_w_uq, b_w_ukv, b_w_out, c_w_qkv, c_rpb, c_w_out, d_w_qkv, d_w_out, router_w, router_b, moe_w_gate, moe_w_up, moe_w_down, moe_ws_gate, moe_ws_up, moe_ws_down, ln1_g, ln1_b, ln2_g, ln2_b, ple_w_in, ple_w_gate):
    prm = dict(a_w_in=a_w_in, a_lambda=a_lambda, a_subln=a_subln, a_w_out=a_w_out,
               b_w_in=b_w_in, b_q_norm=b_q_norm, b_kv_norm=b_kv_norm, b_w_uq=b_w_uq,
               b_w_ukv=b_w_ukv, b_w_out=b_w_out,
               c_w_qkv=c_w_qkv, c_rpb=c_rpb, c_w_out=c_w_out,
               d_w_qkv=d_w_qkv, d_w_out=d_w_out,
               router_w=router_w, router_b=router_b, moe_w_gate=moe_w_gate, moe_w_up=moe_w_up,
               moe_w_down=moe_w_down, moe_ws_gate=moe_ws_gate, moe_ws_up=moe_ws_up,
               moe_ws_down=moe_ws_down,
               ln1_g=ln1_g, ln1_b=ln1_b, ln2_g=ln2_g, ln2_b=ln2_b,
               ple_w_in=ple_w_in, ple_w_gate=ple_w_gate)
    router, layers = _prep_shared(prm)
    y_prompt = _trunk(x_prompt, p_prompt, prm, router, layers)
    y_sample = _trunk(x_sample, p_sample, prm, router, layers)
    return (y_prompt, y_sample)
```

```python
import functools
import math

import numpy as np
import jax
import jax.numpy as jnp
from jax import lax
from jax.experimental import pallas as pl
from jax.experimental.pallas import tpu as pltpu

BF = jnp.bfloat16
F32 = jnp.float32

VMEM_LIMIT_BYTES = 56 * 1024 * 1024
LANES = 128

GRID_W = 64
LN_EPS = 1e-5
RMS_EPS = 1e-6
ROPE_THETA = 10000.0
NEG = -1e30

A_HEADS, A_HEAD_DIM = 8, 64
B_HEADS, B_NOPE, B_ROPE, B_VDIM, B_Q_LORA, B_KV_LORA = 16, 64, 32, 64, 384, 256
C_HEADS, C_HEAD_DIM, C_WIN_ROWS, C_WIN_COLS = 16, 64, 8, 16
D_HEADS, D_HEAD_DIM = 8, 64
D_GROUPS = ((128, 1), (512, 4), (2048, 16))
N_EXPERTS, N_EXPERT_GROUPS, EXPERT_FF = 16, 4, 256

NT_DIMS = (((1,), (1,)), ((), ()))


def _cparams(*sem):
    return pltpu.CompilerParams(dimension_semantics=sem, vmem_limit_bytes=VMEM_LIMIT_BYTES)


def _layer_norm_rows(z, g, b):
    mu = jnp.mean(z, axis=-1, keepdims=True)
    zc = z - mu
    var = jnp.mean(zc * zc, axis=-1, keepdims=True)
    return zc * lax.rsqrt(var + LN_EPS) * g + b


def _rms_rows(z, g):
    return z * lax.rsqrt(jnp.mean(z * z, axis=-1, keepdims=True) + RMS_EPS) * g


def _linear_body(x_ref, w_ref, o_ref):
    o_ref[...] = jnp.dot(x_ref[...].astype(BF), w_ref[...], preferred_element_type=F32).astype(o_ref.dtype)


def _linear(x, w, out_dtype, tm=512, tn=None):
    m, k = x.shape
    n = w.shape[1]
    tn = n if tn is None else tn
    return pl.pallas_call(
        _linear_body,
        grid=(n // tn, m // tm),
        in_specs=[pl.BlockSpec((tm, k), lambda j, i: (i, 0)), pl.BlockSpec((k, tn), lambda j, i: (0, j))],
        out_specs=pl.BlockSpec((tm, tn), lambda j, i: (i, j)),
        out_shape=jax.ShapeDtypeStruct((m, n), out_dtype),
        compiler_params=_cparams("parallel", "parallel"),
        name="linear",
    )(x, w)


def _post_body(o_ref, x_ref, w_ref, g_ref, b_ref, y_ref, *, alpha):
    h = jnp.dot(o_ref[...], w_ref[...], preferred_element_type=F32)
    y_ref[...] = _layer_norm_rows(alpha * x_ref[...] + h, g_ref[...], b_ref[...])


def _post_attn(o, x, w_out, g, b, alpha, tm=512):
    t, ko = o.shape
    d = x.shape[1]
    return pl.pallas_call(
        functools.partial(_post_body, alpha=alpha),
        grid=(t // tm,),
        in_specs=[
            pl.BlockSpec((tm, ko), lambda i: (i, 0)),
            pl.BlockSpec((tm, d), lambda i: (i, 0)),
            pl.BlockSpec((ko, d), lambda i: (0, 0)),
            pl.BlockSpec((1, d), lambda i: (0, 0)),
            pl.BlockSpec((1, d), lambda i: (0, 0)),
        ],
        out_specs=pl.BlockSpec((tm, d), lambda i: (i, 0)),
        out_shape=jax.ShapeDtypeStruct((t, d), F32),
        compiler_params=_cparams("parallel"),
        name="post_attn",
    )(o, x, w_out, g.reshape(1, d), b.reshape(1, d))


LOG2E = 1.4426950408889634
SUM_ROWS = 16


def _softmax_chunk(s, m, shift, mx=None):
    if mx is None:
        mx = jnp.max(s, axis=0, keepdims=True)
    if shift is not None:
        mx = mx - shift
    m_new = jnp.maximum(m, mx)
    ref = m_new if shift is None else m_new + shift
    return m_new, jnp.exp2((s - ref).astype(BF))


def _pipelined_flash(n, nh, tq, qk_fn, sm_fn, pv_fn, s_buf, p_buf, acc_ref):
    assert n >= 4 and n % 2 == 0
    acc_ref[...] = jnp.zeros_like(acc_ref)

    def scores(j, slot):
        mxs = []
        for h in range(nh):
            s = qk_fn(j, h)
            s_buf[slot, h] = s
            mxs.append(jnp.max(s, axis=0, keepdims=True))
        return tuple(mxs)

    def stage(j, slot, carry, do_pv=True, do_qk=True, first=False):
        ms, alphas, mxs = carry
        if do_pv:
            for h in range(nh):
                acc_ref[h] = alphas[h] * acc_ref[h] + pv_fn(j - 1, h, p_buf[1 - slot, h])
        mxs_next = scores(j + 1, 1 - slot) if do_qk else mxs
        new_ms, new_alphas = [], []
        for h in range(nh):
            m_new, p = sm_fn(j, h, s_buf[slot, h], ms[h], mxs[h], first)
            p_buf[slot, h] = p
            new_alphas.append(jnp.exp2(ms[h] - m_new))
            new_ms.append(m_new)
        return tuple(new_ms), tuple(new_alphas), mxs_next

    mxs = scores(0, 0)
    ms = tuple(jnp.full((1, tq), NEG, F32) for _ in range(nh))
    alphas = tuple(jnp.zeros((1, tq), F32) for _ in range(nh))
    carry = stage(0, 0, (ms, alphas, mxs), do_pv=False, first=True)

    def body(t, carry):
        return stage(2 * t + 2, 0, stage(2 * t + 1, 1, carry))

    carry = lax.fori_loop(0, (n - 2) // 2, body, carry)
    _, alphas, _ = stage(n - 1, 1, carry, do_qk=False)
    for h in range(nh):
        acc_ref[h] = alphas[h] * acc_ref[h] + pv_fn(n - 1, h, p_buf[1, h])


def _chunk_off(j, tk):
    return j * tk if isinstance(j, int) else pl.multiple_of(j * tk, tk)


def _attn_a_body(cs_ref, ctab_ref, lam_ref, g_ref, q_ref, k_ref, kpos_ref, vt_ref, o_ref,
                 s_buf, p_buf, acc_ref, *, seq, tq, tk, hps, lam_init):
    hg = pl.program_id(1)
    qi = pl.program_id(2)
    n = seq // tk
    cd = (qi * tq) // tk
    lane = lax.broadcasted_iota(jnp.int32, (tq, LANES), 1)
    qpos = (qi * tq + lax.broadcasted_iota(jnp.int32, (1, tq), 1)).astype(F32)
    dv = vt_ref.shape[2] - SUM_ROWS
    c_slope, f0, q_left, q_right = [], [], [], []
    for hh in range(hps):
        head = hg * hps + hh
        c_slope.append(cs_ref[head])
        f0.append(c_slope[hh] * qpos)
        q = q_ref[0, :, hh * LANES:(hh + 1) * LANES]
        qaug = jnp.broadcast_to(ctab_ref[pl.ds(head, 1), :], (tq, LANES)).astype(BF)
        for half in range(2):
            q_m = jnp.where((lane >= 64 * half) & (lane < 64 * (half + 1)), q, jnp.zeros_like(q))
            q_left.append(jnp.concatenate([q_m, qaug], axis=1))
            q_right.append(jnp.concatenate([q_m, -qaug], axis=1))

    def chunk_of(j):
        jm = j - 1
        c = jnp.where(j == 0, cd, jm + (jm >= cd).astype(jnp.int32))
        return c, c <= cd

    def qk_fn(j, mm):
        hh = mm // 2
        c, left = chunk_of(j)
        off = pl.multiple_of(c * tk, tk)
        k_c = jnp.concatenate([k_ref[0, pl.ds(off, tk), hh * LANES:(hh + 1) * LANES],
                               kpos_ref[pl.ds(off, tk), :]], axis=1)
        q_full = jnp.where(left, q_left[mm], q_right[mm])
        return lax.dot_general(k_c, q_full, NT_DIMS, preferred_element_type=F32)

    def sm_fn(j, mm, s, m, mx, first):
        hh = mm // 2
        if first:
            d = (lax.broadcasted_iota(jnp.int32, (tk, tq), 0) - lax.broadcasted_iota(jnp.int32, (tk, tq), 1)
                 + (cd * tk - qi * tq)).astype(F32)
            return _softmax_chunk(s - (2.0 * c_slope[hh]) * jnp.maximum(d, 0.0), m, f0[hh])
        _, left = chunk_of(j)
        return _softmax_chunk(s, m, jnp.where(left, f0[hh], -f0[hh]), mx)

    def pv_fn(j, mm, p):
        c, _ = chunk_of(j)
        off = pl.multiple_of(c * tk, tk)
        return jnp.dot(vt_ref[0, mm // 2, :, pl.ds(off, tk)], p, preferred_element_type=F32)

    _pipelined_flash(n, 2 * hps, tq, qk_fn, sm_fn, pv_fn, s_buf, p_buf, acc_ref)
    lf = lam_ref[...]
    lam_full = (jnp.exp(jnp.sum(lf[0:1] * lf[1:2], axis=-1, keepdims=True))
                - jnp.exp(jnp.sum(lf[2:3] * lf[3:4], axis=-1, keepdims=True)) + lam_init)
    outs = []
    for hh in range(hps):
        o0, o1 = [acc_ref[2 * hh + half, :dv, :] / acc_ref[2 * hh + half, dv:dv + 1, :] for half in range(2)]
        o = o0 - lam_full * o1
        ms = jnp.mean(o * o, axis=0, keepdims=True)
        outs.append(o * lax.rsqrt(ms + RMS_EPS) * g_ref[...] * (1.0 - lam_init))
    o_ref[0] = jnp.concatenate(outs, axis=0).T.astype(o_ref.dtype)


def _bf16_pieces(c, n=3):
    pieces, rest = [], c.astype(F32)
    for _ in range(n):
        p = rest.astype(BF)
        pieces.append(p.astype(F32))
        rest = rest - p.astype(F32)
    return pieces


def _store_vt_heads(vt_ref, vt, dv, heads_per_slot):
    tm = vt.shape[1]
    tail = jnp.where(lax.broadcasted_iota(jnp.int32, (SUM_ROWS, tm), 0) == 0, 1.0, 0.0).astype(vt_ref.dtype)
    rows = dv + SUM_ROWS
    for h in range(vt.shape[0] // dv):
        slot, r0 = h // heads_per_slot, (h % heads_per_slot) * rows
        vt_ref[0, slot, r0:r0 + dv, :] = vt[h * dv:(h + 1) * dv, :].astype(vt_ref.dtype)
        vt_ref[0, slot, r0 + dv:r0 + rows, :] = tail


def _proj_a_body(x_ref, wqk_ref, wvt_ref, qk_ref, vt_ref):
    xb = x_ref[...].astype(BF)
    qk_ref[...] = jnp.dot(xb, wqk_ref[...], preferred_element_type=F32).astype(BF)
    vt = lax.dot_general(wvt_ref[...], xb, NT_DIMS, preferred_element_type=F32)
    _store_vt_heads(vt_ref, vt, 2 * A_HEAD_DIM, 1)


def _flash_chunk_len(seq, tk_max):
    for min_chunks in (8, 4):
        tk = tk_max
        while tk >= 2 * LANES and (seq % tk or (seq // tk) < min_chunks or (seq // tk) % 2):
            tk //= 2
        if tk >= 2 * LANES:
            return tk
    raise ValueError(f"sequence length {seq} too short for the pipelined attention kernel")


def _flash_scratch(nh, rows, tq, tk):
    return [pltpu.VMEM((2, nh, tk, tq), F32), pltpu.VMEM((2, nh, tk, tq), BF), pltpu.VMEM((nh, rows, tq), F32)]


def _attn_a(qkv, vt, lam, subln_g, lam_init, tq=256, tk=1024):
    bsz, seq, _ = qkv.shape
    tq, tk = min(tq, seq), _flash_chunk_len(seq, tk)
    c_slope = jnp.asarray(2.0 ** (-8.0 * np.arange(1, A_HEADS + 1) / A_HEADS) * LOG2E, dtype=F32)
    c1, c2, c3 = _bf16_pieces(c_slope)
    ctab = jnp.stack([c1, c1, c2, c2, c3, c3], axis=-1)
    ctab = jnp.concatenate([ctab, jnp.zeros((A_HEADS, LANES - 6), F32)], axis=-1)
    pos = np.arange(seq)
    kpos = np.zeros((seq, LANES), np.float32)
    for j in range(3):
        kpos[:, 2 * j] = (pos // LANES) * LANES
        kpos[:, 2 * j + 1] = pos % LANES
    kpos = jnp.asarray(kpos, BF)
    vrows = vt.shape[2]
    hps = 2 if seq <= 4096 else 1
    ngrp = A_HEADS // hps
    body = functools.partial(_attn_a_body, seq=seq, tq=tq, tk=tk, hps=hps, lam_init=lam_init)
    return pl.pallas_call(
        body,
        grid=(bsz, ngrp, seq // tq),
        in_specs=[
            pl.BlockSpec(memory_space=pltpu.SMEM),
            pl.BlockSpec((A_HEADS, LANES), lambda b, h, i: (0, 0)),
            pl.BlockSpec((4, A_HEAD_DIM), lambda b, h, i: (0, 0)),
            pl.BlockSpec((2 * A_HEAD_DIM, 1), lambda b, h, i: (0, 0)),
            pl.BlockSpec((1, tq, hps * LANES), lambda b, h, i: (b, i, h)),
            pl.BlockSpec((1, seq, hps * LANES), lambda b, h, i: (b, 0, ngrp + h)),
            pl.BlockSpec((seq, LANES), lambda b, h, i: (0, 0)),
            pl.BlockSpec((1, hps, vrows, seq), lambda b, h, i: (b, h, 0, 0)),
        ],
        out_specs=pl.BlockSpec((1, tq, hps * LANES), lambda b, h, i: (b, i, h)),
        out_shape=jax.ShapeDtypeStruct((bsz, seq, A_HEADS * LANES), BF),
        scratch_shapes=_flash_scratch(2 * hps, vrows, tq, tk),
        compiler_params=_cparams("parallel", "parallel", "arbitrary"),
        name="attn_a",
    )(c_slope, ctab, lam.astype(F32), subln_g.astype(F32).reshape(2 * A_HEAD_DIM, 1), qkv, qkv, kpos, vt)


def _attn_b_body(q_ref, k_ref, vt_ref, o_ref, s_buf, p_buf, acc_ref, *, seq, tk, nh):
    tq = q_ref.shape[1]
    rows = B_VDIM + SUM_ROWS
    qs = [q_ref[0, :, h * LANES:(h + 1) * LANES] for h in range(nh)]

    def qk_fn(j, h):
        k_c = k_ref[0, pl.ds(_chunk_off(j, tk), tk), h * LANES:(h + 1) * LANES]
        return lax.dot_general(k_c, qs[h], NT_DIMS, preferred_element_type=F32)

    def sm_fn(j, h, s, m, mx, first):
        return _softmax_chunk(s, m, None, mx)

    def pv_fn(j, h, p):
        vt_c = vt_ref[0, h // 2, (h % 2) * rows:(h % 2 + 1) * rows, pl.ds(_chunk_off(j, tk), tk)]
        return jnp.dot(vt_c, p, preferred_element_type=F32)

    _pipelined_flash(seq // tk, nh, tq, qk_fn, sm_fn, pv_fn, s_buf, p_buf, acc_ref)
    outs = [acc_ref[h, :B_VDIM, :] / acc_ref[h, B_VDIM:B_VDIM + 1, :] for h in range(nh)]
    o = jnp.concatenate(outs, axis=0)
    o_ref[0] = o.T.astype(o_ref.dtype)


def _attn_b(q, k, vt, tq=256, tk=1024):
    bsz, seq, _ = q.shape
    tq, tk = min(tq, seq), _flash_chunk_len(seq, tk)
    nh = 4 if seq <= 4096 else 2
    return pl.pallas_call(
        functools.partial(_attn_b_body, seq=seq, tk=tk, nh=nh),
        grid=(bsz, B_HEADS // nh, seq // tq),
        in_specs=[
            pl.BlockSpec((1, tq, nh * LANES), lambda b, h, i: (b, i, h)),
            pl.BlockSpec((1, seq, nh * LANES), lambda b, h, i: (b, 0, h)),
            pl.BlockSpec((1, nh // 2, 2 * (B_VDIM + SUM_ROWS), seq), lambda b, h, i: (b, h, 0, 0)),
        ],
        out_specs=pl.BlockSpec((1, tq, nh * B_VDIM), lambda b, h, i: (b, i, h)),
        out_shape=jax.ShapeDtypeStruct((bsz, seq, B_HEADS * B_VDIM), BF),
        scratch_shapes=_flash_scratch(nh, B_VDIM + SUM_ROWS, tq, tk),
        compiler_params=_cparams("parallel", "parallel", "arbitrary"),
        name="attn_b",
    )(q, k, vt)


def _mla_in_body(x_ref, w_ref, gq_ref, gkv_ref, rc_ref, rs_ref, cq_ref, ckv_ref, kr_ref):
    h = jnp.dot(x_ref[...].astype(BF), w_ref[...], preferred_element_type=F32)
    cq_ref[...] = _rms_rows(h[:, :B_Q_LORA], gq_ref[...]).astype(BF)
    ckv_ref[...] = _rms_rows(h[:, B_Q_LORA:B_Q_LORA + B_KV_LORA], gkv_ref[...]).astype(BF)
    t = h[:, B_Q_LORA + B_KV_LORA:]
    kr = t * rc_ref[...] + pltpu.roll(t, LANES - B_ROPE, 1) * rs_ref[...]
    kr_ref[...] = kr.astype(BF)


def _mla_q_body(cq_ref, w_ref, ta_ref, tb_ref, q_ref):
    t = jnp.dot(cq_ref[...], w_ref[...], preferred_element_type=F32)
    ta = ta_ref[...]
    tb = tb_ref[...]
    for h in range(B_HEADS):
        th = t[:, h * LANES:(h + 1) * LANES]
        q_ref[:, h * LANES:(h + 1) * LANES] = (th * ta + pltpu.roll(th, LANES - B_ROPE, 1) * tb).astype(BF)


def _mla_kv_body(ckv_ref, kr_ref, wk_ref, e_ref, wvt_ref, k_ref, vt_ref):
    ckv = ckv_ref[...]
    k = jnp.dot(ckv, wk_ref[...], preferred_element_type=F32)
    k = k + jnp.dot(kr_ref[...], e_ref[...], preferred_element_type=F32)
    k_ref[...] = k.astype(BF)
    vt = lax.dot_general(wvt_ref[...], ckv, NT_DIMS, preferred_element_type=F32)
    _store_vt_heads(vt_ref, vt, B_VDIM, 2)


def _rope_partner(w):
    half = B_ROPE // 2
    return jnp.concatenate([-w[..., half:], w[..., :half]], axis=-1)


def _mla_tables(seq):
    inv = 1.0 / (ROPE_THETA ** (np.arange(0, B_ROPE, 2, dtype=np.float32) / B_ROPE))
    ang = jnp.arange(seq, dtype=F32)[:, None] * jnp.asarray(inv, F32)[None, :]
    cos = jnp.concatenate([jnp.cos(ang), jnp.cos(ang)], axis=-1)
    sin = jnp.concatenate([jnp.sin(ang), jnp.sin(ang)], axis=-1)
    z = lambda n: jnp.zeros((seq, n), F32)
    rc = jnp.concatenate([cos, z(LANES - B_ROPE)], axis=-1)
    rs = jnp.concatenate([sin, z(LANES - B_ROPE)], axis=-1)
    scale = (B_NOPE + B_ROPE) ** -0.5 * LOG2E
    ta =jnp.concatenate([jnp.full((seq, B_NOPE), scale, F32), cos * scale, z(B_ROPE)], axis=-1)
    tb = jnp.concatenate([z(B_NOPE), sin * scale, z(B_ROPE)], axis=-1)
    return rc, rs, ta, tb


def _mla_weights(w_in, w_uq, w_ukv):
    d = w_in.shape[0]
    kr0 = B_Q_LORA + B_KV_LORA
    w_in_ext = jnp.concatenate(
        [w_in, _rope_partner(w_in[:, kr0:kr0 + B_ROPE]), jnp.zeros((d, LANES - 2 * B_ROPE), w_in.dtype)], axis=-1)
    wq = w_uq.reshape(B_Q_LORA, B_HEADS, B_NOPE + B_ROPE)
    wq_ext = jnp.concatenate([wq, _rope_partner(wq[..., B_NOPE:])], axis=-1).reshape(B_Q_LORA, B_HEADS * LANES)
    wkv = w_ukv.reshape(B_KV_LORA, B_HEADS, B_NOPE + B_VDIM)
    wk = jnp.concatenate([wkv[..., :B_NOPE], jnp.zeros((B_KV_LORA, B_HEADS, LANES - B_NOPE), w_ukv.dtype)], axis=-1)
    wk = wk.reshape(B_KV_LORA, B_HEADS * LANES)
    wv = wkv[..., B_NOPE:].reshape(B_KV_LORA, B_HEADS * B_VDIM)
    place = np.zeros((LANES, B_HEADS, LANES), np.float32)
    for j in range(B_ROPE):
        place[j, :, B_NOPE + j] = 1.0
    place = jnp.asarray(place.reshape(LANES, B_HEADS * LANES), BF)
    return w_in_ext.astype(BF), wq_ext.astype(BF), wk.astype(BF), place, wv.astype(BF)


def _mixer_b(x, w_in, q_norm_g, kv_norm_g, w_uq, w_ukv, tm=512):
    bsz, seq, d = x.shape
    t = bsz * seq
    tm = min(tm, seq)
    nblk = seq // tm
    w_in_ext, wq_ext, wk, place, wv = _mla_weights(w_in, w_uq, w_ukv)
    rc, rs, ta, tb = _mla_tables(seq)
    x2 = x.reshape(t, d)
    n_in = w_in_ext.shape[1]
    row = lambda i: (i, 0)
    fixed = lambda i: (0, 0)
    pos = lambda i: (i % nblk, 0)
    cq, ckv, kr = pl.pallas_call(
        _mla_in_body,
        grid=(t // tm,),
        in_specs=[
            pl.BlockSpec((tm, d), row), pl.BlockSpec((d, n_in), fixed),
            pl.BlockSpec((1, B_Q_LORA), fixed), pl.BlockSpec((1, B_KV_LORA), fixed),
            pl.BlockSpec((tm, LANES), pos), pl.BlockSpec((tm, LANES), pos),
        ],
        out_specs=[pl.BlockSpec((tm, B_Q_LORA), row), pl.BlockSpec((tm, B_KV_LORA), row),
                   pl.BlockSpec((tm, LANES), row)],
        out_shape=[jax.ShapeDtypeStruct((t, B_Q_LORA), BF), jax.ShapeDtypeStruct((t, B_KV_LORA), BF),
                   jax.ShapeDtypeStruct((t, LANES), BF)],
        compiler_params=_cparams("parallel"),
        name="mla_in",
    )(x2, w_in_ext, q_norm_g.astype(F32).reshape(1, -1), kv_norm_g.astype(F32).reshape(1, -1), rc, rs)
    nq = B_HEADS * LANES
    q = pl.pallas_call(
        _mla_q_body,
        grid=(t // tm,),
        in_specs=[pl.BlockSpec((tm, B_Q_LORA), row), pl.BlockSpec((B_Q_LORA, nq), fixed),
                  pl.BlockSpec((tm, LANES), pos), pl.BlockSpec((tm, LANES), pos)],
        out_specs=pl.BlockSpec((tm, nq), row),
        out_shape=jax.ShapeDtypeStruct((t, nq), BF),
        compiler_params=_cparams("parallel"),
        name="mla_q",
    )(cq, wq_ext, ta, tb)
    nv = B_HEADS * B_VDIM
    npair = B_HEADS // 2
    vrows = 2 * (B_VDIM + SUM_ROWS)
    k, vt = pl.pallas_call(
        _mla_kv_body,
        grid=(t // tm,),
        in_specs=[pl.BlockSpec((tm, B_KV_LORA), row), pl.BlockSpec((tm, LANES), row),
                  pl.BlockSpec((B_KV_LORA, nq), fixed), pl.BlockSpec((LANES, nq), fixed),
                  pl.BlockSpec((nv, B_KV_LORA), fixed)],
        out_specs=[pl.BlockSpec((tm, nq), row),
                   pl.BlockSpec((1, npair, vrows, tm), lambda i: (i // nblk, 0, 0, i % nblk))],
        out_shape=[jax.ShapeDtypeStruct((t, nq), BF), jax.ShapeDtypeStruct((bsz, npair, vrows, seq), BF)],
        compiler_params=_cparams("parallel"),
        name="mla_kv",
    )(ckv, kr, wk, place, wv.T)
    o = _attn_b(q.reshape(bsz, seq, nq), k.reshape(bsz, seq, nq), vt)
    return o.reshape(t, nv)


def _mixer_a(x, w_in, lam, subln_g, layer_idx):
    bsz, seq, d = x.shape
    t = bsz * seq
    hd2 = 2 * A_HEAD_DIM
    nq = A_HEADS * hd2
    scale = A_HEAD_DIM ** -0.5 * LOG2E
    wqk = jnp.concatenate([w_in[:, :nq] * scale, w_in[:, nq:2 * nq]], axis=-1).astype(BF)
    wvt = w_in[:, 2 * nq:].T.astype(BF)
    tm = min(512, seq)
    nblk = seq // tm
    vrows = hd2 + SUM_ROWS
    qk, vt = pl.pallas_call(
        _proj_a_body,
        grid=(t // tm,),
        in_specs=[pl.BlockSpec((tm, d), lambda i: (i, 0)), pl.BlockSpec((d, 2 * nq), lambda i: (0, 0)),
                  pl.BlockSpec((nq, d), lambda i: (0, 0))],
        out_specs=[pl.BlockSpec((tm, 2 * nq), lambda i: (i, 0)),
                   pl.BlockSpec((1, A_HEADS, vrows, tm), lambda i: (i // nblk, 0, 0, i % nblk))],
        out_shape=[jax.ShapeDtypeStruct((t, 2 * nq), BF), jax.ShapeDtypeStruct((bsz, A_HEADS, vrows, seq), BF)],
        compiler_params=_cparams("parallel"),
        name="proj_a",
    )(x.reshape(t, d), wqk, wvt)
    lam_init = 0.8 - 0.6 * math.exp(-0.3 * layer_idx)
    o = _attn_a(qk.reshape(bsz, seq, 2 * nq), vt, lam, subln_g, lam_init)
    return o.reshape(t, nq)


def _attn_c_body(q_ref, k0, k1, k2, k3, v0, v1, v2, v3, bt_ref, o_ref):
    kw = jnp.concatenate([k0[0], k1[0], k2[0], k3[0]], axis=0)
    vw = jnp.concatenate([v0[0], v1[0], v2[0], v3[0]], axis=0)
    nq = q_ref.shape[1] // 2
    nk = bt_ref.shape[3]
    lane = lax.broadcasted_iota(jnp.int32, (nq, LANES), 1)
    for hf in range(2):
        q = q_ref[0, hf * nq:(hf + 1) * nq, :]
        kh = kw[hf * nq:hf * nq + nk]
        vh = vw[hf * nq:hf * nq + nk]
        outs = []
        for hh in range(2):
            qm = jnp.where((lane >= 64 * hh) & (lane < 64 * (hh + 1)), q, jnp.zeros_like(q))
            s = (lax.dot_general(qm, kh, NT_DIMS, preferred_element_type=F32)
                 + bt_ref[hh, 0, hf * nq:(hf + 1) * nq, :])
            m = jnp.max(s, axis=1, keepdims=True)
            p = jnp.exp2(s - m)
            l = jnp.sum(p, axis=1, keepdims=True)
            outs.append(jnp.dot(p.astype(BF), vh, preferred_element_type=F32) / l)
        o_ref[0, hf * nq:(hf + 1) * nq, :] = jnp.where(lane < 64, outs[0], outs[1]).astype(o_ref.dtype)


def _nbr_bias_table(rpb):
    col = np.arange(GRID_W)
    col_start = np.clip(col - C_WIN_COLS // 2, 0, GRID_W - C_WIN_COLS)
    col_mask = (col[None, :] >= col_start[:, None]) & (col[None, :] < col_start[:, None] + C_WIN_COLS)
    pad = GRID_W - C_WIN_COLS
    ext = jnp.pad(rpb.astype(F32) * LOG2E, ((0, 0), (0, 0), (pad, pad)), mode="edge")
    toep = jnp.stack([ext[:, :, GRID_W - 1 - qc:2 * GRID_W - 1 - qc] for qc in range(GRID_W)], axis=2)
    toep = jnp.where(jnp.asarray(col_mask)[None, None], toep, NEG)
    neg = jnp.full((rpb.shape[0], GRID_W, GRID_W), NEG, F32)
    half = C_WIN_ROWS // 2
    kinds = []
    for kind in range(3):
        qrows = []
        for t in range(C_WIN_ROWS):
            u0 = (max(t, half), t, min(t, half))[kind]
            ubase = 0 if t < half else half
            blocks = [toep[:, u - t + half - 1] if u0 <= u < u0 + C_WIN_ROWS else neg
                      for u in range(ubase, ubase + C_WIN_ROWS + half)]
            qrows.append(jnp.concatenate(blocks, axis=-1))
        kinds.append(jnp.concatenate(qrows, axis=1))
    return jnp.stack(kinds, axis=1)


def _mixer_c(x, w_qkv, rpb):
    bsz, seq, d = x.shape
    t = bsz * seq
    nq = C_HEADS * C_HEAD_DIM
    scale = C_HEAD_DIM ** -0.5 * LOG2E
    w = jnp.concatenate([w_qkv[:, :nq] * scale, w_qkv[:, nq:]], axis=-1).astype(BF)
    qkv = _linear(x.reshape(t, d), w, BF, tn=nq).reshape(bsz, seq, 3 * nq)
    bt = _nbr_bias_table(rpb)
    npair = C_HEADS // 2
    qtok = C_WIN_ROWS * GRID_W
    ngrp = seq // qtok
    assert ngrp >= 2
    kb = qtok // 2
    nkb = seq // kb
    kspecs = []
    for off in (npair, 2 * npair):
        for j in range(4):
            kspecs.append(pl.BlockSpec(
                (1, kb, LANES),
                lambda h, g, b, j=j, off=off: (b, jnp.clip(2 * g - 1 + j, 0, nkb - 1), off + h)))
    kind = lambda g: jnp.where(g == 0, 0, jnp.where(g == ngrp - 1, 2, 1))
    o = pl.pallas_call(
        _attn_c_body,
        grid=(npair, ngrp, bsz),
        in_specs=[pl.BlockSpec((1, qtok, LANES), lambda h, g, b: (b, g, h))] + kspecs + [
            pl.BlockSpec((2, 1, qtok, bt.shape[3]), lambda h, g, b: (h, kind(g), 0, 0))],
        out_specs=pl.BlockSpec((1, qtok, LANES), lambda h, g, b: (b, g, h)),
        out_shape=jax.ShapeDtypeStruct((bsz, seq, nq), BF),
        compiler_params=_cparams("parallel", "parallel", "arbitrary"),
        name="attn_c",
    )(qkv, *([qkv] * 8), bt)
    return o.reshape(t, nq)


def _attn_d_body(slopes_ref, q_ref, kp, kc, kn, vp, vc, vn, o_ref, lse_ref, *, tq, length, dil, rad):
    i = pl.program_id(2)
    nk = tq + 2 * rad
    qi = lax.broadcasted_iota(jnp.int32, (tq, nk), 0)
    kk = lax.broadcasted_iota(jnp.int32, (tq, nk), 1)
    dist = jnp.abs(kk - rad - qi)
    ki = i * tq - rad + kk
    valid = (dist <= rad) & (ki >= 0) & (ki < length)
    distf = dist.astype(F32) * float(dil)
    lane = lax.broadcasted_iota(jnp.int32, (tq, LANES), 1)
    for hp in range(D_HEADS // 2):
        sl = slice(hp * LANES, (hp + 1) * LANES)
        q = q_ref[0, 0, :, sl]
        kw = jnp.concatenate([kp[0, 0, tq - rad:tq, sl], kc[0, 0, :, sl], kn[0, 0, 0:rad, sl]], axis=0)
        vw = jnp.concatenate([vp[0, 0, tq - rad:tq, sl], vc[0, 0, :, sl], vn[0, 0, 0:rad, sl]], axis=0)
        outs, lses = [], []
        for hh in range(2):
            slope = slopes_ref[hp * 2 + hh]
            qm = jnp.where((lane >= 64 * hh) & (lane < 64 * (hh + 1)), q, jnp.zeros_like(q))
            s = lax.dot_general(qm, kw, NT_DIMS, preferred_element_type=F32)
            s = jnp.where(valid, s - slope * distf, NEG)
            m = jnp.max(s, axis=1, keepdims=True)
            p = jnp.exp2(s - m)
            l = jnp.sum(p, axis=1, keepdims=True)
            outs.append(jnp.dot(p.astype(BF), vw, preferred_element_type=F32) / l)
            lses.append(m + jnp.log(l) * LOG2E)
        o_ref[0, 0, :, sl] = jnp.where(lane < 64, outs[0], outs[1]).astype(o_ref.dtype)
        lse_ref[0, 0, :, sl] = jnp.where(lane < 64, lses[0], lses[1])


def _attn_d_group(qkv, window, dil, tq=256):
    bsz, _, length, ncol = qkv.shape
    rad = window // (2 * dil)
    tq = min(tq, length)
    nq = length // tq
    nh = ncol // 3
    slopes = jnp.asarray(2.0 ** (-8.0 * np.arange(1, D_HEADS + 1) / D_HEADS) * LOG2E, dtype=F32)

    def spec(which, shift):
        return pl.BlockSpec((1, 1, tq, nh), lambda b, c, i: (b, c, jnp.clip(i + shift, 0, nq - 1), which))

    ospec = pl.BlockSpec((1, 1, tq, nh), lambda b, c, i: (b, c, i, 0))
    return pl.pallas_call(
        functools.partial(_attn_d_body, tq=tq, length=length, dil=dil, rad=rad),
        grid=(bsz, dil, nq),
        in_specs=[pl.BlockSpec(memory_space=pltpu.SMEM), spec(0, 0),
                  spec(1, -1), spec(1, 0), spec(1, 1), spec(2, -1), spec(2, 0), spec(2, 1)],
        out_specs=[ospec, ospec],
        out_shape=[jax.ShapeDtypeStruct((bsz, dil, length, nh), BF),
                   jax.ShapeDtypeStruct((bsz, dil, length, nh), F32)],
        compiler_params=_cparams("parallel", "parallel", "arbitrary"),
        name=f"attn_d{dil}",
    )(slopes, qkv, qkv, qkv, qkv, qkv, qkv, qkv)


def _proj_d_body(x_ref, w_ref, o0, o1, o2, scr, *, dils):
    xb = x_ref[...].astype(BF)
    tm = xb.shape[0]
    nblk = scr.shape[0]
    ncol = nblk * LANES
    for gi, (o_ref, dil) in enumerate(zip((o0, o1, o2), dils)):
        res = jnp.dot(xb, w_ref[:, gi * ncol:(gi + 1) * ncol], preferred_element_type=F32)
        if dil == 1:
            o_ref[0, 0] = res.astype(BF)
            continue
        for j in range(nblk):
            scr[j] = res[:, j * LANES:(j + 1) * LANES]
        for c in range(dil):
            for j in range(nblk):
                o_ref[0, c, :, j * LANES:(j + 1) * LANES] = scr[j, pl.ds(c, tm // dil, stride=dil), :].astype(BF)


def _post_d_body(o0, o1, o2, l0, l1, l2, x_ref, w_ref, g_ref, b_ref, y_ref, scr_o, scr_l, *, alpha, dils):
    tm = x_ref.shape[0]

    def interleaved(ref, scr, dil):
        if dil == 1:
            return ref[0, 0].astype(F32)
        nblk = scr.shape[0]
        for c in range(dil):
            for j in range(nblk):
                scr[j, pl.ds(c, tm // dil, stride=dil), :] = ref[0, c, :, j * LANES:(j + 1) * LANES].astype(F32)
        return jnp.concatenate([scr[j] for j in range(nblk)], axis=1)

    ov = [interleaved(r, scr_o.at[j], dil) for j, (r, dil) in enumerate(zip((o0, o1, o2), dils))]
    a0, a1, a2 = [interleaved(r, scr_l.at[j], dil) for j, (r, dil) in enumerate(zip((l0, l1, l2), dils))]
    m = jnp.maximum(jnp.maximum(a0, a1), a2)
    e0, e1, e2 = jnp.exp2(a0 - m), jnp.exp2(a1 - m), jnp.exp2(a2 - m)
    den = e0 + e1 + e2
    o = (e0 / den) * ov[0] + (e1 / den) * ov[1] + (e2 / den) * ov[2]
    h = jnp.dot(o.astype(BF), w_ref[...], preferred_element_type=F32)
    y_ref[...] = _layer_norm_rows(alpha * x_ref[...] + h, g_ref[...], b_ref[...])


def _mixer_d_and_post(x2, bsz, seq, w_qkv, w_out, g, b, alpha, tm=512):
    t, d = x2.shape
    nh = D_HEADS * D_HEAD_DIM
    ng = len(D_GROUPS)
    dils = tuple(dil for _, dil in D_GROUPS)
    tm = min(tm, seq)
    nblk = seq // tm
    scale = D_HEAD_DIM ** -0.5 * LOG2E
    wq = w_qkv.reshape(d, 3, ng, nh)
    wq = jnp.stack([wq[:, 0] * scale, wq[:, 1], wq[:, 2]], axis=1)
    w = wq.transpose(0, 2, 1, 3).reshape(d, ng * 3 * nh).astype(BF)

    def deint_spec(dil, width):
        return pl.BlockSpec((1, dil, tm // dil, width), lambda i: (i // nblk, 0, i % nblk, 0))

    qkvs = pl.pallas_call(
        functools.partial(_proj_d_body, dils=dils),
        grid=(t // tm,),
        in_specs=[pl.BlockSpec((tm, d), lambda i: (i, 0)), pl.BlockSpec((d, ng * 3 * nh), lambda i: (0, 0))],
        out_specs=[deint_spec(dil, 3 * nh) for dil in dils],
        out_shape=[jax.ShapeDtypeStruct((bsz, dil, seq // dil, 3 * nh), BF) for dil in dils],
        scratch_shapes=[pltpu.VMEM((3 * nh // LANES, tm, LANES), F32)],
        compiler_params=_cparams("parallel"),
        name="proj_d",
    )(x2, w)
    os_, ls_ = [], []
    for qkv_g, (window, dil) in zip(qkvs, D_GROUPS):
        o, lse = _attn_d_group(qkv_g, window, dil)
        os_.append(o)
        ls_.append(lse)
    row = lambda i: (i, 0)
    fixed = lambda i: (0, 0)
    return pl.pallas_call(
        functools.partial(_post_d_body, alpha=alpha, dils=dils),
        grid=(t // tm,),
        in_specs=[deint_spec(dil, nh) for dil in dils] * 2 + [
            pl.BlockSpec((tm, d), row), pl.BlockSpec((nh, d), fixed),
            pl.BlockSpec((1, d), fixed), pl.BlockSpec((1, d), fixed)],
        out_specs=pl.BlockSpec((tm, d), row),
        out_shape=jax.ShapeDtypeStruct((t, d), F32),
        scratch_shapes=[pltpu.VMEM((ng, nh // LANES, tm, LANES), F32), pltpu.VMEM((ng, nh // LANES, tm, LANES), F32)],
        compiler_params=_cparams("parallel"),
        name="post_d",
    )(*os_, *ls_, x2, w_out.astype(BF), g.reshape(1, d), b.reshape(1, d))


def _router_gates(x, w2_ref, rb_ref):
    tm = x.shape[0]
    xh = x.astype(BF)
    xl = (x - xh.astype(F32)).astype(BF)
    w2 = w2_ref[...]
    both = lax.dot_general(w2, xh, NT_DIMS, preferred_element_type=F32)
    logits = (both[:N_EXPERTS] + both[N_EXPERTS:]
              + lax.dot_general(w2[:N_EXPERTS], xl, NT_DIMS, preferred_element_type=F32))
    scores = jax.nn.sigmoid(logits)
    biased = scores + rb_ref[...]
    epg = N_EXPERTS // N_EXPERT_GROUPS
    sc = [scores[e:e + 1, :] for e in range(N_EXPERTS)]
    bi = [biased[e:e + 1, :] for e in range(N_EXPERTS)]
    gs = []
    for g in range(N_EXPERT_GROUPS):
        v = bi[g * epg:(g + 1) * epg]
        best = None
        for a in range(epg):
            for c in range(a + 1, epg):
                pair = v[a] + v[c]
                best = pair if best is None else jnp.maximum(best, pair)
        gs.append(best)
    gmax = functools.reduce(jnp.maximum, gs)
    taken = jnp.zeros((1, tm), jnp.bool_)
    cand = []
    for g in range(N_EXPERT_GROUPS):
        sel = (gs[g] == gmax) & jnp.logical_not(taken)
        taken = taken | sel
        for a in range(epg):
            cand.append(jnp.where(sel, bi[g * epg + a], -jnp.inf))
    m1 = functools.reduce(jnp.maximum, cand)
    taken = jnp.zeros((1, tm), jnp.bool_)
    is1 = []
    for e in range(N_EXPERTS):
        hit = (cand[e] == m1) & jnp.logical_not(taken)
        taken = taken | hit
        is1.append(hit)
    cand2 = [jnp.where(is1[e], -jnp.inf, cand[e]) for e in range(N_EXPERTS)]
    m2 = functools.reduce(jnp.maximum, cand2)
    taken = jnp.zeros((1, tm), jnp.bool_)
    is2 = []
    for e in range(N_EXPERTS):
        hit = (cand2[e] == m2) & jnp.logical_not(taken)
        taken = taken | hit
        is2.append(hit)
    zero = jnp.zeros((1, tm), F32)
    w1 = functools.reduce(jnp.add, [jnp.where(is1[e], sc[e], zero) for e in range(N_EXPERTS)])
    w2 = functools.reduce(jnp.add, [jnp.where(is2[e], sc[e], zero) for e in range(N_EXPERTS)])
    den = w1 + w2
    rows = [jnp.where(is1[e], w1 / den, zero) + jnp.where(is2[e], w2 / den, zero) for e in range(N_EXPERTS)]
    rows.append(jnp.ones((1, tm), F32))
    rows.append(jnp.zeros((LANES - N_EXPERTS - 1, tm), F32))
    return jnp.concatenate(rows, axis=0).T


MOE_EXPERTS_PER_STEP = 2


def _moe_body(x_ref, p_ref, w2_ref, rb_ref, wgu_ref, wd_ref, wgus_ref, wds_ref, g_ref, b_ref,
              wpi_ref, wpg_ref, y_ref, gates, xb, acc, *, alpha, n_routed_steps):
    e = pl.program_id(1)
    ff = EXPERT_FF

    def hidden(gu, j):
        return jax.nn.silu(gu[:, 2 * j * ff:(2 * j + 1) * ff]) * gu[:, (2 * j + 1) * ff:(2 * j + 2) * ff]

    @pl.when(e == 0)
    def _():
        x = x_ref[...]
        gates[...] = _router_gates(x, w2_ref, rb_ref)
        xb[...] = x.astype(BF)
        acc[...] = jnp.zeros_like(acc)

    @pl.when(e < n_routed_steps)
    def _():
        gu = jnp.dot(xb[...], wgu_ref[0], preferred_element_type=F32)
        lane = lax.broadcasted_iota(jnp.int32, gates.shape, 1)
        gt = gates[...]
        hs = []
        for j in range(MOE_EXPERTS_PER_STEP):
            gcol = jnp.sum(jnp.where(lane == e * MOE_EXPERTS_PER_STEP + j, gt, 0.0), axis=1, keepdims=True)
            hs.append((hidden(gu, j) * gcol).astype(BF))
        acc[...] += jnp.dot(jnp.concatenate(hs, axis=1), wd_ref[0], preferred_element_type=F32)

    @pl.when(e == n_routed_steps)
    def _():
        gu = jnp.dot(xb[...], wgus_ref[...], preferred_element_type=F32)
        y = acc[...] + jnp.dot(hidden(gu, 0).astype(BF), wds_ref[...], preferred_element_type=F32)
        x2 = _layer_norm_rows(alpha * x_ref[...] + y, g_ref[...], b_ref[...])
        emb = jnp.dot(p_ref[...].astype(BF), wpi_ref[...], preferred_element_type=F32)
        gate = jax.nn.sigmoid(jnp.dot(x2.astype(BF), wpg_ref[...], preferred_element_type=F32))
        y_ref[...] = x2 + gate * emb


def _moe_layer(x2, p2, router, lw, g, b, alpha, tm=1024):
    t, d = x2.shape
    wgu, wd, wgus, wds, wpi, wpg = lw["wgu"], lw["wd"], lw["wgus"], lw["wds"], lw["wpi"], lw["wpg"]
    n_routed_steps = wgu.shape[0]
    w2, rb = router
    tok = lambda i, e: (i, 0)
    fixed = lambda i, e: (0, 0)
    step = lambda i, e: (jnp.minimum(e, n_routed_steps - 1), 0, 0)
    return pl.pallas_call(
        functools.partial(_moe_body, alpha=alpha, n_routed_steps=n_routed_steps),
        grid=(t // tm, n_routed_steps + 1),
        in_specs=[
            pl.BlockSpec((tm, d), tok), pl.BlockSpec((tm, p2.shape[1]), tok),
            pl.BlockSpec((2 * N_EXPERTS, d), fixed), pl.BlockSpec((N_EXPERTS, 1), fixed),
            pl.BlockSpec((1,) + wgu.shape[1:], step), pl.BlockSpec((1,) + wd.shape[1:], step),
            pl.BlockSpec(wgus.shape, fixed), pl.BlockSpec(wds.shape, fixed),
            pl.BlockSpec((1, d), fixed), pl.BlockSpec((1, d), fixed),
            pl.BlockSpec(wpi.shape, fixed), pl.BlockSpec(wpg.shape, fixed),
        ],
        out_specs=pl.BlockSpec((tm, d), tok),
        out_shape=jax.ShapeDtypeStruct((t, d), F32),
        scratch_shapes=[pltpu.VMEM((tm, LANES), F32), pltpu.VMEM((tm, d), BF), pltpu.VMEM((tm, d), F32)],
        compiler_params=_cparams("parallel", "arbitrary"),
        name="moe",
    )(x2, p2, w2, rb, wgu, wd, wgus, wds, g.reshape(1, d), b.reshape(1, d), wpi, wpg)


def _prep_shared(prm):
    depth = prm["ln1_g"].shape[0]
    rw = prm["router_w"].astype(F32)
    wh = rw.astype(BF)
    wl = (rw - wh.astype(F32)).astype(BF)
    router = (jnp.concatenate([wh.T, wl.T], axis=0), prm["router_b"].astype(F32).reshape(N_EXPERTS, 1))
    layers = []
    for i in range(depth):
        eps = MOE_EXPERTS_PER_STEP
        d = prm["moe_w_gate"].shape[2]
        wgu = jnp.concatenate([prm["moe_w_gate"][i], prm["moe_w_up"][i]], axis=-1)
        wgu = wgu.reshape(N_EXPERTS // eps, eps, d, 2 * EXPERT_FF).transpose(0, 2, 1, 3)
        wgu = wgu.reshape(N_EXPERTS // eps, d, eps * 2 * EXPERT_FF).astype(BF)
        wd = prm["moe_w_down"][i].reshape(N_EXPERTS // eps, eps * EXPERT_FF, d).astype(BF)
        wgus = jnp.concatenate([prm["moe_ws_gate"][i], prm["moe_ws_up"][i]], axis=-1).astype(BF)
        layers.append(dict(wgu=wgu, wd=wd, wgus=wgus, wds=prm["moe_ws_down"][i].astype(BF),
                           wpi=prm["ple_w_in"][i].astype(BF), wpg=prm["ple_w_gate"][i].astype(BF)))
    return router, layers


def _trunk(x, p, prm, router, layers):
    depth = prm["ln1_g"].shape[0]
    alpha = (2.0 * depth) ** 0.25
    bsz, seq, d = x.shape
    t = bsz * seq
    x2 = x.reshape(t, d)
    for i in range(depth):
        mixer, j = i % 4, i // 4
        xb = x2.reshape(bsz, seq, d)
        if mixer == 3:
            x2 = _mixer_d_and_post(x2, bsz, seq, prm["d_w_qkv"][j], prm["d_w_out"][j],
                                   prm["ln1_g"][i], prm["ln1_b"][i], alpha)
        else:
            if mixer == 0:
                o = _mixer_a(xb, prm["a_w_in"][j], prm["a_lambda"][j], prm["a_subln"][j], i)
                w_out = prm["a_w_out"][j]
            elif mixer == 1:
                o = _mixer_b(xb, prm["b_w_in"][j], prm["b_q_norm"][j], prm["b_kv_norm"][j],
                             prm["b_w_uq"][j], prm["b_w_ukv"][j])
                w_out = prm["b_w_out"][j]
            else:
                o = _mixer_c(xb, prm["c_w_qkv"][j], prm["c_rpb"][j])
                w_out = prm["c_w_out"][j]
            x2 = _post_attn(o, x2, w_out.astype(BF), prm["ln1_g"][i], prm["ln1_b"][i], alpha)
        x2 = _moe_layer(x2, p[i].reshape(t, -1), router, layers[i], prm["ln2_g"][i], prm["ln2_b"][i], alpha)
    return x2.reshape(bsz, seq, d)


def kernel(x_prompt, x_sample, p_prompt, p_sample, a_w_in, a_lambda, a_subln, a_w_out, b_w_in, b_q_norm, b_kv_norm, b_w_uq, b_w_ukv, b_w_out, c_w_qkv, c_rpb, c_w_out, d_w_qkv, d_w_out, router_w, router_b, moe_w_gate, moe_w_up, moe_w_down, moe_ws_gate, moe_ws_up, moe_ws_down, ln1_g, ln1_b, ln2_g, ln2_b, ple_w_in, ple_w_gate):
    prm = dict(a_w_in=a_w_in, a_lambda=a_lambda, a_subln=a_subln, a_w_out=a_w_out,
               b_w_in=b_w_in, b_q_norm=b_q_norm, b_kv_norm=b_kv_norm, b_w_uq=b_w_uq,
               b_w_ukv=b_w_ukv, b_w_out=b_w_out,
               c_w_qkv=c_w_qkv, c_rpb=c_rpb, c_w_out=c_w_out,
               d_w_qkv=d_w_qkv, d_w_out=d_w_out,
               router_w=router_w, router_b=router_b, moe_w_gate=moe_w_gate, moe_w_up=moe_w_up,
               moe_w_down=moe_w_down, moe_ws_gate=moe_ws_gate, moe_ws_up=moe_ws_up,
               moe_ws_down=moe_ws_down,
               ln1_g=ln1_g, ln1_b=ln1_b, ln2_g=ln2_g, ln2_b=ln2_b,
               ple_w_in=ple_w_in, ple_w_gate=ple_w_gate)
    router, layers = _prep_shared(prm)
    y_prompt = _trunk(x_prompt, p_prompt, prm, router, layers)
    y_sample = _trunk(x_sample, p_sample, prm, router, layers)
    return (y_prompt, y_sample)
```

```python
import functools
import math

import numpy as np
import jax
import jax.numpy as jnp
from jax import lax
from jax.experimental import pallas as pl
from jax.experimental.pallas import tpu as pltpu

BF = jnp.bfloat16
F32 = jnp.float32

VMEM_LIMIT_BYTES = 56 * 1024 * 1024
LANES = 128

GRID_W = 64
LN_EPS = 1e-5
RMS_EPS = 1e-6
ROPE_THETA = 10000.0
NEG = -1e30

A_HEADS, A_HEAD_DIM = 8, 64
B_HEADS, B_NOPE, B_ROPE, B_VDIM, B_Q_LORA, B_KV_LORA = 16, 64, 32, 64, 384, 256
C_HEADS, C_HEAD_DIM, C_WIN_ROWS, C_WIN_COLS = 16, 64, 8, 16
D_HEADS, D_HEAD_DIM = 8, 64
D_GROUPS = ((128, 1), (512, 4), (2048, 16))
N_EXPERTS, N_EXPERT_GROUPS, EXPERT_FF = 16, 4, 256

NT_DIMS = (((1,), (1,)), ((), ()))


def _cparams(*sem):
    return pltpu.CompilerParams(dimension_semantics=sem, vmem_limit_bytes=VMEM_LIMIT_BYTES)


def _layer_norm_rows(z, g, b):
    mu = jnp.mean(z, axis=-1, keepdims=True)
    zc = z - mu
    var = jnp.mean(zc * zc, axis=-1, keepdims=True)
    return zc * lax.rsqrt(var + LN_EPS) * g + b


def _rms_rows(z, g):
    return z * lax.rsqrt(jnp.mean(z * z, axis=-1, keepdims=True) + RMS_EPS) * g


def _linear_body(x_ref, w_ref, o_ref):
    o_ref[...] = jnp.dot(x_ref[...].astype(BF), w_ref[...], preferred_element_type=F32).astype(o_ref.dtype)


def _linear(x, w, out_dtype, tm=512, tn=None):
    m, k = x.shape
    n = w.shape[1]
    tn = n if tn is None else tn
    return pl.pallas_call(
        _linear_body,
        grid=(n // tn, m // tm),
        in_specs=[pl.BlockSpec((tm, k), lambda j, i: (i, 0)), pl.BlockSpec((k, tn), lambda j, i: (0, j))],
        out_specs=pl.BlockSpec((tm, tn), lambda j, i: (i, j)),
        out_shape=jax.ShapeDtypeStruct((m, n), out_dtype),
        compiler_params=_cparams("parallel", "parallel"),
        name="linear",
    )(x, w)


LOG2E = 1.4426950408889634
SUM_ROWS = 16


def _softmax_chunk(s, m, shift, mx=None):
    if mx is None:
        mx = jnp.max(s, axis=0, keepdims=True)
    if shift is not None:
        mx = mx - shift
    m_new = jnp.maximum(m, mx)
    ref = m_new if shift is None else m_new + shift
    return m_new, jnp.exp2(s - ref).astype(BF)


def _pipelined_flash(n, nh, tq, qk_fn, sm_fn, pv_fn, s_buf, p_buf, acc_ref):
    assert n >= 4 and n % 2 == 0
    acc_ref[...] = jnp.zeros_like(acc_ref)

    def scores(j, slot):
        mxs = []
        for h in range(nh):
            s = qk_fn(j, h)
            s_buf[slot, h] = s
            mxs.append(jnp.max(s, axis=0, keepdims=True))
        return tuple(mxs)

    def stage(j, slot, carry, do_pv=True, do_qk=True, first=False):
        ms, alphas, mxs = carry
        if do_pv:
            for h in range(nh):
                acc_ref[h] = alphas[h] * acc_ref[h] + pv_fn(j - 1, h, p_buf[1 - slot, h])
        mxs_next = scores(j + 1, 1 - slot) if do_qk else mxs
        new_ms, new_alphas = [], []
        for h in range(nh):
            m_new, p = sm_fn(j, h, s_buf[slot, h], ms[h], mxs[h], first)
            p_buf[slot, h] = p
            new_alphas.append(jnp.exp2(ms[h] - m_new))
            new_ms.append(m_new)
        return tuple(new_ms), tuple(new_alphas), mxs_next

    mxs = scores(0, 0)
    ms = tuple(jnp.full((1, tq), NEG, F32) for _ in range(nh))
    alphas = tuple(jnp.zeros((1, tq), F32) for _ in range(nh))
    carry = stage(0, 0, (ms, alphas, mxs), do_pv=False, first=True)

    def body(t, carry):
        return stage(2 * t + 2, 0, stage(2 * t + 1, 1, carry))

    carry = lax.fori_loop(0, (n - 2) // 2, body, carry)
    _, alphas, _ = stage(n - 1, 1, carry, do_qk=False)
    for h in range(nh):
        acc_ref[h] = alphas[h] * acc_ref[h] + pv_fn(n - 1, h, p_buf[1, h])


def _chunk_off(j, tk):
    return j * tk if isinstance(j, int) else pl.multiple_of(j * tk, tk)


def _attn_a_body(cs_ref, ctab_ref, lam_ref, g_ref, q_ref, k_ref, kpos_ref, vt_ref, o_ref,
                 s_buf, p_buf, acc_ref, *, seq, tq, tk, hps, lam_init):
    hg = pl.program_id(1)
    qi = pl.program_id(2)
    n = seq // tk
    cd = (qi * tq) // tk
    lane = lax.broadcasted_iota(jnp.int32, (tq, LANES), 1)
    qpos = (qi * tq + lax.broadcasted_iota(jnp.int32, (1, tq), 1)).astype(F32)
    dv = vt_ref.shape[2] - SUM_ROWS
    c_slope, f0, q_left, q_right = [], [], [], []
    for hh in range(hps):
        head = hg * hps + hh
        c_slope.append(cs_ref[head])
        f0.append(c_slope[hh] * qpos)
        q = q_ref[0, :, hh * LANES:(hh + 1) * LANES]
        qaug = jnp.broadcast_to(ctab_ref[pl.ds(head, 1), :], (tq, LANES)).astype(BF)
        for half in range(2):
            q_m = jnp.where((lane >= 64 * half) & (lane < 64 * (half + 1)), q, jnp.zeros_like(q))
            q_left.append(jnp.concatenate([q_m, qaug], axis=1))
            q_right.append(jnp.concatenate([q_m, -qaug], axis=1))

    def chunk_of(j):
        jm = j - 1
        c = jnp.where(j == 0, cd, jm + (jm >= cd).astype(jnp.int32))
        return c, c <= cd

    def qk_fn(j, mm):
        hh = mm // 2
        c, left = chunk_of(j)
        off = pl.multiple_of(c * tk, tk)
        k_c = jnp.concatenate([k_ref[0, pl.ds(off, tk), hh * LANES:(hh + 1) * LANES],
                               kpos_ref[pl.ds(off, tk), :]], axis=1)
        q_full = jnp.where(left, q_left[mm], q_right[mm])
        return lax.dot_general(k_c, q_full, NT_DIMS, preferred_element_type=F32)

    def sm_fn(j, mm, s, m, mx, first):
        hh = mm // 2
        if first:
            d = (lax.broadcasted_iota(jnp.int32, (tk, tq), 0) - lax.broadcasted_iota(jnp.int32, (tk, tq), 1)
                 + (cd * tk - qi * tq)).astype(F32)
            return _softmax_chunk(s - (2.0 * c_slope[hh]) * jnp.maximum(d, 0.0), m, f0[hh])
        _, left = chunk_of(j)
        return _softmax_chunk(s, m, jnp.where(left, f0[hh], -f0[hh]), mx)

    def pv_fn(j, mm, p):
        c, _ = chunk_of(j)
        off = pl.multiple_of(c * tk, tk)
        return jnp.dot(vt_ref[0, mm // 2, :, pl.ds(off, tk)], p, preferred_element_type=F32)

    _pipelined_flash(n, 2 * hps, tq, qk_fn, sm_fn, pv_fn, s_buf, p_buf, acc_ref)
    lf = lam_ref[...]
    lam_full = (jnp.exp(jnp.sum(lf[0:1] * lf[1:2], axis=-1, keepdims=True))
                - jnp.exp(jnp.sum(lf[2:3] * lf[3:4], axis=-1, keepdims=True)) + lam_init)
    outs = []
    for hh in range(hps):
        o0, o1 = [acc_ref[2 * hh + half, :dv, :] / acc_ref[2 * hh + half, dv:dv + 1, :] for half in range(2)]
        o = o0 - lam_full * o1
        ms = jnp.mean(o * o, axis=0, keepdims=True)
        outs.append(o * lax.rsqrt(ms + RMS_EPS) * g_ref[...] * (1.0 - lam_init))
    o_ref[0] = jnp.concatenate(outs, axis=0).T.astype(o_ref.dtype)


def _bf16_pieces(c, n=3):
    pieces, rest = [], c.astype(F32)
    for _ in range(n):
        p = rest.astype(BF)
        pieces.append(p.astype(F32))
        rest = rest - p.astype(F32)
    return pieces


def _store_vt_heads(vt_ref, vt, dv, heads_per_slot):
    tm = vt.shape[1]
    tail = jnp.where(lax.broadcasted_iota(jnp.int32, (SUM_ROWS, tm), 0) == 0, 1.0, 0.0).astype(vt_ref.dtype)
    rows = dv + SUM_ROWS
    for h in range(vt.shape[0] // dv):
        slot, r0 = h // heads_per_slot, (h % heads_per_slot) * rows
        vt_ref[0, slot, r0:r0 + dv, :] = vt[h * dv:(h + 1) * dv, :].astype(vt_ref.dtype)
        vt_ref[0, slot, r0 + dv:r0 + rows, :] = tail


def _proj_a_body(x_ref, wqk_ref, wvt_ref, qk_ref, vt_ref):
    xb = x_ref[...].astype(BF)
    qk_ref[...] = jnp.dot(xb, wqk_ref[...], preferred_element_type=F32).astype(BF)
    vt = lax.dot_general(wvt_ref[...], xb, NT_DIMS, preferred_element_type=F32)
    _store_vt_heads(vt_ref, vt, 2 * A_HEAD_DIM, 1)


def _flash_chunk_len(seq, tk_max):
    for min_chunks in (8, 4):
        tk = tk_max
        while tk >= 2 * LANES and (seq % tk or (seq // tk) < min_chunks or (seq // tk) % 2):
            tk //= 2
        if tk >= 2 * LANES:
            return tk
    raise ValueError(f"sequence length {seq} too short for the pipelined attention kernel")


def _flash_scratch(nh, rows, tq, tk):
    return [pltpu.VMEM((2, nh, tk, tq), F32), pltpu.VMEM((2, nh, tk, tq), BF), pltpu.VMEM((nh, rows, tq), F32)]


def _attn_a(qkv, vt, lam, subln_g, lam_init, tq=256, tk=1024):
    bsz, seq, _ = qkv.shape
    tq, tk = min(tq, seq), _flash_chunk_len(seq, tk)
    c_slope = jnp.asarray(2.0 ** (-8.0 * np.arange(1, A_HEADS + 1) / A_HEADS) * LOG2E, dtype=F32)
    c1, c2, c3 = _bf16_pieces(c_slope)
    ctab = jnp.stack([c1, c1, c2, c2, c3, c3], axis=-1)
    ctab = jnp.concatenate([ctab, jnp.zeros((A_HEADS, LANES - 6), F32)], axis=-1)
    pos = np.arange(seq)
    kpos = np.zeros((seq, LANES), np.float32)
    for j in range(3):
        kpos[:, 2 * j] = (pos // LANES) * LANES
        kpos[:, 2 * j + 1] = pos % LANES
    kpos = jnp.asarray(kpos, BF)
    vrows = vt.shape[2]
    hps = 2 if seq <= 4096 else 1
    ngrp = A_HEADS // hps
    body = functools.partial(_attn_a_body, seq=seq, tq=tq, tk=tk, hps=hps, lam_init=lam_init)
    return pl.pallas_call(
        body,
        grid=(bsz, ngrp, seq // tq),
        in_specs=[
            pl.BlockSpec(memory_space=pltpu.SMEM),
            pl.BlockSpec((A_HEADS, LANES), lambda b, h, i: (0, 0)),
            pl.BlockSpec((4, A_HEAD_DIM), lambda b, h, i: (0, 0)),
            pl.BlockSpec((2 * A_HEAD_DIM, 1), lambda b, h, i: (0, 0)),
            pl.BlockSpec((1, tq, hps * LANES), lambda b, h, i: (b, i, h)),
            pl.BlockSpec((1, seq, hps * LANES), lambda b, h, i: (b, 0, ngrp + h)),
            pl.BlockSpec((seq, LANES), lambda b, h, i: (0, 0)),
            pl.BlockSpec((1, hps, vrows, seq), lambda b, h, i: (b, h, 0, 0)),
        ],
        out_specs=pl.BlockSpec((1, tq, hps * LANES), lambda b, h, i: (b, i, h)),
        out_shape=jax.ShapeDtypeStruct((bsz, seq, A_HEADS * LANES), BF),
        scratch_shapes=_flash_scratch(2 * hps, vrows, tq, tk),
        compiler_params=_cparams("parallel", "parallel", "arbitrary"),
        name="attn_a",
    )(c_slope, ctab, lam.astype(F32), subln_g.astype(F32).reshape(2 * A_HEAD_DIM, 1), qkv, qkv, kpos, vt)


def _attn_b_body(q_ref, k_ref, vt_ref, o_ref, s_buf, p_buf, acc_ref, *, seq, tk, nh):
    tq = q_ref.shape[1]
    rows = B_VDIM + SUM_ROWS
    qs = [q_ref[0, :, h * LANES:(h + 1) * LANES] for h in range(nh)]

    def qk_fn(j, h):
        k_c = k_ref[0, pl.ds(_chunk_off(j, tk), tk), h * LANES:(h + 1) * LANES]
        return lax.dot_general(k_c, qs[h], NT_DIMS, preferred_element_type=F32)

    def sm_fn(j, h, s, m, mx, first):
        return _softmax_chunk(s, m, None, mx)

    def pv_fn(j, h, p):
        vt_c = vt_ref[0, h // 2, (h % 2) * rows:(h % 2 + 1) * rows, pl.ds(_chunk_off(j, tk), tk)]
        return jnp.dot(vt_c, p, preferred_element_type=F32)

    _pipelined_flash(seq // tk, nh, tq, qk_fn, sm_fn, pv_fn, s_buf, p_buf, acc_ref)
    outs = [acc_ref[h, :B_VDIM, :] / acc_ref[h, B_VDIM:B_VDIM + 1, :] for h in range(nh)]
    o = jnp.concatenate(outs, axis=0)
    o_ref[0] = o.T.astype(o_ref.dtype)


def _attn_b(q, k, vt, tq=256, tk=1024):
    bsz, seq, _ = q.shape
    tq, tk = min(tq, seq), _flash_chunk_len(seq, tk)
    nh = 4 if seq <= 4096 else 2
    return pl.pallas_call(
        functools.partial(_attn_b_body, seq=seq, tk=tk, nh=nh),
        grid=(bsz, B_HEADS // nh, seq // tq),
        in_specs=[
            pl.BlockSpec((1, tq, nh * LANES), lambda b, h, i: (b, i, h)),
            pl.BlockSpec((1, seq, nh * LANES), lambda b, h, i: (b, 0, h)),
            pl.BlockSpec((1, nh // 2, 2 * (B_VDIM + SUM_ROWS), seq), lambda b, h, i: (b, h, 0, 0)),
        ],
        out_specs=pl.BlockSpec((1, tq, nh * B_VDIM), lambda b, h, i: (b, i, h)),
        out_shape=jax.ShapeDtypeStruct((bsz, seq, B_HEADS * B_VDIM), BF),
        scratch_shapes=_flash_scratch(nh, B_VDIM + SUM_ROWS, tq, tk),
        compiler_params=_cparams("parallel", "parallel", "arbitrary"),
        name="attn_b",
    )(q, k, vt)


def _mla_in_body(x_ref, w_ref, gq_ref, gkv_ref, rc_ref, rs_ref, cq_ref, ckv_ref, kr_ref):
    h = jnp.dot(x_ref[...].astype(BF), w_ref[...], preferred_element_type=F32)
    cq_ref[...] = _rms_rows(h[:, :B_Q_LORA], gq_ref[...]).astype(BF)
    ckv_ref[...] = _rms_rows(h[:, B_Q_LORA:B_Q_LORA + B_KV_LORA], gkv_ref[...]).astype(BF)
    t = h[:, B_Q_LORA + B_KV_LORA:]
    kr = t * rc_ref[...] + pltpu.roll(t, LANES - B_ROPE, 1) * rs_ref[...]
    kr_ref[...] = kr.astype(BF)


def _mla_q_body(cq_ref, w_ref, ta_ref, tb_ref, q_ref):
    t = jnp.dot(cq_ref[...], w_ref[...], preferred_element_type=F32)
    ta = ta_ref[...]
    tb = tb_ref[...]
    for h in range(B_HEADS):
        th = t[:, h * LANES:(h + 1) * LANES]
        q_ref[:, h * LANES:(h + 1) * LANES] = (th * ta + pltpu.roll(th, LANES - B_ROPE, 1) * tb).astype(BF)


def _mla_kv_body(ckv_ref, kr_ref, wk_ref, e_ref, wvt_ref, k_ref, vt_ref):
    ckv = ckv_ref[...]
    k = jnp.dot(ckv, wk_ref[...], preferred_element_type=F32)
    k = k + jnp.dot(kr_ref[...], e_ref[...], preferred_element_type=F32)
    k_ref[...] = k.astype(BF)
    vt = lax.dot_general(wvt_ref[...], ckv, NT_DIMS, preferred_element_type=F32)
    _store_vt_heads(vt_ref, vt, B_VDIM, 2)


def _rope_partner(w):
    half = B_ROPE // 2
    return jnp.concatenate([-w[..., half:], w[..., :half]], axis=-1)


def _mla_tables(seq):
    inv = 1.0 / (ROPE_THETA ** (np.arange(0, B_ROPE, 2, dtype=np.float32) / B_ROPE))
    ang = jnp.arange(seq, dtype=F32)[:, None] * jnp.asarray(inv, F32)[None, :]
    cos = jnp.concatenate([jnp.cos(ang), jnp.cos(ang)], axis=-1)
    sin = jnp.concatenate([jnp.sin(ang), jnp.sin(ang)], axis=-1)
    z = lambda n: jnp.zeros((seq, n), F32)
    rc = jnp.concatenate([cos, z(LANES - B_ROPE)], axis=-1)
    rs = jnp.concatenate([sin, z(LANES - B_ROPE)], axis=-1)
    scale = (B_NOPE + B_ROPE) ** -0.5 * LOG2E
    ta =jnp.concatenate([jnp.full((seq, B_NOPE), scale, F32), cos * scale, z(B_ROPE)], axis=-1)
    tb = jnp.concatenate([z(B_NOPE), sin * scale, z(B_ROPE)], axis=-1)
    return rc, rs, ta, tb


def _mla_weights(w_in, w_uq, w_ukv):
    d = w_in.shape[0]
    kr0 = B_Q_LORA + B_KV_LORA
    w_in_ext = jnp.concatenate(
        [w_in, _rope_partner(w_in[:, kr0:kr0 + B_ROPE]), jnp.zeros((d, LANES - 2 * B_ROPE), w_in.dtype)], axis=-1)
    wq = w_uq.reshape(B_Q_LORA, B_HEADS, B_NOPE + B_ROPE)
    wq_ext = jnp.concatenate([wq, _rope_partner(wq[..., B_NOPE:])], axis=-1).reshape(B_Q_LORA, B_HEADS * LANES)
    wkv = w_ukv.reshape(B_KV_LORA, B_HEADS, B_NOPE + B_VDIM)
    wk = jnp.concatenate([wkv[..., :B_NOPE], jnp.zeros((B_KV_LORA, B_HEADS, LANES - B_NOPE), w_ukv.dtype)], axis=-1)
    wk = wk.reshape(B_KV_LORA, B_HEADS * LANES)
    wv = wkv[..., B_NOPE:].reshape(B_KV_LORA, B_HEADS * B_VDIM)
    place = np.zeros((LANES, B_HEADS, LANES), np.float32)
    for j in range(B_ROPE):
        place[j, :, B_NOPE + j] = 1.0
    place = jnp.asarray(place.reshape(LANES, B_HEADS * LANES), BF)
    return w_in_ext.astype(BF), wq_ext.astype(BF), wk.astype(BF), place, wv.astype(BF)


def _mixer_b(x, w_in, q_norm_g, kv_norm_g, w_uq, w_ukv, tm=512):
    bsz, seq, d = x.shape
    t = bsz * seq
    tm = min(tm, seq)
    nblk = seq // tm
    w_in_ext, wq_ext, wk, place, wv = _mla_weights(w_in, w_uq, w_ukv)
    rc, rs, ta, tb = _mla_tables(seq)
    x2 = x.reshape(t, d)
    n_in = w_in_ext.shape[1]
    row = lambda i: (i, 0)
    fixed = lambda i: (0, 0)
    pos = lambda i: (i % nblk, 0)
    cq, ckv, kr = pl.pallas_call(
        _mla_in_body,
        grid=(t // tm,),
        in_specs=[
            pl.BlockSpec((tm, d), row), pl.BlockSpec((d, n_in), fixed),
            pl.BlockSpec((1, B_Q_LORA), fixed), pl.BlockSpec((1, B_KV_LORA), fixed),
            pl.BlockSpec((tm, LANES), pos), pl.BlockSpec((tm, LANES), pos),
        ],
        out_specs=[pl.BlockSpec((tm, B_Q_LORA), row), pl.BlockSpec((tm, B_KV_LORA), row),
                   pl.BlockSpec((tm, LANES), row)],
        out_shape=[jax.ShapeDtypeStruct((t, B_Q_LORA), BF), jax.ShapeDtypeStruct((t, B_KV_LORA), BF),
                   jax.ShapeDtypeStruct((t, LANES), BF)],
        compiler_params=_cparams("parallel"),
        name="mla_in",
    )(x2, w_in_ext, q_norm_g.astype(F32).reshape(1, -1), kv_norm_g.astype(F32).reshape(1, -1), rc, rs)
    nq = B_HEADS * LANES
    q = pl.pallas_call(
        _mla_q_body,
        grid=(t // tm,),
        in_specs=[pl.BlockSpec((tm, B_Q_LORA), row), pl.BlockSpec((B_Q_LORA, nq), fixed),
                  pl.BlockSpec((tm, LANES), pos), pl.BlockSpec((tm, LANES), pos)],
        out_specs=pl.BlockSpec((tm, nq), row),
        out_shape=jax.ShapeDtypeStruct((t, nq), BF),
        compiler_params=_cparams("parallel"),
        name="mla_q",
    )(cq, wq_ext, ta, tb)
    nv = B_HEADS * B_VDIM
    npair = B_HEADS // 2
    vrows = 2 * (B_VDIM + SUM_ROWS)
    k, vt = pl.pallas_call(
        _mla_kv_body,
        grid=(t // tm,),
        in_specs=[pl.BlockSpec((tm, B_KV_LORA), row), pl.BlockSpec((tm, LANES), row),
                  pl.BlockSpec((B_KV_LORA, nq), fixed), pl.BlockSpec((LANES, nq), fixed),
                  pl.BlockSpec((nv, B_KV_LORA), fixed)],
        out_specs=[pl.BlockSpec((tm, nq), row),
                   pl.BlockSpec((1, npair, vrows, tm), lambda i: (i // nblk, 0, 0, i % nblk))],
        out_shape=[jax.ShapeDtypeStruct((t, nq), BF), jax.ShapeDtypeStruct((bsz, npair, vrows, seq), BF)],
        compiler_params=_cparams("parallel"),
        name="mla_kv",
    )(ckv, kr, wk, place, wv.T)
    o = _attn_b(q.reshape(bsz, seq, nq), k.reshape(bsz, seq, nq), vt)
    return o.reshape(t, nv)


def _mixer_a(x, w_in, lam, subln_g, layer_idx):
    bsz, seq, d = x.shape
    t = bsz * seq
    hd2 = 2 * A_HEAD_DIM
    nq = A_HEADS * hd2
    scale = A_HEAD_DIM ** -0.5 * LOG2E
    wqk = jnp.concatenate([w_in[:, :nq] * scale, w_in[:, nq:2 * nq]], axis=-1).astype(BF)
    wvt = w_in[:, 2 * nq:].T.astype(BF)
    tm = min(512, seq)
    nblk = seq // tm
    vrows = hd2 + SUM_ROWS
    qk, vt = pl.pallas_call(
        _proj_a_body,
        grid=(t // tm,),
        in_specs=[pl.BlockSpec((tm, d), lambda i: (i, 0)), pl.BlockSpec((d, 2 * nq), lambda i: (0, 0)),
                  pl.BlockSpec((nq, d), lambda i: (0, 0))],
        out_specs=[pl.BlockSpec((tm, 2 * nq), lambda i: (i, 0)),
                   pl.BlockSpec((1, A_HEADS, vrows, tm), lambda i: (i // nblk, 0, 0, i % nblk))],
        out_shape=[jax.ShapeDtypeStruct((t, 2 * nq), BF), jax.ShapeDtypeStruct((bsz, A_HEADS, vrows, seq), BF)],
        compiler_params=_cparams("parallel"),
        name="proj_a",
    )(x.reshape(t, d), wqk, wvt)
    lam_init = 0.8 - 0.6 * math.exp(-0.3 * layer_idx)
    o = _attn_a(qk.reshape(bsz, seq, 2 * nq), vt, lam, subln_g, lam_init)
    return o.reshape(t, nq)


def _attn_c_body(q_ref, k0, k1, k2, k3, v0, v1, v2, v3, bt_ref, o_ref):
    kw = jnp.concatenate([k0[0], k1[0], k2[0], k3[0]], axis=0)
    vw = jnp.concatenate([v0[0], v1[0], v2[0], v3[0]], axis=0)
    nq = q_ref.shape[1] // 2
    nk = bt_ref.shape[3]
    lane = lax.broadcasted_iota(jnp.int32, (nq, LANES), 1)
    for hf in range(2):
        q = q_ref[0, hf * nq:(hf + 1) * nq, :]
        kh = kw[hf * nq:hf * nq + nk]
        vh = vw[hf * nq:hf * nq + nk]
        outs = []
        for hh in range(2):
            qm = jnp.where((lane >= 64 * hh) & (lane < 64 * (hh + 1)), q, jnp.zeros_like(q))
            s = (lax.dot_general(qm, kh, NT_DIMS, preferred_element_type=F32)
                 + bt_ref[hh, 0, hf * nq:(hf + 1) * nq, :])
            m = jnp.max(s, axis=1, keepdims=True)
            p = jnp.exp2(s - m)
            l = jnp.sum(p, axis=1, keepdims=True)
            outs.append(jnp.dot(p.astype(BF), vh, preferred_element_type=F32) / l)
        o_ref[0, hf * nq:(hf + 1) * nq, :] = jnp.where(lane < 64, outs[0], outs[1]).astype(o_ref.dtype)


def _nbr_bias_table(rpb):
    col = np.arange(GRID_W)
    col_start = np.clip(col - C_WIN_COLS // 2, 0, GRID_W - C_WIN_COLS)
    col_mask = (col[None, :] >= col_start[:, None]) & (col[None, :] < col_start[:, None] + C_WIN_COLS)
    pad = GRID_W - C_WIN_COLS
    ext = jnp.pad(rpb.astype(F32) * LOG2E, ((0, 0), (0, 0), (pad, pad)), mode="edge")
    toep = jnp.stack([ext[:, :, GRID_W - 1 - qc:2 * GRID_W - 1 - qc] for qc in range(GRID_W)], axis=2)
    toep = jnp.where(jnp.asarray(col_mask)[None, None], toep, NEG)
    neg = jnp.full((rpb.shape[0], GRID_W, GRID_W), NEG, F32)
    half = C_WIN_ROWS // 2
    kinds = []
    for kind in range(3):
        qrows = []
        for t in range(C_WIN_ROWS):
            u0 = (max(t, half), t, min(t, half))[kind]
            ubase = 0 if t < half else half
            blocks = [toep[:, u - t + half - 1] if u0 <= u < u0 + C_WIN_ROWS else neg
                      for u in range(ubase, ubase + C_WIN_ROWS + half)]
            qrows.append(jnp.concatenate(blocks, axis=-1))
        kinds.append(jnp.concatenate(qrows, axis=1))
    return jnp.stack(kinds, axis=1)


def _mixer_c(x, w_qkv, rpb):
    bsz, seq, d = x.shape
    t = bsz * seq
    nq = C_HEADS * C_HEAD_DIM
    scale = C_HEAD_DIM ** -0.5 * LOG2E
    w = jnp.concatenate([w_qkv[:, :nq] * scale, w_qkv[:, nq:]], axis=-1).astype(BF)
    qkv = _linear(x.reshape(t, d), w, BF, tn=nq).reshape(bsz, seq, 3 * nq)
    bt = _nbr_bias_table(rpb)
    npair = C_HEADS // 2
    qtok = C_WIN_ROWS * GRID_W
    ngrp = seq // qtok
    assert ngrp >= 2
    kb = qtok // 2
    nkb = seq // kb
    kspecs = []
    for off in (npair, 2 * npair):
        for j in range(4):
            kspecs.append(pl.BlockSpec(
                (1, kb, LANES),
                lambda h, g, b, j=j, off=off: (b, jnp.clip(2 * g - 1 + j, 0, nkb - 1), off + h)))
    kind = lambda g: jnp.where(g == 0, 0, jnp.where(g == ngrp - 1, 2, 1))
    o = pl.pallas_call(
        _attn_c_body,
        grid=(npair, ngrp, bsz),
        in_specs=[pl.BlockSpec((1, qtok, LANES), lambda h, g, b: (b, g, h))] + kspecs + [
            pl.BlockSpec((2, 1, qtok, bt.shape[3]), lambda h, g, b: (h, kind(g), 0, 0))],
        out_specs=pl.BlockSpec((1, qtok, LANES), lambda h, g, b: (b, g, h)),
        out_shape=jax.ShapeDtypeStruct((bsz, seq, nq), BF),
        compiler_params=_cparams("parallel", "parallel", "arbitrary"),
        name="attn_c",
    )(qkv, *([qkv] * 8), bt)
    return o.reshape(t, nq)


def _attn_d_body(slopes_ref, q_ref, kp, kc, kn, vp, vc, vn, o_ref, lse_ref, *, tq, length, dil, rad):
    i = pl.program_id(2)
    nk = tq + 2 * rad
    qi = lax.broadcasted_iota(jnp.int32, (tq, nk), 0)
    kk = lax.broadcasted_iota(jnp.int32, (tq, nk), 1)
    dist = jnp.abs(kk - rad - qi)
    ki = i * tq - rad + kk
    valid = (dist <= rad) & (ki >= 0) & (ki < length)
    distf = dist.astype(F32) * float(dil)
    lane = lax.broadcasted_iota(jnp.int32, (tq, LANES), 1)
    for hp in range(D_HEADS // 2):
        sl = slice(hp * LANES, (hp + 1) * LANES)
        q = q_ref[0, 0, :, sl]
        kw = jnp.concatenate([kp[0, 0, tq - rad:tq, sl], kc[0, 0, :, sl], kn[0, 0, 0:rad, sl]], axis=0)
        vw = jnp.concatenate([vp[0, 0, tq - rad:tq, sl], vc[0, 0, :, sl], vn[0, 0, 0:rad, sl]], axis=0)
        outs, lses = [], []
        for hh in range(2):
            slope = slopes_ref[hp * 2 + hh]
            qm = jnp.where((lane >= 64 * hh) & (lane < 64 * (hh + 1)), q, jnp.zeros_like(q))
            s = lax.dot_general(qm, kw, NT_DIMS, preferred_element_type=F32)
            s = jnp.where(valid, s - slope * distf, NEG)
            m = jnp.max(s, axis=1, keepdims=True)
            p = jnp.exp2(s - m)
            l = jnp.sum(p, axis=1, keepdims=True)
            outs.append(jnp.dot(p.astype(BF), vw, preferred_element_type=F32) / l)
            lses.append(m + jnp.log(l) * LOG2E)
        o_ref[0, 0, :, sl] = jnp.where(lane < 64, outs[0], outs[1]).astype(o_ref.dtype)
        lse_ref[0, 0, :, sl] = jnp.where(lane < 64, lses[0], lses[1])


def _attn_d_group(qkv, window, dil, tq=256):
    bsz, _, length, ncol = qkv.shape
    rad = window // (2 * dil)
    tq = min(tq, length)
    nq = length // tq
    nh = ncol // 3
    slopes = jnp.asarray(2.0 ** (-8.0 * np.arange(1, D_HEADS + 1) / D_HEADS) * LOG2E, dtype=F32)

    def spec(which, shift):
        return pl.BlockSpec((1, 1, tq, nh), lambda b, c, i: (b, c, jnp.clip(i + shift, 0, nq - 1), which))

    ospec = pl.BlockSpec((1, 1, tq, nh), lambda b, c, i: (b, c, i, 0))
    return pl.pallas_call(
        functools.partial(_attn_d_body, tq=tq, length=length, dil=dil, rad=rad),
        grid=(bsz, dil, nq),
        in_specs=[pl.BlockSpec(memory_space=pltpu.SMEM), spec(0, 0),
                  spec(1, -1), spec(1, 0), spec(1, 1), spec(2, -1), spec(2, 0), spec(2, 1)],
        out_specs=[ospec, ospec],
        out_shape=[jax.ShapeDtypeStruct((bsz, dil, length, nh), BF),
                   jax.ShapeDtypeStruct((bsz, dil, length, nh), F32)],
        compiler_params=_cparams("parallel", "parallel", "arbitrary"),
        name=f"attn_d{dil}",
    )(slopes, qkv, qkv, qkv, qkv, qkv, qkv, qkv)


def _proj_d_body(x_ref, w_ref, o0, o1, o2, scr, *, dils):
    xb = x_ref[...].astype(BF)
    tm = xb.shape[0]
    nblk = scr.shape[0]
    ncol = nblk * LANES
    for gi, (o_ref, dil) in enumerate(zip((o0, o1, o2), dils)):
        res = jnp.dot(xb, w_ref[:, gi * ncol:(gi + 1) * ncol], preferred_element_type=F32)
        if dil == 1:
            o_ref[0, 0] = res.astype(BF)
            continue
        for j in range(nblk):
            scr[j] = res[:, j * LANES:(j + 1) * LANES]
        for c in range(dil):
            for j in range(nblk):
                o_ref[0, c, :, j * LANES:(j + 1) * LANES] = scr[j, pl.ds(c, tm // dil, stride=dil), :].astype(BF)


def _post_d_body(o0, o1, o2, l0, l1, l2, x_ref, w_ref, g_ref, b_ref, y_ref, scr_o, scr_l, *, alpha, dils):
    tm = x_ref.shape[0]

    def interleaved(ref, scr, dil):
        if dil == 1:
            return ref[0, 0].astype(F32)
        nblk = scr.shape[0]
        for c in range(dil):
            for j in range(nblk):
                scr[j, pl.ds(c, tm // dil, stride=dil), :] = ref[0, c, :, j * LANES:(j + 1) * LANES].astype(F32)
        return jnp.concatenate([scr[j] for j in range(nblk)], axis=1)

    ov = [interleaved(r, scr_o.at[j], dil) for j, (r, dil) in enumerate(zip((o0, o1, o2), dils))]
    a0, a1, a2 = [interleaved(r, scr_l.at[j], dil) for j, (r, dil) in enumerate(zip((l0, l1, l2), dils))]
    m = jnp.maximum(jnp.maximum(a0, a1), a2)
    e0, e1, e2 = jnp.exp2(a0 - m), jnp.exp2(a1 - m), jnp.exp2(a2 - m)
    den = e0 + e1 + e2
    o = (e0 / den) * ov[0] + (e1 / den) * ov[1] + (e2 / den) * ov[2]
    h = jnp.dot(o.astype(BF), w_ref[...], preferred_element_type=F32)
    y_ref[...] = _layer_norm_rows(alpha * x_ref[...] + h, g_ref[...], b_ref[...])


def _mixer_d_and_post(x2, bsz, seq, w_qkv, w_out, g, b, alpha, tm=512):
    t, d = x2.shape
    nh = D_HEADS * D_HEAD_DIM
    ng = len(D_GROUPS)
    dils = tuple(dil for _, dil in D_GROUPS)
    tm = min(tm, seq)
    nblk = seq // tm
    scale = D_HEAD_DIM ** -0.5 * LOG2E
    wq = w_qkv.reshape(d, 3, ng, nh)
    wq = jnp.stack([wq[:, 0] * scale, wq[:, 1], wq[:, 2]], axis=1)
    w = wq.transpose(0, 2, 1, 3).reshape(d, ng * 3 * nh).astype(BF)

    def deint_spec(dil, width):
        return pl.BlockSpec((1, dil, tm // dil, width), lambda i: (i // nblk, 0, i % nblk, 0))

    qkvs = pl.pallas_call(
        functools.partial(_proj_d_body, dils=dils),
        grid=(t // tm,),
        in_specs=[pl.BlockSpec((tm, d), lambda i: (i, 0)), pl.BlockSpec((d, ng * 3 * nh), lambda i: (0, 0))],
        out_specs=[deint_spec(dil, 3 * nh) for dil in dils],
        out_shape=[jax.ShapeDtypeStruct((bsz, dil, seq // dil, 3 * nh), BF) for dil in dils],
        scratch_shapes=[pltpu.VMEM((3 * nh // LANES, tm, LANES), F32)],
        compiler_params=_cparams("parallel"),
        name="proj_d",
    )(x2, w)
    os_, ls_ = [], []
    for qkv_g, (window, dil) in zip(qkvs, D_GROUPS):
        o, lse = _attn_d_group(qkv_g, window, dil)
        os_.append(o)
        ls_.append(lse)
    row = lambda i: (i, 0)
    fixed = lambda i: (0, 0)
    return pl.pallas_call(
        functools.partial(_post_d_body, alpha=alpha, dils=dils),
        grid=(t // tm,),
        in_specs=[deint_spec(dil, nh) for dil in dils] * 2 + [
            pl.BlockSpec((tm, d), row), pl.BlockSpec((nh, d), fixed),
            pl.BlockSpec((1, d), fixed), pl.BlockSpec((1, d), fixed)],
        out_specs=pl.BlockSpec((tm, d), row),
        out_shape=jax.ShapeDtypeStruct((t, d), F32),
        scratch_shapes=[pltpu.VMEM((ng, nh // LANES, tm, LANES), F32), pltpu.VMEM((ng, nh // LANES, tm, LANES), F32)],
        compiler_params=_cparams("parallel"),
        name="post_d",
    )(*os_, *ls_, x2, w_out.astype(BF), g.reshape(1, d), b.reshape(1, d))


def _router_gates(x, w2_ref, rb_ref):
    tm = x.shape[0]
    xh = x.astype(BF)
    xl = (x - xh.astype(F32)).astype(BF)
    w2 = w2_ref[...]
    both = lax.dot_general(w2, xh, NT_DIMS, preferred_element_type=F32)
    logits = (both[:N_EXPERTS] + both[N_EXPERTS:]
              + lax.dot_general(w2[:N_EXPERTS], xl, NT_DIMS, preferred_element_type=F32))
    scores = jax.nn.sigmoid(logits)
    biased = scores + rb_ref[...]
    epg = N_EXPERTS // N_EXPERT_GROUPS
    sc = [scores[e:e + 1, :] for e in range(N_EXPERTS)]
    bi = [biased[e:e + 1, :] for e in range(N_EXPERTS)]
    gs = []
    for g in range(N_EXPERT_GROUPS):
        v = bi[g * epg:(g + 1) * epg]
        best = None
        for a in range(epg):
            for c in range(a + 1, epg):
                pair = v[a] + v[c]
                best = pair if best is None else jnp.maximum(best, pair)
        gs.append(best)
    gmax = functools.reduce(jnp.maximum, gs)
    taken = jnp.zeros((1, tm), jnp.bool_)
    cand = []
    for g in range(N_EXPERT_GROUPS):
        sel = (gs[g] == gmax) & jnp.logical_not(taken)
        taken = taken | sel
        for a in range(epg):
            cand.append(jnp.where(sel, bi[g * epg + a], -jnp.inf))
    m1 = functools.reduce(jnp.maximum, cand)
    taken = jnp.zeros((1, tm), jnp.bool_)
    is1 = []
    for e in range(N_EXPERTS):
        hit = (cand[e] == m1) & jnp.logical_not(taken)
        taken = taken | hit
        is1.append(hit)
    cand2 = [jnp.where(is1[e], -jnp.inf, cand[e]) for e in range(N_EXPERTS)]
    m2 = functools.reduce(jnp.maximum, cand2)
    taken = jnp.zeros((1, tm), jnp.bool_)
    is2 = []
    for e in range(N_EXPERTS):
        hit = (cand2[e] == m2) & jnp.logical_not(taken)
        taken = taken | hit
        is2.append(hit)
    zero = jnp.zeros((1, tm), F32)
    w1 = functools.reduce(jnp.add, [jnp.where(is1[e], sc[e], zero) for e in range(N_EXPERTS)])
    w2 = functools.reduce(jnp.add, [jnp.where(is2[e], sc[e], zero) for e in range(N_EXPERTS)])
    den = w1 + w2
    rows = [jnp.where(is1[e], w1 / den, zero) + jnp.where(is2[e], w2 / den, zero) for e in range(N_EXPERTS)]
    rows.append(jnp.ones((1, tm), F32))
    rows.append(jnp.zeros((LANES - N_EXPERTS - 1, tm), F32))
    return jnp.concatenate(rows, axis=0).T


MOE_EXPERTS_PER_STEP = 2


def _moe_body(*refs, alpha, n_routed_steps, with_mixer_out):
    if with_mixer_out:
        o_ref, wo_ref, g1_ref, b1_ref, *refs = refs
    (x_ref, p_ref, w2_ref, rb_ref, wgu_ref, wd_ref, wgus_ref, wds_ref, g_ref, b_ref, wpi_ref, wpg_ref,
     y_ref, gates, xb, acc, x1) = refs
    e = pl.program_id(1)
    ff = EXPERT_FF

    def hidden(gu, j):
        return jax.nn.silu(gu[:, 2 * j * ff:(2 * j + 1) * ff]) * gu[:, (2 * j + 1) * ff:(2 * j + 2) * ff]

    @pl.when(e == 0)
    def _():
        x = x_ref[...]
        if with_mixer_out:
            h = jnp.dot(o_ref[...], wo_ref[...], preferred_element_type=F32)
            x = _layer_norm_rows(alpha * x + h, g1_ref[...], b1_ref[...])
        x1[...] = x
        gates[...] = _router_gates(x, w2_ref, rb_ref)
        xb[...] = x.astype(BF)
        acc[...] = jnp.zeros_like(acc)

    @pl.when(e < n_routed_steps)
    def _():
        gu = jnp.dot(xb[...], wgu_ref[0], preferred_element_type=F32)
        lane = lax.broadcasted_iota(jnp.int32, gates.shape, 1)
        gt = gates[...]
        hs = []
        for j in range(MOE_EXPERTS_PER_STEP):
            gcol = jnp.sum(jnp.where(lane == e * MOE_EXPERTS_PER_STEP + j, gt, 0.0), axis=1, keepdims=True)
            hs.append((hidden(gu, j) * gcol).astype(BF))
        acc[...] += jnp.dot(jnp.concatenate(hs, axis=1), wd_ref[0], preferred_element_type=F32)

    @pl.when(e == n_routed_steps)
    def _():
        gu = jnp.dot(xb[...], wgus_ref[...], preferred_element_type=F32)
        y = acc[...] + jnp.dot(hidden(gu, 0).astype(BF), wds_ref[...], preferred_element_type=F32)
        x2 = _layer_norm_rows(alpha * x1[...] + y, g_ref[...], b_ref[...])
        emb = jnp.dot(p_ref[...].astype(BF), wpi_ref[...], preferred_element_type=F32)
        gate = jax.nn.sigmoid(jnp.dot(x2.astype(BF), wpg_ref[...], preferred_element_type=F32))
        y_ref[...] = x2 + gate * emb


def _moe_layer(x2, p2, router, lw, g, b, alpha, mixer_out=None, tm=1024):
    t, d = x2.shape
    wgu, wd, wgus, wds, wpi, wpg = lw["wgu"], lw["wd"], lw["wgus"], lw["wds"], lw["wpi"], lw["wpg"]
    n_routed_steps = wgu.shape[0]
    w2, rb = router
    tok = lambda i, e: (i, 0)
    fixed = lambda i, e: (0, 0)
    step = lambda i, e: (jnp.minimum(e, n_routed_steps - 1), 0, 0)
    pre_specs, pre_args = [], []
    if mixer_out is not None:
        o, w_out, g1, b1 = mixer_out
        pre_specs = [pl.BlockSpec((tm, o.shape[1]), tok), pl.BlockSpec(w_out.shape, fixed),
                     pl.BlockSpec((1, d), fixed), pl.BlockSpec((1, d), fixed)]
        pre_args = [o, w_out, g1.reshape(1, d), b1.reshape(1, d)]
    return pl.pallas_call(
        functools.partial(_moe_body, alpha=alpha, n_routed_steps=n_routed_steps,
                          with_mixer_out=mixer_out is not None),
        grid=(t // tm, n_routed_steps + 1),
        in_specs=pre_specs + [
            pl.BlockSpec((tm, d), tok), pl.BlockSpec((tm, p2.shape[1]), tok),
            pl.BlockSpec((2 * N_EXPERTS, d), fixed), pl.BlockSpec((N_EXPERTS, 1), fixed),
            pl.BlockSpec((1,) + wgu.shape[1:], step), pl.BlockSpec((1,) + wd.shape[1:], step),
            pl.BlockSpec(wgus.shape, fixed), pl.BlockSpec(wds.shape, fixed),
            pl.BlockSpec((1, d), fixed), pl.BlockSpec((1, d), fixed),
            pl.BlockSpec(wpi.shape, fixed), pl.BlockSpec(wpg.shape, fixed),
        ],
        out_specs=pl.BlockSpec((tm, d), tok),
        out_shape=jax.ShapeDtypeStruct((t, d), F32),
        scratch_shapes=[pltpu.VMEM((tm, LANES), F32), pltpu.VMEM((tm, d), BF), pltpu.VMEM((tm, d), F32),
                        pltpu.VMEM((tm, d), F32)],
        compiler_params=_cparams("parallel", "arbitrary"),
        name="moe",
    )(*pre_args, x2, p2, w2, rb, wgu, wd, wgus, wds, g.reshape(1, d), b.reshape(1, d), wpi, wpg)


def _prep_shared(prm):
    depth = prm["ln1_g"].shape[0]
    rw = prm["router_w"].astype(F32)
    wh = rw.astype(BF)
    wl = (rw - wh.astype(F32)).astype(BF)
    router = (jnp.concatenate([wh.T, wl.T], axis=0), prm["router_b"].astype(F32).reshape(N_EXPERTS, 1))
    layers = []
    for i in range(depth):
        eps = MOE_EXPERTS_PER_STEP
        d = prm["moe_w_gate"].shape[2]
        wgu = jnp.concatenate([prm["moe_w_gate"][i], prm["moe_w_up"][i]], axis=-1)
        wgu = wgu.reshape(N_EXPERTS // eps, eps, d, 2 * EXPERT_FF).transpose(0, 2, 1, 3)
        wgu = wgu.reshape(N_EXPERTS // eps, d, eps * 2 * EXPERT_FF).astype(BF)
        wd = prm["moe_w_down"][i].reshape(N_EXPERTS // eps, eps * EXPERT_FF, d).astype(BF)
        wgus = jnp.concatenate([prm["moe_ws_gate"][i], prm["moe_ws_up"][i]], axis=-1).astype(BF)
        layers.append(dict(wgu=wgu, wd=wd, wgus=wgus, wds=prm["moe_ws_down"][i].astype(BF),
                           wpi=prm["ple_w_in"][i].astype(BF), wpg=prm["ple_w_gate"][i].astype(BF)))
    return router, layers


def _trunk(x, p, prm, router, layers):
    depth = prm["ln1_g"].shape[0]
    alpha = (2.0 * depth) ** 0.25
    bsz, seq, d = x.shape
    t = bsz * seq
    x2 = x.reshape(t, d)
    for i in range(depth):
        mixer, j = i % 4, i // 4
        xb = x2.reshape(bsz, seq, d)
        mixer_out = None
        if mixer == 3:
            x2 = _mixer_d_and_post(x2, bsz, seq, prm["d_w_qkv"][j], prm["d_w_out"][j],
                                   prm["ln1_g"][i], prm["ln1_b"][i], alpha)
        else:
            if mixer == 0:
                o = _mixer_a(xb, prm["a_w_in"][j], prm["a_lambda"][j], prm["a_subln"][j], i)
                w_out = prm["a_w_out"][j]
            elif mixer == 1:
                o = _mixer_b(xb, prm["b_w_in"][j], prm["b_q_norm"][j], prm["b_kv_norm"][j],
                             prm["b_w_uq"][j], prm["b_w_ukv"][j])
                w_out = prm["b_w_out"][j]
            else:
                o = _mixer_c(xb, prm["c_w_qkv"][j], prm["c_rpb"][j])
                w_out = prm["c_w_out"][j]
            mixer_out = (o, w_out.astype(BF), prm["ln1_g"][i], prm["ln1_b"][i])
        x2 = _moe_layer(x2, p[i].reshape(t, -1), router, layers[i], prm["ln2_g"][i], prm["ln2_b"][i], alpha,
                        mixer_out)
    return x2.reshape(bsz, seq, d)


def kernel(x_prompt, x_sample, p_prompt, p_sample, a_w_in, a_lambda, a_subln, a_w_out, b_w_in, b_q_norm, b_kv_norm, b_w_uq, b_w_ukv, b_w_out, c_w_qkv, c_rpb, c_w_out, d_w_qkv, d_w_out, router_w, router_b, moe_w_gate, moe_w_up, moe_w_down, moe_ws_gate, moe_ws_up, moe_ws_down, ln1_g, ln1_b, ln2_g, ln2_b, ple_w_in, ple_w_gate):
    prm = dict(a_w_in=a_w_in, a_lambda=a_lambda, a_subln=a_subln, a_w_out=a_w_out,
               b_w_in=b_w_in, b_q_norm=b_q_norm, b_kv_norm=b_kv_norm, b_w_uq=b_w_uq,
               b_w_ukv=b_w_ukv, b_w_out=b_w_out,
               c_w_qkv=c_w_qkv, c_rpb=c_rpb, c_w_out=c_w_out,
               d_w_qkv=d_w_qkv, d_w_out=d_w_out,
               router_w=router_w, router_b=router_b, moe_w_gate=moe_w_gate, moe_w_up=moe_w_up,
               moe_w_down=moe_w_down, moe_ws_gate=moe_ws_gate, moe_ws_up=moe_ws_up,
               moe_ws_down=moe_ws_down,
               ln1_g=ln1_g, ln1_b=ln1_b, ln2_g=ln2_g, ln2_b=ln2_b,
               ple_w_in=ple_w_in, ple_w_gate=ple_w_gate)
    router, layers = _prep_shared(prm)
    y_prompt = _trunk(x_prompt, p_prompt, prm, router, layers)
    y_sample = _trunk(x_sample, p_sample, prm, router, layers)
    return (y_prompt, y_sample)
```

```python
import functools
import math

import numpy as np
import jax
import jax.numpy as jnp
from jax import lax
from jax.experimental import pallas as pl
from jax.experimental.pallas import tpu as pltpu

BF = jnp.bfloat16
F32 = jnp.float32

VMEM_LIMIT_BYTES = 56 * 1024 * 1024
LANES = 128

GRID_W = 64
LN_EPS = 1e-5
RMS_EPS = 1e-6
ROPE_THETA = 10000.0
NEG = -1e30

A_HEADS, A_HEAD_DIM = 8, 64
B_HEADS, B_NOPE, B_ROPE, B_VDIM, B_Q_LORA, B_KV_LORA = 16, 64, 32, 64, 384, 256
C_HEADS, C_HEAD_DIM, C_WIN_ROWS, C_WIN_COLS = 16, 64, 8, 16
D_HEADS, D_HEAD_DIM = 8, 64
D_GROUPS = ((128, 1), (512, 4), (2048, 16))
N_EXPERTS, N_EXPERT_GROUPS, EXPERT_FF = 16, 4, 256

NT_DIMS = (((1,), (1,)), ((), ()))


def _cparams(*sem):
    return pltpu.CompilerParams(dimension_semantics=sem, vmem_limit_bytes=VMEM_LIMIT_BYTES)


def _layer_norm_rows(z, g, b):
    mu = jnp.mean(z, axis=-1, keepdims=True)
    zc = z - mu
    var = jnp.mean(zc * zc, axis=-1, keepdims=True)
    return zc * lax.rsqrt(var + LN_EPS) * g + b


def _rms_rows(z, g):
    return z * lax.rsqrt(jnp.mean(z * z, axis=-1, keepdims=True) + RMS_EPS) * g


def _linear_body(x_ref, w_ref, o_ref):
    o_ref[...] = jnp.dot(x_ref[...].astype(BF), w_ref[...], preferred_element_type=F32).astype(o_ref.dtype)


def _linear(x, w, out_dtype, tm=512, tn=None):
    m, k = x.shape
    n = w.shape[1]
    tn = n if tn is None else tn
    return pl.pallas_call(
        _linear_body,
        grid=(n // tn, m // tm),
        in_specs=[pl.BlockSpec((tm, k), lambda j, i: (i, 0)), pl.BlockSpec((k, tn), lambda j, i: (0, j))],
        out_specs=pl.BlockSpec((tm, tn), lambda j, i: (i, j)),
        out_shape=jax.ShapeDtypeStruct((m, n), out_dtype),
        compiler_params=_cparams("parallel", "parallel"),
        name="linear",
    )(x, w)


LOG2E = 1.4426950408889634
SUM_ROWS = 16


def _softmax_chunk(s, m, shift, mx=None):
    if mx is None:
        mx = jnp.max(s, axis=0, keepdims=True)
    if shift is not None:
        mx = mx - shift
    m_new = jnp.maximum(m, mx)
    ref = m_new if shift is None else m_new + shift
    return m_new, jnp.exp2(s - ref).astype(BF)


def _pipelined_flash(n, nh, tq, qk_fn, sm_fn, pv_fn, s_buf, p_buf, acc_ref):
    assert n >= 4 and n % 2 == 0
    acc_ref[...] = jnp.zeros_like(acc_ref)

    def scores(j, slot):
        mxs = []
        for h in range(nh):
            s = qk_fn(j, h)
            s_buf[slot, h] = s
            mxs.append(jnp.max(s, axis=0, keepdims=True))
        return tuple(mxs)

    def stage(j, slot, carry, do_pv=True, do_qk=True, first=False):
        ms, alphas, mxs = carry
        if do_pv:
            for h in range(nh):
                acc_ref[h] = alphas[h] * acc_ref[h] + pv_fn(j - 1, h, p_buf[1 - slot, h])
        mxs_next = scores(j + 1, 1 - slot) if do_qk else mxs
        new_ms, new_alphas = [], []
        for h in range(nh):
            m_new, p = sm_fn(j, h, s_buf[slot, h], ms[h], mxs[h], first)
            p_buf[slot, h] = p
            new_alphas.append(jnp.exp2(ms[h] - m_new))
            new_ms.append(m_new)
        return tuple(new_ms), tuple(new_alphas), mxs_next

    mxs = scores(0, 0)
    ms = tuple(jnp.full((1, tq), NEG, F32) for _ in range(nh))
    alphas = tuple(jnp.zeros((1, tq), F32) for _ in range(nh))
    carry = stage(0, 0, (ms, alphas, mxs), do_pv=False, first=True)

    def body(t, carry):
        return stage(2 * t + 2, 0, stage(2 * t + 1, 1, carry))

    carry = lax.fori_loop(0, (n - 2) // 2, body, carry)
    _, alphas, _ = stage(n - 1, 1, carry, do_qk=False)
    for h in range(nh):
        acc_ref[h] = alphas[h] * acc_ref[h] + pv_fn(n - 1, h, p_buf[1, h])


def _chunk_off(j, tk):
    return j * tk if isinstance(j, int) else pl.multiple_of(j * tk, tk)


def _attn_a_body(cs_ref, ctab_ref, lam_ref, g_ref, q_ref, k_ref, kpos_ref, vt_ref, o_ref,
                 s_buf, p_buf, acc_ref, *, seq, tq, tk, hps, lam_init):
    hg = pl.program_id(1)
    qi = pl.program_id(2)
    n = seq // tk
    cd = (qi * tq) // tk
    lane = lax.broadcasted_iota(jnp.int32, (tq, LANES), 1)
    qpos = (qi * tq + lax.broadcasted_iota(jnp.int32, (1, tq), 1)).astype(F32)
    dv = vt_ref.shape[2] - SUM_ROWS
    c_slope, f0, q_left, q_right = [], [], [], []
    for hh in range(hps):
        head = hg * hps + hh
        c_slope.append(cs_ref[head])
        f0.append(c_slope[hh] * qpos)
        q = q_ref[0, :, hh * LANES:(hh + 1) * LANES]
        qaug = jnp.broadcast_to(ctab_ref[pl.ds(head, 1), :], (tq, LANES)).astype(BF)
        for half in range(2):
            q_m = jnp.where((lane >= 64 * half) & (lane < 64 * (half + 1)), q, jnp.zeros_like(q))
            q_left.append(jnp.concatenate([q_m, qaug], axis=1))
            q_right.append(jnp.concatenate([q_m, -qaug], axis=1))

    def chunk_of(j):
        jm = j - 1
        c = jnp.where(j == 0, cd, jm + (jm >= cd).astype(jnp.int32))
        return c, c <= cd

    def qk_fn(j, mm):
        hh = mm // 2
        c, left = chunk_of(j)
        off = pl.multiple_of(c * tk, tk)
        k_c = jnp.concatenate([k_ref[0, pl.ds(off, tk), hh * LANES:(hh + 1) * LANES],
                               kpos_ref[pl.ds(off, tk), :]], axis=1)
        q_full = jnp.where(left, q_left[mm], q_right[mm])
        return lax.dot_general(k_c, q_full, NT_DIMS, preferred_element_type=F32)

    def sm_fn(j, mm, s, m, mx, first):
        hh = mm // 2
        if first:
            d = (lax.broadcasted_iota(jnp.int32, (tk, tq), 0) - lax.broadcasted_iota(jnp.int32, (tk, tq), 1)
                 + (cd * tk - qi * tq)).astype(F32)
            return _softmax_chunk(s - (2.0 * c_slope[hh]) * jnp.maximum(d, 0.0), m, f0[hh])
        _, left = chunk_of(j)
        return _softmax_chunk(s, m, jnp.where(left, f0[hh], -f0[hh]), mx)

    def pv_fn(j, mm, p):
        c, _ = chunk_of(j)
        off = pl.multiple_of(c * tk, tk)
        return jnp.dot(vt_ref[0, mm // 2, :, pl.ds(off, tk)], p, preferred_element_type=F32)

    _pipelined_flash(n, 2 * hps, tq, qk_fn, sm_fn, pv_fn, s_buf, p_buf, acc_ref)
    lf = lam_ref[...]
    lam_full = (jnp.exp(jnp.sum(lf[0:1] * lf[1:2], axis=-1, keepdims=True))
                - jnp.exp(jnp.sum(lf[2:3] * lf[3:4], axis=-1, keepdims=True)) + lam_init)
    outs = []
    for hh in range(hps):
        o0, o1 = [acc_ref[2 * hh + half, :dv, :] / acc_ref[2 * hh + half, dv:dv + 1, :] for half in range(2)]
        o = o0 - lam_full * o1
        ms = jnp.mean(o * o, axis=0, keepdims=True)
        outs.append(o * lax.rsqrt(ms + RMS_EPS) * g_ref[...] * (1.0 - lam_init))
    o_ref[0] = jnp.concatenate(outs, axis=0).T.astype(o_ref.dtype)


def _bf16_pieces(c, n=3):
    pieces, rest = [], c.astype(F32)
    for _ in range(n):
        p = rest.astype(BF)
        pieces.append(p.astype(F32))
        rest = rest - p.astype(F32)
    return pieces


def _store_vt_heads(vt_ref, vt, dv, heads_per_slot):
    tm = vt.shape[1]
    tail = jnp.where(lax.broadcasted_iota(jnp.int32, (SUM_ROWS, tm), 0) == 0, 1.0, 0.0).astype(vt_ref.dtype)
    rows = dv + SUM_ROWS
    for h in range(vt.shape[0] // dv):
        slot, r0 = h // heads_per_slot, (h % heads_per_slot) * rows
        vt_ref[0, slot, r0:r0 + dv, :] = vt[h * dv:(h + 1) * dv, :].astype(vt_ref.dtype)
        vt_ref[0, slot, r0 + dv:r0 + rows, :] = tail


def _proj_a_body(x_ref, wqk_ref, wvt_ref, qk_ref, vt_ref):
    xb = x_ref[...].astype(BF)
    qk_ref[...] = jnp.dot(xb, wqk_ref[...], preferred_element_type=F32).astype(BF)
    vt = lax.dot_general(wvt_ref[...], xb, NT_DIMS, preferred_element_type=F32)
    _store_vt_heads(vt_ref, vt, 2 * A_HEAD_DIM, 1)


def _flash_chunk_len(seq, tk_max):
    for min_chunks in (8, 4):
        tk = tk_max
        while tk >= 2 * LANES and (seq % tk or (seq // tk) < min_chunks or (seq // tk) % 2):
            tk //= 2
        if tk >= 2 * LANES:
            return tk
    raise ValueError(f"sequence length {seq} too short for the pipelined attention kernel")


def _flash_scratch(nh, rows, tq, tk):
    return [pltpu.VMEM((2, nh, tk, tq), F32), pltpu.VMEM((2, nh, tk, tq), BF), pltpu.VMEM((nh, rows, tq), F32)]


def _attn_a(qkv, vt, lam, subln_g, lam_init, tq=256, tk=1024):
    bsz, seq, _ = qkv.shape
    tq, tk = min(tq, seq), _flash_chunk_len(seq, tk)
    c_slope = jnp.asarray(2.0 ** (-8.0 * np.arange(1, A_HEADS + 1) / A_HEADS) * LOG2E, dtype=F32)
    c1, c2, c3 = _bf16_pieces(c_slope)
    ctab = jnp.stack([c1, c1, c2, c2, c3, c3], axis=-1)
    ctab = jnp.concatenate([ctab, jnp.zeros((A_HEADS, LANES - 6), F32)], axis=-1)
    pos = np.arange(seq)
    kpos = np.zeros((seq, LANES), np.float32)
    for j in range(3):
        kpos[:, 2 * j] = (pos // LANES) * LANES
        kpos[:, 2 * j + 1] = pos % LANES
    kpos = jnp.asarray(kpos, BF)
    vrows = vt.shape[2]
    hps = 2 if seq <= 4096 else 1
    ngrp = A_HEADS // hps
    body = functools.partial(_attn_a_body, seq=seq, tq=tq, tk=tk, hps=hps, lam_init=lam_init)
    return pl.pallas_call(
        body,
        grid=(bsz, ngrp, seq // tq),
        in_specs=[
            pl.BlockSpec(memory_space=pltpu.SMEM),
            pl.BlockSpec((A_HEADS, LANES), lambda b, h, i: (0, 0)),
            pl.BlockSpec((4, A_HEAD_DIM), lambda b, h, i: (0, 0)),
            pl.BlockSpec((2 * A_HEAD_DIM, 1), lambda b, h, i: (0, 0)),
            pl.BlockSpec((1, tq, hps * LANES), lambda b, h, i: (b, i, h)),
            pl.BlockSpec((1, seq, hps * LANES), lambda b, h, i: (b, 0, ngrp + h)),
            pl.BlockSpec((seq, LANES), lambda b, h, i: (0, 0)),
            pl.BlockSpec((1, hps, vrows, seq), lambda b, h, i: (b, h, 0, 0)),
        ],
        out_specs=pl.BlockSpec((1, tq, hps * LANES), lambda b, h, i: (b, i, h)),
        out_shape=jax.ShapeDtypeStruct((bsz, seq, A_HEADS * LANES), BF),
        scratch_shapes=_flash_scratch(2 * hps, vrows, tq, tk),
        compiler_params=_cparams("parallel", "parallel", "arbitrary"),
        name="attn_a",
    )(c_slope, ctab, lam.astype(F32), subln_g.astype(F32).reshape(2 * A_HEAD_DIM, 1), qkv, qkv, kpos, vt)


def _attn_b_body(q_ref, k_ref, vt_ref, o_ref, s_buf, p_buf, acc_ref, *, seq, tk, nh):
    tq = q_ref.shape[1]
    rows = B_VDIM + SUM_ROWS
    qs = [q_ref[0, :, h * LANES:(h + 1) * LANES] for h in range(nh)]

    def qk_fn(j, h):
        k_c = k_ref[0, pl.ds(_chunk_off(j, tk), tk), h * LANES:(h + 1) * LANES]
        return lax.dot_general(k_c, qs[h], NT_DIMS, preferred_element_type=F32)

    def sm_fn(j, h, s, m, mx, first):
        return _softmax_chunk(s, m, None, mx)

    def pv_fn(j, h, p):
        vt_c = vt_ref[0, h // 2, (h % 2) * rows:(h % 2 + 1) * rows, pl.ds(_chunk_off(j, tk), tk)]
        return jnp.dot(vt_c, p, preferred_element_type=F32)

    _pipelined_flash(seq // tk, nh, tq, qk_fn, sm_fn, pv_fn, s_buf, p_buf, acc_ref)
    outs = [acc_ref[h, :B_VDIM, :] / acc_ref[h, B_VDIM:B_VDIM + 1, :] for h in range(nh)]
    o = jnp.concatenate(outs, axis=0)
    o_ref[0] = o.T.astype(o_ref.dtype)


def _attn_b(q, k, vt, tq=256, tk=1024):
    bsz, seq, _ = q.shape
    tq, tk = min(tq, seq), _flash_chunk_len(seq, tk)
    nh = 4 if seq <= 4096 else 2
    return pl.pallas_call(
        functools.partial(_attn_b_body, seq=seq, tk=tk, nh=nh),
        grid=(bsz, B_HEADS // nh, seq // tq),
        in_specs=[
            pl.BlockSpec((1, tq, nh * LANES), lambda b, h, i: (b, i, h)),
            pl.BlockSpec((1, seq, nh * LANES), lambda b, h, i: (b, 0, h)),
            pl.BlockSpec((1, nh // 2, 2 * (B_VDIM + SUM_ROWS), seq), lambda b, h, i: (b, h, 0, 0)),
        ],
        out_specs=pl.BlockSpec((1, tq, nh * B_VDIM), lambda b, h, i: (b, i, h)),
        out_shape=jax.ShapeDtypeStruct((bsz, seq, B_HEADS * B_VDIM), BF),
        scratch_shapes=_flash_scratch(nh, B_VDIM + SUM_ROWS, tq, tk),
        compiler_params=_cparams("parallel", "parallel", "arbitrary"),
        name="attn_b",
    )(q, k, vt)


def _mla_in_body(x_ref, w_ref, gq_ref, gkv_ref, rc_ref, rs_ref, cq_ref, ckv_ref, kr_ref):
    h = jnp.dot(x_ref[...].astype(BF), w_ref[...], preferred_element_type=F32)
    cq_ref[...] = _rms_rows(h[:, :B_Q_LORA], gq_ref[...]).astype(BF)
    ckv_ref[...] = _rms_rows(h[:, B_Q_LORA:B_Q_LORA + B_KV_LORA], gkv_ref[...]).astype(BF)
    t = h[:, B_Q_LORA + B_KV_LORA:]
    kr = t * rc_ref[...] + pltpu.roll(t, LANES - B_ROPE, 1) * rs_ref[...]
    kr_ref[...] = kr.astype(BF)


def _mla_q_body(cq_ref, w_ref, ta_ref, tb_ref, q_ref):
    t = jnp.dot(cq_ref[...], w_ref[...], preferred_element_type=F32)
    ta = ta_ref[...]
    tb = tb_ref[...]
    for h in range(B_HEADS):
        th = t[:, h * LANES:(h + 1) * LANES]
        q_ref[:, h * LANES:(h + 1) * LANES] = (th * ta + pltpu.roll(th, LANES - B_ROPE, 1) * tb).astype(BF)


def _mla_kv_body(ckv_ref, kr_ref, wk_ref, e_ref, wvt_ref, k_ref, vt_ref):
    ckv = ckv_ref[...]
    k = jnp.dot(ckv, wk_ref[...], preferred_element_type=F32)
    k = k + jnp.dot(kr_ref[...], e_ref[...], preferred_element_type=F32)
    k_ref[...] = k.astype(BF)
    vt = lax.dot_general(wvt_ref[...], ckv, NT_DIMS, preferred_element_type=F32)
    _store_vt_heads(vt_ref, vt, B_VDIM, 2)


def _rope_partner(w):
    half = B_ROPE // 2
    return jnp.concatenate([-w[..., half:], w[..., :half]], axis=-1)


def _mla_tables(seq):
    inv = 1.0 / (ROPE_THETA ** (np.arange(0, B_ROPE, 2, dtype=np.float32) / B_ROPE))
    ang = jnp.arange(seq, dtype=F32)[:, None] * jnp.asarray(inv, F32)[None, :]
    cos = jnp.concatenate([jnp.cos(ang), jnp.cos(ang)], axis=-1)
    sin = jnp.concatenate([jnp.sin(ang), jnp.sin(ang)], axis=-1)
    z = lambda n: jnp.zeros((seq, n), F32)
    rc = jnp.concatenate([cos, z(LANES - B_ROPE)], axis=-1)
    rs = jnp.concatenate([sin, z(LANES - B_ROPE)], axis=-1)
    scale = (B_NOPE + B_ROPE) ** -0.5 * LOG2E
    ta =jnp.concatenate([jnp.full((seq, B_NOPE), scale, F32), cos * scale, z(B_ROPE)], axis=-1)
    tb = jnp.concatenate([z(B_NOPE), sin * scale, z(B_ROPE)], axis=-1)
    return rc, rs, ta, tb


def _mla_weights(w_in, w_uq, w_ukv):
    d = w_in.shape[0]
    kr0 = B_Q_LORA + B_KV_LORA
    w_in_ext = jnp.concatenate(
        [w_in, _rope_partner(w_in[:, kr0:kr0 + B_ROPE]), jnp.zeros((d, LANES - 2 * B_ROPE), w_in.dtype)], axis=-1)
    wq = w_uq.reshape(B_Q_LORA, B_HEADS, B_NOPE + B_ROPE)
    wq_ext = jnp.concatenate([wq, _rope_partner(wq[..., B_NOPE:])], axis=-1).reshape(B_Q_LORA, B_HEADS * LANES)
    wkv = w_ukv.reshape(B_KV_LORA, B_HEADS, B_NOPE + B_VDIM)
    wk = jnp.concatenate([wkv[..., :B_NOPE], jnp.zeros((B_KV_LORA, B_HEADS, LANES - B_NOPE), w_ukv.dtype)], axis=-1)
    wk = wk.reshape(B_KV_LORA, B_HEADS * LANES)
    wv = wkv[..., B_NOPE:].reshape(B_KV_LORA, B_HEADS * B_VDIM)
    place = np.zeros((LANES, B_HEADS, LANES), np.float32)
    for j in range(B_ROPE):
        place[j, :, B_NOPE + j] = 1.0
    place = jnp.asarray(place.reshape(LANES, B_HEADS * LANES), BF)
    return w_in_ext.astype(BF), wq_ext.astype(BF), wk.astype(BF), place, wv.astype(BF)


def _mixer_b(x, w_in, q_norm_g, kv_norm_g, w_uq, w_ukv, tm=512):
    bsz, seq, d = x.shape
    t = bsz * seq
    tm = min(tm, seq)
    nblk = seq // tm
    w_in_ext, wq_ext, wk, place, wv = _mla_weights(w_in, w_uq, w_ukv)
    rc, rs, ta, tb = _mla_tables(seq)
    x2 = x.reshape(t, d)
    n_in = w_in_ext.shape[1]
    row = lambda i: (i, 0)
    fixed = lambda i: (0, 0)
    pos = lambda i: (i % nblk, 0)
    cq, ckv, kr = pl.pallas_call(
        _mla_in_body,
        grid=(t // tm,),
        in_specs=[
            pl.BlockSpec((tm, d), row), pl.BlockSpec((d, n_in), fixed),
            pl.BlockSpec((1, B_Q_LORA), fixed), pl.BlockSpec((1, B_KV_LORA), fixed),
            pl.BlockSpec((tm, LANES), pos), pl.BlockSpec((tm, LANES), pos),
        ],
        out_specs=[pl.BlockSpec((tm, B_Q_LORA), row), pl.BlockSpec((tm, B_KV_LORA), row),
                   pl.BlockSpec((tm, LANES), row)],
        out_shape=[jax.ShapeDtypeStruct((t, B_Q_LORA), BF), jax.ShapeDtypeStruct((t, B_KV_LORA), BF),
                   jax.ShapeDtypeStruct((t, LANES), BF)],
        compiler_params=_cparams("parallel"),
        name="mla_in",
    )(x2, w_in_ext, q_norm_g.astype(F32).reshape(1, -1), kv_norm_g.astype(F32).reshape(1, -1), rc, rs)
    nq = B_HEADS * LANES
    q = pl.pallas_call(
        _mla_q_body,
        grid=(t // tm,),
        in_specs=[pl.BlockSpec((tm, B_Q_LORA), row), pl.BlockSpec((B_Q_LORA, nq), fixed),
                  pl.BlockSpec((tm, LANES), pos), pl.BlockSpec((tm, LANES), pos)],
        out_specs=pl.BlockSpec((tm, nq), row),
        out_shape=jax.ShapeDtypeStruct((t, nq), BF),
        compiler_params=_cparams("parallel"),
        name="mla_q",
    )(cq, wq_ext, ta, tb)
    nv = B_HEADS * B_VDIM
    npair = B_HEADS // 2
    vrows = 2 * (B_VDIM + SUM_ROWS)
    k, vt = pl.pallas_call(
        _mla_kv_body,
        grid=(t // tm,),
        in_specs=[pl.BlockSpec((tm, B_KV_LORA), row), pl.BlockSpec((tm, LANES), row),
                  pl.BlockSpec((B_KV_LORA, nq), fixed), pl.BlockSpec((LANES, nq), fixed),
                  pl.BlockSpec((nv, B_KV_LORA), fixed)],
        out_specs=[pl.BlockSpec((tm, nq), row),
                   pl.BlockSpec((1, npair, vrows, tm), lambda i: (i // nblk, 0, 0, i % nblk))],
        out_shape=[jax.ShapeDtypeStruct((t, nq), BF), jax.ShapeDtypeStruct((bsz, npair, vrows, seq), BF)],
        compiler_params=_cparams("parallel"),
        name="mla_kv",
    )(ckv, kr, wk, place, wv.T)
    o = _attn_b(q.reshape(bsz, seq, nq), k.reshape(bsz, seq, nq), vt)
    return o.reshape(t, nv)


def _mixer_a(x, w_in, lam, subln_g, layer_idx):
    bsz, seq, d = x.shape
    t = bsz * seq
    hd2 = 2 * A_HEAD_DIM
    nq = A_HEADS * hd2
    scale = A_HEAD_DIM ** -0.5 * LOG2E
    wqk = jnp.concatenate([w_in[:, :nq] * scale, w_in[:, nq:2 * nq]], axis=-1).astype(BF)
    wvt = w_in[:, 2 * nq:].T.astype(BF)
    tm = min(512, seq)
    nblk = seq // tm
    vrows = hd2 + SUM_ROWS
    qk, vt = pl.pallas_call(
        _proj_a_body,
        grid=(t // tm,),
        in_specs=[pl.BlockSpec((tm, d), lambda i: (i, 0)), pl.BlockSpec((d, 2 * nq), lambda i: (0, 0)),
                  pl.BlockSpec((nq, d), lambda i: (0, 0))],
        out_specs=[pl.BlockSpec((tm, 2 * nq), lambda i: (i, 0)),
                   pl.BlockSpec((1, A_HEADS, vrows, tm), lambda i: (i // nblk, 0, 0, i % nblk))],
        out_shape=[jax.ShapeDtypeStruct((t, 2 * nq), BF), jax.ShapeDtypeStruct((bsz, A_HEADS, vrows, seq), BF)],
        compiler_params=_cparams("parallel"),
        name="proj_a",
    )(x.reshape(t, d), wqk, wvt)
    lam_init = 0.8 - 0.6 * math.exp(-0.3 * layer_idx)
    o = _attn_a(qk.reshape(bsz, seq, 2 * nq), vt, lam, subln_g, lam_init)
    return o.reshape(t, nq)


def _attn_c_body(q_ref, k0, k1, k2, k3, v0, v1, v2, v3, bt_ref, o_ref):
    kw = jnp.concatenate([k0[0], k1[0], k2[0], k3[0]], axis=0)
    vw = jnp.concatenate([v0[0], v1[0], v2[0], v3[0]], axis=0)
    nq = q_ref.shape[1] // 2
    nk = bt_ref.shape[3]
    lane = lax.broadcasted_iota(jnp.int32, (nq, LANES), 1)
    for hf in range(2):
        q = q_ref[0, hf * nq:(hf + 1) * nq, :]
        kh = kw[hf * nq:hf * nq + nk]
        vh = vw[hf * nq:hf * nq + nk]
        outs = []
        for hh in range(2):
            qm = jnp.where((lane >= 64 * hh) & (lane < 64 * (hh + 1)), q, jnp.zeros_like(q))
            s = (lax.dot_general(qm, kh, NT_DIMS, preferred_element_type=F32)
                 + bt_ref[hh, 0, hf * nq:(hf + 1) * nq, :])
            m = jnp.max(s, axis=1, keepdims=True)
            p = jnp.exp2(s - m)
            l = jnp.sum(p, axis=1, keepdims=True)
            outs.append(jnp.dot(p.astype(BF), vh, preferred_element_type=F32) / l)
        o_ref[0, hf * nq:(hf + 1) * nq, :] = jnp.where(lane < 64, outs[0], outs[1]).astype(o_ref.dtype)


def _nbr_bias_table(rpb):
    col = np.arange(GRID_W)
    col_start = np.clip(col - C_WIN_COLS // 2, 0, GRID_W - C_WIN_COLS)
    col_mask = (col[None, :] >= col_start[:, None]) & (col[None, :] < col_start[:, None] + C_WIN_COLS)
    pad = GRID_W - C_WIN_COLS
    ext = jnp.pad(rpb.astype(F32) * LOG2E, ((0, 0), (0, 0), (pad, pad)), mode="edge")
    toep = jnp.stack([ext[:, :, GRID_W - 1 - qc:2 * GRID_W - 1 - qc] for qc in range(GRID_W)], axis=2)
    toep = jnp.where(jnp.asarray(col_mask)[None, None], toep, NEG)
    neg = jnp.full((rpb.shape[0], GRID_W, GRID_W), NEG, F32)
    half = C_WIN_ROWS // 2
    kinds = []
    for kind in range(3):
        qrows = []
        for t in range(C_WIN_ROWS):
            u0 = (max(t, half), t, min(t, half))[kind]
            ubase = 0 if t < half else half
            blocks = [toep[:, u - t + half - 1] if u0 <= u < u0 + C_WIN_ROWS else neg
                      for u in range(ubase, ubase + C_WIN_ROWS + half)]
            qrows.append(jnp.concatenate(blocks, axis=-1))
        kinds.append(jnp.concatenate(qrows, axis=1))
    return jnp.stack(kinds, axis=1)


def _mixer_c(x, w_qkv, rpb):
    bsz, seq, d = x.shape
    t = bsz * seq
    nq = C_HEADS * C_HEAD_DIM
    scale = C_HEAD_DIM ** -0.5 * LOG2E
    w = jnp.concatenate([w_qkv[:, :nq] * scale, w_qkv[:, nq:]], axis=-1).astype(BF)
    qkv = _linear(x.reshape(t, d), w, BF, tn=nq).reshape(bsz, seq, 3 * nq)
    bt = _nbr_bias_table(rpb)
    npair = C_HEADS // 2
    qtok = C_WIN_ROWS * GRID_W
    ngrp = seq // qtok
    assert ngrp >= 2
    kb = qtok // 2
    nkb = seq // kb
    kspecs = []
    for off in (npair, 2 * npair):
        for j in range(4):
            kspecs.append(pl.BlockSpec(
                (1, kb, LANES),
                lambda h, g, b, j=j, off=off: (b, jnp.clip(2 * g - 1 + j, 0, nkb - 1), off + h)))
    kind = lambda g: jnp.where(g == 0, 0, jnp.where(g == ngrp - 1, 2, 1))
    o = pl.pallas_call(
        _attn_c_body,
        grid=(npair, ngrp, bsz),
        in_specs=[pl.BlockSpec((1, qtok, LANES), lambda h, g, b: (b, g, h))] + kspecs + [
            pl.BlockSpec((2, 1, qtok, bt.shape[3]), lambda h, g, b: (h, kind(g), 0, 0))],
        out_specs=pl.BlockSpec((1, qtok, LANES), lambda h, g, b: (b, g, h)),
        out_shape=jax.ShapeDtypeStruct((bsz, seq, nq), BF),
        compiler_params=_cparams("parallel", "parallel", "arbitrary"),
        name="attn_c",
    )(qkv, *([qkv] * 8), bt)
    return o.reshape(t, nq)


def _attn_d_body(slopes_ref, q_ref, kp, kc, kn, vp, vc, vn, o_ref, lse_ref, *, tq, length, dil, rad):
    i = pl.program_id(2)
    nk = tq + 2 * rad
    qi = lax.broadcasted_iota(jnp.int32, (tq, nk), 0)
    kk = lax.broadcasted_iota(jnp.int32, (tq, nk), 1)
    dist = jnp.abs(kk - rad - qi)
    ki = i * tq - rad + kk
    valid = (dist <= rad) & (ki >= 0) & (ki < length)
    distf = dist.astype(F32) * float(dil)
    lane = lax.broadcasted_iota(jnp.int32, (tq, LANES), 1)
    for hp in range(D_HEADS // 2):
        sl = slice(hp * LANES, (hp + 1) * LANES)
        q = q_ref[0, 0, :, sl]
        kw = jnp.concatenate([kp[0, 0, tq - rad:tq, sl], kc[0, 0, :, sl], kn[0, 0, 0:rad, sl]], axis=0)
        vw = jnp.concatenate([vp[0, 0, tq - rad:tq, sl], vc[0, 0, :, sl], vn[0, 0, 0:rad, sl]], axis=0)
        outs, lses = [], []
        for hh in range(2):
            slope = slopes_ref[hp * 2 + hh]
            qm = jnp.where((lane >= 64 * hh) & (lane < 64 * (hh + 1)), q, jnp.zeros_like(q))
            s = lax.dot_general(qm, kw, NT_DIMS, preferred_element_type=F32)
            s = jnp.where(valid, s - slope * distf, NEG)
            m = jnp.max(s, axis=1, keepdims=True)
            p = jnp.exp2(s - m)
            l = jnp.sum(p, axis=1, keepdims=True)
            outs.append(jnp.dot(p.astype(BF), vw, preferred_element_type=F32) / l)
            lses.append(m + jnp.log(l) * LOG2E)
        o_ref[0, 0, :, sl] = jnp.where(lane < 64, outs[0], outs[1]).astype(o_ref.dtype)
        lse_ref[0, 0, :, sl] = jnp.where(lane < 64, lses[0], lses[1])


def _attn_d_group(qkv, window, dil, tq=256):
    bsz, _, length, ncol = qkv.shape
    rad = window // (2 * dil)
    tq = min(tq, length)
    nq = length // tq
    nh = ncol // 3
    slopes = jnp.asarray(2.0 ** (-8.0 * np.arange(1, D_HEADS + 1) / D_HEADS) * LOG2E, dtype=F32)

    def spec(which, shift):
        return pl.BlockSpec((1, 1, tq, nh), lambda b, c, i: (b, c, jnp.clip(i + shift, 0, nq - 1), which))

    ospec = pl.BlockSpec((1, 1, tq, nh), lambda b, c, i: (b, c, i, 0))
    return pl.pallas_call(
        functools.partial(_attn_d_body, tq=tq, length=length, dil=dil, rad=rad),
        grid=(bsz, dil, nq),
        in_specs=[pl.BlockSpec(memory_space=pltpu.SMEM), spec(0, 0),
                  spec(1, -1), spec(1, 0), spec(1, 1), spec(2, -1), spec(2, 0), spec(2, 1)],
        out_specs=[ospec, ospec],
        out_shape=[jax.ShapeDtypeStruct((bsz, dil, length, nh), BF),
                   jax.ShapeDtypeStruct((bsz, dil, length, nh), F32)],
        compiler_params=_cparams("parallel", "parallel", "arbitrary"),
        name=f"attn_d{dil}",
    )(slopes, qkv, qkv, qkv, qkv, qkv, qkv, qkv)


def _proj_d_body(x_ref, w_ref, o0, o1, o2, scr, *, dils):
    xb = x_ref[...].astype(BF)
    tm = xb.shape[0]
    nblk = scr.shape[0]
    ncol = nblk * LANES
    for gi, (o_ref, dil) in enumerate(zip((o0, o1, o2), dils)):
        res = jnp.dot(xb, w_ref[:, gi * ncol:(gi + 1) * ncol], preferred_element_type=F32)
        if dil == 1:
            o_ref[0, 0] = res.astype(BF)
            continue
        for j in range(nblk):
            scr[j] = res[:, j * LANES:(j + 1) * LANES]
        for c in range(dil):
            for j in range(nblk):
                o_ref[0, c, :, j * LANES:(j + 1) * LANES] = scr[j, pl.ds(c, tm // dil, stride=dil), :].astype(BF)


def _post_d_body(o0, o1, o2, l0, l1, l2, x_ref, w_ref, g_ref, b_ref, y_ref, scr_o, scr_l, *, alpha, dils):
    tm = x_ref.shape[0]

    def interleaved(ref, scr, dil):
        if dil == 1:
            return ref[0, 0].astype(F32)
        nblk = scr.shape[0]
        for c in range(dil):
            for j in range(nblk):
                scr[j, pl.ds(c, tm // dil, stride=dil), :] = ref[0, c, :, j * LANES:(j + 1) * LANES].astype(F32)
        return jnp.concatenate([scr[j] for j in range(nblk)], axis=1)

    ov = [interleaved(r, scr_o.at[j], dil) for j, (r, dil) in enumerate(zip((o0, o1, o2), dils))]
    a0, a1, a2 = [interleaved(r, scr_l.at[j], dil) for j, (r, dil) in enumerate(zip((l0, l1, l2), dils))]
    m = jnp.maximum(jnp.maximum(a0, a1), a2)
    e0, e1, e2 = jnp.exp2(a0 - m), jnp.exp2(a1 - m), jnp.exp2(a2 - m)
    den = e0 + e1 + e2
    o = (e0 / den) * ov[0] + (e1 / den) * ov[1] + (e2 / den) * ov[2]
    h = jnp.dot(o.astype(BF), w_ref[...], preferred_element_type=F32)
    y_ref[...] = _layer_norm_rows(alpha * x_ref[...] + h, g_ref[...], b_ref[...])


def _mixer_d_and_post(x2, bsz, seq, w_qkv, w_out, g, b, alpha, tm=512):
    t, d = x2.shape
    nh = D_HEADS * D_HEAD_DIM
    ng = len(D_GROUPS)
    dils = tuple(dil for _, dil in D_GROUPS)
    tm = min(tm, seq)
    nblk = seq // tm
    scale = D_HEAD_DIM ** -0.5 * LOG2E
    wq = w_qkv.reshape(d, 3, ng, nh)
    wq = jnp.stack([wq[:, 0] * scale, wq[:, 1], wq[:, 2]], axis=1)
    w = wq.transpose(0, 2, 1, 3).reshape(d, ng * 3 * nh).astype(BF)

    def deint_spec(dil, width):
        return pl.BlockSpec((1, dil, tm // dil, width), lambda i: (i // nblk, 0, i % nblk, 0))

    qkvs = pl.pallas_call(
        functools.partial(_proj_d_body, dils=dils),
        grid=(t // tm,),
        in_specs=[pl.BlockSpec((tm, d), lambda i: (i, 0)), pl.BlockSpec((d, ng * 3 * nh), lambda i: (0, 0))],
        out_specs=[deint_spec(dil, 3 * nh) for dil in dils],
        out_shape=[jax.ShapeDtypeStruct((bsz, dil, seq // dil, 3 * nh), BF) for dil in dils],
        scratch_shapes=[pltpu.VMEM((3 * nh // LANES, tm, LANES), F32)],
        compiler_params=_cparams("parallel"),
        name="proj_d",
    )(x2, w)
    os_, ls_ = [], []
    for qkv_g, (window, dil) in zip(qkvs, D_GROUPS):
        o, lse = _attn_d_group(qkv_g, window, dil)
        os_.append(o)
        ls_.append(lse)
    row = lambda i: (i, 0)
    fixed = lambda i: (0, 0)
    return pl.pallas_call(
        functools.partial(_post_d_body, alpha=alpha, dils=dils),
        grid=(t // tm,),
        in_specs=[deint_spec(dil, nh) for dil in dils] * 2 + [
            pl.BlockSpec((tm, d), row), pl.BlockSpec((nh, d), fixed),
            pl.BlockSpec((1, d), fixed), pl.BlockSpec((1, d), fixed)],
        out_specs=pl.BlockSpec((tm, d), row),
        out_shape=jax.ShapeDtypeStruct((t, d), F32),
        scratch_shapes=[pltpu.VMEM((ng, nh // LANES, tm, LANES), F32), pltpu.VMEM((ng, nh // LANES, tm, LANES), F32)],
        compiler_params=_cparams("parallel"),
        name="post_d",
    )(*os_, *ls_, x2, w_out.astype(BF), g.reshape(1, d), b.reshape(1, d))


def _router_gates(x, w2_ref, rb_ref):
    tm = x.shape[0]
    xh = x.astype(BF)
    xl = (x - xh.astype(F32)).astype(BF)
    w2 = w2_ref[...]
    both = lax.dot_general(w2, xh, NT_DIMS, preferred_element_type=F32)
    logits = (both[:N_EXPERTS] + both[N_EXPERTS:]
              + lax.dot_general(w2[:N_EXPERTS], xl, NT_DIMS, preferred_element_type=F32))
    scores = jax.nn.sigmoid(logits)
    biased = scores + rb_ref[...]
    epg = N_EXPERTS // N_EXPERT_GROUPS
    sc = [scores[e:e + 1, :] for e in range(N_EXPERTS)]
    bi = [biased[e:e + 1, :] for e in range(N_EXPERTS)]
    gs = []
    for g in range(N_EXPERT_GROUPS):
        v = bi[g * epg:(g + 1) * epg]
        best = None
        for a in range(epg):
            for c in range(a + 1, epg):
                pair = v[a] + v[c]
                best = pair if best is None else jnp.maximum(best, pair)
        gs.append(best)
    gmax = functools.reduce(jnp.maximum, gs)
    taken = jnp.zeros((1, tm), jnp.bool_)
    cand = []
    for g in range(N_EXPERT_GROUPS):
        sel = (gs[g] == gmax) & jnp.logical_not(taken)
        taken = taken | sel
        for a in range(epg):
            cand.append(jnp.where(sel, bi[g * epg + a], -jnp.inf))
    m1 = functools.reduce(jnp.maximum, cand)
    taken = jnp.zeros((1, tm), jnp.bool_)
    is1 = []
    for e in range(N_EXPERTS):
        hit = (cand[e] == m1) & jnp.logical_not(taken)
        taken = taken | hit
        is1.append(hit)
    cand2 = [jnp.where(is1[e], -jnp.inf, cand[e]) for e in range(N_EXPERTS)]
    m2 = functools.reduce(jnp.maximum, cand2)
    taken = jnp.zeros((1, tm), jnp.bool_)
    is2 = []
    for e in range(N_EXPERTS):
        hit = (cand2[e] == m2) & jnp.logical_not(taken)
        taken = taken | hit
        is2.append(hit)
    zero = jnp.zeros((1, tm), F32)
    w1 = functools.reduce(jnp.add, [jnp.where(is1[e], sc[e], zero) for e in range(N_EXPERTS)])
    w2 = functools.reduce(jnp.add, [jnp.where(is2[e], sc[e], zero) for e in range(N_EXPERTS)])
    den = w1 + w2
    rows = [jnp.where(is1[e], w1 / den, zero) + jnp.where(is2[e], w2 / den, zero) for e in range(N_EXPERTS)]
    rows.append(jnp.zeros((LANES - N_EXPERTS, tm), F32))
    return jnp.concatenate(rows, axis=0).T


MOE_EXPERTS_PER_STEP = 2


def _moe_body(*refs, alpha, n_routed_steps, with_mixer_out):
    if with_mixer_out:
        o_ref, wo_ref, g1_ref, b1_ref, *refs = refs
    (x_ref, p_ref, w2_ref, rb_ref, wgu_ref, wd_ref, wgus_ref, wds_ref, g_ref, b_ref, wpi_ref, wpg_ref,
     y_ref, gates, xb, acc, x1) = refs
    e = pl.program_id(1)
    ff = EXPERT_FF

    def hidden(gu, j):
        return jax.nn.silu(gu[:, 2 * j * ff:(2 * j + 1) * ff]) * gu[:, (2 * j + 1) * ff:(2 * j + 2) * ff]

    @pl.when(e == 0)
    def _():
        x = x_ref[...]
        if with_mixer_out:
            h = jnp.dot(o_ref[...], wo_ref[...], preferred_element_type=F32)
            x = _layer_norm_rows(alpha * x + h, g1_ref[...], b1_ref[...])
        x1[...] = x
        gates[...] = _router_gates(x, w2_ref, rb_ref)
        xb[...] = x.astype(BF)
        acc[...] = jnp.zeros_like(acc)

    @pl.when(e < n_routed_steps)
    def _():
        gu = jnp.dot(xb[...], wgu_ref[0], preferred_element_type=F32)
        lane = lax.broadcasted_iota(jnp.int32, gates.shape, 1)
        gt = gates[...]
        hs = []
        for j in range(MOE_EXPERTS_PER_STEP):
            gcol = jnp.sum(jnp.where(lane == e * MOE_EXPERTS_PER_STEP + j, gt, 0.0), axis=1, keepdims=True)
            hs.append((hidden(gu, j) * gcol).astype(BF))
        acc[...] += jnp.dot(jnp.concatenate(hs, axis=1), wd_ref[0], preferred_element_type=F32)

    @pl.when(e == n_routed_steps)
    def _():
        gu = jnp.dot(xb[...], wgus_ref[...], preferred_element_type=F32)
        y = acc[...] + jnp.dot(hidden(gu, 0).astype(BF), wds_ref[...], preferred_element_type=F32)
        x2 = _layer_norm_rows(alpha * x1[...] + y, g_ref[...], b_ref[...])
        emb = jnp.dot(p_ref[...].astype(BF), wpi_ref[...], preferred_element_type=F32)
        gate = jax.nn.sigmoid(jnp.dot(x2.astype(BF), wpg_ref[...], preferred_element_type=F32))
        y_ref[...] = x2 + gate * emb


def _moe_layer(x2, p2, router, lw, g, b, alpha, mixer_out=None, tm=1024):
    t, d = x2.shape
    wgu, wd, wgus, wds, wpi, wpg = lw["wgu"], lw["wd"], lw["wgus"], lw["wds"], lw["wpi"], lw["wpg"]
    n_routed_steps = wgu.shape[0]
    w2, rb = router
    tok = lambda i, e: (i, 0)
    fixed = lambda i, e: (0, 0)
    step = lambda i, e: (jnp.minimum(e, n_routed_steps - 1), 0, 0)
    pre_specs, pre_args = [], []
    if mixer_out is not None:
        o, w_out, g1, b1 = mixer_out
        pre_specs = [pl.BlockSpec((tm, o.shape[1]), tok), pl.BlockSpec(w_out.shape, fixed),
                     pl.BlockSpec((1, d), fixed), pl.BlockSpec((1, d), fixed)]
        pre_args = [o, w_out, g1.reshape(1, d), b1.reshape(1, d)]
    return pl.pallas_call(
        functools.partial(_moe_body, alpha=alpha, n_routed_steps=n_routed_steps,
                          with_mixer_out=mixer_out is not None),
        grid=(t // tm, n_routed_steps + 1),
        in_specs=pre_specs + [
            pl.BlockSpec((tm, d), tok), pl.BlockSpec((tm, p2.shape[1]), tok),
            pl.BlockSpec((2 * N_EXPERTS, d), fixed), pl.BlockSpec((N_EXPERTS, 1), fixed),
            pl.BlockSpec((1,) + wgu.shape[1:], step), pl.BlockSpec((1,) + wd.shape[1:], step),
            pl.BlockSpec(wgus.shape, fixed), pl.BlockSpec(wds.shape, fixed),
            pl.BlockSpec((1, d), fixed), pl.BlockSpec((1, d), fixed),
            pl.BlockSpec(wpi.shape, fixed), pl.BlockSpec(wpg.shape, fixed),
        ],
        out_specs=pl.BlockSpec((tm, d), tok),
        out_shape=jax.ShapeDtypeStruct((t, d), F32),
        scratch_shapes=[pltpu.VMEM((tm, LANES), F32), pltpu.VMEM((tm, d), BF), pltpu.VMEM((tm, d), F32),
                        pltpu.VMEM((tm, d), F32)],
        compiler_params=_cparams("parallel", "arbitrary"),
        name="moe",
    )(*pre_args, x2, p2, w2, rb, wgu, wd, wgus, wds, g.reshape(1, d), b.reshape(1, d), wpi, wpg)


def _prep_shared(prm):
    depth = prm["ln1_g"].shape[0]
    rw = prm["router_w"].astype(F32)
    wh = rw.astype(BF)
    wl = (rw - wh.astype(F32)).astype(BF)
    router = (jnp.concatenate([wh.T, wl.T], axis=0), prm["router_b"].astype(F32).reshape(N_EXPERTS, 1))
    layers = []
    for i in range(depth):
        eps = MOE_EXPERTS_PER_STEP
        d = prm["moe_w_gate"].shape[2]
        wgu = jnp.concatenate([prm["moe_w_gate"][i], prm["moe_w_up"][i]], axis=-1)
        wgu = wgu.reshape(N_EXPERTS // eps, eps, d, 2 * EXPERT_FF).transpose(0, 2, 1, 3)
        wgu = wgu.reshape(N_EXPERTS // eps, d, eps * 2 * EXPERT_FF).astype(BF)
        wd = prm["moe_w_down"][i].reshape(N_EXPERTS // eps, eps * EXPERT_FF, d).astype(BF)
        wgus = jnp.concatenate([prm["moe_ws_gate"][i], prm["moe_ws_up"][i]], axis=-1).astype(BF)
        layers.append(dict(wgu=wgu, wd=wd, wgus=wgus, wds=prm["moe_ws_down"][i].astype(BF),
                           wpi=prm["ple_w_in"][i].astype(BF), wpg=prm["ple_w_gate"][i].astype(BF)))
    return router, layers


def _trunk(x, p, prm, router, layers):
    depth = prm["ln1_g"].shape[0]
    alpha = (2.0 * depth) ** 0.25
    bsz, seq, d = x.shape
    t = bsz * seq
    x2 = x.reshape(t, d)
    for i in range(depth):
        mixer, j = i % 4, i // 4
        xb = x2.reshape(bsz, seq, d)
        mixer_out = None
        if mixer == 3:
            x2 = _mixer_d_and_post(x2, bsz, seq, prm["d_w_qkv"][j], prm["d_w_out"][j],
                                   prm["ln1_g"][i], prm["ln1_b"][i], alpha)
        else:
            if mixer == 0:
                o = _mixer_a(xb, prm["a_w_in"][j], prm["a_lambda"][j], prm["a_subln"][j], i)
                w_out = prm["a_w_out"][j]
            elif mixer == 1:
                o = _mixer_b(xb, prm["b_w_in"][j], prm["b_q_norm"][j], prm["b_kv_norm"][j],
                             prm["b_w_uq"][j], prm["b_w_ukv"][j])
                w_out = prm["b_w_out"][j]
            else:
                o = _mixer_c(xb, prm["c_w_qkv"][j], prm["c_rpb"][j])
                w_out = prm["c_w_out"][j]
            mixer_out = (o, w_out.astype(BF), prm["ln1_g"][i], prm["ln1_b"][i])
        x2 = _moe_layer(x2, p[i].reshape(t, -1), router, layers[i], prm["ln2_g"][i], prm["ln2_b"][i], alpha,
                        mixer_out)
    return x2.reshape(bsz, seq, d)


def kernel(x_prompt, x_sample, p_prompt, p_sample, a_w_in, a_lambda, a_subln, a_w_out, b_w_in, b_q_norm, b_kv_norm, b_w_uq, b_w_ukv, b_w_out, c_w_qkv, c_rpb, c_w_out, d_w_qkv, d_w_out, router_w, router_b, moe_w_gate, moe_w_up, moe_w_down, moe_ws_gate, moe_ws_up, moe_ws_down, ln1_g, ln1_b, ln2_g, ln2_b, ple_w_in, ple_w_gate):
    prm = dict(a_w_in=a_w_in, a_lambda=a_lambda, a_subln=a_subln, a_w_out=a_w_out,
               b_w_in=b_w_in, b_q_norm=b_q_norm, b_kv_norm=b_kv_norm, b_w_uq=b_w_uq,
               b_w_ukv=b_w_ukv, b_w_out=b_w_out,
               c_w_qkv=c_w_qkv, c_rpb=c_rpb, c_w_out=c_w_out,
               d_w_qkv=d_w_qkv, d_w_out=d_w_out,
               router_w=router_w, router_b=router_b, moe_w_gate=moe_w_gate, moe_w_up=moe_w_up,
               moe_w_down=moe_w_down, moe_ws_gate=moe_ws_gate, moe_ws_up=moe_ws_up,
               moe_ws_down=moe_ws_down,
               ln1_g=ln1_g, ln1_b=ln1_b, ln2_g=ln2_g, ln2_b=ln2_b,
               ple_w_in=ple_w_in, ple_w_gate=ple_w_gate)
    router, layers = _prep_shared(prm)
    y_prompt = _trunk(x_prompt, p_prompt, prm, router, layers)
    y_sample = _trunk(x_sample, p_sample, prm, router, layers)
    return (y_prompt, y_sample)
```

```python
import functools
import math

import numpy as np
import jax
import jax.numpy as jnp
from jax import lax
from jax.experimental import pallas as pl
from jax.experimental.pallas import tpu as pltpu

BF = jnp.bfloat16
F32 = jnp.float32

VMEM_LIMIT_BYTES = 56 * 1024 * 1024
LANES = 128

GRID_W = 64
LN_EPS = 1e-5
RMS_EPS = 1e-6
ROPE_THETA = 10000.0
NEG = -1e30

A_HEADS, A_HEAD_DIM = 8, 64
B_HEADS, B_NOPE, B_ROPE, B_VDIM, B_Q_LORA, B_KV_LORA = 16, 64, 32, 64, 384, 256
C_HEADS, C_HEAD_DIM, C_WIN_ROWS, C_WIN_COLS = 16, 64, 8, 16
D_HEADS, D_HEAD_DIM = 8, 64
D_GROUPS = ((128, 1), (512, 4), (2048, 16))
N_EXPERTS, N_EXPERT_GROUPS, EXPERT_FF = 16, 4, 256

NT_DIMS = (((1,), (1,)), ((), ()))


def _cparams(*sem):
    return pltpu.CompilerParams(dimension_semantics=sem, vmem_limit_bytes=VMEM_LIMIT_BYTES)


def _layer_norm_rows(z, g, b):
    mu = jnp.mean(z, axis=-1, keepdims=True)
    zc = z - mu
    var = jnp.mean(zc * zc, axis=-1, keepdims=True)
    return zc * lax.rsqrt(var + LN_EPS) * g + b


def _rms_rows(z, g):
    return z * lax.rsqrt(jnp.mean(z * z, axis=-1, keepdims=True) + RMS_EPS) * g


def _linear_body(x_ref, w_ref, o_ref):
    o_ref[...] = jnp.dot(x_ref[...].astype(BF), w_ref[...], preferred_element_type=F32).astype(o_ref.dtype)


def _linear(x, w, out_dtype, tm=512, tn=None):
    m, k = x.shape
    n = w.shape[1]
    tn = n if tn is None else tn
    return pl.pallas_call(
        _linear_body,
        grid=(n // tn, m // tm),
        in_specs=[pl.BlockSpec((tm, k), lambda j, i: (i, 0)), pl.BlockSpec((k, tn), lambda j, i: (0, j))],
        out_specs=pl.BlockSpec((tm, tn), lambda j, i: (i, j)),
        out_shape=jax.ShapeDtypeStruct((m, n), out_dtype),
        compiler_params=_cparams("parallel", "parallel"),
        name="linear",
    )(x, w)


LOG2E = 1.4426950408889634
SUM_ROWS = 16


def _softmax_chunk(s, m, shift, mx=None):
    if mx is None:
        mx = jnp.max(s, axis=0, keepdims=True)
    if shift is not None:
        mx = mx - shift
    m_new = jnp.maximum(m, mx)
    ref = m_new if shift is None else m_new + shift
    return m_new, jnp.exp2(s - ref).astype(BF)


def _pipelined_flash(n, nh, tq, qk_fn, sm_fn, pv_fn, s_buf, p_buf, acc_ref):
    assert n >= 4 and n % 2 == 0
    acc_ref[...] = jnp.zeros_like(acc_ref)

    def scores(j, slot):
        mxs = []
        for h in range(nh):
            s = qk_fn(j, h)
            s_buf[slot, h] = s
            mxs.append(jnp.max(s, axis=0, keepdims=True))
        return tuple(mxs)

    def stage(j, slot, carry, do_pv=True, do_qk=True, first=False):
        ms, alphas, mxs = carry
        if do_pv:
            for h in range(nh):
                acc_ref[h] = alphas[h] * acc_ref[h] + pv_fn(j - 1, h, p_buf[1 - slot, h])
        mxs_next = scores(j + 1, 1 - slot) if do_qk else mxs
        new_ms, new_alphas = [], []
        for h in range(nh):
            m_new, p = sm_fn(j, h, s_buf[slot, h], ms[h], mxs[h], first)
            p_buf[slot, h] = p
            new_alphas.append(jnp.exp2(ms[h] - m_new))
            new_ms.append(m_new)
        return tuple(new_ms), tuple(new_alphas), mxs_next

    mxs = scores(0, 0)
    ms = tuple(jnp.full((1, tq), NEG, F32) for _ in range(nh))
    alphas = tuple(jnp.zeros((1, tq), F32) for _ in range(nh))
    carry = stage(0, 0, (ms, alphas, mxs), do_pv=False, first=True)

    def body(t, carry):
        return stage(2 * t + 2, 0, stage(2 * t + 1, 1, carry))

    carry = lax.fori_loop(0, (n - 2) // 2, body, carry)
    _, alphas, _ = stage(n - 1, 1, carry, do_qk=False)
    for h in range(nh):
        acc_ref[h] = alphas[h] * acc_ref[h] + pv_fn(n - 1, h, p_buf[1, h])


def _chunk_off(j, tk):
    return j * tk if isinstance(j, int) else pl.multiple_of(j * tk, tk)


def _attn_a_body(cs_ref, ctab_ref, lam_ref, g_ref, q_ref, k_ref, kpos_ref, vt_ref, o_ref,
                 s_buf, p_buf, acc_ref, *, seq, tq, tk, hps, lam_init):
    hg = pl.program_id(1)
    qi = pl.program_id(2)
    n = seq // tk
    cd = (qi * tq) // tk
    lane = lax.broadcasted_iota(jnp.int32, (tq, LANES), 1)
    qpos = (qi * tq + lax.broadcasted_iota(jnp.int32, (1, tq), 1)).astype(F32)
    dv = vt_ref.shape[2] - SUM_ROWS
    c_slope, f0, q_left, q_right = [], [], [], []
    for hh in range(hps):
        head = hg * hps + hh
        c_slope.append(cs_ref[head])
        f0.append(c_slope[hh] * qpos)
        q = q_ref[0, :, hh * LANES:(hh + 1) * LANES]
        qaug = jnp.broadcast_to(ctab_ref[pl.ds(head, 1), :], (tq, LANES)).astype(BF)
        for half in range(2):
            q_m = jnp.where((lane >= 64 * half) & (lane < 64 * (half + 1)), q, jnp.zeros_like(q))
            q_left.append(jnp.concatenate([q_m, qaug], axis=1))
            q_right.append(jnp.concatenate([q_m, -qaug], axis=1))

    def chunk_of(j):
        jm = j - 1
        c = jnp.where(j == 0, cd, jm + (jm >= cd).astype(jnp.int32))
        return c, c <= cd

    def qk_fn(j, mm):
        hh = mm // 2
        c, left = chunk_of(j)
        off = pl.multiple_of(c * tk, tk)
        k_c = jnp.concatenate([k_ref[0, pl.ds(off, tk), hh * LANES:(hh + 1) * LANES],
                               kpos_ref[pl.ds(off, tk), :]], axis=1)
        q_full = jnp.where(left, q_left[mm], q_right[mm])
        return lax.dot_general(k_c, q_full, NT_DIMS, preferred_element_type=F32)

    def sm_fn(j, mm, s, m, mx, first):
        hh = mm // 2
        if first:
            d = (lax.broadcasted_iota(jnp.int32, (tk, tq), 0) - lax.broadcasted_iota(jnp.int32, (tk, tq), 1)
                 + (cd * tk - qi * tq)).astype(F32)
            return _softmax_chunk(s - (2.0 * c_slope[hh]) * jnp.maximum(d, 0.0), m, f0[hh])
        _, left = chunk_of(j)
        return _softmax_chunk(s, m, jnp.where(left, f0[hh], -f0[hh]), mx)

    def pv_fn(j, mm, p):
        c, _ = chunk_of(j)
        off = pl.multiple_of(c * tk, tk)
        return jnp.dot(vt_ref[0, mm // 2, :, pl.ds(off, tk)], p, preferred_element_type=F32)

    _pipelined_flash(n, 2 * hps, tq, qk_fn, sm_fn, pv_fn, s_buf, p_buf, acc_ref)
    lf = lam_ref[...]
    lam_full = (jnp.exp(jnp.sum(lf[0:1] * lf[1:2], axis=-1, keepdims=True))
                - jnp.exp(jnp.sum(lf[2:3] * lf[3:4], axis=-1, keepdims=True)) + lam_init)
    outs = []
    for hh in range(hps):
        o0, o1 = [acc_ref[2 * hh + half, :dv, :] / acc_ref[2 * hh + half, dv:dv + 1, :] for half in range(2)]
        o = o0 - lam_full * o1
        ms = jnp.mean(o * o, axis=0, keepdims=True)
        outs.append(o * lax.rsqrt(ms + RMS_EPS) * g_ref[...] * (1.0 - lam_init))
    o_ref[0] = jnp.concatenate(outs, axis=0).T.astype(o_ref.dtype)


def _bf16_pieces(c, n=3):
    pieces, rest = [], c.astype(F32)
    for _ in range(n):
        p = rest.astype(BF)
        pieces.append(p.astype(F32))
        rest = rest - p.astype(F32)
    return pieces


def _store_vt_heads(vt_ref, vt, dv, heads_per_slot):
    tm = vt.shape[1]
    tail = jnp.where(lax.broadcasted_iota(jnp.int32, (SUM_ROWS, tm), 0) == 0, 1.0, 0.0).astype(vt_ref.dtype)
    rows = dv + SUM_ROWS
    for h in range(vt.shape[0] // dv):
        slot, r0 = h // heads_per_slot, (h % heads_per_slot) * rows
        vt_ref[0, slot, r0:r0 + dv, :] = vt[h * dv:(h + 1) * dv, :].astype(vt_ref.dtype)
        vt_ref[0, slot, r0 + dv:r0 + rows, :] = tail


def _proj_a_body(x_ref, wqk_ref, wvt_ref, qk_ref, vt_ref):
    xb = x_ref[...].astype(BF)
    qk_ref[...] = jnp.dot(xb, wqk_ref[...], preferred_element_type=F32).astype(BF)
    vt = lax.dot_general(wvt_ref[...], xb, NT_DIMS, preferred_element_type=F32)
    _store_vt_heads(vt_ref, vt, 2 * A_HEAD_DIM, 1)


def _flash_chunk_len(seq, tk_max):
    for min_chunks in (8, 4):
        tk = tk_max
        while tk >= 2 * LANES and (seq % tk or (seq // tk) < min_chunks or (seq // tk) % 2):
            tk //= 2
        if tk >= 2 * LANES:
            return tk
    raise ValueError(f"sequence length {seq} too short for the pipelined attention kernel")


def _flash_scratch(nh, rows, tq, tk):
    return [pltpu.VMEM((2, nh, tk, tq), F32), pltpu.VMEM((2, nh, tk, tq), BF), pltpu.VMEM((nh, rows, tq), F32)]


def _attn_a(qkv, vt, lam, subln_g, lam_init, tq=256, tk=1024):
    bsz, seq, _ = qkv.shape
    tq, tk = min(tq, seq), _flash_chunk_len(seq, tk)
    c_slope = jnp.asarray(2.0 ** (-8.0 * np.arange(1, A_HEADS + 1) / A_HEADS) * LOG2E, dtype=F32)
    c1, c2, c3 = _bf16_pieces(c_slope)
    ctab = jnp.stack([c1, c1, c2, c2, c3, c3], axis=-1)
    ctab = jnp.concatenate([ctab, jnp.zeros((A_HEADS, LANES - 6), F32)], axis=-1)
    pos = np.arange(seq)
    kpos = np.zeros((seq, LANES), np.float32)
    for j in range(3):
        kpos[:, 2 * j] = (pos // LANES) * LANES
        kpos[:, 2 * j + 1] = pos % LANES
    kpos = jnp.asarray(kpos, BF)
    vrows = vt.shape[2]
    hps = 2 if seq <= 4096 else 1
    ngrp = A_HEADS // hps
    body = functools.partial(_attn_a_body, seq=seq, tq=tq, tk=tk, hps=hps, lam_init=lam_init)
    return pl.pallas_call(
        body,
        grid=(bsz, ngrp, seq // tq),
        in_specs=[
            pl.BlockSpec(memory_space=pltpu.SMEM),
            pl.BlockSpec((A_HEADS, LANES), lambda b, h, i: (0, 0)),
            pl.BlockSpec((4, A_HEAD_DIM), lambda b, h, i: (0, 0)),
            pl.BlockSpec((2 * A_HEAD_DIM, 1), lambda b, h, i: (0, 0)),
            pl.BlockSpec((1, tq, hps * LANES), lambda b, h, i: (b, i, h)),
            pl.BlockSpec((1, seq, hps * LANES), lambda b, h, i: (b, 0, ngrp + h)),
            pl.BlockSpec((seq, LANES), lambda b, h, i: (0, 0)),
            pl.BlockSpec((1, hps, vrows, seq), lambda b, h, i: (b, h, 0, 0)),
        ],
        out_specs=pl.BlockSpec((1, tq, hps * LANES), lambda b, h, i: (b, i, h)),
        out_shape=jax.ShapeDtypeStruct((bsz, seq, A_HEADS * LANES), BF),
        scratch_shapes=_flash_scratch(2 * hps, vrows, tq, tk),
        compiler_params=_cparams("parallel", "parallel", "arbitrary"),
        name="attn_a",
    )(c_slope, ctab, lam.astype(F32), subln_g.astype(F32).reshape(2 * A_HEAD_DIM, 1), qkv, qkv, kpos, vt)


def _attn_b_body(q_ref, k_ref, vt_ref, o_ref, s_buf, p_buf, acc_ref, *, seq, tk, nh):
    tq = q_ref.shape[1]
    rows = B_VDIM + SUM_ROWS
    qs = [q_ref[0, :, h * LANES:(h + 1) * LANES] for h in range(nh)]

    def qk_fn(j, h):
        k_c = k_ref[0, pl.ds(_chunk_off(j, tk), tk), h * LANES:(h + 1) * LANES]
        return lax.dot_general(k_c, qs[h], NT_DIMS, preferred_element_type=F32)

    def sm_fn(j, h, s, m, mx, first):
        return _softmax_chunk(s, m, None, mx)

    def pv_fn(j, h, p):
        vt_c = vt_ref[0, h // 2, (h % 2) * rows:(h % 2 + 1) * rows, pl.ds(_chunk_off(j, tk), tk)]
        return jnp.dot(vt_c, p, preferred_element_type=F32)

    _pipelined_flash(seq // tk, nh, tq, qk_fn, sm_fn, pv_fn, s_buf, p_buf, acc_ref)
    outs = [acc_ref[h, :B_VDIM, :] / acc_ref[h, B_VDIM:B_VDIM + 1, :] for h in range(nh)]
    o = jnp.concatenate(outs, axis=0)
    o_ref[0] = o.T.astype(o_ref.dtype)


def _attn_b(q, k, vt, tq=256, tk=1024):
    bsz, seq, _ = q.shape
    tq, tk = min(tq, seq), _flash_chunk_len(seq, tk)
    nh = 4 if seq <= 4096 else 2
    return pl.pallas_call(
        functools.partial(_attn_b_body, seq=seq, tk=tk, nh=nh),
        grid=(bsz, B_HEADS // nh, seq // tq),
        in_specs=[
            pl.BlockSpec((1, tq, nh * LANES), lambda b, h, i: (b, i, h)),
            pl.BlockSpec((1, seq, nh * LANES), lambda b, h, i: (b, 0, h)),
            pl.BlockSpec((1, nh // 2, 2 * (B_VDIM + SUM_ROWS), seq), lambda b, h, i: (b, h, 0, 0)),
        ],
        out_specs=pl.BlockSpec((1, tq, nh * B_VDIM), lambda b, h, i: (b, i, h)),
        out_shape=jax.ShapeDtypeStruct((bsz, seq, B_HEADS * B_VDIM), BF),
        scratch_shapes=_flash_scratch(nh, B_VDIM + SUM_ROWS, tq, tk),
        compiler_params=_cparams("parallel", "parallel", "arbitrary"),
        name="attn_b",
    )(q, k, vt)


def _mla_in_body(x_ref, w_ref, gq_ref, gkv_ref, rc_ref, rs_ref, cq_ref, ckv_ref, kr_ref):
    h = jnp.dot(x_ref[...].astype(BF), w_ref[...], preferred_element_type=F32)
    cq_ref[...] = _rms_rows(h[:, :B_Q_LORA], gq_ref[...]).astype(BF)
    ckv_ref[...] = _rms_rows(h[:, B_Q_LORA:B_Q_LORA + B_KV_LORA], gkv_ref[...]).astype(BF)
    t = h[:, B_Q_LORA + B_KV_LORA:]
    kr = t * rc_ref[...] + pltpu.roll(t, LANES - B_ROPE, 1) * rs_ref[...]
    kr_ref[...] = kr.astype(BF)


def _mla_q_body(cq_ref, w_ref, ta_ref, tb_ref, q_ref):
    t = jnp.dot(cq_ref[...], w_ref[...], preferred_element_type=F32)
    ta = ta_ref[...]
    tb = tb_ref[...]
    for h in range(B_HEADS):
        th = t[:, h * LANES:(h + 1) * LANES]
        q_ref[:, h * LANES:(h + 1) * LANES] = (th * ta + pltpu.roll(th, LANES - B_ROPE, 1) * tb).astype(BF)


def _mla_kv_body(ckv_ref, kr_ref, wk_ref, e_ref, wvt_ref, k_ref, vt_ref):
    ckv = ckv_ref[...]
    k = jnp.dot(ckv, wk_ref[...], preferred_element_type=F32)
    k = k + jnp.dot(kr_ref[...], e_ref[...], preferred_element_type=F32)
    k_ref[...] = k.astype(BF)
    vt = lax.dot_general(wvt_ref[...], ckv, NT_DIMS, preferred_element_type=F32)
    _store_vt_heads(vt_ref, vt, B_VDIM, 2)


def _rope_partner(w):
    half = B_ROPE // 2
    return jnp.concatenate([-w[..., half:], w[..., :half]], axis=-1)


def _mla_tables(seq):
    inv = 1.0 / (ROPE_THETA ** (np.arange(0, B_ROPE, 2, dtype=np.float32) / B_ROPE))
    ang = jnp.arange(seq, dtype=F32)[:, None] * jnp.asarray(inv, F32)[None, :]
    cos = jnp.concatenate([jnp.cos(ang), jnp.cos(ang)], axis=-1)
    sin = jnp.concatenate([jnp.sin(ang), jnp.sin(ang)], axis=-1)
    z = lambda n: jnp.zeros((seq, n), F32)
    rc = jnp.concatenate([cos, z(LANES - B_ROPE)], axis=-1)
    rs = jnp.concatenate([sin, z(LANES - B_ROPE)], axis=-1)
    scale = (B_NOPE + B_ROPE) ** -0.5 * LOG2E
    ta =jnp.concatenate([jnp.full((seq, B_NOPE), scale, F32), cos * scale, z(B_ROPE)], axis=-1)
    tb = jnp.concatenate([z(B_NOPE), sin * scale, z(B_ROPE)], axis=-1)
    return rc, rs, ta, tb


def _mla_weights(w_in, w_uq, w_ukv):
    d = w_in.shape[0]
    kr0 = B_Q_LORA + B_KV_LORA
    w_in_ext = jnp.concatenate(
        [w_in, _rope_partner(w_in[:, kr0:kr0 + B_ROPE]), jnp.zeros((d, LANES - 2 * B_ROPE), w_in.dtype)], axis=-1)
    wq = w_uq.reshape(B_Q_LORA, B_HEADS, B_NOPE + B_ROPE)
    wq_ext = jnp.concatenate([wq, _rope_partner(wq[..., B_NOPE:])], axis=-1).reshape(B_Q_LORA, B_HEADS * LANES)
    wkv = w_ukv.reshape(B_KV_LORA, B_HEADS, B_NOPE + B_VDIM)
    wk = jnp.concatenate([wkv[..., :B_NOPE], jnp.zeros((B_KV_LORA, B_HEADS, LANES - B_NOPE), w_ukv.dtype)], axis=-1)
    wk = wk.reshape(B_KV_LORA, B_HEADS * LANES)
    wv = wkv[..., B_NOPE:].reshape(B_KV_LORA, B_HEADS * B_VDIM)
    place = np.zeros((LANES, B_HEADS, LANES), np.float32)
    for j in range(B_ROPE):
        place[j, :, B_NOPE + j] = 1.0
    place = jnp.asarray(place.reshape(LANES, B_HEADS * LANES), BF)
    return w_in_ext.astype(BF), wq_ext.astype(BF), wk.astype(BF), place, wv.astype(BF)


def _mixer_b(x, w_in, q_norm_g, kv_norm_g, w_uq, w_ukv, tm=512):
    bsz, seq, d = x.shape
    t = bsz * seq
    tm = min(tm, seq)
    nblk = seq // tm
    w_in_ext, wq_ext, wk, place, wv = _mla_weights(w_in, w_uq, w_ukv)
    rc, rs, ta, tb = _mla_tables(seq)
    x2 = x.reshape(t, d)
    n_in = w_in_ext.shape[1]
    row = lambda i: (i, 0)
    fixed = lambda i: (0, 0)
    pos = lambda i: (i % nblk, 0)
    cq, ckv, kr = pl.pallas_call(
        _mla_in_body,
        grid=(t // tm,),
        in_specs=[
            pl.BlockSpec((tm, d), row), pl.BlockSpec((d, n_in), fixed),
            pl.BlockSpec((1, B_Q_LORA), fixed), pl.BlockSpec((1, B_KV_LORA), fixed),
            pl.BlockSpec((tm, LANES), pos), pl.BlockSpec((tm, LANES), pos),
        ],
        out_specs=[pl.BlockSpec((tm, B_Q_LORA), row), pl.BlockSpec((tm, B_KV_LORA), row),
                   pl.BlockSpec((tm, LANES), row)],
        out_shape=[jax.ShapeDtypeStruct((t, B_Q_LORA), BF), jax.ShapeDtypeStruct((t, B_KV_LORA), BF),
                   jax.ShapeDtypeStruct((t, LANES), BF)],
        compiler_params=_cparams("parallel"),
        name="mla_in",
    )(x2, w_in_ext, q_norm_g.astype(F32).reshape(1, -1), kv_norm_g.astype(F32).reshape(1, -1), rc, rs)
    nq = B_HEADS * LANES
    q = pl.pallas_call(
        _mla_q_body,
        grid=(t // tm,),
        in_specs=[pl.BlockSpec((tm, B_Q_LORA), row), pl.BlockSpec((B_Q_LORA, nq), fixed),
                  pl.BlockSpec((tm, LANES), pos), pl.BlockSpec((tm, LANES), pos)],
        out_specs=pl.BlockSpec((tm, nq), row),
        out_shape=jax.ShapeDtypeStruct((t, nq), BF),
        compiler_params=_cparams("parallel"),
        name="mla_q",
    )(cq, wq_ext, ta, tb)
    nv = B_HEADS * B_VDIM
    npair = B_HEADS // 2
    vrows = 2 * (B_VDIM + SUM_ROWS)
    k, vt = pl.pallas_call(
        _mla_kv_body,
        grid=(t // tm,),
        in_specs=[pl.BlockSpec((tm, B_KV_LORA), row), pl.BlockSpec((tm, LANES), row),
                  pl.BlockSpec((B_KV_LORA, nq), fixed), pl.BlockSpec((LANES, nq), fixed),
                  pl.BlockSpec((nv, B_KV_LORA), fixed)],
        out_specs=[pl.BlockSpec((tm, nq), row),
                   pl.BlockSpec((1, npair, vrows, tm), lambda i: (i // nblk, 0, 0, i % nblk))],
        out_shape=[jax.ShapeDtypeStruct((t, nq), BF), jax.ShapeDtypeStruct((bsz, npair, vrows, seq), BF)],
        compiler_params=_cparams("parallel"),
        name="mla_kv",
    )(ckv, kr, wk, place, wv.T)
    o = _attn_b(q.reshape(bsz, seq, nq), k.reshape(bsz, seq, nq), vt)
    return o.reshape(t, nv)


def _mixer_a(x, w_in, lam, subln_g, layer_idx):
    bsz, seq, d = x.shape
    t = bsz * seq
    hd2 = 2 * A_HEAD_DIM
    nq = A_HEADS * hd2
    scale = A_HEAD_DIM ** -0.5 * LOG2E
    wqk = jnp.concatenate([w_in[:, :nq] * scale, w_in[:, nq:2 * nq]], axis=-1).astype(BF)
    wvt = w_in[:, 2 * nq:].T.astype(BF)
    tm = min(512, seq)
    nblk = seq // tm
    vrows = hd2 + SUM_ROWS
    qk, vt = pl.pallas_call(
        _proj_a_body,
        grid=(t // tm,),
        in_specs=[pl.BlockSpec((tm, d), lambda i: (i, 0)), pl.BlockSpec((d, 2 * nq), lambda i: (0, 0)),
                  pl.BlockSpec((nq, d), lambda i: (0, 0))],
        out_specs=[pl.BlockSpec((tm, 2 * nq), lambda i: (i, 0)),
                   pl.BlockSpec((1, A_HEADS, vrows, tm), lambda i: (i // nblk, 0, 0, i % nblk))],
        out_shape=[jax.ShapeDtypeStruct((t, 2 * nq), BF), jax.ShapeDtypeStruct((bsz, A_HEADS, vrows, seq), BF)],
        compiler_params=_cparams("parallel"),
        name="proj_a",
    )(x.reshape(t, d), wqk, wvt)
    lam_init = 0.8 - 0.6 * math.exp(-0.3 * layer_idx)
    o = _attn_a(qk.reshape(bsz, seq, 2 * nq), vt, lam, subln_g, lam_init)
    return o.reshape(t, nq)


def _attn_c_body(q_ref, k0, k1, k2, k3, v0, v1, v2, v3, bt_ref, o_ref):
    kw = jnp.concatenate([k0[0], k1[0], k2[0], k3[0]], axis=0)
    vw = jnp.concatenate([v0[0], v1[0], v2[0], v3[0]], axis=0)
    nq = q_ref.shape[1] // 2
    nk = bt_ref.shape[3]
    lane = lax.broadcasted_iota(jnp.int32, (nq, LANES), 1)
    for hf in range(2):
        q = q_ref[0, hf * nq:(hf + 1) * nq, :]
        kh = kw[hf * nq:hf * nq + nk]
        vh = vw[hf * nq:hf * nq + nk]
        outs = []
        for hh in range(2):
            qm = jnp.where((lane >= 64 * hh) & (lane < 64 * (hh + 1)), q, jnp.zeros_like(q))
            s = (lax.dot_general(qm, kh, NT_DIMS, preferred_element_type=F32)
                 + bt_ref[hh, 0, hf * nq:(hf + 1) * nq, :])
            m = jnp.max(s, axis=1, keepdims=True)
            p = jnp.exp2(s - m)
            l = jnp.sum(p, axis=1, keepdims=True)
            outs.append(jnp.dot(p.astype(BF), vh, preferred_element_type=F32) / l)
        o_ref[0, hf * nq:(hf + 1) * nq, :] = jnp.where(lane < 64, outs[0], outs[1]).astype(o_ref.dtype)


def _nbr_bias_table(rpb):
    col = np.arange(GRID_W)
    col_start = np.clip(col - C_WIN_COLS // 2, 0, GRID_W - C_WIN_COLS)
    col_mask = (col[None, :] >= col_start[:, None]) & (col[None, :] < col_start[:, None] + C_WIN_COLS)
    pad = GRID_W - C_WIN_COLS
    ext = jnp.pad(rpb.astype(F32) * LOG2E, ((0, 0), (0, 0), (pad, pad)), mode="edge")
    toep = jnp.stack([ext[:, :, GRID_W - 1 - qc:2 * GRID_W - 1 - qc] for qc in range(GRID_W)], axis=2)
    toep = jnp.where(jnp.asarray(col_mask)[None, None], toep, NEG)
    neg = jnp.full((rpb.shape[0], GRID_W, GRID_W), NEG, F32)
    half = C_WIN_ROWS // 2
    kinds = []
    for kind in range(3):
        qrows = []
        for t in range(C_WIN_ROWS):
            u0 = (max(t, half), t, min(t, half))[kind]
            ubase = 0 if t < half else half
            blocks = [toep[:, u - t + half - 1] if u0 <= u < u0 + C_WIN_ROWS else neg
                      for u in range(ubase, ubase + C_WIN_ROWS + half)]
            qrows.append(jnp.concatenate(blocks, axis=-1))
        kinds.append(jnp.concatenate(qrows, axis=1))
    return jnp.stack(kinds, axis=1)


def _mixer_c(x, w_qkv, rpb):
    bsz, seq, d = x.shape
    t = bsz * seq
    nq = C_HEADS * C_HEAD_DIM
    scale = C_HEAD_DIM ** -0.5 * LOG2E
    w = jnp.concatenate([w_qkv[:, :nq] * scale, w_qkv[:, nq:]], axis=-1).astype(BF)
    qkv = _linear(x.reshape(t, d), w, BF, tn=nq).reshape(bsz, seq, 3 * nq)
    bt = _nbr_bias_table(rpb)
    npair = C_HEADS // 2
    qtok = C_WIN_ROWS * GRID_W
    ngrp = seq // qtok
    assert ngrp >= 2
    kb = qtok // 2
    nkb = seq // kb
    kspecs = []
    for off in (npair, 2 * npair):
        for j in range(4):
            kspecs.append(pl.BlockSpec(
                (1, kb, LANES),
                lambda h, g, b, j=j, off=off: (b, jnp.clip(2 * g - 1 + j, 0, nkb - 1), off + h)))
    kind = lambda g: jnp.where(g == 0, 0, jnp.where(g == ngrp - 1, 2, 1))
    o = pl.pallas_call(
        _attn_c_body,
        grid=(npair, ngrp, bsz),
        in_specs=[pl.BlockSpec((1, qtok, LANES), lambda h, g, b: (b, g, h))] + kspecs + [
            pl.BlockSpec((2, 1, qtok, bt.shape[3]), lambda h, g, b: (h, kind(g), 0, 0))],
        out_specs=pl.BlockSpec((1, qtok, LANES), lambda h, g, b: (b, g, h)),
        out_shape=jax.ShapeDtypeStruct((bsz, seq, nq), BF),
        compiler_params=_cparams("parallel", "parallel", "arbitrary"),
        name="attn_c",
    )(qkv, *([qkv] * 8), bt)
    return o.reshape(t, nq)


def _attn_d_body(bias_ref, q_ref, kp, kc, kn, vp, vc, vn, o_ref, lse_ref, *, tq, length, rad):
    i = pl.program_id(2)
    nk = tq + 2 * rad
    ki = i * tq - rad + lax.broadcasted_iota(jnp.int32, (1, nk), 1)
    edge = jnp.where((ki >= 0) & (ki < length), 0.0, NEG)
    lane = lax.broadcasted_iota(jnp.int32, (tq, LANES), 1)
    for hp in range(D_HEADS // 2):
        sl = slice(hp * LANES, (hp + 1) * LANES)
        q = q_ref[0, 0, :, sl]
        kw = jnp.concatenate([kp[0, 0, tq - rad:tq, sl], kc[0, 0, :, sl], kn[0, 0, 0:rad, sl]], axis=0)
        vw = jnp.concatenate([vp[0, 0, tq - rad:tq, sl], vc[0, 0, :, sl], vn[0, 0, 0:rad, sl]], axis=0)
        outs, lses = [], []
        for hh in range(2):
            qm = jnp.where((lane >= 64 * hh) & (lane < 64 * (hh + 1)), q, jnp.zeros_like(q))
            s = lax.dot_general(qm, kw, NT_DIMS, preferred_element_type=F32)
            s = s + (bias_ref[hp * 2 + hh] + edge)
            m = jnp.max(s, axis=1, keepdims=True)
            p = jnp.exp2(s - m)
            l = jnp.sum(p, axis=1, keepdims=True)
            outs.append(jnp.dot(p.astype(BF), vw, preferred_element_type=F32) / l)
            lses.append(m + jnp.log(l) * LOG2E)
        o_ref[0, 0, :, sl] = jnp.where(lane < 64, outs[0], outs[1]).astype(o_ref.dtype)
        lse_ref[0, 0, :, sl] = jnp.where(lane < 64, lses[0], lses[1])


def _attn_d_group(qkv, window, dil, tq=128):
    bsz, _, length, ncol = qkv.shape
    rad = window // (2 * dil)
    tq = min(tq, length)
    nq = length // tq
    nh = ncol // 3
    nk = tq + 2 * rad
    slopes = 2.0 ** (-8.0 * np.arange(1, D_HEADS + 1) / D_HEADS) * LOG2E
    dist = np.abs(np.arange(nk)[None, :] - rad - np.arange(tq)[:, None])
    bias = np.where(dist <= rad, -slopes[:, None, None] * (dist * dil), NEG)
    bias = jnp.asarray(bias, F32)

    def spec(which, shift):
        return pl.BlockSpec((1, 1, tq, nh), lambda b, c, i: (b, c, jnp.clip(i + shift, 0, nq - 1), which))

    ospec = pl.BlockSpec((1, 1, tq, nh), lambda b, c, i: (b, c, i, 0))
    return pl.pallas_call(
        functools.partial(_attn_d_body, tq=tq, length=length, rad=rad),
        grid=(bsz, dil, nq),
        in_specs=[pl.BlockSpec(bias.shape, lambda b, c, i: (0, 0, 0)), spec(0, 0),
                  spec(1, -1), spec(1, 0), spec(1, 1), spec(2, -1), spec(2, 0), spec(2, 1)],
        out_specs=[ospec, ospec],
        out_shape=[jax.ShapeDtypeStruct((bsz, dil, length, nh), BF),
                   jax.ShapeDtypeStruct((bsz, dil, length, nh), F32)],
        compiler_params=_cparams("parallel", "parallel", "arbitrary"),
        name=f"attn_d{dil}",
    )(bias, qkv, qkv, qkv, qkv, qkv, qkv, qkv)


def _proj_d_body(x_ref, w_ref, o0, o1, o2, scr, *, dils):
    xb = x_ref[...].astype(BF)
    tm = xb.shape[0]
    nblk = scr.shape[0]
    ncol = nblk * LANES
    for gi, (o_ref, dil) in enumerate(zip((o0, o1, o2), dils)):
        res = jnp.dot(xb, w_ref[:, gi * ncol:(gi + 1) * ncol], preferred_element_type=F32)
        if dil == 1:
            o_ref[0, 0] = res.astype(BF)
            continue
        for j in range(nblk):
            scr[j] = res[:, j * LANES:(j + 1) * LANES]
        for c in range(dil):
            for j in range(nblk):
                o_ref[0, c, :, j * LANES:(j + 1) * LANES] = scr[j, pl.ds(c, tm // dil, stride=dil), :].astype(BF)


def _post_d_body(o0, o1, o2, l0, l1, l2, x_ref, w_ref, g_ref, b_ref, y_ref, scr_o, scr_l, *, alpha, dils):
    tm = x_ref.shape[0]

    def interleaved(ref, scr, dil):
        if dil == 1:
            return ref[0, 0].astype(F32)
        nblk = scr.shape[0]
        for c in range(dil):
            for j in range(nblk):
                scr[j, pl.ds(c, tm // dil, stride=dil), :] = ref[0, c, :, j * LANES:(j + 1) * LANES].astype(F32)
        return jnp.concatenate([scr[j] for j in range(nblk)], axis=1)

    ov = [interleaved(r, scr_o.at[j], dil) for j, (r, dil) in enumerate(zip((o0, o1, o2), dils))]
    a0, a1, a2 = [interleaved(r, scr_l.at[j], dil) for j, (r, dil) in enumerate(zip((l0, l1, l2), dils))]
    m = jnp.maximum(jnp.maximum(a0, a1), a2)
    e0, e1, e2 = jnp.exp2(a0 - m), jnp.exp2(a1 - m), jnp.exp2(a2 - m)
    den = e0 + e1 + e2
    o = (e0 / den) * ov[0] + (e1 / den) * ov[1] + (e2 / den) * ov[2]
    h = jnp.dot(o.astype(BF), w_ref[...], preferred_element_type=F32)
    y_ref[...] = _layer_norm_rows(alpha * x_ref[...] + h, g_ref[...], b_ref[...])


def _mixer_d_and_post(x2, bsz, seq, w_qkv, w_out, g, b, alpha, tm=512):
    t, d = x2.shape
    nh = D_HEADS * D_HEAD_DIM
    ng = len(D_GROUPS)
    dils = tuple(dil for _, dil in D_GROUPS)
    tm = min(tm, seq)
    nblk = seq // tm
    scale = D_HEAD_DIM ** -0.5 * LOG2E
    wq = w_qkv.reshape(d, 3, ng, nh)
    wq = jnp.stack([wq[:, 0] * scale, wq[:, 1], wq[:, 2]], axis=1)
    w = wq.transpose(0, 2, 1, 3).reshape(d, ng * 3 * nh).astype(BF)

    def deint_spec(dil, width):
        return pl.BlockSpec((1, dil, tm // dil, width), lambda i: (i // nblk, 0, i % nblk, 0))

    qkvs = pl.pallas_call(
        functools.partial(_proj_d_body, dils=dils),
        grid=(t // tm,),
        in_specs=[pl.BlockSpec((tm, d), lambda i: (i, 0)), pl.BlockSpec((d, ng * 3 * nh), lambda i: (0, 0))],
        out_specs=[deint_spec(dil, 3 * nh) for dil in dils],
        out_shape=[jax.ShapeDtypeStruct((bsz, dil, seq // dil, 3 * nh), BF) for dil in dils],
        scratch_shapes=[pltpu.VMEM((3 * nh // LANES, tm, LANES), F32)],
        compiler_params=_cparams("parallel"),
        name="proj_d",
    )(x2, w)
    os_, ls_ = [], []
    for qkv_g, (window, dil) in zip(qkvs, D_GROUPS):
        o, lse = _attn_d_group(qkv_g, window, dil)
        os_.append(o)
        ls_.append(lse)
    row = lambda i: (i, 0)
    fixed = lambda i: (0, 0)
    return pl.pallas_call(
        functools.partial(_post_d_body, alpha=alpha, dils=dils),
        grid=(t // tm,),
        in_specs=[deint_spec(dil, nh) for dil in dils] * 2 + [
            pl.BlockSpec((tm, d), row), pl.BlockSpec((nh, d), fixed),
            pl.BlockSpec((1, d), fixed), pl.BlockSpec((1, d), fixed)],
        out_specs=pl.BlockSpec((tm, d), row),
        out_shape=jax.ShapeDtypeStruct((t, d), F32),
        scratch_shapes=[pltpu.VMEM((ng, nh // LANES, tm, LANES), F32), pltpu.VMEM((ng, nh // LANES, tm, LANES), F32)],
        compiler_params=_cparams("parallel"),
        name="post_d",
    )(*os_, *ls_, x2, w_out.astype(BF), g.reshape(1, d), b.reshape(1, d))


def _router_gates(x, w2_ref, rb_ref):
    tm = x.shape[0]
    xh = x.astype(BF)
    xl = (x - xh.astype(F32)).astype(BF)
    w2 = w2_ref[...]
    both = lax.dot_general(w2, xh, NT_DIMS, preferred_element_type=F32)
    logits = (both[:N_EXPERTS] + both[N_EXPERTS:]
              + lax.dot_general(w2[:N_EXPERTS], xl, NT_DIMS, preferred_element_type=F32))
    scores = jax.nn.sigmoid(logits)
    biased = scores + rb_ref[...]
    epg = N_EXPERTS // N_EXPERT_GROUPS
    sc = [scores[e:e + 1, :] for e in range(N_EXPERTS)]
    bi = [biased[e:e + 1, :] for e in range(N_EXPERTS)]
    gs = []
    for g in range(N_EXPERT_GROUPS):
        v = bi[g * epg:(g + 1) * epg]
        best = None
        for a in range(epg):
            for c in range(a + 1, epg):
                pair = v[a] + v[c]
                best = pair if best is None else jnp.maximum(best, pair)
        gs.append(best)
    gmax = functools.reduce(jnp.maximum, gs)
    taken = jnp.zeros((1, tm), jnp.bool_)
    cand = []
    for g in range(N_EXPERT_GROUPS):
        sel = (gs[g] == gmax) & jnp.logical_not(taken)
        taken = taken | sel
        for a in range(epg):
            cand.append(jnp.where(sel, bi[g * epg + a], -jnp.inf))
    m1 = functools.reduce(jnp.maximum, cand)
    taken = jnp.zeros((1, tm), jnp.bool_)
    is1 = []
    for e in range(N_EXPERTS):
        hit = (cand[e] == m1) & jnp.logical_not(taken)
        taken = taken | hit
        is1.append(hit)
    cand2 = [jnp.where(is1[e], -jnp.inf, cand[e]) for e in range(N_EXPERTS)]
    m2 = functools.reduce(jnp.maximum, cand2)
    taken = jnp.zeros((1, tm), jnp.bool_)
    is2 = []
    for e in range(N_EXPERTS):
        hit = (cand2[e] == m2) & jnp.logical_not(taken)
        taken = taken | hit
        is2.append(hit)
    zero = jnp.zeros((1, tm), F32)
    w1 = functools.reduce(jnp.add, [jnp.where(is1[e], sc[e], zero) for e in range(N_EXPERTS)])
    w2 = functools.reduce(jnp.add, [jnp.where(is2[e], sc[e], zero) for e in range(N_EXPERTS)])
    den = w1 + w2
    rows = [jnp.where(is1[e], w1 / den, zero) + jnp.where(is2[e], w2 / den, zero) for e in range(N_EXPERTS)]
    rows.append(jnp.zeros((LANES - N_EXPERTS, tm), F32))
    return jnp.concatenate(rows, axis=0).T


MOE_EXPERTS_PER_STEP = 2


def _moe_body(*refs, alpha, n_routed_steps, with_mixer_out):
    if with_mixer_out:
        o_ref, wo_ref, g1_ref, b1_ref, *refs = refs
    (x_ref, p_ref, w2_ref, rb_ref, wgu_ref, wd_ref, wgus_ref, wds_ref, g_ref, b_ref, wpi_ref, wpg_ref,
     y_ref, gates, xb, acc, x1) = refs
    e = pl.program_id(1)
    ff = EXPERT_FF

    def hidden(gu, j):
        return jax.nn.silu(gu[:, 2 * j * ff:(2 * j + 1) * ff]) * gu[:, (2 * j + 1) * ff:(2 * j + 2) * ff]

    tm = x_ref.shape[0]
    halves = [slice(r * tm // 2, (r + 1) * tm // 2) for r in range(2)]

    @pl.when(e == 0)
    def _():
        for rows in halves:
            x = x_ref[rows, :]
            if with_mixer_out:
                h = jnp.dot(o_ref[rows, :], wo_ref[...], preferred_element_type=F32)
                x = _layer_norm_rows(alpha * x + h, g1_ref[...], b1_ref[...])
            x1[rows, :] = x
            gates[rows, :] = _router_gates(x, w2_ref, rb_ref)
            xb[rows, :] = x.astype(BF)
        acc[...] = jnp.zeros_like(acc)

    @pl.when(e < n_routed_steps)
    def _():
        gu = jnp.dot(xb[...], wgu_ref[0], preferred_element_type=F32)
        lane = lax.broadcasted_iota(jnp.int32, gates.shape, 1)
        gt = gates[...]
        hs = []
        for j in range(MOE_EXPERTS_PER_STEP):
            gcol = jnp.sum(jnp.where(lane == e * MOE_EXPERTS_PER_STEP + j, gt, 0.0), axis=1, keepdims=True)
            hs.append((hidden(gu, j) * gcol).astype(BF))
        acc[...] += jnp.dot(jnp.concatenate(hs, axis=1), wd_ref[0], preferred_element_type=F32)

    @pl.when(e == n_routed_steps)
    def _():
        for rows in halves:
            gu = jnp.dot(xb[rows, :], wgus_ref[...], preferred_element_type=F32)
            y = acc[rows, :] + jnp.dot(hidden(gu, 0).astype(BF), wds_ref[...], preferred_element_type=F32)
            x2 = _layer_norm_rows(alpha * x1[rows, :] + y, g_ref[...], b_ref[...])
            emb = jnp.dot(p_ref[rows, :].astype(BF), wpi_ref[...], preferred_element_type=F32)
            gate = jax.nn.sigmoid(jnp.dot(x2.astype(BF), wpg_ref[...], preferred_element_type=F32))
            y_ref[rows, :] = x2 + gate * emb


def _moe_layer(x2, p2, router, lw, g, b, alpha, mixer_out=None, tm=1024):
    t, d = x2.shape
    wgu, wd, wgus, wds, wpi, wpg = lw["wgu"], lw["wd"], lw["wgus"], lw["wds"], lw["wpi"], lw["wpg"]
    n_routed_steps = wgu.shape[0]
    w2, rb = router
    tok = lambda i, e: (i, 0)
    fixed = lambda i, e: (0, 0)
    step = lambda i, e: (jnp.minimum(e, n_routed_steps - 1), 0, 0)
    pre_specs, pre_args = [], []
    if mixer_out is not None:
        o, w_out, g1, b1 = mixer_out
        pre_specs = [pl.BlockSpec((tm, o.shape[1]), tok), pl.BlockSpec(w_out.shape, fixed),
                     pl.BlockSpec((1, d), fixed), pl.BlockSpec((1, d), fixed)]
        pre_args = [o, w_out, g1.reshape(1, d), b1.reshape(1, d)]
    return pl.pallas_call(
        functools.partial(_moe_body, alpha=alpha, n_routed_steps=n_routed_steps,
                          with_mixer_out=mixer_out is not None),
        grid=(t // tm, n_routed_steps + 1),
        in_specs=pre_specs + [
            pl.BlockSpec((tm, d), tok), pl.BlockSpec((tm, p2.shape[1]), tok),
            pl.BlockSpec((2 * N_EXPERTS, d), fixed), pl.BlockSpec((N_EXPERTS, 1), fixed),
            pl.BlockSpec((1,) + wgu.shape[1:], step), pl.BlockSpec((1,) + wd.shape[1:], step),
            pl.BlockSpec(wgus.shape, fixed), pl.BlockSpec(wds.shape, fixed),
            pl.BlockSpec((1, d), fixed), pl.BlockSpec((1, d), fixed),
            pl.BlockSpec(wpi.shape, fixed), pl.BlockSpec(wpg.shape, fixed),
        ],
        out_specs=pl.BlockSpec((tm, d), tok),
        out_shape=jax.ShapeDtypeStruct((t, d), F32),
        scratch_shapes=[pltpu.VMEM((tm, LANES), F32), pltpu.VMEM((tm, d), BF), pltpu.VMEM((tm, d), F32),
                        pltpu.VMEM((tm, d), F32)],
        compiler_params=_cparams("parallel", "arbitrary"),
        name="moe",
    )(*pre_args, x2, p2, w2, rb, wgu, wd, wgus, wds, g.reshape(1, d), b.reshape(1, d), wpi, wpg)


def _prep_shared(prm):
    depth = prm["ln1_g"].shape[0]
    rw = prm["router_w"].astype(F32)
    wh = rw.astype(BF)
    wl = (rw - wh.astype(F32)).astype(BF)
    router = (jnp.concatenate([wh.T, wl.T], axis=0), prm["router_b"].astype(F32).reshape(N_EXPERTS, 1))
    layers = []
    for i in range(depth):
        eps = MOE_EXPERTS_PER_STEP
        d = prm["moe_w_gate"].shape[2]
        wgu = jnp.concatenate([prm["moe_w_gate"][i], prm["moe_w_up"][i]], axis=-1)
        wgu = wgu.reshape(N_EXPERTS // eps, eps, d, 2 * EXPERT_FF).transpose(0, 2, 1, 3)
        wgu = wgu.reshape(N_EXPERTS // eps, d, eps * 2 * EXPERT_FF).astype(BF)
        wd = prm["moe_w_down"][i].reshape(N_EXPERTS // eps, eps * EXPERT_FF, d).astype(BF)
        wgus = jnp.concatenate([prm["moe_ws_gate"][i], prm["moe_ws_up"][i]], axis=-1).astype(BF)
        layers.append(dict(wgu=wgu, wd=wd, wgus=wgus, wds=prm["moe_ws_down"][i].astype(BF),
                           wpi=prm["ple_w_in"][i].astype(BF), wpg=prm["ple_w_gate"][i].astype(BF)))
    return router, layers


def _trunk(x, p, prm, router, layers):
    depth = prm["ln1_g"].shape[0]
    alpha = (2.0 * depth) ** 0.25
    bsz, seq, d = x.shape
    t = bsz * seq
    x2 = x.reshape(t, d)
    for i in range(depth):
        mixer, j = i % 4, i // 4
        xb = x2.reshape(bsz, seq, d)
        mixer_out = None
        if mixer == 3:
            x2 = _mixer_d_and_post(x2, bsz, seq, prm["d_w_qkv"][j], prm["d_w_out"][j],
                                   prm["ln1_g"][i], prm["ln1_b"][i], alpha)
        else:
            if mixer == 0:
                o = _mixer_a(xb, prm["a_w_in"][j], prm["a_lambda"][j], prm["a_subln"][j], i)
                w_out = prm["a_w_out"][j]
            elif mixer == 1:
                o = _mixer_b(xb, prm["b_w_in"][j], prm["b_q_norm"][j], prm["b_kv_norm"][j],
                             prm["b_w_uq"][j], prm["b_w_ukv"][j])
                w_out = prm["b_w_out"][j]
            else:
                o = _mixer_c(xb, prm["c_w_qkv"][j], prm["c_rpb"][j])
                w_out = prm["c_w_out"][j]
            mixer_out = (o, w_out.astype(BF), prm["ln1_g"][i], prm["ln1_b"][i])
        x2 = _moe_layer(x2, p[i].reshape(t, -1), router, layers[i], prm["ln2_g"][i], prm["ln2_b"][i], alpha,
                        mixer_out)
    return x2.reshape(bsz, seq, d)


def kernel(x_prompt, x_sample, p_prompt, p_sample, a_w_in, a_lambda, a_subln, a_w_out, b_w_in, b_q_norm, b_kv_norm, b_w_uq, b_w_ukv, b_w_out, c_w_qkv, c_rpb, c_w_out, d_w_qkv, d_w_out, router_w, router_b, moe_w_gate, moe_w_up, moe_w_down, moe_ws_gate, moe_ws_up, moe_ws_down, ln1_g, ln1_b, ln2_g, ln2_b, ple_w_in, ple_w_gate):
    prm = dict(a_w_in=a_w_in, a_lambda=a_lambda, a_subln=a_subln, a_w_out=a_w_out,
               b_w_in=b_w_in, b_q_norm=b_q_norm, b_kv_norm=b_kv_norm, b_w_uq=b_w_uq,
               b_w_ukv=b_w_ukv, b_w_out=b_w_out,
               c_w_qkv=c_w_qkv, c_rpb=c_rpb, c_w_out=c_w_out,
               d_w_qkv=d_w_qkv, d_w_out=d_w_out,
               router_w=router_w, router_b=router_b, moe_w_gate=moe_w_gate, moe_w_up=moe_w_up,
               moe_w_down=moe_w_down, moe_ws_gate=moe_ws_gate, moe_ws_up=moe_ws_up,
               moe_ws_down=moe_ws_down,
               ln1_g=ln1_g, ln1_b=ln1_b, ln2_g=ln2_g, ln2_b=ln2_b,
               ple_w_in=ple_w_in, ple_w_gate=ple_w_gate)
    router, layers = _prep_shared(prm)
    y_prompt = _trunk(x_prompt, p_prompt, prm, router, layers)
    y_sample = _trunk(x_sample, p_sample, prm, router, layers)
    return (y_prompt, y_sample)
```

```python
import functools
import math

import numpy as np
import jax
import jax.numpy as jnp
from jax import lax
from jax.experimental import pallas as pl
from jax.experimental.pallas import tpu as pltpu

BF = jnp.bfloat16
F32 = jnp.float32

VMEM_LIMIT_BYTES = 56 * 1024 * 1024
LANES = 128

GRID_W = 64
LN_EPS = 1e-5
RMS_EPS = 1e-6
ROPE_THETA = 10000.0
NEG = -1e30

A_HEADS, A_HEAD_DIM = 8, 64
B_HEADS, B_NOPE, B_ROPE, B_VDIM, B_Q_LORA, B_KV_LORA = 16, 64, 32, 64, 384, 256
C_HEADS, C_HEAD_DIM, C_WIN_ROWS, C_WIN_COLS = 16, 64, 8, 16
D_HEADS, D_HEAD_DIM = 8, 64
D_GROUPS = ((128, 1), (512, 4), (2048, 16))
N_EXPERTS, N_EXPERT_GROUPS, EXPERT_FF = 16, 4, 256

NT_DIMS = (((1,), (1,)), ((), ()))


def _cparams(*sem):
    return pltpu.CompilerParams(dimension_semantics=sem, vmem_limit_bytes=VMEM_LIMIT_BYTES)


def _layer_norm_rows(z, g, b):
    mu = jnp.mean(z, axis=-1, keepdims=True)
    zc = z - mu
    var = jnp.mean(zc * zc, axis=-1, keepdims=True)
    return zc * lax.rsqrt(var + LN_EPS) * g + b


def _rms_rows(z, g):
    return z * lax.rsqrt(jnp.mean(z * z, axis=-1, keepdims=True) + RMS_EPS) * g


def _linear_body(x_ref, w_ref, o_ref):
    o_ref[...] = jnp.dot(x_ref[...].astype(BF), w_ref[...], preferred_element_type=F32).astype(o_ref.dtype)


def _linear(x, w, out_dtype, tm=512, tn=None):
    m, k = x.shape
    n = w.shape[1]
    tn = n if tn is None else tn
    return pl.pallas_call(
        _linear_body,
        grid=(n // tn, m // tm),
        in_specs=[pl.BlockSpec((tm, k), lambda j, i: (i, 0)), pl.BlockSpec((k, tn), lambda j, i: (0, j))],
        out_specs=pl.BlockSpec((tm, tn), lambda j, i: (i, j)),
        out_shape=jax.ShapeDtypeStruct((m, n), out_dtype),
        compiler_params=_cparams("parallel", "parallel"),
        name="linear",
    )(x, w)


LOG2E = 1.4426950408889634
SUM_ROWS = 16


def _softmax_chunk(s, m, shift, mx=None):
    if mx is None:
        mx = jnp.max(s, axis=0, keepdims=True)
    if shift is not None:
        mx = mx - shift
    m_new = jnp.maximum(m, mx)
    ref = m_new if shift is None else m_new + shift
    return m_new, jnp.exp2(s - ref).astype(BF)


def _pipelined_flash(n, nh, tq, qk_fn, sm_fn, pv_fn, s_buf, p_buf, acc_ref):
    assert n >= 4 and n % 2 == 0
    acc_ref[...] = jnp.zeros_like(acc_ref)

    def scores(j, slot):
        mxs = []
        for h in range(nh):
            s = qk_fn(j, h)
            s_buf[slot, h] = s
            mxs.append(jnp.max(s, axis=0, keepdims=True))
        return tuple(mxs)

    def stage(j, slot, carry, do_pv=True, do_qk=True, first=False):
        ms, alphas, mxs = carry
        if do_pv:
            for h in range(nh):
                acc_ref[h] = alphas[h] * acc_ref[h] + pv_fn(j - 1, h, p_buf[1 - slot, h])
        mxs_next = scores(j + 1, 1 - slot) if do_qk else mxs
        new_ms, new_alphas = [], []
        for h in range(nh):
            m_new, p = sm_fn(j, h, s_buf[slot, h], ms[h], mxs[h], first)
            p_buf[slot, h] = p
            new_alphas.append(jnp.exp2(ms[h] - m_new))
            new_ms.append(m_new)
        return tuple(new_ms), tuple(new_alphas), mxs_next

    mxs = scores(0, 0)
    ms = tuple(jnp.full((1, tq), NEG, F32) for _ in range(nh))
    alphas = tuple(jnp.zeros((1, tq), F32) for _ in range(nh))
    carry = stage(0, 0, (ms, alphas, mxs), do_pv=False, first=True)

    def body(t, carry):
        return stage(2 * t + 2, 0, stage(2 * t + 1, 1, carry))

    carry = lax.fori_loop(0, (n - 2) // 2, body, carry)
    _, alphas, _ = stage(n - 1, 1, carry, do_qk=False)
    for h in range(nh):
        acc_ref[h] = alphas[h] * acc_ref[h] + pv_fn(n - 1, h, p_buf[1, h])


def _chunk_off(j, tk):
    return j * tk if isinstance(j, int) else pl.multiple_of(j * tk, tk)


def _attn_a_body(cs_ref, ctab_ref, lam_ref, g_ref, q_ref, k_ref, kpos_ref, vt_ref, o_ref,
                 s_buf, p_buf, acc_ref, *, seq, tq, tk, hps, lam_init):
    hg = pl.program_id(1)
    qi = pl.program_id(2)
    n = seq // tk
    cd = (qi * tq) // tk
    lane = lax.broadcasted_iota(jnp.int32, (tq, LANES), 1)
    qpos = (qi * tq + lax.broadcasted_iota(jnp.int32, (1, tq), 1)).astype(F32)
    dv = vt_ref.shape[2] - SUM_ROWS
    c_slope, f0, q_left, q_right = [], [], [], []
    for hh in range(hps):
        head = hg * hps + hh
        c_slope.append(cs_ref[head])
        f0.append(c_slope[hh] * qpos)
        q = q_ref[0, :, hh * LANES:(hh + 1) * LANES]
        qaug = jnp.broadcast_to(ctab_ref[pl.ds(head, 1), :], (tq, LANES)).astype(BF)
        for half in range(2):
            q_m = jnp.where((lane >= 64 * half) & (lane < 64 * (half + 1)), q, jnp.zeros_like(q))
            q_left.append(jnp.concatenate([q_m, qaug], axis=1))
            q_right.append(jnp.concatenate([q_m, -qaug], axis=1))

    def chunk_of(j):
        jm = j - 1
        c = jnp.where(j == 0, cd, jm + (jm >= cd).astype(jnp.int32))
        return c, c <= cd

    def qk_fn(j, mm):
        hh = mm // 2
        c, left = chunk_of(j)
        off = pl.multiple_of(c * tk, tk)
        k_c = jnp.concatenate([k_ref[0, pl.ds(off, tk), hh * LANES:(hh + 1) * LANES],
                               kpos_ref[pl.ds(off, tk), :]], axis=1)
        q_full = jnp.where(left, q_left[mm], q_right[mm])
        return lax.dot_general(k_c, q_full, NT_DIMS, preferred_element_type=F32)

    def sm_fn(j, mm, s, m, mx, first):
        hh = mm // 2
        if first:
            d = (lax.broadcasted_iota(jnp.int32, (tk, tq), 0) - lax.broadcasted_iota(jnp.int32, (tk, tq), 1)
                 + (cd * tk - qi * tq)).astype(F32)
            return _softmax_chunk(s - (2.0 * c_slope[hh]) * jnp.maximum(d, 0.0), m, f0[hh])
        _, left = chunk_of(j)
        return _softmax_chunk(s, m, jnp.where(left, f0[hh], -f0[hh]), mx)

    def pv_fn(j, mm, p):
        c, _ = chunk_of(j)
        off = pl.multiple_of(c * tk, tk)
        return jnp.dot(vt_ref[0, mm // 2, :, pl.ds(off, tk)], p, preferred_element_type=F32)

    _pipelined_flash(n, 2 * hps, tq, qk_fn, sm_fn, pv_fn, s_buf, p_buf, acc_ref)
    lf = lam_ref[...]
    lam_full = (jnp.exp(jnp.sum(lf[0:1] * lf[1:2], axis=-1, keepdims=True))
                - jnp.exp(jnp.sum(lf[2:3] * lf[3:4], axis=-1, keepdims=True)) + lam_init)
    outs = []
    for hh in range(hps):
        o0, o1 = [acc_ref[2 * hh + half, :dv, :] / acc_ref[2 * hh + half, dv:dv + 1, :] for half in range(2)]
        o = o0 - lam_full * o1
        ms = jnp.mean(o * o, axis=0, keepdims=True)
        outs.append(o * lax.rsqrt(ms + RMS_EPS) * g_ref[...] * (1.0 - lam_init))
    o_ref[0] = jnp.concatenate(outs, axis=0).T.astype(o_ref.dtype)


def _bf16_pieces(c, n=3):
    pieces, rest = [], c.astype(F32)
    for _ in range(n):
        p = rest.astype(BF)
        pieces.append(p.astype(F32))
        rest = rest - p.astype(F32)
    return pieces


def _store_vt_heads(vt_ref, vt, dv, heads_per_slot):
    tm = vt.shape[1]
    tail = jnp.where(lax.broadcasted_iota(jnp.int32, (SUM_ROWS, tm), 0) == 0, 1.0, 0.0).astype(vt_ref.dtype)
    rows = dv + SUM_ROWS
    for h in range(vt.shape[0] // dv):
        slot, r0 = h // heads_per_slot, (h % heads_per_slot) * rows
        vt_ref[0, slot, r0:r0 + dv, :] = vt[h * dv:(h + 1) * dv, :].astype(vt_ref.dtype)
        vt_ref[0, slot, r0 + dv:r0 + rows, :] = tail


def _proj_a_body(x_ref, wqk_ref, wvt_ref, qk_ref, vt_ref):
    xb = x_ref[...].astype(BF)
    qk_ref[...] = jnp.dot(xb, wqk_ref[...], preferred_element_type=F32).astype(BF)
    vt = lax.dot_general(wvt_ref[...], xb, NT_DIMS, preferred_element_type=F32)
    _store_vt_heads(vt_ref, vt, 2 * A_HEAD_DIM, 1)


def _flash_chunk_len(seq, tk_max):
    for min_chunks in (8, 4):
        tk = tk_max
        while tk >= 2 * LANES and (seq % tk or (seq // tk) < min_chunks or (seq // tk) % 2):
            tk //= 2
        if tk >= 2 * LANES:
            return tk
    raise ValueError(f"sequence length {seq} too short for the pipelined attention kernel")


def _flash_scratch(nh, rows, tq, tk):
    return [pltpu.VMEM((2, nh, tk, tq), F32), pltpu.VMEM((2, nh, tk, tq), BF), pltpu.VMEM((nh, rows, tq), F32)]


def _attn_a(qkv, vt, lam, subln_g, lam_init, tq=256, tk=1024):
    bsz, seq, _ = qkv.shape
    tq, tk = min(tq, seq), _flash_chunk_len(seq, tk)
    c_slope = jnp.asarray(2.0 ** (-8.0 * np.arange(1, A_HEADS + 1) / A_HEADS) * LOG2E, dtype=F32)
    c1, c2, c3 = _bf16_pieces(c_slope)
    ctab = jnp.stack([c1, c1, c2, c2, c3, c3], axis=-1)
    ctab = jnp.concatenate([ctab, jnp.zeros((A_HEADS, LANES - 6), F32)], axis=-1)
    pos = np.arange(seq)
    kpos = np.zeros((seq, LANES), np.float32)
    for j in range(3):
        kpos[:, 2 * j] = (pos // LANES) * LANES
        kpos[:, 2 * j + 1] = pos % LANES
    kpos = jnp.asarray(kpos, BF)
    vrows = vt.shape[2]
    hps = 2 if seq <= 4096 else 1
    ngrp = A_HEADS // hps
    body = functools.partial(_attn_a_body, seq=seq, tq=tq, tk=tk, hps=hps, lam_init=lam_init)
    return pl.pallas_call(
        body,
        grid=(bsz, ngrp, seq // tq),
        in_specs=[
            pl.BlockSpec(memory_space=pltpu.SMEM),
            pl.BlockSpec((A_HEADS, LANES), lambda b, h, i: (0, 0)),
            pl.BlockSpec((4, A_HEAD_DIM), lambda b, h, i: (0, 0)),
            pl.BlockSpec((2 * A_HEAD_DIM, 1), lambda b, h, i: (0, 0)),
            pl.BlockSpec((1, tq, hps * LANES), lambda b, h, i: (b, i, h)),
            pl.BlockSpec((1, seq, hps * LANES), lambda b, h, i: (b, 0, ngrp + h)),
            pl.BlockSpec((seq, LANES), lambda b, h, i: (0, 0)),
            pl.BlockSpec((1, hps, vrows, seq), lambda b, h, i: (b, h, 0, 0)),
        ],
        out_specs=pl.BlockSpec((1, tq, hps * LANES), lambda b, h, i: (b, i, h)),
        out_shape=jax.ShapeDtypeStruct((bsz, seq, A_HEADS * LANES), BF),
        scratch_shapes=_flash_scratch(2 * hps, vrows, tq, tk),
        compiler_params=_cparams("parallel", "parallel", "arbitrary"),
        name="attn_a",
    )(c_slope, ctab, lam.astype(F32), subln_g.astype(F32).reshape(2 * A_HEAD_DIM, 1), qkv, qkv, kpos, vt)


def _attn_b_body(q_ref, k_ref, vt_ref, o_ref, s_buf, p_buf, acc_ref, *, seq, tk, nh):
    tq = q_ref.shape[1]
    rows = B_VDIM + SUM_ROWS
    qs = [q_ref[0, :, h * LANES:(h + 1) * LANES] for h in range(nh)]

    def qk_fn(j, h):
        k_c = k_ref[0, pl.ds(_chunk_off(j, tk), tk), h * LANES:(h + 1) * LANES]
        return lax.dot_general(k_c, qs[h], NT_DIMS, preferred_element_type=F32)

    def sm_fn(j, h, s, m, mx, first):
        return _softmax_chunk(s, m, None, mx)

    def pv_fn(j, h, p):
        vt_c = vt_ref[0, h // 2, (h % 2) * rows:(h % 2 + 1) * rows, pl.ds(_chunk_off(j, tk), tk)]
        return jnp.dot(vt_c, p, preferred_element_type=F32)

    _pipelined_flash(seq // tk, nh, tq, qk_fn, sm_fn, pv_fn, s_buf, p_buf, acc_ref)
    outs = [acc_ref[h, :B_VDIM, :] / acc_ref[h, B_VDIM:B_VDIM + 1, :] for h in range(nh)]
    o = jnp.concatenate(outs, axis=0)
    o_ref[0] = o.T.astype(o_ref.dtype)


def _attn_b(q, k, vt, tq=256, tk=1024):
    bsz, seq, _ = q.shape
    tq, tk = min(tq, seq), _flash_chunk_len(seq, tk)
    nh = 4 if seq <= 4096 else 2
    return pl.pallas_call(
        functools.partial(_attn_b_body, seq=seq, tk=tk, nh=nh),
        grid=(bsz, B_HEADS // nh, seq // tq),
        in_specs=[
            pl.BlockSpec((1, tq, nh * LANES), lambda b, h, i: (b, i, h)),
            pl.BlockSpec((1, seq, nh * LANES), lambda b, h, i: (b, 0, h)),
            pl.BlockSpec((1, nh // 2, 2 * (B_VDIM + SUM_ROWS), seq), lambda b, h, i: (b, h, 0, 0)),
        ],
        out_specs=pl.BlockSpec((1, tq, nh * B_VDIM), lambda b, h, i: (b, i, h)),
        out_shape=jax.ShapeDtypeStruct((bsz, seq, B_HEADS * B_VDIM), BF),
        scratch_shapes=_flash_scratch(nh, B_VDIM + SUM_ROWS, tq, tk),
        compiler_params=_cparams("parallel", "parallel", "arbitrary"),
        name="attn_b",
    )(q, k, vt)


def _mla_in_body(x_ref, w_ref, gq_ref, gkv_ref, rc_ref, rs_ref, cq_ref, ckv_ref, kr_ref):
    h = jnp.dot(x_ref[...].astype(BF), w_ref[...], preferred_element_type=F32)
    cq_ref[...] = _rms_rows(h[:, :B_Q_LORA], gq_ref[...]).astype(BF)
    ckv_ref[...] = _rms_rows(h[:, B_Q_LORA:B_Q_LORA + B_KV_LORA], gkv_ref[...]).astype(BF)
    t = h[:, B_Q_LORA + B_KV_LORA:]
    kr = t * rc_ref[...] + pltpu.roll(t, LANES - B_ROPE, 1) * rs_ref[...]
    kr_ref[...] = kr.astype(BF)


def _mla_q_body(cq_ref, w_ref, ta_ref, tb_ref, q_ref):
    t = jnp.dot(cq_ref[...], w_ref[...], preferred_element_type=F32)
    ta = ta_ref[...]
    tb = tb_ref[...]
    for h in range(B_HEADS):
        th = t[:, h * LANES:(h + 1) * LANES]
        q_ref[:, h * LANES:(h + 1) * LANES] = (th * ta + pltpu.roll(th, LANES - B_ROPE, 1) * tb).astype(BF)


def _mla_kv_body(ckv_ref, kr_ref, wk_ref, e_ref, wvt_ref, k_ref, vt_ref):
    ckv = ckv_ref[...]
    k = jnp.dot(ckv, wk_ref[...], preferred_element_type=F32)
    k = k + jnp.dot(kr_ref[...], e_ref[...], preferred_element_type=F32)
    k_ref[...] = k.astype(BF)
    vt = lax.dot_general(wvt_ref[...], ckv, NT_DIMS, preferred_element_type=F32)
    _store_vt_heads(vt_ref, vt, B_VDIM, 2)


def _rope_partner(w):
    half = B_ROPE // 2
    return jnp.concatenate([-w[..., half:], w[..., :half]], axis=-1)


def _mla_tables(seq):
    inv = 1.0 / (ROPE_THETA ** (np.arange(0, B_ROPE, 2, dtype=np.float32) / B_ROPE))
    ang = jnp.arange(seq, dtype=F32)[:, None] * jnp.asarray(inv, F32)[None, :]
    cos = jnp.concatenate([jnp.cos(ang), jnp.cos(ang)], axis=-1)
    sin = jnp.concatenate([jnp.sin(ang), jnp.sin(ang)], axis=-1)
    z = lambda n: jnp.zeros((seq, n), F32)
    rc = jnp.concatenate([cos, z(LANES - B_ROPE)], axis=-1)
    rs = jnp.concatenate([sin, z(LANES - B_ROPE)], axis=-1)
    scale = (B_NOPE + B_ROPE) ** -0.5 * LOG2E
    ta =jnp.concatenate([jnp.full((seq, B_NOPE), scale, F32), cos * scale, z(B_ROPE)], axis=-1)
    tb = jnp.concatenate([z(B_NOPE), sin * scale, z(B_ROPE)], axis=-1)
    return rc, rs, ta, tb


def _mla_weights(w_in, w_uq, w_ukv):
    d = w_in.shape[0]
    kr0 = B_Q_LORA + B_KV_LORA
    w_in_ext = jnp.concatenate(
        [w_in, _rope_partner(w_in[:, kr0:kr0 + B_ROPE]), jnp.zeros((d, LANES - 2 * B_ROPE), w_in.dtype)], axis=-1)
    wq = w_uq.reshape(B_Q_LORA, B_HEADS, B_NOPE + B_ROPE)
    wq_ext = jnp.concatenate([wq, _rope_partner(wq[..., B_NOPE:])], axis=-1).reshape(B_Q_LORA, B_HEADS * LANES)
    wkv = w_ukv.reshape(B_KV_LORA, B_HEADS, B_NOPE + B_VDIM)
    wk = jnp.concatenate([wkv[..., :B_NOPE], jnp.zeros((B_KV_LORA, B_HEADS, LANES - B_NOPE), w_ukv.dtype)], axis=-1)
    wk = wk.reshape(B_KV_LORA, B_HEADS * LANES)
    wv = wkv[..., B_NOPE:].reshape(B_KV_LORA, B_HEADS * B_VDIM)
    place = np.zeros((LANES, B_HEADS, LANES), np.float32)
    for j in range(B_ROPE):
        place[j, :, B_NOPE + j] = 1.0
    place = jnp.asarray(place.reshape(LANES, B_HEADS * LANES), BF)
    return w_in_ext.astype(BF), wq_ext.astype(BF), wk.astype(BF), place, wv.astype(BF)


def _mixer_b(x, w_in, q_norm_g, kv_norm_g, w_uq, w_ukv, tm=512):
    bsz, seq, d = x.shape
    t = bsz * seq
    tm = min(tm, seq)
    nblk = seq // tm
    w_in_ext, wq_ext, wk, place, wv = _mla_weights(w_in, w_uq, w_ukv)
    rc, rs, ta, tb = _mla_tables(seq)
    x2 = x.reshape(t, d)
    n_in = w_in_ext.shape[1]
    row = lambda i: (i, 0)
    fixed = lambda i: (0, 0)
    pos = lambda i: (i % nblk, 0)
    cq, ckv, kr = pl.pallas_call(
        _mla_in_body,
        grid=(t // tm,),
        in_specs=[
            pl.BlockSpec((tm, d), row), pl.BlockSpec((d, n_in), fixed),
            pl.BlockSpec((1, B_Q_LORA), fixed), pl.BlockSpec((1, B_KV_LORA), fixed),
            pl.BlockSpec((tm, LANES), pos), pl.BlockSpec((tm, LANES), pos),
        ],
        out_specs=[pl.BlockSpec((tm, B_Q_LORA), row), pl.BlockSpec((tm, B_KV_LORA), row),
                   pl.BlockSpec((tm, LANES), row)],
        out_shape=[jax.ShapeDtypeStruct((t, B_Q_LORA), BF), jax.ShapeDtypeStruct((t, B_KV_LORA), BF),
                   jax.ShapeDtypeStruct((t, LANES), BF)],
        compiler_params=_cparams("parallel"),
        name="mla_in",
    )(x2, w_in_ext, q_norm_g.astype(F32).reshape(1, -1), kv_norm_g.astype(F32).reshape(1, -1), rc, rs)
    nq = B_HEADS * LANES
    q = pl.pallas_call(
        _mla_q_body,
        grid=(t // tm,),
        in_specs=[pl.BlockSpec((tm, B_Q_LORA), row), pl.BlockSpec((B_Q_LORA, nq), fixed),
                  pl.BlockSpec((tm, LANES), pos), pl.BlockSpec((tm, LANES), pos)],
        out_specs=pl.BlockSpec((tm, nq), row),
        out_shape=jax.ShapeDtypeStruct((t, nq), BF),
        compiler_params=_cparams("parallel"),
        name="mla_q",
    )(cq, wq_ext, ta, tb)
    nv = B_HEADS * B_VDIM
    npair = B_HEADS // 2
    vrows = 2 * (B_VDIM + SUM_ROWS)
    k, vt = pl.pallas_call(
        _mla_kv_body,
        grid=(t // tm,),
        in_specs=[pl.BlockSpec((tm, B_KV_LORA), row), pl.BlockSpec((tm, LANES), row),
                  pl.BlockSpec((B_KV_LORA, nq), fixed), pl.BlockSpec((LANES, nq), fixed),
                  pl.BlockSpec((nv, B_KV_LORA), fixed)],
        out_specs=[pl.BlockSpec((tm, nq), row),
                   pl.BlockSpec((1, npair, vrows, tm), lambda i: (i // nblk, 0, 0, i % nblk))],
        out_shape=[jax.ShapeDtypeStruct((t, nq), BF), jax.ShapeDtypeStruct((bsz, npair, vrows, seq), BF)],
        compiler_params=_cparams("parallel"),
        name="mla_kv",
    )(ckv, kr, wk, place, wv.T)
    o = _attn_b(q.reshape(bsz, seq, nq), k.reshape(bsz, seq, nq), vt)
    return o.reshape(t, nv)


def _mixer_a(x, w_in, lam, subln_g, layer_idx):
    bsz, seq, d = x.shape
    t = bsz * seq
    hd2 = 2 * A_HEAD_DIM
    nq = A_HEADS * hd2
    scale = A_HEAD_DIM ** -0.5 * LOG2E
    wqk = jnp.concatenate([w_in[:, :nq] * scale, w_in[:, nq:2 * nq]], axis=-1).astype(BF)
    wvt = w_in[:, 2 * nq:].T.astype(BF)
    tm = min(512, seq)
    nblk = seq // tm
    vrows = hd2 + SUM_ROWS
    qk, vt = pl.pallas_call(
        _proj_a_body,
        grid=(t // tm,),
        in_specs=[pl.BlockSpec((tm, d), lambda i: (i, 0)), pl.BlockSpec((d, 2 * nq), lambda i: (0, 0)),
                  pl.BlockSpec((nq, d), lambda i: (0, 0))],
        out_specs=[pl.BlockSpec((tm, 2 * nq), lambda i: (i, 0)),
                   pl.BlockSpec((1, A_HEADS, vrows, tm), lambda i: (i // nblk, 0, 0, i % nblk))],
        out_shape=[jax.ShapeDtypeStruct((t, 2 * nq), BF), jax.ShapeDtypeStruct((bsz, A_HEADS, vrows, seq), BF)],
        compiler_params=_cparams("parallel"),
        name="proj_a",
    )(x.reshape(t, d), wqk, wvt)
    lam_init = 0.8 - 0.6 * math.exp(-0.3 * layer_idx)
    o = _attn_a(qk.reshape(bsz, seq, 2 * nq), vt, lam, subln_g, lam_init)
    return o.reshape(t, nq)


def _attn_c_body(q_ref, k0, k1, k2, k3, v0, v1, v2, v3, bt_ref, o_ref):
    kw = jnp.concatenate([k0[0], k1[0], k2[0], k3[0]], axis=0)
    vw = jnp.concatenate([v0[0], v1[0], v2[0], v3[0]], axis=0)
    nq = q_ref.shape[1] // 2
    nk = bt_ref.shape[3]
    lane = lax.broadcasted_iota(jnp.int32, (nq, LANES), 1)
    for hf in range(2):
        q = q_ref[0, hf * nq:(hf + 1) * nq, :]
        kh = kw[hf * nq:hf * nq + nk]
        vh = vw[hf * nq:hf * nq + nk]
        outs = []
        for hh in range(2):
            qm = jnp.where((lane >= 64 * hh) & (lane < 64 * (hh + 1)), q, jnp.zeros_like(q))
            s = (lax.dot_general(qm, kh, NT_DIMS, preferred_element_type=F32)
                 + bt_ref[hh, 0, hf * nq:(hf + 1) * nq, :])
            m = jnp.max(s, axis=1, keepdims=True)
            p = jnp.exp2(s - m)
            l = jnp.sum(p, axis=1, keepdims=True)
            outs.append(jnp.dot(p.astype(BF), vh, preferred_element_type=F32) / l)
        o_ref[0, hf * nq:(hf + 1) * nq, :] = jnp.where(lane < 64, outs[0], outs[1]).astype(o_ref.dtype)


def _nbr_bias_table(rpb):
    col = np.arange(GRID_W)
    col_start = np.clip(col - C_WIN_COLS // 2, 0, GRID_W - C_WIN_COLS)
    col_mask = (col[None, :] >= col_start[:, None]) & (col[None, :] < col_start[:, None] + C_WIN_COLS)
    pad = GRID_W - C_WIN_COLS
    ext = jnp.pad(rpb.astype(F32) * LOG2E, ((0, 0), (0, 0), (pad, pad)), mode="edge")
    toep = jnp.stack([ext[:, :, GRID_W - 1 - qc:2 * GRID_W - 1 - qc] for qc in range(GRID_W)], axis=2)
    toep = jnp.where(jnp.asarray(col_mask)[None, None], toep, NEG)
    neg = jnp.full((rpb.shape[0], GRID_W, GRID_W), NEG, F32)
    half = C_WIN_ROWS // 2
    kinds = []
    for kind in range(3):
        qrows = []
        for t in range(C_WIN_ROWS):
            u0 = (max(t, half), t, min(t, half))[kind]
            ubase = 0 if t < half else half
            blocks = [toep[:, u - t + half - 1] if u0 <= u < u0 + C_WIN_ROWS else neg
                      for u in range(ubase, ubase + C_WIN_ROWS + half)]
            qrows.append(jnp.concatenate(blocks, axis=-1))
        kinds.append(jnp.concatenate(qrows, axis=1))
    return jnp.stack(kinds, axis=1)


def _mixer_c(x, w_qkv, rpb):
    bsz, seq, d = x.shape
    t = bsz * seq
    nq = C_HEADS * C_HEAD_DIM
    scale = C_HEAD_DIM ** -0.5 * LOG2E
    w = jnp.concatenate([w_qkv[:, :nq] * scale, w_qkv[:, nq:]], axis=-1).astype(BF)
    qkv = _linear(x.reshape(t, d), w, BF, tn=nq).reshape(bsz, seq, 3 * nq)
    bt = _nbr_bias_table(rpb)
    npair = C_HEADS // 2
    qtok = C_WIN_ROWS * GRID_W
    ngrp = seq // qtok
    assert ngrp >= 2
    kb = qtok // 2
    nkb = seq // kb
    kspecs = []
    for off in (npair, 2 * npair):
        for j in range(4):
            kspecs.append(pl.BlockSpec(
                (1, kb, LANES),
                lambda h, g, b, j=j, off=off: (b, jnp.clip(2 * g - 1 + j, 0, nkb - 1), off + h)))
    kind = lambda g: jnp.where(g == 0, 0, jnp.where(g == ngrp - 1, 2, 1))
    o = pl.pallas_call(
        _attn_c_body,
        grid=(npair, ngrp, bsz),
        in_specs=[pl.BlockSpec((1, qtok, LANES), lambda h, g, b: (b, g, h))] + kspecs + [
            pl.BlockSpec((2, 1, qtok, bt.shape[3]), lambda h, g, b: (h, kind(g), 0, 0))],
        out_specs=pl.BlockSpec((1, qtok, LANES), lambda h, g, b: (b, g, h)),
        out_shape=jax.ShapeDtypeStruct((bsz, seq, nq), BF),
        compiler_params=_cparams("parallel", "parallel", "arbitrary"),
        name="attn_c",
    )(qkv, *([qkv] * 8), bt)
    return o.reshape(t, nq)


def _attn_d_body(bias_ref, q_ref, kp, kc, kn, vp, vc, vn, o_ref, lse_ref, *, tq, length, rad):
    i = pl.program_id(2)
    nk = tq + 2 * rad
    ki = i * tq - rad + lax.broadcasted_iota(jnp.int32, (1, nk), 1)
    edge = jnp.where((ki >= 0) & (ki < length), 0.0, NEG)
    lane = lax.broadcasted_iota(jnp.int32, (tq, LANES), 1)
    for hp in range(D_HEADS // 2):
        sl = slice(hp * LANES, (hp + 1) * LANES)
        q = q_ref[0, 0, :, sl]
        kw = jnp.concatenate([kp[0, 0, tq - rad:tq, sl], kc[0, 0, :, sl], kn[0, 0, 0:rad, sl]], axis=0)
        vw = jnp.concatenate([vp[0, 0, tq - rad:tq, sl], vc[0, 0, :, sl], vn[0, 0, 0:rad, sl]], axis=0)
        outs, lses = [], []
        for hh in range(2):
            qm = jnp.where((lane >= 64 * hh) & (lane < 64 * (hh + 1)), q, jnp.zeros_like(q))
            s = lax.dot_general(qm, kw, NT_DIMS, preferred_element_type=F32)
            s = s + (bias_ref[hp * 2 + hh] + edge)
            m = jnp.max(s, axis=1, keepdims=True)
            p = jnp.exp2(s - m)
            l = jnp.sum(p, axis=1, keepdims=True)
            outs.append(jnp.dot(p.astype(BF), vw, preferred_element_type=F32) / l)
            lses.append(m + jnp.log(l) * LOG2E)
        o_ref[0, 0, :, sl] = jnp.where(lane < 64, outs[0], outs[1]).astype(o_ref.dtype)
        lse_ref[0, 0, :, sl] = jnp.where(lane < 64, lses[0], lses[1])


def _attn_d_group(qkv, window, dil, tq=128):
    bsz, _, length, ncol = qkv.shape
    rad = window // (2 * dil)
    tq = min(tq, length)
    nq = length // tq
    nh = ncol // 3
    nk = tq + 2 * rad
    slopes = 2.0 ** (-8.0 * np.arange(1, D_HEADS + 1) / D_HEADS) * LOG2E
    dist = np.abs(np.arange(nk)[None, :] - rad - np.arange(tq)[:, None])
    bias = np.where(dist <= rad, -slopes[:, None, None] * (dist * dil), NEG)
    bias = jnp.asarray(bias, F32)

    def spec(which, shift):
        return pl.BlockSpec((1, 1, tq, nh), lambda b, c, i: (b, c, jnp.clip(i + shift, 0, nq - 1), which))

    ospec = pl.BlockSpec((1, 1, tq, nh), lambda b, c, i: (b, c, i, 0))
    return pl.pallas_call(
        functools.partial(_attn_d_body, tq=tq, length=length, rad=rad),
        grid=(bsz, dil, nq),
        in_specs=[pl.BlockSpec(bias.shape, lambda b, c, i: (0, 0, 0)), spec(0, 0),
                  spec(1, -1), spec(1, 0), spec(1, 1), spec(2, -1), spec(2, 0), spec(2, 1)],
        out_specs=[ospec, ospec],
        out_shape=[jax.ShapeDtypeStruct((bsz, dil, length, nh), BF),
                   jax.ShapeDtypeStruct((bsz, dil, length, nh), F32)],
        compiler_params=_cparams("parallel", "parallel", "arbitrary"),
        name=f"attn_d{dil}",
    )(bias, qkv, qkv, qkv, qkv, qkv, qkv, qkv)


def _proj_d_body(x_ref, w_ref, o0, o1, o2, scr, *, dils):
    xb = x_ref[...].astype(BF)
    tm = xb.shape[0]
    nblk = scr.shape[0]
    ncol = nblk * LANES
    for gi, (o_ref, dil) in enumerate(zip((o0, o1, o2), dils)):
        res = jnp.dot(xb, w_ref[:, gi * ncol:(gi + 1) * ncol], preferred_element_type=F32)
        if dil == 1:
            o_ref[0, 0] = res.astype(BF)
            continue
        for j in range(nblk):
            scr[j] = res[:, j * LANES:(j + 1) * LANES]
        for c in range(dil):
            for j in range(nblk):
                o_ref[0, c, :, j * LANES:(j + 1) * LANES] = scr[j, pl.ds(c, tm // dil, stride=dil), :].astype(BF)


def _post_d_body(o0, o1, o2, l0, l1, l2, x_ref, w_ref, g_ref, b_ref, y_ref, scr_o, scr_l, *, alpha, dils):
    tm = x_ref.shape[0]

    def interleaved(ref, scr, dil):
        if dil == 1:
            return ref[0, 0].astype(F32)
        nblk = scr.shape[0]
        for c in range(dil):
            for j in range(nblk):
                scr[j, pl.ds(c, tm // dil, stride=dil), :] = ref[0, c, :, j * LANES:(j + 1) * LANES].astype(F32)
        return jnp.concatenate([scr[j] for j in range(nblk)], axis=1)

    ov = [interleaved(r, scr_o.at[j], dil) for j, (r, dil) in enumerate(zip((o0, o1, o2), dils))]
    a0, a1, a2 = [interleaved(r, scr_l.at[j], dil) for j, (r, dil) in enumerate(zip((l0, l1, l2), dils))]
    m = jnp.maximum(jnp.maximum(a0, a1), a2)
    e0, e1, e2 = jnp.exp2(a0 - m), jnp.exp2(a1 - m), jnp.exp2(a2 - m)
    den = e0 + e1 + e2
    o = (e0 / den) * ov[0] + (e1 / den) * ov[1] + (e2 / den) * ov[2]
    h = jnp.dot(o.astype(BF), w_ref[...], preferred_element_type=F32)
    y_ref[...] = _layer_norm_rows(alpha * x_ref[...] + h, g_ref[...], b_ref[...])


def _mixer_d_and_post(x2, bsz, seq, w_qkv, w_out, g, b, alpha, tm=512):
    t, d = x2.shape
    nh = D_HEADS * D_HEAD_DIM
    ng = len(D_GROUPS)
    dils = tuple(dil for _, dil in D_GROUPS)
    tm = min(tm, seq)
    nblk = seq // tm
    scale = D_HEAD_DIM ** -0.5 * LOG2E
    wq = w_qkv.reshape(d, 3, ng, nh)
    wq = jnp.stack([wq[:, 0] * scale, wq[:, 1], wq[:, 2]], axis=1)
    w = wq.transpose(0, 2, 1, 3).reshape(d, ng * 3 * nh).astype(BF)

    def deint_spec(dil, width):
        return pl.BlockSpec((1, dil, tm // dil, width), lambda i: (i // nblk, 0, i % nblk, 0))

    qkvs = pl.pallas_call(
        functools.partial(_proj_d_body, dils=dils),
        grid=(t // tm,),
        in_specs=[pl.BlockSpec((tm, d), lambda i: (i, 0)), pl.BlockSpec((d, ng * 3 * nh), lambda i: (0, 0))],
        out_specs=[deint_spec(dil, 3 * nh) for dil in dils],
        out_shape=[jax.ShapeDtypeStruct((bsz, dil, seq // dil, 3 * nh), BF) for dil in dils],
        scratch_shapes=[pltpu.VMEM((3 * nh // LANES, tm, LANES), F32)],
        compiler_params=_cparams("parallel"),
        name="proj_d",
    )(x2, w)
    os_, ls_ = [], []
    for qkv_g, (window, dil) in zip(qkvs, D_GROUPS):
        o, lse = _attn_d_group(qkv_g, window, dil)
        os_.append(o)
        ls_.append(lse)
    row = lambda i: (i, 0)
    fixed = lambda i: (0, 0)
    return pl.pallas_call(
        functools.partial(_post_d_body, alpha=alpha, dils=dils),
        grid=(t // tm,),
        in_specs=[deint_spec(dil, nh) for dil in dils] * 2 + [
            pl.BlockSpec((tm, d), row), pl.BlockSpec((nh, d), fixed),
            pl.BlockSpec((1, d), fixed), pl.BlockSpec((1, d), fixed)],
        out_specs=pl.BlockSpec((tm, d), row),
        out_shape=jax.ShapeDtypeStruct((t, d), F32),
        scratch_shapes=[pltpu.VMEM((ng, nh // LANES, tm, LANES), F32), pltpu.VMEM((ng, nh // LANES, tm, LANES), F32)],
        compiler_params=_cparams("parallel"),
        name="post_d",
    )(*os_, *ls_, x2, w_out.astype(BF), g.reshape(1, d), b.reshape(1, d))


def _router_gates(x, w2_ref, rb_ref):
    tm = x.shape[0]
    xh = x.astype(BF)
    xl = (x - xh.astype(F32)).astype(BF)
    w2 = w2_ref[...]
    both = lax.dot_general(w2, xh, NT_DIMS, preferred_element_type=F32)
    logits = (both[:N_EXPERTS] + both[N_EXPERTS:]
              + lax.dot_general(w2[:N_EXPERTS], xl, NT_DIMS, preferred_element_type=F32))
    scores = jax.nn.sigmoid(logits)
    biased = scores + rb_ref[...]
    epg = N_EXPERTS // N_EXPERT_GROUPS
    sc = [scores[e:e + 1, :] for e in range(N_EXPERTS)]
    bi = [biased[e:e + 1, :] for e in range(N_EXPERTS)]
    gs = []
    for g in range(N_EXPERT_GROUPS):
        v = bi[g * epg:(g + 1) * epg]
        best = None
        for a in range(epg):
            for c in range(a + 1, epg):
                pair = v[a] + v[c]
                best = pair if best is None else jnp.maximum(best, pair)
        gs.append(best)
    gmax = functools.reduce(jnp.maximum, gs)
    taken = jnp.zeros((1, tm), jnp.bool_)
    cand = []
    for g in range(N_EXPERT_GROUPS):
        sel = (gs[g] == gmax) & jnp.logical_not(taken)
        taken = taken | sel
        for a in range(epg):
            cand.append(jnp.where(sel, bi[g * epg + a], -jnp.inf))
    m1 = functools.reduce(jnp.maximum, cand)
    taken = jnp.zeros((1, tm), jnp.bool_)
    is1 = []
    for e in range(N_EXPERTS):
        hit = (cand[e] == m1) & jnp.logical_not(taken)
        taken = taken | hit
        is1.append(hit)
    cand2 = [jnp.where(is1[e], -jnp.inf, cand[e]) for e in range(N_EXPERTS)]
    m2 = functools.reduce(jnp.maximum, cand2)
    taken = jnp.zeros((1, tm), jnp.bool_)
    is2 = []
    for e in range(N_EXPERTS):
        hit = (cand2[e] == m2) & jnp.logical_not(taken)
        taken = taken | hit
        is2.append(hit)
    zero = jnp.zeros((1, tm), F32)
    w1 = functools.reduce(jnp.add, [jnp.where(is1[e], sc[e], zero) for e in range(N_EXPERTS)])
    w2 = functools.reduce(jnp.add, [jnp.where(is2[e], sc[e], zero) for e in range(N_EXPERTS)])
    den = w1 + w2
    rows = [jnp.where(is1[e], w1 / den, zero) + jnp.where(is2[e], w2 / den, zero) for e in range(N_EXPERTS)]
    rows.append(jnp.zeros((LANES - N_EXPERTS, tm), F32))
    return jnp.concatenate(rows, axis=0).T


MOE_EXPERTS_PER_STEP = 2


def _moe_body(*refs, alpha, n_routed_steps, with_mixer_out):
    if with_mixer_out:
        o_ref, wo_ref, g1_ref, b1_ref, *refs = refs
    (x_ref, p_ref, w2_ref, rb_ref, wgu_ref, wd_ref, wgus_ref, wds_ref, g_ref, b_ref, wpi_ref, wpg_ref,
     y_ref, gates, xb, acc, x1) = refs
    e = pl.program_id(1)
    ff = EXPERT_FF

    def hidden(gu, j):
        return jax.nn.silu(gu[:, 2 * j * ff:(2 * j + 1) * ff]) * gu[:, (2 * j + 1) * ff:(2 * j + 2) * ff]

    tm = x_ref.shape[0]
    halves = [slice(r * tm // 2, (r + 1) * tm // 2) for r in range(2)]

    @pl.when(e == 0)
    def _():
        xs = [x_ref[rows, :] for rows in halves]
        if with_mixer_out:
            hs = [jnp.dot(o_ref[rows, :], wo_ref[...], preferred_element_type=F32) for rows in halves]
            xs = [_layer_norm_rows(alpha * x + h, g1_ref[...], b1_ref[...]) for x, h in zip(xs, hs)]
        for rows, x in zip(halves, xs):
            x1[rows, :] = x
            xb[rows, :] = x.astype(BF)
        for rows, x in zip(halves, xs):
            gates[rows, :] = _router_gates(x, w2_ref, rb_ref)
        acc[...] = jnp.zeros_like(acc)

    @pl.when(e < n_routed_steps)
    def _():
        gu = jnp.dot(xb[...], wgu_ref[0], preferred_element_type=F32)
        lane = lax.broadcasted_iota(jnp.int32, gates.shape, 1)
        gt = gates[...]
        hs = []
        for j in range(MOE_EXPERTS_PER_STEP):
            gcol = jnp.sum(jnp.where(lane == e * MOE_EXPERTS_PER_STEP + j, gt, 0.0), axis=1, keepdims=True)
            hs.append((hidden(gu, j) * gcol).astype(BF))
        acc[...] += jnp.dot(jnp.concatenate(hs, axis=1), wd_ref[0], preferred_element_type=F32)

    @pl.when(e == n_routed_steps)
    def _():
        gus = [jnp.dot(xb[rows, :], wgus_ref[...], preferred_element_type=F32) for rows in halves]
        hids = [hidden(gu, 0).astype(BF) for gu in gus]
        ys = [acc[rows, :] + jnp.dot(hid, wds_ref[...], preferred_element_type=F32)
              for rows, hid in zip(halves, hids)]
        x2s = [_layer_norm_rows(alpha * x1[rows, :] + y, g_ref[...], b_ref[...]) for rows, y in zip(halves, ys)]
        embs = [jnp.dot(p_ref[rows, :].astype(BF), wpi_ref[...], preferred_element_type=F32) for rows in halves]
        gts = [jnp.dot(x2.astype(BF), wpg_ref[...], preferred_element_type=F32) for x2 in x2s]
        for rows, x2, gt, emb in zip(halves, x2s, gts, embs):
            y_ref[rows, :] = x2 + jax.nn.sigmoid(gt) * emb


def _moe_layer(x2, p2, router, lw, g, b, alpha, mixer_out=None, tm=1024):
    t, d = x2.shape
    wgu, wd, wgus, wds, wpi, wpg = lw["wgu"], lw["wd"], lw["wgus"], lw["wds"], lw["wpi"], lw["wpg"]
    n_routed_steps = wgu.shape[0]
    w2, rb = router
    tok = lambda i, e: (i, 0)
    fixed = lambda i, e: (0, 0)
    step = lambda i, e: (jnp.minimum(e, n_routed_steps - 1), 0, 0)
    pre_specs, pre_args = [], []
    if mixer_out is not None:
        o, w_out, g1, b1 = mixer_out
        pre_specs = [pl.BlockSpec((tm, o.shape[1]), tok), pl.BlockSpec(w_out.shape, fixed),
                     pl.BlockSpec((1, d), fixed), pl.BlockSpec((1, d), fixed)]
        pre_args = [o, w_out, g1.reshape(1, d), b1.reshape(1, d)]
    return pl.pallas_call(
        functools.partial(_moe_body, alpha=alpha, n_routed_steps=n_routed_steps,
                          with_mixer_out=mixer_out is not None),
        grid=(t // tm, n_routed_steps + 1),
        in_specs=pre_specs + [
            pl.BlockSpec((tm, d), tok), pl.BlockSpec((tm, p2.shape[1]), tok),
            pl.BlockSpec((2 * N_EXPERTS, d), fixed), pl.BlockSpec((N_EXPERTS, 1), fixed),
            pl.BlockSpec((1,) + wgu.shape[1:], step), pl.BlockSpec((1,) + wd.shape[1:], step),
            pl.BlockSpec(wgus.shape, fixed), pl.BlockSpec(wds.shape, fixed),
            pl.BlockSpec((1, d), fixed), pl.BlockSpec((1, d), fixed),
            pl.BlockSpec(wpi.shape, fixed), pl.BlockSpec(wpg.shape, fixed),
        ],
        out_specs=pl.BlockSpec((tm, d), tok),
        out_shape=jax.ShapeDtypeStruct((t, d), F32),
        scratch_shapes=[pltpu.VMEM((tm, LANES), F32), pltpu.VMEM((tm, d), BF), pltpu.VMEM((tm, d), F32),
                        pltpu.VMEM((tm, d), F32)],
        compiler_params=_cparams("parallel", "arbitrary"),
        name="moe",
    )(*pre_args, x2, p2, w2, rb, wgu, wd, wgus, wds, g.reshape(1, d), b.reshape(1, d), wpi, wpg)


def _prep_shared(prm):
    depth = prm["ln1_g"].shape[0]
    rw = prm["router_w"].astype(F32)
    wh = rw.astype(BF)
    wl = (rw - wh.astype(F32)).astype(BF)
    router = (jnp.concatenate([wh.T, wl.T], axis=0), prm["router_b"].astype(F32).reshape(N_EXPERTS, 1))
    layers = []
    for i in range(depth):
        eps = MOE_EXPERTS_PER_STEP
        d = prm["moe_w_gate"].shape[2]
        wgu = jnp.concatenate([prm["moe_w_gate"][i], prm["moe_w_up"][i]], axis=-1)
        wgu = wgu.reshape(N_EXPERTS // eps, eps, d, 2 * EXPERT_FF).transpose(0, 2, 1, 3)
        wgu = wgu.reshape(N_EXPERTS // eps, d, eps * 2 * EXPERT_FF).astype(BF)
        wd = prm["moe_w_down"][i].reshape(N_EXPERTS // eps, eps * EXPERT_FF, d).astype(BF)
        wgus = jnp.concatenate([prm["moe_ws_gate"][i], prm["moe_ws_up"][i]], axis=-1).astype(BF)
        layers.append(dict(wgu=wgu, wd=wd, wgus=wgus, wds=prm["moe_ws_down"][i].astype(BF),
                           wpi=prm["ple_w_in"][i].astype(BF), wpg=prm["ple_w_gate"][i].astype(BF)))
    return router, layers


def _trunk(x, p, prm, router, layers):
    depth = prm["ln1_g"].shape[0]
    alpha = (2.0 * depth) ** 0.25
    bsz, seq, d = x.shape
    t = bsz * seq
    x2 = x.reshape(t, d)
    for i in range(depth):
        mixer, j = i % 4, i // 4
        xb = x2.reshape(bsz, seq, d)
        mixer_out = None
        if mixer == 3:
            x2 = _mixer_d_and_post(x2, bsz, seq, prm["d_w_qkv"][j], prm["d_w_out"][j],
                                   prm["ln1_g"][i], prm["ln1_b"][i], alpha)
        else:
            if mixer == 0:
                o = _mixer_a(xb, prm["a_w_in"][j], prm["a_lambda"][j], prm["a_subln"][j], i)
                w_out = prm["a_w_out"][j]
            elif mixer == 1:
                o = _mixer_b(xb, prm["b_w_in"][j], prm["b_q_norm"][j], prm["b_kv_norm"][j],
                             prm["b_w_uq"][j], prm["b_w_ukv"][j])
                w_out = prm["b_w_out"][j]
            else:
                o = _mixer_c(xb, prm["c_w_qkv"][j], prm["c_rpb"][j])
                w_out = prm["c_w_out"][j]
            mixer_out = (o, w_out.astype(BF), prm["ln1_g"][i], prm["ln1_b"][i])
        x2 = _moe_layer(x2, p[i].reshape(t, -1), router, layers[i], prm["ln2_g"][i], prm["ln2_b"][i], alpha,
                        mixer_out)
    return x2.reshape(bsz, seq, d)


def kernel(x_prompt, x_sample, p_prompt, p_sample, a_w_in, a_lambda, a_subln, a_w_out, b_w_in, b_q_norm, b_kv_norm, b_w_uq, b_w_ukv, b_w_out, c_w_qkv, c_rpb, c_w_out, d_w_qkv, d_w_out, router_w, router_b, moe_w_gate, moe_w_up, moe_w_down, moe_ws_gate, moe_ws_up, moe_ws_down, ln1_g, ln1_b, ln2_g, ln2_b, ple_w_in, ple_w_gate):
    prm = dict(a_w_in=a_w_in, a_lambda=a_lambda, a_subln=a_subln, a_w_out=a_w_out,
               b_w_in=b_w_in, b_q_norm=b_q_norm, b_kv_norm=b_kv_norm, b_w_uq=b_w_uq,
               b_w_ukv=b_w_ukv, b_w_out=b_w_out,
               c_w_qkv=c_w_qkv, c_rpb=c_rpb, c_w_out=c_w_out,
               d_w_qkv=d_w_qkv, d_w_out=d_w_out,
               router_w=router_w, router_b=router_b, moe_w_gate=moe_w_gate, moe_w_up=moe_w_up,
               moe_w_down=moe_w_down, moe_ws_gate=moe_ws_gate, moe_ws_up=moe_ws_up,
               moe_ws_down=moe_ws_down,
               ln1_g=ln1_g, ln1_b=ln1_b, ln2_g=ln2_g, ln2_b=ln2_b,
               ple_w_in=ple_w_in, ple_w_gate=ple_w_gate)
    router, layers = _prep_shared(prm)
    y_prompt = _trunk(x_prompt, p_prompt, prm, router, layers)
    y_sample = _trunk(x_sample, p_sample, prm, router, layers)
    return (y_prompt, y_sample)
```

```python
import functools
import math

import numpy as np
import jax
import jax.numpy as jnp
from jax import lax
from jax.experimental import pallas as pl
from jax.experimental.pallas import tpu as pltpu

BF = jnp.bfloat16
F32 = jnp.float32

VMEM_LIMIT_BYTES = 56 * 1024 * 1024
LANES = 128

GRID_W = 64
LN_EPS = 1e-5
RMS_EPS = 1e-6
ROPE_THETA = 10000.0
NEG = -1e30

A_HEADS, A_HEAD_DIM = 8, 64
B_HEADS, B_NOPE, B_ROPE, B_VDIM, B_Q_LORA, B_KV_LORA = 16, 64, 32, 64, 384, 256
C_HEADS, C_HEAD_DIM, C_WIN_ROWS, C_WIN_COLS = 16, 64, 8, 16
D_HEADS, D_HEAD_DIM = 8, 64
D_GROUPS = ((128, 1), (512, 4), (2048, 16))
N_EXPERTS, N_EXPERT_GROUPS, EXPERT_FF = 16, 4, 256

NT_DIMS = (((1,), (1,)), ((), ()))


def _cparams(*sem):
    return pltpu.CompilerParams(dimension_semantics=sem, vmem_limit_bytes=VMEM_LIMIT_BYTES)


def _layer_norm_rows(z, g, b):
    mu = jnp.mean(z, axis=-1, keepdims=True)
    zc = z - mu
    var = jnp.mean(zc * zc, axis=-1, keepdims=True)
    return zc * lax.rsqrt(var + LN_EPS) * g + b


def _rms_rows(z, g):
    return z * lax.rsqrt(jnp.mean(z * z, axis=-1, keepdims=True) + RMS_EPS) * g


def _linear_body(x_ref, w_ref, o_ref):
    o_ref[...] = jnp.dot(x_ref[...].astype(BF), w_ref[...], preferred_element_type=F32).astype(o_ref.dtype)


def _linear(x, w, out_dtype, tm=512, tn=None):
    m, k = x.shape
    n = w.shape[1]
    tn = n if tn is None else tn
    return pl.pallas_call(
        _linear_body,
        grid=(n // tn, m // tm),
        in_specs=[pl.BlockSpec((tm, k), lambda j, i: (i, 0)), pl.BlockSpec((k, tn), lambda j, i: (0, j))],
        out_specs=pl.BlockSpec((tm, tn), lambda j, i: (i, j)),
        out_shape=jax.ShapeDtypeStruct((m, n), out_dtype),
        compiler_params=_cparams("parallel", "parallel"),
        name="linear",
    )(x, w)


LOG2E = 1.4426950408889634
SUM_ROWS = 16


def _softmax_chunk(s, m, shift, mx=None):
    if mx is None:
        mx = jnp.max(s, axis=0, keepdims=True)
    if shift is not None:
        mx = mx - shift
    m_new = jnp.maximum(m, mx)
    ref = m_new if shift is None else m_new + shift
    return m_new, jnp.exp2(s - ref).astype(BF)


def _pipelined_flash(n, nh, tq, qk_fn, sm_fn, pv_fn, s_buf, p_buf, acc_ref):
    assert n >= 4 and n % 2 == 0
    acc_ref[...] = jnp.zeros_like(acc_ref)

    def scores(j, slot):
        mxs = []
        for h in range(nh):
            s = qk_fn(j, h)
            s_buf[slot, h] = s
            mxs.append(jnp.max(s, axis=0, keepdims=True))
        return tuple(mxs)

    def stage(j, slot, carry, do_pv=True, do_qk=True, first=False):
        ms, alphas, mxs = carry
        if do_pv:
            for h in range(nh):
                acc_ref[h] = alphas[h] * acc_ref[h] + pv_fn(j - 1, h, p_buf[1 - slot, h])
        mxs_next = scores(j + 1, 1 - slot) if do_qk else mxs
        new_ms, new_alphas = [], []
        for h in range(nh):
            m_new, p = sm_fn(j, h, s_buf[slot, h], ms[h], mxs[h], first)
            p_buf[slot, h] = p
            new_alphas.append(jnp.exp2(ms[h] - m_new))
            new_ms.append(m_new)
        return tuple(new_ms), tuple(new_alphas), mxs_next

    mxs = scores(0, 0)
    ms = tuple(jnp.full((1, tq), NEG, F32) for _ in range(nh))
    alphas = tuple(jnp.zeros((1, tq), F32) for _ in range(nh))
    carry = stage(0, 0, (ms, alphas, mxs), do_pv=False, first=True)

    def body(t, carry):
        return stage(2 * t + 2, 0, stage(2 * t + 1, 1, carry))

    carry = lax.fori_loop(0, (n - 2) // 2, body, carry)
    _, alphas, _ = stage(n - 1, 1, carry, do_qk=False)
    for h in range(nh):
        acc_ref[h] = alphas[h] * acc_ref[h] + pv_fn(n - 1, h, p_buf[1, h])


def _chunk_off(j, tk):
    return j * tk if isinstance(j, int) else pl.multiple_of(j * tk, tk)


def _attn_a_body(cs_ref, ctab_ref, lam_ref, g_ref, q_ref, k_ref, kpos_ref, vt_ref, o_ref,
                 s_buf, p_buf, acc_ref, *, seq, tq, tk, hps, lam_init):
    hg = pl.program_id(1)
    qi = pl.program_id(2)
    n = seq // tk
    cd = (qi * tq) // tk
    lane = lax.broadcasted_iota(jnp.int32, (tq, LANES), 1)
    qpos = (qi * tq + lax.broadcasted_iota(jnp.int32, (1, tq), 1)).astype(F32)
    dv = vt_ref.shape[2] - SUM_ROWS
    c_slope, f0, q_left, q_right = [], [], [], []
    for hh in range(hps):
        head = hg * hps + hh
        c_slope.append(cs_ref[head])
        f0.append(c_slope[hh] * qpos)
        q = q_ref[0, :, hh * LANES:(hh + 1) * LANES]
        qaug = jnp.broadcast_to(ctab_ref[pl.ds(head, 1), :], (tq, LANES)).astype(BF)
        for half in range(2):
            q_m = jnp.where((lane >= 64 * half) & (lane < 64 * (half + 1)), q, jnp.zeros_like(q))
            q_left.append(jnp.concatenate([q_m, qaug], axis=1))
            q_right.append(jnp.concatenate([q_m, -qaug], axis=1))

    def chunk_of(j):
        jm = j - 1
        c = jnp.where(j == 0, cd, jm + (jm >= cd).astype(jnp.int32))
        return c, c <= cd

    def qk_fn(j, mm):
        hh = mm // 2
        c, left = chunk_of(j)
        off = pl.multiple_of(c * tk, tk)
        k_c = jnp.concatenate([k_ref[0, pl.ds(off, tk), hh * LANES:(hh + 1) * LANES],
                               kpos_ref[pl.ds(off, tk), :]], axis=1)
        q_full = jnp.where(left, q_left[mm], q_right[mm])
        return lax.dot_general(k_c, q_full, NT_DIMS, preferred_element_type=F32)

    def sm_fn(j, mm, s, m, mx, first):
        hh = mm // 2
        if first:
            d = (lax.broadcasted_iota(jnp.int32, (tk, tq), 0) - lax.broadcasted_iota(jnp.int32, (tk, tq), 1)
                 + (cd * tk - qi * tq)).astype(F32)
            return _softmax_chunk(s - (2.0 * c_slope[hh]) * jnp.maximum(d, 0.0), m, f0[hh])
        _, left = chunk_of(j)
        return _softmax_chunk(s, m, jnp.where(left, f0[hh], -f0[hh]), mx)

    def pv_fn(j, mm, p):
        c, _ = chunk_of(j)
        off = pl.multiple_of(c * tk, tk)
        return jnp.dot(vt_ref[0, mm // 2, :, pl.ds(off, tk)], p, preferred_element_type=F32)

    _pipelined_flash(n, 2 * hps, tq, qk_fn, sm_fn, pv_fn, s_buf, p_buf, acc_ref)
    lf = lam_ref[...]
    lam_full = (jnp.exp(jnp.sum(lf[0:1] * lf[1:2], axis=-1, keepdims=True))
                - jnp.exp(jnp.sum(lf[2:3] * lf[3:4], axis=-1, keepdims=True)) + lam_init)
    outs = []
    for hh in range(hps):
        o0, o1 = [acc_ref[2 * hh + half, :dv, :] / acc_ref[2 * hh + half, dv:dv + 1, :] for half in range(2)]
        o = o0 - lam_full * o1
        ms = jnp.mean(o * o, axis=0, keepdims=True)
        outs.append(o * lax.rsqrt(ms + RMS_EPS) * g_ref[...] * (1.0 - lam_init))
    o_ref[0] = jnp.concatenate(outs, axis=0).T.astype(o_ref.dtype)


def _bf16_pieces(c, n=3):
    pieces, rest = [], c.astype(F32)
    for _ in range(n):
        p = rest.astype(BF)
        pieces.append(p.astype(F32))
        rest = rest - p.astype(F32)
    return pieces


def _store_vt_heads(vt_ref, vt, dv, heads_per_slot):
    tm = vt.shape[1]
    tail = jnp.where(lax.broadcasted_iota(jnp.int32, (SUM_ROWS, tm), 0) == 0, 1.0, 0.0).astype(vt_ref.dtype)
    rows = dv + SUM_ROWS
    for h in range(vt.shape[0] // dv):
        slot, r0 = h // heads_per_slot, (h % heads_per_slot) * rows
        vt_ref[0, slot, r0:r0 + dv, :] = vt[h * dv:(h + 1) * dv, :].astype(vt_ref.dtype)
        vt_ref[0, slot, r0 + dv:r0 + rows, :] = tail


def _proj_a_body(x_ref, wqk_ref, wvt_ref, qk_ref, vt_ref):
    xb = x_ref[...].astype(BF)
    qk_ref[...] = jnp.dot(xb, wqk_ref[...], preferred_element_type=F32).astype(BF)
    vt = lax.dot_general(wvt_ref[...], xb, NT_DIMS, preferred_element_type=F32)
    _store_vt_heads(vt_ref, vt, 2 * A_HEAD_DIM, 1)


def _flash_chunk_len(seq, tk_max):
    for min_chunks in (8, 4):
        tk = tk_max
        while tk >= 2 * LANES and (seq % tk or (seq // tk) < min_chunks or (seq // tk) % 2):
            tk //= 2
        if tk >= 2 * LANES:
            return tk
    raise ValueError(f"sequence length {seq} too short for the pipelined attention kernel")


def _flash_scratch(nh, rows, tq, tk):
    return [pltpu.VMEM((2, nh, tk, tq), F32), pltpu.VMEM((2, nh, tk, tq), BF), pltpu.VMEM((nh, rows, tq), F32)]


def _attn_a(qkv, vt, lam, subln_g, lam_init, tq=256, tk=2048):
    bsz, seq, _ = qkv.shape
    tq, tk = min(tq, seq), _flash_chunk_len(seq, tk)
    c_slope = jnp.asarray(2.0 ** (-8.0 * np.arange(1, A_HEADS + 1) / A_HEADS) * LOG2E, dtype=F32)
    c1, c2, c3 = _bf16_pieces(c_slope)
    ctab = jnp.stack([c1, c1, c2, c2, c3, c3], axis=-1)
    ctab = jnp.concatenate([ctab, jnp.zeros((A_HEADS, LANES - 6), F32)], axis=-1)
    pos = np.arange(seq)
    kpos = np.zeros((seq, LANES), np.float32)
    for j in range(3):
        kpos[:, 2 * j] = (pos // LANES) * LANES
        kpos[:, 2 * j + 1] = pos % LANES
    kpos = jnp.asarray(kpos, BF)
    vrows = vt.shape[2]
    hps = 2 if seq <= 4096 else 1
    ngrp = A_HEADS // hps
    body = functools.partial(_attn_a_body, seq=seq, tq=tq, tk=tk, hps=hps, lam_init=lam_init)
    return pl.pallas_call(
        body,
        grid=(bsz, ngrp, seq // tq),
        in_specs=[
            pl.BlockSpec(memory_space=pltpu.SMEM),
            pl.BlockSpec((A_HEADS, LANES), lambda b, h, i: (0, 0)),
            pl.BlockSpec((4, A_HEAD_DIM), lambda b, h, i: (0, 0)),
            pl.BlockSpec((2 * A_HEAD_DIM, 1), lambda b, h, i: (0, 0)),
            pl.BlockSpec((1, tq, hps * LANES), lambda b, h, i: (b, i, h)),
            pl.BlockSpec((1, seq, hps * LANES), lambda b, h, i: (b, 0, ngrp + h)),
            pl.BlockSpec((seq, LANES), lambda b, h, i: (0, 0)),
            pl.BlockSpec((1, hps, vrows, seq), lambda b, h, i: (b, h, 0, 0)),
        ],
        out_specs=pl.BlockSpec((1, tq, hps * LANES), lambda b, h, i: (b, i, h)),
        out_shape=jax.ShapeDtypeStruct((bsz, seq, A_HEADS * LANES), BF),
        scratch_shapes=_flash_scratch(2 * hps, vrows, tq, tk),
        compiler_params=_cparams("parallel", "parallel", "arbitrary"),
        name="attn_a",
    )(c_slope, ctab, lam.astype(F32), subln_g.astype(F32).reshape(2 * A_HEAD_DIM, 1), qkv, qkv, kpos, vt)


def _attn_b_body(q_ref, k_ref, vt_ref, o_ref, s_buf, p_buf, acc_ref, *, seq, tk, nh):
    tq = q_ref.shape[1]
    rows = B_VDIM + SUM_ROWS
    qs = [q_ref[0, :, h * LANES:(h + 1) * LANES] for h in range(nh)]

    def qk_fn(j, h):
        k_c = k_ref[0, pl.ds(_chunk_off(j, tk), tk), h * LANES:(h + 1) * LANES]
        return lax.dot_general(k_c, qs[h], NT_DIMS, preferred_element_type=F32)

    def sm_fn(j, h, s, m, mx, first):
        return _softmax_chunk(s, m, None, mx)

    def pv_fn(j, h, p):
        vt_c = vt_ref[0, h // 2, (h % 2) * rows:(h % 2 + 1) * rows, pl.ds(_chunk_off(j, tk), tk)]
        return jnp.dot(vt_c, p, preferred_element_type=F32)

    _pipelined_flash(seq // tk, nh, tq, qk_fn, sm_fn, pv_fn, s_buf, p_buf, acc_ref)
    outs = [acc_ref[h, :B_VDIM, :] / acc_ref[h, B_VDIM:B_VDIM + 1, :] for h in range(nh)]
    o = jnp.concatenate(outs, axis=0)
    o_ref[0] = o.T.astype(o_ref.dtype)


def _attn_b(q, k, vt, tq=256, tk=2048):
    bsz, seq, _ = q.shape
    tq, tk = min(tq, seq), _flash_chunk_len(seq, tk)
    nh = 4 if seq <= 4096 else 2
    return pl.pallas_call(
        functools.partial(_attn_b_body, seq=seq, tk=tk, nh=nh),
        grid=(bsz, B_HEADS // nh, seq // tq),
        in_specs=[
            pl.BlockSpec((1, tq, nh * LANES), lambda b, h, i: (b, i, h)),
            pl.BlockSpec((1, seq, nh * LANES), lambda b, h, i: (b, 0, h)),
            pl.BlockSpec((1, nh // 2, 2 * (B_VDIM + SUM_ROWS), seq), lambda b, h, i: (b, h, 0, 0)),
        ],
        out_specs=pl.BlockSpec((1, tq, nh * B_VDIM), lambda b, h, i: (b, i, h)),
        out_shape=jax.ShapeDtypeStruct((bsz, seq, B_HEADS * B_VDIM), BF),
        scratch_shapes=_flash_scratch(nh, B_VDIM + SUM_ROWS, tq, tk),
        compiler_params=_cparams("parallel", "parallel", "arbitrary"),
        name="attn_b",
    )(q, k, vt)


def _mla_in_body(x_ref, w_ref, gq_ref, gkv_ref, rc_ref, rs_ref, cq_ref, ckv_ref, kr_ref):
    h = jnp.dot(x_ref[...].astype(BF), w_ref[...], preferred_element_type=F32)
    cq_ref[...] = _rms_rows(h[:, :B_Q_LORA], gq_ref[...]).astype(BF)
    ckv_ref[...] = _rms_rows(h[:, B_Q_LORA:B_Q_LORA + B_KV_LORA], gkv_ref[...]).astype(BF)
    t = h[:, B_Q_LORA + B_KV_LORA:]
    kr = t * rc_ref[...] + pltpu.roll(t, LANES - B_ROPE, 1) * rs_ref[...]
    kr_ref[...] = kr.astype(BF)


def _mla_q_body(cq_ref, w_ref, ta_ref, tb_ref, q_ref):
    t = jnp.dot(cq_ref[...], w_ref[...], preferred_element_type=F32)
    ta = ta_ref[...]
    tb = tb_ref[...]
    for h in range(B_HEADS):
        th = t[:, h * LANES:(h + 1) * LANES]
        q_ref[:, h * LANES:(h + 1) * LANES] = (th * ta + pltpu.roll(th, LANES - B_ROPE, 1) * tb).astype(BF)


def _mla_kv_body(ckv_ref, kr_ref, wk_ref, e_ref, wvt_ref, k_ref, vt_ref):
    ckv = ckv_ref[...]
    k = jnp.dot(ckv, wk_ref[...], preferred_element_type=F32)
    k = k + jnp.dot(kr_ref[...], e_ref[...], preferred_element_type=F32)
    k_ref[...] = k.astype(BF)
    vt = lax.dot_general(wvt_ref[...], ckv, NT_DIMS, preferred_element_type=F32)
    _store_vt_heads(vt_ref, vt, B_VDIM, 2)


def _rope_partner(w):
    half = B_ROPE // 2
    return jnp.concatenate([-w[..., half:], w[..., :half]], axis=-1)


def _mla_tables(seq):
    inv = 1.0 / (ROPE_THETA ** (np.arange(0, B_ROPE, 2, dtype=np.float32) / B_ROPE))
    ang = jnp.arange(seq, dtype=F32)[:, None] * jnp.asarray(inv, F32)[None, :]
    cos = jnp.concatenate([jnp.cos(ang), jnp.cos(ang)], axis=-1)
    sin = jnp.concatenate([jnp.sin(ang), jnp.sin(ang)], axis=-1)
    z = lambda n: jnp.zeros((seq, n), F32)
    rc = jnp.concatenate([cos, z(LANES - B_ROPE)], axis=-1)
    rs = jnp.concatenate([sin, z(LANES - B_ROPE)], axis=-1)
    scale = (B_NOPE + B_ROPE) ** -0.5 * LOG2E
    ta =jnp.concatenate([jnp.full((seq, B_NOPE), scale, F32), cos * scale, z(B_ROPE)], axis=-1)
    tb = jnp.concatenate([z(B_NOPE), sin * scale, z(B_ROPE)], axis=-1)
    return rc, rs, ta, tb


def _mla_weights(w_in, w_uq, w_ukv):
    d = w_in.shape[0]
    kr0 = B_Q_LORA + B_KV_LORA
    w_in_ext = jnp.concatenate(
        [w_in, _rope_partner(w_in[:, kr0:kr0 + B_ROPE]), jnp.zeros((d, LANES - 2 * B_ROPE), w_in.dtype)], axis=-1)
    wq = w_uq.reshape(B_Q_LORA, B_HEADS, B_NOPE + B_ROPE)
    wq_ext = jnp.concatenate([wq, _rope_partner(wq[..., B_NOPE:])], axis=-1).reshape(B_Q_LORA, B_HEADS * LANES)
    wkv = w_ukv.reshape(B_KV_LORA, B_HEADS, B_NOPE + B_VDIM)
    wk = jnp.concatenate([wkv[..., :B_NOPE], jnp.zeros((B_KV_LORA, B_HEADS, LANES - B_NOPE), w_ukv.dtype)], axis=-1)
    wk = wk.reshape(B_KV_LORA, B_HEADS * LANES)
    wv = wkv[..., B_NOPE:].reshape(B_KV_LORA, B_HEADS * B_VDIM)
    place = np.zeros((LANES, B_HEADS, LANES), np.float32)
    for j in range(B_ROPE):
        place[j, :, B_NOPE + j] = 1.0
    place = jnp.asarray(place.reshape(LANES, B_HEADS * LANES), BF)
    return w_in_ext.astype(BF), wq_ext.astype(BF), wk.astype(BF), place, wv.astype(BF)


def _mixer_b(x, w_in, q_norm_g, kv_norm_g, w_uq, w_ukv, tm=512):
    bsz, seq, d = x.shape
    t = bsz * seq
    tm = min(tm, seq)
    nblk = seq // tm
    w_in_ext, wq_ext, wk, place, wv = _mla_weights(w_in, w_uq, w_ukv)
    rc, rs, ta, tb = _mla_tables(seq)
    x2 = x.reshape(t, d)
    n_in = w_in_ext.shape[1]
    row = lambda i: (i, 0)
    fixed = lambda i: (0, 0)
    pos = lambda i: (i % nblk, 0)
    cq, ckv, kr = pl.pallas_call(
        _mla_in_body,
        grid=(t // tm,),
        in_specs=[
            pl.BlockSpec((tm, d), row), pl.BlockSpec((d, n_in), fixed),
            pl.BlockSpec((1, B_Q_LORA), fixed), pl.BlockSpec((1, B_KV_LORA), fixed),
            pl.BlockSpec((tm, LANES), pos), pl.BlockSpec((tm, LANES), pos),
        ],
        out_specs=[pl.BlockSpec((tm, B_Q_LORA), row), pl.BlockSpec((tm, B_KV_LORA), row),
                   pl.BlockSpec((tm, LANES), row)],
        out_shape=[jax.ShapeDtypeStruct((t, B_Q_LORA), BF), jax.ShapeDtypeStruct((t, B_KV_LORA), BF),
                   jax.ShapeDtypeStruct((t, LANES), BF)],
        compiler_params=_cparams("parallel"),
        name="mla_in",
    )(x2, w_in_ext, q_norm_g.astype(F32).reshape(1, -1), kv_norm_g.astype(F32).reshape(1, -1), rc, rs)
    nq = B_HEADS * LANES
    q = pl.pallas_call(
        _mla_q_body,
        grid=(t // tm,),
        in_specs=[pl.BlockSpec((tm, B_Q_LORA), row), pl.BlockSpec((B_Q_LORA, nq), fixed),
                  pl.BlockSpec((tm, LANES), pos), pl.BlockSpec((tm, LANES), pos)],
        out_specs=pl.BlockSpec((tm, nq), row),
        out_shape=jax.ShapeDtypeStruct((t, nq), BF),
        compiler_params=_cparams("parallel"),
        name="mla_q",
    )(cq, wq_ext, ta, tb)
    nv = B_HEADS * B_VDIM
    npair = B_HEADS // 2
    vrows = 2 * (B_VDIM + SUM_ROWS)
    k, vt = pl.pallas_call(
        _mla_kv_body,
        grid=(t // tm,),
        in_specs=[pl.BlockSpec((tm, B_KV_LORA), row), pl.BlockSpec((tm, LANES), row),
                  pl.BlockSpec((B_KV_LORA, nq), fixed), pl.BlockSpec((LANES, nq), fixed),
                  pl.BlockSpec((nv, B_KV_LORA), fixed)],
        out_specs=[pl.BlockSpec((tm, nq), row),
                   pl.BlockSpec((1, npair, vrows, tm), lambda i: (i // nblk, 0, 0, i % nblk))],
        out_shape=[jax.ShapeDtypeStruct((t, nq), BF), jax.ShapeDtypeStruct((bsz, npair, vrows, seq), BF)],
        compiler_params=_cparams("parallel"),
        name="mla_kv",
    )(ckv, kr, wk, place, wv.T)
    o = _attn_b(q.reshape(bsz, seq, nq), k.reshape(bsz, seq, nq), vt)
    return o.reshape(t, nv)


def _mixer_a(x, w_in, lam, subln_g, layer_idx):
    bsz, seq, d = x.shape
    t = bsz * seq
    hd2 = 2 * A_HEAD_DIM
    nq = A_HEADS * hd2
    scale = A_HEAD_DIM ** -0.5 * LOG2E
    wqk = jnp.concatenate([w_in[:, :nq] * scale, w_in[:, nq:2 * nq]], axis=-1).astype(BF)
    wvt = w_in[:, 2 * nq:].T.astype(BF)
    tm = min(512, seq)
    nblk = seq // tm
    vrows = hd2 + SUM_ROWS
    qk, vt = pl.pallas_call(
        _proj_a_body,
        grid=(t // tm,),
        in_specs=[pl.BlockSpec((tm, d), lambda i: (i, 0)), pl.BlockSpec((d, 2 * nq), lambda i: (0, 0)),
                  pl.BlockSpec((nq, d), lambda i: (0, 0))],
        out_specs=[pl.BlockSpec((tm, 2 * nq), lambda i: (i, 0)),
                   pl.BlockSpec((1, A_HEADS, vrows, tm), lambda i: (i // nblk, 0, 0, i % nblk))],
        out_shape=[jax.ShapeDtypeStruct((t, 2 * nq), BF), jax.ShapeDtypeStruct((bsz, A_HEADS, vrows, seq), BF)],
        compiler_params=_cparams("parallel"),
        name="proj_a",
    )(x.reshape(t, d), wqk, wvt)
    lam_init = 0.8 - 0.6 * math.exp(-0.3 * layer_idx)
    o = _attn_a(qk.reshape(bsz, seq, 2 * nq), vt, lam, subln_g, lam_init)
    return o.reshape(t, nq)


def _attn_c_body(q_ref, k0, k1, k2, k3, v0, v1, v2, v3, bt_ref, o_ref):
    kw = jnp.concatenate([k0[0], k1[0], k2[0], k3[0]], axis=0)
    vw = jnp.concatenate([v0[0], v1[0], v2[0], v3[0]], axis=0)
    nq = q_ref.shape[1] // 2
    nk = bt_ref.shape[3]
    lane = lax.broadcasted_iota(jnp.int32, (nq, LANES), 1)
    for hf in range(2):
        q = q_ref[0, hf * nq:(hf + 1) * nq, :]
        kh = kw[hf * nq:hf * nq + nk]
        vh = vw[hf * nq:hf * nq + nk]
        outs = []
        for hh in range(2):
            qm = jnp.where((lane >= 64 * hh) & (lane < 64 * (hh + 1)), q, jnp.zeros_like(q))
            s = (lax.dot_general(qm, kh, NT_DIMS, preferred_element_type=F32)
                 + bt_ref[hh, 0, hf * nq:(hf + 1) * nq, :])
            m = jnp.max(s, axis=1, keepdims=True)
            p = jnp.exp2(s - m)
            l = jnp.sum(p, axis=1, keepdims=True)
            outs.append(jnp.dot(p.astype(BF), vh, preferred_element_type=F32) / l)
        o_ref[0, hf * nq:(hf + 1) * nq, :] = jnp.where(lane < 64, outs[0], outs[1]).astype(o_ref.dtype)


def _nbr_bias_table(rpb):
    col = np.arange(GRID_W)
    col_start = np.clip(col - C_WIN_COLS // 2, 0, GRID_W - C_WIN_COLS)
    col_mask = (col[None, :] >= col_start[:, None]) & (col[None, :] < col_start[:, None] + C_WIN_COLS)
    pad = GRID_W - C_WIN_COLS
    ext = jnp.pad(rpb.astype(F32) * LOG2E, ((0, 0), (0, 0), (pad, pad)), mode="edge")
    toep = jnp.stack([ext[:, :, GRID_W - 1 - qc:2 * GRID_W - 1 - qc] for qc in range(GRID_W)], axis=2)
    toep = jnp.where(jnp.asarray(col_mask)[None, None], toep, NEG)
    neg = jnp.full((rpb.shape[0], GRID_W, GRID_W), NEG, F32)
    half = C_WIN_ROWS // 2
    kinds = []
    for kind in range(3):
        qrows = []
        for t in range(C_WIN_ROWS):
            u0 = (max(t, half), t, min(t, half))[kind]
            ubase = 0 if t < half else half
            blocks = [toep[:, u - t + half - 1] if u0 <= u < u0 + C_WIN_ROWS else neg
                      for u in range(ubase, ubase + C_WIN_ROWS + half)]
            qrows.append(jnp.concatenate(blocks, axis=-1))
        kinds.append(jnp.concatenate(qrows, axis=1))
    return jnp.stack(kinds, axis=1)


def _mixer_c(x, w_qkv, rpb):
    bsz, seq, d = x.shape
    t = bsz * seq
    nq = C_HEADS * C_HEAD_DIM
    scale = C_HEAD_DIM ** -0.5 * LOG2E
    w = jnp.concatenate([w_qkv[:, :nq] * scale, w_qkv[:, nq:]], axis=-1).astype(BF)
    qkv = _linear(x.reshape(t, d), w, BF, tn=nq).reshape(bsz, seq, 3 * nq)
    bt = _nbr_bias_table(rpb)
    npair = C_HEADS // 2
    qtok = C_WIN_ROWS * GRID_W
    ngrp = seq // qtok
    assert ngrp >= 2
    kb = qtok // 2
    nkb = seq // kb
    kspecs = []
    for off in (npair, 2 * npair):
        for j in range(4):
            kspecs.append(pl.BlockSpec(
                (1, kb, LANES),
                lambda h, g, b, j=j, off=off: (b, jnp.clip(2 * g - 1 + j, 0, nkb - 1), off + h)))
    kind = lambda g: jnp.where(g == 0, 0, jnp.where(g == ngrp - 1, 2, 1))
    o = pl.pallas_call(
        _attn_c_body,
        grid=(npair, ngrp, bsz),
        in_specs=[pl.BlockSpec((1, qtok, LANES), lambda h, g, b: (b, g, h))] + kspecs + [
            pl.BlockSpec((2, 1, qtok, bt.shape[3]), lambda h, g, b: (h, kind(g), 0, 0))],
        out_specs=pl.BlockSpec((1, qtok, LANES), lambda h, g, b: (b, g, h)),
        out_shape=jax.ShapeDtypeStruct((bsz, seq, nq), BF),
        compiler_params=_cparams("parallel", "parallel", "arbitrary"),
        name="attn_c",
    )(qkv, *([qkv] * 8), bt)
    return o.reshape(t, nq)


def _attn_d_body(bias_ref, q_ref, kp, kc, kn, vp, vc, vn, o_ref, lse_ref, *, tq, length, rad):
    i = pl.program_id(2)
    nk = tq + 2 * rad
    ki = i * tq - rad + lax.broadcasted_iota(jnp.int32, (1, nk), 1)
    edge = jnp.where((ki >= 0) & (ki < length), 0.0, NEG)
    lane = lax.broadcasted_iota(jnp.int32, (tq, LANES), 1)
    for hp in range(D_HEADS // 2):
        sl = slice(hp * LANES, (hp + 1) * LANES)
        q = q_ref[0, 0, :, sl]
        kw = jnp.concatenate([kp[0, 0, tq - rad:tq, sl], kc[0, 0, :, sl], kn[0, 0, 0:rad, sl]], axis=0)
        vw = jnp.concatenate([vp[0, 0, tq - rad:tq, sl], vc[0, 0, :, sl], vn[0, 0, 0:rad, sl]], axis=0)
        outs, lses = [], []
        for hh in range(2):
            qm = jnp.where((lane >= 64 * hh) & (lane < 64 * (hh + 1)), q, jnp.zeros_like(q))
            s = lax.dot_general(qm, kw, NT_DIMS, preferred_element_type=F32)
            s = s + (bias_ref[hp * 2 + hh] + edge)
            m = jnp.max(s, axis=1, keepdims=True)
            p = jnp.exp2(s - m)
            l = jnp.sum(p, axis=1, keepdims=True)
            outs.append(jnp.dot(p.astype(BF), vw, preferred_element_type=F32) / l)
            lses.append(m + jnp.log(l) * LOG2E)
        o_ref[0, 0, :, sl] = jnp.where(lane < 64, outs[0], outs[1]).astype(o_ref.dtype)
        lse_ref[0, 0, :, sl] = jnp.where(lane < 64, lses[0], lses[1])


def _attn_d_group(qkv, window, dil, tq=128):
    bsz, _, length, ncol = qkv.shape
    rad = window // (2 * dil)
    tq = min(tq, length)
    nq = length // tq
    nh = ncol // 3
    nk = tq + 2 * rad
    slopes = 2.0 ** (-8.0 * np.arange(1, D_HEADS + 1) / D_HEADS) * LOG2E
    dist = np.abs(np.arange(nk)[None, :] - rad - np.arange(tq)[:, None])
    bias = np.where(dist <= rad, -slopes[:, None, None] * (dist * dil), NEG)
    bias = jnp.asarray(bias, F32)

    def spec(which, shift):
        return pl.BlockSpec((1, 1, tq, nh), lambda b, c, i: (b, c, jnp.clip(i + shift, 0, nq - 1), which))

    ospec = pl.BlockSpec((1, 1, tq, nh), lambda b, c, i: (b, c, i, 0))
    return pl.pallas_call(
        functools.partial(_attn_d_body, tq=tq, length=length, rad=rad),
        grid=(bsz, dil, nq),
        in_specs=[pl.BlockSpec(bias.shape, lambda b, c, i: (0, 0, 0)), spec(0, 0),
                  spec(1, -1), spec(1, 0), spec(1, 1), spec(2, -1), spec(2, 0), spec(2, 1)],
        out_specs=[ospec, ospec],
        out_shape=[jax.ShapeDtypeStruct((bsz, dil, length, nh), BF),
                   jax.ShapeDtypeStruct((bsz, dil, length, nh), F32)],
        compiler_params=_cparams("parallel", "parallel", "arbitrary"),
        name=f"attn_d{dil}",
    )(bias, qkv, qkv, qkv, qkv, qkv, qkv, qkv)


def _proj_d_body(x_ref, w_ref, o0, o1, o2, scr, *, dils):
    xb = x_ref[...].astype(BF)
    tm = xb.shape[0]
    nblk = scr.shape[0]
    ncol = nblk * LANES
    for gi, (o_ref, dil) in enumerate(zip((o0, o1, o2), dils)):
        res = jnp.dot(xb, w_ref[:, gi * ncol:(gi + 1) * ncol], preferred_element_type=F32)
        if dil == 1:
            o_ref[0, 0] = res.astype(BF)
            continue
        for j in range(nblk):
            scr[j] = res[:, j * LANES:(j + 1) * LANES]
        for c in range(dil):
            for j in range(nblk):
                o_ref[0, c, :, j * LANES:(j + 1) * LANES] = scr[j, pl.ds(c, tm // dil, stride=dil), :].astype(BF)


def _post_d_body(o0, o1, o2, l0, l1, l2, x_ref, w_ref, g_ref, b_ref, y_ref, scr_o, scr_l, *, alpha, dils):
    tm = x_ref.shape[0]

    def interleaved(ref, scr, dil):
        if dil == 1:
            return ref[0, 0].astype(F32)
        nblk = scr.shape[0]
        for c in range(dil):
            for j in range(nblk):
                scr[j, pl.ds(c, tm // dil, stride=dil), :] = ref[0, c, :, j * LANES:(j + 1) * LANES].astype(F32)
        return jnp.concatenate([scr[j] for j in range(nblk)], axis=1)

    ov = [interleaved(r, scr_o.at[j], dil) for j, (r, dil) in enumerate(zip((o0, o1, o2), dils))]
    a0, a1, a2 = [interleaved(r, scr_l.at[j], dil) for j, (r, dil) in enumerate(zip((l0, l1, l2), dils))]
    m = jnp.maximum(jnp.maximum(a0, a1), a2)
    e0, e1, e2 = jnp.exp2(a0 - m), jnp.exp2(a1 - m), jnp.exp2(a2 - m)
    den = e0 + e1 + e2
    o = (e0 / den) * ov[0] + (e1 / den) * ov[1] + (e2 / den) * ov[2]
    h = jnp.dot(o.astype(BF), w_ref[...], preferred_element_type=F32)
    y_ref[...] = _layer_norm_rows(alpha * x_ref[...] + h, g_ref[...], b_ref[...])


def _mixer_d_and_post(x2, bsz, seq, w_qkv, w_out, g, b, alpha, tm=512):
    t, d = x2.shape
    nh = D_HEADS * D_HEAD_DIM
    ng = len(D_GROUPS)
    dils = tuple(dil for _, dil in D_GROUPS)
    tm = min(tm, seq)
    nblk = seq // tm
    scale = D_HEAD_DIM ** -0.5 * LOG2E
    wq = w_qkv.reshape(d, 3, ng, nh)
    wq = jnp.stack([wq[:, 0] * scale, wq[:, 1], wq[:, 2]], axis=1)
    w = wq.transpose(0, 2, 1, 3).reshape(d, ng * 3 * nh).astype(BF)

    def deint_spec(dil, width):
        return pl.BlockSpec((1, dil, tm // dil, width), lambda i: (i // nblk, 0, i % nblk, 0))

    qkvs = pl.pallas_call(
        functools.partial(_proj_d_body, dils=dils),
        grid=(t // tm,),
        in_specs=[pl.BlockSpec((tm, d), lambda i: (i, 0)), pl.BlockSpec((d, ng * 3 * nh), lambda i: (0, 0))],
        out_specs=[deint_spec(dil, 3 * nh) for dil in dils],
        out_shape=[jax.ShapeDtypeStruct((bsz, dil, seq // dil, 3 * nh), BF) for dil in dils],
        scratch_shapes=[pltpu.VMEM((3 * nh // LANES, tm, LANES), F32)],
        compiler_params=_cparams("parallel"),
        name="proj_d",
    )(x2, w)
    os_, ls_ = [], []
    for qkv_g, (window, dil) in zip(qkvs, D_GROUPS):
        o, lse = _attn_d_group(qkv_g, window, dil)
        os_.append(o)
        ls_.append(lse)
    row = lambda i: (i, 0)
    fixed = lambda i: (0, 0)
    return pl.pallas_call(
        functools.partial(_post_d_body, alpha=alpha, dils=dils),
        grid=(t // tm,),
        in_specs=[deint_spec(dil, nh) for dil in dils] * 2 + [
            pl.BlockSpec((tm, d), row), pl.BlockSpec((nh, d), fixed),
            pl.BlockSpec((1, d), fixed), pl.BlockSpec((1, d), fixed)],
        out_specs=pl.BlockSpec((tm, d), row),
        out_shape=jax.ShapeDtypeStruct((t, d), F32),
        scratch_shapes=[pltpu.VMEM((ng, nh // LANES, tm, LANES), F32), pltpu.VMEM((ng, nh // LANES, tm, LANES), F32)],
        compiler_params=_cparams("parallel"),
        name="post_d",
    )(*os_, *ls_, x2, w_out.astype(BF), g.reshape(1, d), b.reshape(1, d))


def _router_gates(x, w2_ref, rb_ref):
    tm = x.shape[0]
    xh = x.astype(BF)
    xl = (x - xh.astype(F32)).astype(BF)
    w2 = w2_ref[...]
    both = lax.dot_general(w2, xh, NT_DIMS, preferred_element_type=F32)
    logits = (both[:N_EXPERTS] + both[N_EXPERTS:]
              + lax.dot_general(w2[:N_EXPERTS], xl, NT_DIMS, preferred_element_type=F32))
    scores = jax.nn.sigmoid(logits)
    biased = scores + rb_ref[...]
    epg = N_EXPERTS // N_EXPERT_GROUPS
    sc = [scores[e:e + 1, :] for e in range(N_EXPERTS)]
    bi = [biased[e:e + 1, :] for e in range(N_EXPERTS)]
    gs = []
    for g in range(N_EXPERT_GROUPS):
        v = bi[g * epg:(g + 1) * epg]
        best = None
        for a in range(epg):
            for c in range(a + 1, epg):
                pair = v[a] + v[c]
                best = pair if best is None else jnp.maximum(best, pair)
        gs.append(best)
    gmax = functools.reduce(jnp.maximum, gs)
    taken = jnp.zeros((1, tm), jnp.bool_)
    cand = []
    for g in range(N_EXPERT_GROUPS):
        sel = (gs[g] == gmax) & jnp.logical_not(taken)
        taken = taken | sel
        for a in range(epg):
            cand.append(jnp.where(sel, bi[g * epg + a], -jnp.inf))
    m1 = functools.reduce(jnp.maximum, cand)
    taken = jnp.zeros((1, tm), jnp.bool_)
    is1 = []
    for e in range(N_EXPERTS):
        hit = (cand[e] == m1) & jnp.logical_not(taken)
        taken = taken | hit
        is1.append(hit)
    cand2 = [jnp.where(is1[e], -jnp.inf, cand[e]) for e in range(N_EXPERTS)]
    m2 = functools.reduce(jnp.maximum, cand2)
    taken = jnp.zeros((1, tm), jnp.bool_)
    is2 = []
    for e in range(N_EXPERTS):
        hit = (cand2[e] == m2) & jnp.logical_not(taken)
        taken = taken | hit
        is2.append(hit)
    zero = jnp.zeros((1, tm), F32)
    w1 = functools.reduce(jnp.add, [jnp.where(is1[e], sc[e], zero) for e in range(N_EXPERTS)])
    w2 = functools.reduce(jnp.add, [jnp.where(is2[e], sc[e], zero) for e in range(N_EXPERTS)])
    den = w1 + w2
    rows = [jnp.where(is1[e], w1 / den, zero) + jnp.where(is2[e], w2 / den, zero) for e in range(N_EXPERTS)]
    rows.append(jnp.zeros((LANES - N_EXPERTS, tm), F32))
    return jnp.concatenate(rows, axis=0).T


MOE_EXPERTS_PER_STEP = 2


def _moe_body(*refs, alpha, n_routed_steps, with_mixer_out):
    if with_mixer_out:
        o_ref, wo_ref, g1_ref, b1_ref, *refs = refs
    (x_ref, p_ref, w2_ref, rb_ref, wgu_ref, wd_ref, wgus_ref, wds_ref, g_ref, b_ref, wpi_ref, wpg_ref,
     y_ref, gates, xb, acc, x1) = refs
    e = pl.program_id(1)
    ff = EXPERT_FF

    def hidden(gu, j):
        return jax.nn.silu(gu[:, 2 * j * ff:(2 * j + 1) * ff]) * gu[:, (2 * j + 1) * ff:(2 * j + 2) * ff]

    tm = x_ref.shape[0]
    halves = [slice(r * tm // 2, (r + 1) * tm // 2) for r in range(2)]

    @pl.when(e == 0)
    def _():
        for rows in halves:
            x = x_ref[rows, :]
            if with_mixer_out:
                h = jnp.dot(o_ref[rows, :], wo_ref[...], preferred_element_type=F32)
                x = _layer_norm_rows(alpha * x + h, g1_ref[...], b1_ref[...])
            x1[rows, :] = x
            gates[rows, :] = _router_gates(x, w2_ref, rb_ref)
            xb[rows, :] = x.astype(BF)
        acc[...] = jnp.zeros_like(acc)

    @pl.when(e < n_routed_steps)
    def _():
        gu = jnp.dot(xb[...], wgu_ref[0], preferred_element_type=F32)
        lane = lax.broadcasted_iota(jnp.int32, gates.shape, 1)
        gt = gates[...]
        hs = []
        for j in range(MOE_EXPERTS_PER_STEP):
            gcol = jnp.sum(jnp.where(lane == e * MOE_EXPERTS_PER_STEP + j, gt, 0.0), axis=1, keepdims=True)
            hs.append((hidden(gu, j) * gcol).astype(BF))
        acc[...] += jnp.dot(jnp.concatenate(hs, axis=1), wd_ref[0], preferred_element_type=F32)

    @pl.when(e == n_routed_steps)
    def _():
        for rows in halves:
            gu = jnp.dot(xb[rows, :], wgus_ref[...], preferred_element_type=F32)
            y = acc[rows, :] + jnp.dot(hidden(gu, 0).astype(BF), wds_ref[...], preferred_element_type=F32)
            x2 = _layer_norm_rows(alpha * x1[rows, :] + y, g_ref[...], b_ref[...])
            emb = jnp.dot(p_ref[rows, :].astype(BF), wpi_ref[...], preferred_element_type=F32)
            gate = jax.nn.sigmoid(jnp.dot(x2.astype(BF), wpg_ref[...], preferred_element_type=F32))
            y_ref[rows, :] = x2 + gate * emb


def _moe_layer(x2, p2, router, lw, g, b, alpha, mixer_out=None, tm=1024):
    t, d = x2.shape
    wgu, wd, wgus, wds, wpi, wpg = lw["wgu"], lw["wd"], lw["wgus"], lw["wds"], lw["wpi"], lw["wpg"]
    n_routed_steps = wgu.shape[0]
    w2, rb = router
    tok = lambda i, e: (i, 0)
    fixed = lambda i, e: (0, 0)
    step = lambda i, e: (jnp.minimum(e, n_routed_steps - 1), 0, 0)
    pre_specs, pre_args = [], []
    if mixer_out is not None:
        o, w_out, g1, b1 = mixer_out
        pre_specs = [pl.BlockSpec((tm, o.shape[1]), tok), pl.BlockSpec(w_out.shape, fixed),
                     pl.BlockSpec((1, d), fixed), pl.BlockSpec((1, d), fixed)]
        pre_args = [o, w_out, g1.reshape(1, d), b1.reshape(1, d)]
    return pl.pallas_call(
        functools.partial(_moe_body, alpha=alpha, n_routed_steps=n_routed_steps,
                          with_mixer_out=mixer_out is not None),
        grid=(t // tm, n_routed_steps + 1),
        in_specs=pre_specs + [
            pl.BlockSpec((tm, d), tok), pl.BlockSpec((tm, p2.shape[1]), tok),
            pl.BlockSpec((2 * N_EXPERTS, d), fixed), pl.BlockSpec((N_EXPERTS, 1), fixed),
            pl.BlockSpec((1,) + wgu.shape[1:], step), pl.BlockSpec((1,) + wd.shape[1:], step),
            pl.BlockSpec(wgus.shape, fixed), pl.BlockSpec(wds.shape, fixed),
            pl.BlockSpec((1, d), fixed), pl.BlockSpec((1, d), fixed),
            pl.BlockSpec(wpi.shape, fixed), pl.BlockSpec(wpg.shape, fixed),
        ],
        out_specs=pl.BlockSpec((tm, d), tok),
        out_shape=jax.ShapeDtypeStruct((t, d), F32),
        scratch_shapes=[pltpu.VMEM((tm, LANES), F32), pltpu.VMEM((tm, d), BF), pltpu.VMEM((tm, d), F32),
                        pltpu.VMEM((tm, d), F32)],
        compiler_params=_cparams("parallel", "arbitrary"),
        name="moe",
    )(*pre_args, x2, p2, w2, rb, wgu, wd, wgus, wds, g.reshape(1, d), b.reshape(1, d), wpi, wpg)


def _prep_shared(prm):
    depth = prm["ln1_g"].shape[0]
    rw = prm["router_w"].astype(F32)
    wh = rw.astype(BF)
    wl = (rw - wh.astype(F32)).astype(BF)
    router = (jnp.concatenate([wh.T, wl.T], axis=0), prm["router_b"].astype(F32).reshape(N_EXPERTS, 1))
    layers = []
    for i in range(depth):
        eps = MOE_EXPERTS_PER_STEP
        d = prm["moe_w_gate"].shape[2]
        wgu = jnp.concatenate([prm["moe_w_gate"][i], prm["moe_w_up"][i]], axis=-1)
        wgu = wgu.reshape(N_EXPERTS // eps, eps, d, 2 * EXPERT_FF).transpose(0, 2, 1, 3)
        wgu = wgu.reshape(N_EXPERTS // eps, d, eps * 2 * EXPERT_FF).astype(BF)
        wd = prm["moe_w_down"][i].reshape(N_EXPERTS // eps, eps * EXPERT_FF, d).astype(BF)
        wgus = jnp.concatenate([prm["moe_ws_gate"][i], prm["moe_ws_up"][i]], axis=-1).astype(BF)
        layers.append(dict(wgu=wgu, wd=wd, wgus=wgus, wds=prm["moe_ws_down"][i].astype(BF),
                           wpi=prm["ple_w_in"][i].astype(BF), wpg=prm["ple_w_gate"][i].astype(BF)))
    return router, layers


def _trunk(x, p, prm, router, layers):
    depth = prm["ln1_g"].shape[0]
    alpha = (2.0 * depth) ** 0.25
    bsz, seq, d = x.shape
    t = bsz * seq
    x2 = x.reshape(t, d)
    for i in range(depth):
        mixer, j = i % 4, i // 4
        xb = x2.reshape(bsz, seq, d)
        mixer_out = None
        if mixer == 3:
            x2 = _mixer_d_and_post(x2, bsz, seq, prm["d_w_qkv"][j], prm["d_w_out"][j],
                                   prm["ln1_g"][i], prm["ln1_b"][i], alpha)
        else:
            if mixer == 0:
                o = _mixer_a(xb, prm["a_w_in"][j], prm["a_lambda"][j], prm["a_subln"][j], i)
                w_out = prm["a_w_out"][j]
            elif mixer == 1:
                o = _mixer_b(xb, prm["b_w_in"][j], prm["b_q_norm"][j], prm["b_kv_norm"][j],
                             prm["b_w_uq"][j], prm["b_w_ukv"][j])
                w_out = prm["b_w_out"][j]
            else:
                o = _mixer_c(xb, prm["c_w_qkv"][j], prm["c_rpb"][j])
                w_out = prm["c_w_out"][j]
            mixer_out = (o, w_out.astype(BF), prm["ln1_g"][i], prm["ln1_b"][i])
        x2 = _moe_layer(x2, p[i].reshape(t, -1), router, layers[i], prm["ln2_g"][i], prm["ln2_b"][i], alpha,
                        mixer_out)
    return x2.reshape(bsz, seq, d)


def kernel(x_prompt, x_sample, p_prompt, p_sample, a_w_in, a_lambda, a_subln, a_w_out, b_w_in, b_q_norm, b_kv_norm, b_w_uq, b_w_ukv, b_w_out, c_w_qkv, c_rpb, c_w_out, d_w_qkv, d_w_out, router_w, router_b, moe_w_gate, moe_w_up, moe_w_down, moe_ws_gate, moe_ws_up, moe_ws_down, ln1_g, ln1_b, ln2_g, ln2_b, ple_w_in, ple_w_gate):
    prm = dict(a_w_in=a_w_in, a_lambda=a_lambda, a_subln=a_subln, a_w_out=a_w_out,
               b_w_in=b_w_in, b_q_norm=b_q_norm, b_kv_norm=b_kv_norm, b_w_uq=b_w_uq,
               b_w_ukv=b_w_ukv, b_w_out=b_w_out,
               c_w_qkv=c_w_qkv, c_rpb=c_rpb, c_w_out=c_w_out,
               d_w_qkv=d_w_qkv, d_w_out=d_w_out,
               router_w=router_w, router_b=router_b, moe_w_gate=moe_w_gate, moe_w_up=moe_w_up,
               moe_w_down=moe_w_down, moe_ws_gate=moe_ws_gate, moe_ws_up=moe_ws_up,
               moe_ws_down=moe_ws_down,
               ln1_g=ln1_g, ln1_b=ln1_b, ln2_g=ln2_g, ln2_b=ln2_b,
               ple_w_in=ple_w_in, ple_w_gate=ple_w_gate)
    router, layers = _prep_shared(prm)
    y_prompt = _trunk(x_prompt, p_prompt, prm, router, layers)
    y_sample = _trunk(x_sample, p_sample, prm, router, layers)
    return (y_prompt, y_sample)
```
